```python
import math
import jax, jax.numpy as jnp
from jax import lax
import numpy as np


D_MODEL = 1024
BATCH = 2
SEQ = 16384
DEPTH = 1
DEC_BATCH = 8
DEC_SEQ = 2048
PAST_LEN = 128

FOURIER_GROUPS = 8
FOURIER_GROUP_DIM = 64
FOURIER_WIDTH = FOURIER_GROUPS * FOURIER_GROUP_DIM
HYENA_GROUPS = 8
HYENA_WIDTH = D_MODEL - FOURIER_WIDTH
HYENA_ORDER = 2
HYENA_IN_WIDTH = (HYENA_ORDER + 1) * HYENA_WIDTH
IN_WIDTH = FOURIER_WIDTH + HYENA_IN_WIDTH
MIX_WIDTH = FOURIER_WIDTH + HYENA_WIDTH
SHORT_CONV = 3
FILTER_BANDS = 16
FILTER_EMB = 1 + 2 * FILTER_BANDS
FILTER_ORDER = 64
N_DIRECTIONS = 2
DECAY_TARGET = 1e-2
FAST_DECAY_PCT = 0.3
SLOW_DECAY_PCT = 1.5
N_EXPERTS = 32
TOP_K = 4
D_FF = 1024
SWIGLU_LIMIT = 7.0
SWIGLU_ALPHA = 1.702
ROW_BLOCK = 512
LN_EPS = 1e-5
RMS_EPS = 1e-6
DEEPNORM_ALPHA = (2 * DEPTH) ** 0.25
DEEPNORM_BETA = (8 * DEPTH) ** -0.25

kernel_name = "fnet_hyena_moe_deepnorm_encoder"


def layer_norm(x, g, b):
    xf = x.astype(jnp.float32)
    mu = jnp.mean(xf, axis=-1, keepdims=True)
    xc = xf - mu
    var = jnp.mean(xc * xc, axis=-1, keepdims=True)
    y = xc * lax.rsqrt(var + LN_EPS) * g.astype(jnp.float32) + b.astype(jnp.float32)
    return y.astype(x.dtype)


def rms_norm(x, g):
    xf = x.astype(jnp.float32)
    y = xf * lax.rsqrt(jnp.mean(xf * xf, axis=-1, keepdims=True) + RMS_EPS) * g.astype(jnp.float32)
    return y.astype(x.dtype)


def fourier_group(p_f):
    B, L, _ = p_f.shape
    xg = p_f.reshape(B, L, FOURIER_GROUPS, FOURIER_GROUP_DIM).astype(jnp.float32)
    y = jnp.fft.fftn(xg, axes=(1, 3), norm="ortho").real
    return y.reshape(B, L, FOURIER_WIDTH).astype(p_f.dtype)


def centred_short_conv(u, w, b):
    L = u.shape[1]
    up = jnp.pad(u, ((0, 0), (1, 1), (0, 0)))
    return up[:, :L] * w[0] + up[:, 1:L + 1] * w[1] + up[:, 2:] * w[2] + b


def positional_features(L):
    t = jnp.linspace(0.0, 1.0, L, dtype=jnp.float32)[:, None]
    w = (2.0 * math.pi / L) * jnp.arange(L, dtype=jnp.float32)[:, None]
    bands = jnp.linspace(1e-4, FILTER_BANDS - 1, FILTER_BANDS, dtype=jnp.float32)[None, :]
    fw = bands * w
    return jnp.concatenate([t, jnp.cos(fw), -jnp.sin(fw)], axis=-1)


def hyena_filter(L, w1, b1, w2, b2, w3, b3, w_last, freq, decay):
    f32 = jnp.float32
    z = positional_features(L)
    fr = freq.astype(f32)
    h = jnp.sin(fr * (z @ w1.astype(f32) + b1.astype(f32)))
    h = jnp.sin(fr * (h @ w2.astype(f32) + b2.astype(f32)))
    h = jnp.sin(fr * (h @ w3.astype(f32) + b3.astype(f32)))
    h = (h @ w_last.astype(f32)).reshape(L, N_DIRECTIONS, HYENA_WIDTH)
    t = jnp.linspace(0.0, 1.0, L, dtype=f32)[:, None, None]
    h = h * jnp.exp(-t * jnp.abs(decay.astype(f32)))
    h = h / jnp.sum(jnp.abs(h), axis=(0, 1), keepdims=True)
    hf, hb = h[:, 0], h[:, 1]
    return jnp.concatenate(
        [hf[:1] + hb[:1], hf[1:], jnp.zeros((1, HYENA_WIDTH), f32), hb[:0:-1]], axis=0)


def bidirectional_long_conv(u, k, skip):
    L = u.shape[1]
    u_f = jnp.fft.rfft(u, n=2 * L, axis=1)
    k_f = jnp.fft.rfft(k, axis=0)
    y = jnp.fft.irfft(u_f * k_f[None], n=2 * L, axis=1)[:, :L]
    return y + u * skip.astype(jnp.float32)


def hyena_group(p_h, w_short, b_short, filt_w1, filt_b1, filt_w2, filt_b2, filt_w3, filt_b3,
                filt_w_out, filt_freq, filt_decay, hyena_skip):
    L = p_h.shape[1]
    uc = centred_short_conv(p_h, w_short, b_short)
    x0, x1, v = jnp.split(uc, HYENA_ORDER + 1, axis=-1)
    k = hyena_filter(L, filt_w1, filt_b1, filt_w2, filt_b2, filt_w3, filt_b3,
                     filt_w_out, filt_freq, filt_decay)
    y = bidirectional_long_conv((v * x1).astype(jnp.float32), k, hyena_skip)
    return x0 * y.astype(p_h.dtype)


def routed_moe(x, w_router, b_router, w_gate_up, b_gate_up, w_down, b_down):
    lead = x.shape[:-1]
    xt = x.reshape(-1, D_MODEL)
    T = xt.shape[0]
    logits = (xt @ w_router + b_router).astype(jnp.float32)
    top_logit, top_e = lax.top_k(logits, TOP_K)
    gates = jax.nn.softmax(top_logit, axis=-1).astype(x.dtype)
    n_assign = T * TOP_K
    flat_e = top_e.reshape(-1).astype(jnp.int32)
    order = jnp.argsort(flat_e).astype(jnp.int32)
    sorted_e = flat_e[order]
    counts = jnp.bincount(flat_e, length=N_EXPERTS).astype(jnp.int32)
    padded = (counts + ROW_BLOCK - 1) // ROW_BLOCK * ROW_BLOCK
    pad_end = jnp.cumsum(padded)
    pad_start = pad_end - padded
    start = jnp.cumsum(counts) - counts
    dest_sorted = pad_start[sorted_e] + jnp.arange(n_assign, dtype=jnp.int32) - start[sorted_e]
    n_blocks = -(-n_assign // ROW_BLOCK) + N_EXPERTS
    n_rows = n_blocks * ROW_BLOCK
    row_token = jnp.full((n_rows,), T, jnp.int32).at[dest_sorted].set(order // TOP_K)
    block_start = jnp.arange(n_blocks, dtype=jnp.int32) * ROW_BLOCK
    block_expert = jnp.minimum(jnp.searchsorted(pad_end, block_start, side="right"),
                               N_EXPERTS - 1).astype(jnp.int32)
    x_padded = jnp.concatenate([xt, jnp.zeros((1, D_MODEL), xt.dtype)], axis=0)

    def expert_block(args):
        rows, e = args
        xb = x_padded[rows]
        gu = xb @ w_gate_up[e] + b_gate_up[e]
        gate = jnp.minimum(gu[:, :D_FF], SWIGLU_LIMIT)
        up = jnp.clip(gu[:, D_FF:], -SWIGLU_LIMIT, SWIGLU_LIMIT)
        h = (up + 1.0) * (gate * jax.nn.sigmoid(SWIGLU_ALPHA * gate))
        return h @ w_down[e] + b_down[e]

    y_rows = lax.map(expert_block, (row_token.reshape(n_blocks, ROW_BLOCK), block_expert))
    y_rows = y_rows.reshape(n_rows, D_MODEL)
    dest = jnp.zeros((n_assign,), jnp.int32).at[order].set(dest_sorted).reshape(T, TOP_K)
    y = jnp.einsum("tk,tkd->td", gates, y_rows[dest])
    return y.reshape(*lead, D_MODEL)


def encoder_layer(x, w_in, b_in, w_short, b_short, filt_w1, filt_b1, filt_w2, filt_b2,
                  filt_w3, filt_b3, filt_w_out, filt_freq, filt_decay, hyena_skip,
                  g_fourier, g_hyena, w_out, b_out, ln1_g, ln1_b, w_router, b_router,
                  w_gate_up, b_gate_up, w_down, b_down, ln2_g, ln2_b):
    proj = x @ w_in + b_in
    y_f = fourier_group(proj[..., :FOURIER_WIDTH])
    y_h = hyena_group(proj[..., FOURIER_WIDTH:], w_short, b_short, filt_w1, filt_b1,
                      filt_w2, filt_b2, filt_w3, filt_b3, filt_w_out, filt_freq,
                      filt_decay, hyena_skip)
    mixed = jnp.concatenate([rms_norm(y_f, g_fourier), rms_norm(y_h, g_hyena)], axis=-1)
    mixed = mixed @ w_out + b_out
    x = layer_norm(DEEPNORM_ALPHA * x + mixed, ln1_g, ln1_b)
    moe_out = routed_moe(x, w_router, b_router, w_gate_up, b_gate_up, w_down, b_down)
    return layer_norm(DEEPNORM_ALPHA * x + moe_out, ln2_g, ln2_b)


def encoder_trunk(x, ln_in_g, ln_in_b, layers):
    x = layer_norm(x, ln_in_g, ln_in_b)
    for l in range(DEPTH):
        x = encoder_layer(x, **{name: arr[l] for name, arr in layers.items()})
    return x


def setup_inputs(seed: int = 0) -> dict:
    key = jax.random.key(seed)
    ks = jax.random.split(key, 32)
    f32 = jnp.float32

    def nrm(k, shape, scale):
        return jax.random.normal(k, shape, f32) * scale

    w_in = nrm(ks[4], (DEPTH, D_MODEL, IN_WIDTH), D_MODEL ** -0.5)
    w_in = w_in.at[:, :, FOURIER_WIDTH + 2 * HYENA_WIDTH:].multiply(DEEPNORM_BETA)
    max_decay = math.log(DECAY_TARGET) / FAST_DECAY_PCT
    min_decay = math.log(DECAY_TARGET) / SLOW_DECAY_PCT
    base_decay = jnp.linspace(min_decay, max_decay, HYENA_WIDTH, dtype=f32)
    return {
        "x_prompt": nrm(ks[0], (BATCH, SEQ, D_MODEL), 1.0),
        "x_sample": nrm(ks[1], (DEC_BATCH, DEC_SEQ, D_MODEL), 1.0),
        "ln_in_g": 1.0 + nrm(ks[2], (D_MODEL,), 0.02),
        "ln_in_b": nrm(ks[3], (D_MODEL,), 0.02),
        "w_in": w_in,
        "b_in": nrm(ks[5], (DEPTH, IN_WIDTH), 0.02),
        "w_short": nrm(ks[6], (DEPTH, SHORT_CONV, HYENA_IN_WIDTH), SHORT_CONV ** -0.5),
        "b_short": nrm(ks[7], (DEPTH, HYENA_IN_WIDTH), 0.02),
        "filt_w1": nrm(ks[8], (DEPTH, FILTER_EMB, FILTER_ORDER), FILTER_EMB ** -0.5),
        "filt_b1": nrm(ks[9], (DEPTH, FILTER_ORDER), 0.1),
        "filt_w2": nrm(ks[10], (DEPTH, FILTER_ORDER, FILTER_ORDER), FILTER_ORDER ** -0.5),
        "filt_b2": nrm(ks[11], (DEPTH, FILTER_ORDER), 0.1),
        "filt_w3": nrm(ks[12], (DEPTH, FILTER_ORDER, FILTER_ORDER), FILTER_ORDER ** -0.5),
        "filt_b3": nrm(ks[13], (DEPTH, FILTER_ORDER), 0.1),
        "filt_w_out": nrm(ks[14], (DEPTH, FILTER_ORDER, N_DIRECTIONS * HYENA_WIDTH), FILTER_ORDER ** -0.5),
        "filt_freq": 1.0 + nrm(ks[15], (DEPTH, FILTER_ORDER), 0.1),
        "filt_decay": base_decay[None, None, :] + nrm(ks[16], (DEPTH, N_DIRECTIONS, HYENA_WIDTH), 0.5),
        "hyena_skip": nrm(ks[17], (DEPTH, HYENA_WIDTH), 0.02),
        "g_fourier": 1.0 + nrm(ks[18], (DEPTH, FOURIER_WIDTH), 0.02),
        "g_hyena": 1.0 + nrm(ks[19], (DEPTH, HYENA_WIDTH), 0.02),
        "w_out": nrm(ks[20], (DEPTH, MIX_WIDTH, D_MODEL), MIX_WIDTH ** -0.5 * DEEPNORM_BETA),
        "b_out": nrm(ks[21], (DEPTH, D_MODEL), 0.02),
        "ln1_g": 1.0 + nrm(ks[22], (DEPTH, D_MODEL), 0.02),
        "ln1_b": nrm(ks[23], (DEPTH, D_MODEL), 0.02),
        "w_router": nrm(ks[24], (DEPTH, D_MODEL, N_EXPERTS), D_MODEL ** -0.5),
        "b_router": nrm(ks[25], (DEPTH, N_EXPERTS), 0.01),
        "w_gate_up": nrm(ks[26], (DEPTH, N_EXPERTS, D_MODEL, 2 * D_FF), D_MODEL ** -0.5),
        "b_gate_up": nrm(ks[27], (DEPTH, N_EXPERTS, 2 * D_FF), 0.02),
        "w_down": nrm(ks[28], (DEPTH, N_EXPERTS, D_FF, D_MODEL), D_FF ** -0.5 * DEEPNORM_BETA),
        "b_down": nrm(ks[29], (DEPTH, N_EXPERTS, D_MODEL), 0.02),
        "ln2_g": 1.0 + nrm(ks[30], (DEPTH, D_MODEL), 0.02),
        "ln2_b": nrm(ks[31], (DEPTH, D_MODEL), 0.02),
    }


def reference(x_prompt, x_sample, ln_in_g, ln_in_b, w_in, b_in, w_short, b_short,
              filt_w1, filt_b1, filt_w2, filt_b2, filt_w3, filt_b3, filt_w_out, filt_freq,
              filt_decay, hyena_skip, g_fourier, g_hyena, w_out, b_out, ln1_g, ln1_b,
              w_router, b_router, w_gate_up, b_gate_up, w_down, b_down, ln2_g, ln2_b):
    layers = dict(w_in=w_in, b_in=b_in, w_short=w_short, b_short=b_short,
                  filt_w1=filt_w1, filt_b1=filt_b1, filt_w2=filt_w2, filt_b2=filt_b2,
                  filt_w3=filt_w3, filt_b3=filt_b3, filt_w_out=filt_w_out,
                  filt_freq=filt_freq, filt_decay=filt_decay, hyena_skip=hyena_skip,
                  g_fourier=g_fourier, g_hyena=g_hyena, w_out=w_out, b_out=b_out,
                  ln1_g=ln1_g, ln1_b=ln1_b, w_router=w_router, b_router=b_router,
                  w_gate_up=w_gate_up, b_gate_up=b_gate_up, w_down=w_down, b_down=b_down,
                  ln2_g=ln2_g, ln2_b=ln2_b)
    y_prompt = encoder_trunk(x_prompt, ln_in_g, ln_in_b, layers)
    y_sample = encoder_trunk(x_sample, ln_in_g, ln_in_b, layers)
    return (y_prompt, y_sample)
```

```python
import functools
import math

import numpy as np
import jax
import jax.numpy as jnp
from jax import lax
from jax.experimental import pallas as pl
from jax.experimental.pallas import tpu as pltpu

F32 = jnp.float32
BF16 = jnp.bfloat16
I32 = jnp.int32

LN_EPS = 1e-5
RMS_EPS = 1e-6
SWIGLU_LIMIT = 7.0
SWIGLU_ALPHA = 1.702
TOP_K = 4
FOURIER_GROUP_DIM = 64
FILTER_BANDS = 16

LANES = 128
MINOR = 128
ROW_TILE = 512
IN_TILE = 256
TOKEN_TILE = 512
SEG_ALIGN = 16
EXPERT_BLOCK = 512
VMEM_LIMIT = 56 * 1024 * 1024


def _params(n_grid, semantics="parallel"):
    return pltpu.CompilerParams(dimension_semantics=(semantics,) * n_grid,
                                vmem_limit_bytes=VMEM_LIMIT)


def _dot(a, b):
    return jnp.dot(a, b, preferred_element_type=F32)


def _split(a):
    hi = a.astype(BF16)
    lo = (a - hi.astype(F32)).astype(BF16)
    return hi, lo


def _dot3(a, b):
    ah, al = _split(a)
    bh, bl = _split(b)
    return _dot(ah, bh) + _dot(ah, bl) + _dot(al, bh)


def _layer_norm(x, g, b):
    mu = jnp.mean(x, axis=-1, keepdims=True)
    xc = x - mu
    var = jnp.mean(xc * xc, axis=-1, keepdims=True)
    return xc * lax.rsqrt(var + LN_EPS) * g + b


def _rms_norm(x, g):
    return x * lax.rsqrt(jnp.mean(x * x, axis=-1, keepdims=True) + RMS_EPS) * g


def _in_kernel(x_ref, xp_ref, xq_ref, lng_ref, lnb_ref, win_ref, bin_ref, bd_ref, wsh_ref, bsh_ref,
               xn_ref, z_ref, x0_ref, w_ref, *, tiles_per_seq, apply_ln, fw):
    it = pl.program_id(0) % tiles_per_seq
    tm = x_ref.shape[0]

    def norm(x):
        return _layer_norm(x, lng_ref[...], lnb_ref[...]) if apply_ln else x

    xn = norm(x_ref[...])
    xn_ref[...] = xn
    halo = norm(jnp.concatenate([xp_ref[...], xq_ref[...]], axis=0))
    proj = _dot(xn.astype(BF16), win_ref[...]) + bin_ref[...]
    projh = _dot(halo.astype(BF16), win_ref[:, fw:]) + bin_ref[:, fw:]

    z = _dot(proj[:, :fw].astype(BF16), bd_ref[...])
    z_ref[0, 0] = z[:, :fw]
    z_ref[0, 1] = z[:, fw:]

    ph = proj[:, fw:]
    prev_row = jnp.where(it > 0, projh[7:8], 0.0)
    next_row = jnp.where(it < tiles_per_seq - 1, projh[8:9], 0.0)
    rows = lax.broadcasted_iota(I32, ph.shape, 0)
    up = jnp.where(rows == 0, prev_row, pltpu.roll(ph, 1, 0))
    dn = jnp.where(rows == tm - 1, next_row, pltpu.roll(ph, tm - 1, 0))
    wsh = wsh_ref[...]
    uc = up * wsh[0:1] + ph * wsh[1:2] + dn * wsh[2:3] + bsh_ref[...]
    hw = uc.shape[1] // 3
    x0_ref[...] = uc[:, :hw]
    w_ref[...] = uc[:, 2 * hw:] * uc[:, hw:2 * hw]


def _in_proj(x2d, batch, seq, lng, lnb, win_bf, b_in, bd_bf, w_short, b_short, apply_ln):
    t, d = x2d.shape
    fw = bd_bf.shape[0]
    inw = win_bf.shape[1]
    hw = (inw - fw) // 3
    tm = IN_TILE
    tps = seq // tm
    r8 = tm // 8
    nblk8 = t // 8
    const = lambda i: (0, 0)
    kern = functools.partial(_in_kernel, tiles_per_seq=tps, apply_ln=apply_ln, fw=fw)
    return pl.pallas_call(
        kern,
        grid=(t // tm,),
        in_specs=[
            pl.BlockSpec((tm, d), lambda i: (i, 0)),
            pl.BlockSpec((8, d), lambda i: (jnp.maximum(i * r8 - 1, 0), 0)),
            pl.BlockSpec((8, d), lambda i: (jnp.minimum((i + 1) * r8, nblk8 - 1), 0)),
            pl.BlockSpec((1, d), const), pl.BlockSpec((1, d), const),
            pl.BlockSpec((d, inw), const), pl.BlockSpec((1, inw), const),
            pl.BlockSpec((fw, 2 * fw), const),
            pl.BlockSpec((3, 3 * hw), const), pl.BlockSpec((1, 3 * hw), const),
        ],
        out_specs=[
            pl.BlockSpec((tm, d), lambda i: (i, 0)),
            pl.BlockSpec((1, 2, tm, fw), lambda i: (i // tps, 0, i % tps, 0)),
            pl.BlockSpec((tm, hw), lambda i: (i, 0)),
            pl.BlockSpec((tm, hw), lambda i: (i, 0)),
        ],
        out_shape=[
            jax.ShapeDtypeStruct((t, d), F32),
            jax.ShapeDtypeStruct((batch, 2, seq, fw), F32),
            jax.ShapeDtypeStruct((t, hw), F32),
            jax.ShapeDtypeStruct((t, hw), F32),
        ],
        compiler_params=_params(1),
        name="in_proj",
    )(x2d, x2d, x2d, lng, lnb, win_bf, b_in, bd_bf, w_short, b_short)


def _filter_kernel(bands_ref, w1_ref, b1_ref, w2_ref, b2_ref, w3_ref, b3_ref, wl_ref, fr_ref, dec_ref,
                   k_ref, asum_ref, *, seq):
    i = pl.program_id(0)
    tile, c = k_ref.shape
    rep = c // LANES
    r = i * tile + lax.broadcasted_iota(I32, (tile, LANES), 0)
    j = jnp.where(r < seq, r, 2 * seq - r).astype(F32)
    t = j * (1.0 / (seq - 1))
    wj = (2.0 * math.pi / seq) * j
    lane = lax.broadcasted_iota(I32, (tile, LANES), 1)
    arg = bands_ref[...] * wj
    feat = jnp.where(lane == 0, t,
                     jnp.where(lane <= FILTER_BANDS, jnp.cos(arg),
                               jnp.where(lane <= 2 * FILTER_BANDS, -jnp.sin(arg), 0.0)))
    fr = fr_ref[...]
    h = jnp.sin(fr * (_dot3(feat, w1_ref[...]) + b1_ref[...]))
    h = jnp.sin(fr * (_dot3(h, w2_ref[...]) + b2_ref[...]))
    h = jnp.sin(fr * (_dot3(h, w3_ref[...]) + b3_ref[...]))
    hl = _dot3(h, wl_ref[...])
    t4 = jnp.concatenate([t] * rep, axis=1)
    r4 = jnp.concatenate([r] * rep, axis=1)
    dec = jnp.abs(dec_ref[...])
    hf = hl[:, :c] * jnp.exp(-t4 * dec[0:1])
    hb = hl[:, c:] * jnp.exp(-t4 * dec[1:2])
    fwd = r4 < seq
    k = jnp.where(fwd, hf, hb)
    k = jnp.where(r4 == 0, hf + hb, k)
    k = jnp.where(r4 == seq, 0.0, k)
    a = jnp.where(fwd, jnp.abs(hf), jnp.abs(hb))
    a = jnp.where(r4 == 0, jnp.abs(hf) + jnp.abs(hb), a)
    a = jnp.where(r4 == seq, 0.0, a)
    k_ref[...] = k

    @pl.when(i == 0)
    def _():
        asum_ref[...] = jnp.zeros_like(asum_ref)

    asum_ref[...] += jnp.sum(a, axis=0, keepdims=True)


def _hyena_filter(seq, bands_row, w1p, b1, w2, b2, w3, b3, wl, freq, decay):
    c = decay.shape[1]
    tile = ROW_TILE
    n = 2 * seq
    full = lambda a: pl.BlockSpec(a.shape, lambda i: (0,) * a.ndim)
    args = (bands_row, w1p, b1, w2, b2, w3, b3, wl, freq, decay)
    return pl.pallas_call(
        functools.partial(_filter_kernel, seq=seq),
        grid=(n // tile,),
        in_specs=[full(a) for a in args],
        out_specs=[pl.BlockSpec((tile, c), lambda i: (i, 0)), pl.BlockSpec((1, c), lambda i: (0, 0))],
        out_shape=[jax.ShapeDtypeStruct((n, c), F32), jax.ShapeDtypeStruct((1, c), F32)],
        compiler_params=_params(1, "arbitrary"),
        name="hyena_filter",
    )(*args)


def _major_kernel(g_ref, x_ref, o_ref):
    g = g_ref[...]
    for b in range(x_ref.shape[2]):
        o_ref[0, :, b, :] = _dot(g, x_ref[0, :, b, :].astype(BF16))


def _dft_major(g_bf, x4):
    b, k, n2, c = x4.shape
    m = g_bf.shape[0]
    bb = 8
    return pl.pallas_call(
        _major_kernel,
        grid=(b, n2 // bb),
        in_specs=[pl.BlockSpec((m, k), lambda i, j: (0, 0)),
                  pl.BlockSpec((1, k, bb, c), lambda i, j: (i, 0, j, 0))],
        out_specs=pl.BlockSpec((1, m, bb, c), lambda i, j: (i, 0, j, 0)),
        out_shape=jax.ShapeDtypeStruct((b, m, n2, c), F32),
        compiler_params=_params(2),
        name="dft_major",
    )(g_bf, x4)


def _minor_kernel(*refs, mode, scale, k1b):
    if mode == "conv":
        y_ref, tw_ref, wf_ref, wi_ref, kf_ref, o_ref = refs
    else:
        y_ref, tw_ref, wf_ref, o_ref = refs
    n2 = y_ref.shape[3]
    ct = y_ref.shape[4]
    rep = ct // LANES
    for j in range(k1b):
        yr = y_ref[0, 0, j]
        yi = y_ref[0, 1, j]
        cs = jnp.concatenate([tw_ref[0, j]] * rep, axis=1)
        sn = jnp.concatenate([tw_ref[1, j]] * rep, axis=1)
        a = jnp.concatenate([yr * cs + yi * sn, yi * cs - yr * sn], axis=0).astype(BF16)
        if mode == "real":
            o_ref[0, :, j, :] = _dot(wf_ref[:n2], a) * scale
            continue
        z = _dot(wf_ref[...], a)
        zr, zi = z[:n2], z[n2:]
        if mode == "fwd":
            o_ref[0, 0, j] = zr * scale
            o_ref[0, 1, j] = zi * scale
            continue
        kr = kf_ref[0, 0, j]
        ki = kf_ref[0, 1, j]
        p = jnp.concatenate([zr * kr - zi * ki, zr * ki + zi * kr], axis=0).astype(BF16)
        q = _dot(wi_ref[...], p)
        qr, qi = q[:n2], q[n2:]
        o_ref[0, 0, j] = qr * cs - qi * sn
        o_ref[0, 1, j] = qi * cs + qr * sn


def _dft_minor(y5, tw, wf_bf, mode, scale=1.0, wi_bf=None, kf5=None):
    b, _, n1, n2, c = y5.shape
    k1b = min(8, n1)
    ct = c
    grid = (b, n1 // k1b)
    dspec = pl.BlockSpec((1, 2, k1b, n2, ct), lambda i, j: (i, 0, j, 0, 0))
    in_specs = [dspec,
                pl.BlockSpec((2, k1b, n2, LANES), lambda i, j: (0, j, 0, 0)),
                pl.BlockSpec(wf_bf.shape, lambda i, j: (0, 0))]
    args = [y5, tw, wf_bf]
    if mode == "conv":
        in_specs += [pl.BlockSpec(wi_bf.shape, lambda i, j: (0, 0)),
                     pl.BlockSpec((1, 2, k1b, n2, ct), lambda i, j: (0, 0, j, 0, 0))]
        args += [wi_bf, kf5]
    if mode == "real":
        out_spec = pl.BlockSpec((1, n2, k1b, ct), lambda i, j: (i, 0, j, 0))
        out_shape = jax.ShapeDtypeStruct((b, n2, n1, c), F32)
    else:
        out_spec = dspec
        out_shape = jax.ShapeDtypeStruct((b, 2, n1, n2, c), F32)
    return pl.pallas_call(
        functools.partial(_minor_kernel, mode=mode, scale=scale, k1b=k1b),
        grid=grid, in_specs=in_specs, out_specs=out_spec, out_shape=out_shape,
        compiler_params=_params(2),
        name="dft_minor_" + mode,
    )(*args)


def _cs(num, den):
    ang = 2.0 * np.pi * (num % den) / den
    return np.cos(ang), np.sin(ang)


def _complex_block(c, s, sign):
    return np.block([[c, -sign * s], [sign * s, c]]).astype(np.float32)


def _major_matrix(n1, a_in, a_out, sign, real_in=False):
    c, s = _cs(np.outer(np.arange(a_out), np.arange(a_in)), n1)
    g = _complex_block(c, s, sign)
    return g[:, :a_in] if real_in else g


def _minor_matrix(n2, sign):
    c, s = _cs(np.outer(np.arange(n2), np.arange(n2)), n2)
    return _complex_block(c, s, sign)


def _twiddles(n1, n2):
    c, s = _cs(np.outer(np.arange(n1), np.arange(n2)), n1 * n2)
    tw = jnp.asarray(np.stack([c, s]).astype(np.float32))
    return jnp.broadcast_to(tw[..., None], (2, n1, n2, LANES))


def _out_kernel(yf_ref, yc_ref, x0_ref, w_ref, xn_ref, inv_ref, skip_ref, gf_ref, gh_ref, wo_ref, bo_ref,
                g1_ref, b1_ref, x1_ref, *, alpha):
    c = yc_ref.shape[1]
    w = w_ref[...]
    yh = x0_ref[...] * (yc_ref[...] * inv_ref[...] + skip_ref[...] * w)
    mf = _rms_norm(yf_ref[...], gf_ref[...]).astype(BF16)
    mh = _rms_norm(yh, gh_ref[...]).astype(BF16)
    m = _dot(mf, wo_ref[:c]) + _dot(mh, wo_ref[c:]) + bo_ref[...]
    x1_ref[...] = _layer_norm(alpha * xn_ref[...] + m, g1_ref[...], b1_ref[...])


def _out_proj(yf, yc, x0, w, xn, inv_norm, skip, gf, gh, wo_bf, bo, g1, b1, alpha):
    t, d = xn.shape
    c = yc.shape[1]
    tm = ROW_TILE
    row = lambda i: (i, 0)
    const = lambda i: (0, 0)
    return pl.pallas_call(
        functools.partial(_out_kernel, alpha=alpha),
        grid=(t // tm,),
        in_specs=[
            pl.BlockSpec((tm, c), row),
            pl.BlockSpec((tm, c), row), pl.BlockSpec((tm, c), row), pl.BlockSpec((tm, c), row),
            pl.BlockSpec((tm, d), row),
            pl.BlockSpec((1, c), const), pl.BlockSpec((1, c), const),
            pl.BlockSpec((1, c), const), pl.BlockSpec((1, c), const),
            pl.BlockSpec(wo_bf.shape, const), pl.BlockSpec((1, d), const),
            pl.BlockSpec((1, d), const), pl.BlockSpec((1, d), const),
        ],
        out_specs=pl.BlockSpec((tm, d), row),
        out_shape=jax.ShapeDtypeStruct((t, d), F32),
        compiler_params=_params(1),
        name="out_proj",
    )(yf, yc, x0, w, xn, inv_norm, skip, gf, gh, wo_bf, bo, g1, b1)


def _route_kernel(x_ref, wr_ref, br_ref, idx_ref, gate_ref, cnt_ref, *, top_k):
    tt = x_ref.shape[0]
    e = wr_ref.shape[0]
    nt = (((1,), (1,)), ((), ()))
    xh, xl = _split(x_ref[...])
    wh, wl = _split(wr_ref[...])
    logits = (lax.dot_general(wh, xh, nt, preferred_element_type=F32)
              + lax.dot_general(wh, xl, nt, preferred_element_type=F32)
              + lax.dot_general(wl, xh, nt, preferred_element_type=F32)) + br_ref[...]
    iota_e = lax.broadcasted_iota(I32, (e, tt), 0).astype(F32)
    l = logits
    tops, idxs, hots = [], [], []
    for _ in range(top_k):
        m = jnp.max(l, axis=0, keepdims=True)
        idx = jnp.min(jnp.where(l == m, iota_e, float(e)), axis=0, keepdims=True)
        hot = iota_e == idx
        l = jnp.where(hot, -jnp.inf, l)
        tops.append(m)
        idxs.append(idx)
        hots.append(hot)
    ex = [jnp.exp(m - tops[0]) for m in tops]
    den = ex[0]
    for v in ex[1:]:
        den = den + v
    gates = [v / den for v in ex]
    hot_all = jnp.zeros((e, tt), F32)
    for hot in hots:
        hot_all = jnp.where(hot, 1.0, hot_all)
    upper = (lax.broadcasted_iota(I32, (tt, tt), 0) < lax.broadcasted_iota(I32, (tt, tt), 1))
    before = _dot(hot_all.astype(BF16), jnp.where(upper, 1.0, 0.0).astype(BF16))
    ranks = [jnp.sum(jnp.where(hot, before, 0.0), axis=0, keepdims=True).astype(I32) for hot in hots]
    pad = 8 - top_k
    idx_ref[0] = jnp.concatenate([v.astype(I32) for v in idxs] + ranks, axis=0)
    gate_ref[0] = jnp.concatenate(gates + [jnp.zeros((pad, tt), F32)], axis=0)
    cnt_ref[0] = jnp.sum(hot_all, axis=1, keepdims=True).astype(I32)


def _route(x1, wr_t, br_col):
    t, d = x1.shape
    e = wr_t.shape[0]
    tt = TOKEN_TILE
    nt = t // tt
    return pl.pallas_call(
        functools.partial(_route_kernel, top_k=TOP_K),
        grid=(nt,),
        in_specs=[pl.BlockSpec((tt, d), lambda i: (i, 0)),
                  pl.BlockSpec((e, d), lambda i: (0, 0)),
                  pl.BlockSpec((e, 1), lambda i: (0, 0))],
        out_specs=[pl.BlockSpec((1, 2 * TOP_K, tt), lambda i: (i, 0, 0)),
                   pl.BlockSpec((1, 8, tt), lambda i: (i, 0, 0)),
                   pl.BlockSpec((1, e, 1), lambda i: (i, 0, 0))],
        out_shape=[jax.ShapeDtypeStruct((nt, 2 * TOP_K, tt), I32),
                   jax.ShapeDtypeStruct((nt, 8, tt), F32),
                   jax.ShapeDtypeStruct((nt, e, 1), I32)],
        compiler_params=_params(1),
        name="moe_route",
    )(x1, wr_t, br_col)


SEG_PIECES = int(math.log2(TOKEN_TILE // SEG_ALIGN)) + 1


def _segment_copies(j, n_experts, loff_s, seg_s, goff_s, local, remote, sem, to_remote, wait):
    def body(e, carry):
        s = j * n_experts + e
        n = seg_s[s] // SEG_ALIGN
        lo = loff_s[s]
        go = goff_s[s]
        for p in reversed(range(SEG_PIECES)):
            size = SEG_ALIGN << p
            start = ((n >> (p + 1)) << (p + 1)) * SEG_ALIGN

            @pl.when(((n >> p) & 1) == 1)
            def _():
                lref = local.at[pl.ds(pl.multiple_of(lo + start, SEG_ALIGN), size)]
                rref = remote.at[pl.ds(pl.multiple_of(go + start, SEG_ALIGN), size)]
                cp = (pltpu.make_async_copy(lref, rref, sem) if to_remote
                      else pltpu.make_async_copy(rref, lref, sem))
                if wait:
                    cp.wait()
                else:
                    cp.start()
        return carry

    lax.fori_loop(0, n_experts, body, 0)


def _dispatch_kernel(loff_s, seg_s, goff_s, x_ref, idx_ref, init_ref, xs_ref, pos_ref, buf, sem,
                     *, n_experts, top_k):
    del init_ref
    j = pl.program_id(0)
    tt = x_ref.shape[0]
    lr = buf.shape[0]
    idx = idx_ref[0]
    tope = idx[:top_k]
    base = jnp.zeros_like(tope)
    for e in range(n_experts):
        base = jnp.where(tope == e, loff_s[j * n_experts + e], base)
    pos = base + idx[top_k:]
    pos_ref[0] = jnp.concatenate([pos, jnp.zeros((8 - top_k, tt), I32)], axis=0)
    rows = lax.broadcasted_iota(I32, (lr, tt), 0)
    perm = jnp.zeros((lr, tt), F32)
    for k in range(top_k):
        perm = jnp.where(rows == pos[k:k + 1], 1.0, perm)
    buf[...] = _dot(perm.astype(BF16), x_ref[...].astype(BF16)).astype(BF16)
    _segment_copies(j, n_experts, loff_s, seg_s, goff_s, buf, xs_ref, sem, True, False)
    _segment_copies(j, n_experts, loff_s, seg_s, goff_s, buf, xs_ref, sem, True, True)


def _dispatch(x1, idx, loff, seg, goff, n_rows, local_rows):
    t, d = x1.shape
    tt = TOKEN_TILE
    nt = t // tt
    e = loff.shape[0] // nt
    init = jnp.zeros((n_rows, d), BF16)
    grid_spec = pltpu.PrefetchScalarGridSpec(
        num_scalar_prefetch=3,
        grid=(nt,),
        in_specs=[pl.BlockSpec((tt, d), lambda i, *_: (i, 0)),
                  pl.BlockSpec((1, 2 * TOP_K, tt), lambda i, *_: (i, 0, 0)),
                  pl.BlockSpec(memory_space=pl.ANY)],
        out_specs=[pl.BlockSpec(memory_space=pl.ANY),
                   pl.BlockSpec((1, 8, tt), lambda i, *_: (i, 0, 0))],
        scratch_shapes=[pltpu.VMEM((local_rows, d), BF16), pltpu.SemaphoreType.DMA(())],
    )
    return pl.pallas_call(
        functools.partial(_dispatch_kernel, n_experts=e, top_k=TOP_K),
        grid_spec=grid_spec,
        out_shape=[jax.ShapeDtypeStruct((n_rows, d), BF16), jax.ShapeDtypeStruct((nt, 8, tt), I32)],
        input_output_aliases={5: 0},
        compiler_params=_params(1, "arbitrary"),
        name="moe_dispatch",
    )(loff, seg, goff, x1, idx, init)


def _combine_kernel(loff_s, seg_s, goff_s, ys_ref, pos_ref, gate_ref, x1_ref, g2_ref, b2_ref, o_ref,
                    buf, sem, *, n_experts, top_k, alpha):
    j = pl.program_id(0)
    tt = x1_ref.shape[0]
    lr = buf.shape[0]
    buf[...] = jnp.zeros_like(buf)
    _segment_copies(j, n_experts, loff_s, seg_s, goff_s, buf, ys_ref, sem, False, False)
    _segment_copies(j, n_experts, loff_s, seg_s, goff_s, buf, ys_ref, sem, False, True)
    pos = pos_ref[...]
    gate = gate_ref[...]
    cols = lax.broadcasted_iota(I32, (tt, lr), 1)
    wsel = jnp.zeros((tt, lr), F32)
    for k in range(top_k):
        wsel = jnp.where(cols == pos[:, k:k + 1], gate[:, k:k + 1], wsel)
    moe = _dot(wsel.astype(BF16), buf[...])
    o_ref[...] = _layer_norm(alpha * x1_ref[...] + moe, g2_ref[...], b2_ref[...])


def _combine(ys, pos_tm, gate_tm, x1, g2, b2, loff, seg, goff, local_rows, alpha):
    t, d = x1.shape
    tt = TOKEN_TILE
    nt = t // tt
    e = loff.shape[0] // nt
    grid_spec = pltpu.PrefetchScalarGridSpec(
        num_scalar_prefetch=3,
        grid=(nt,),
        in_specs=[pl.BlockSpec(memory_space=pl.ANY),
                  pl.BlockSpec((tt, 8), lambda i, *_: (i, 0)),
                  pl.BlockSpec((tt, 8), lambda i, *_: (i, 0)),
                  pl.BlockSpec((tt, d), lambda i, *_: (i, 0)),
                  pl.BlockSpec((1, d), lambda i, *_: (0, 0)),
                  pl.BlockSpec((1, d), lambda i, *_: (0, 0))],
        out_specs=pl.BlockSpec((tt, d), lambda i, *_: (i, 0)),
        scratch_shapes=[pltpu.VMEM((local_rows, d), BF16), pltpu.SemaphoreType.DMA(())],
    )
    return pl.pallas_call(
        functools.partial(_combine_kernel, n_experts=e, top_k=TOP_K, alpha=alpha),
        grid_spec=grid_spec,
        out_shape=jax.ShapeDtypeStruct((t, d), F32),
        compiler_params=_params(1, "arbitrary"),
        name="moe_combine",
    )(loff, seg, goff, ys, pos_tm, gate_tm, x1, g2, b2)


def _expert_kernel(be_s, nb_s, xs_ref, wgu_ref, bgu_ref, wd_ref, bd_ref, ys_ref):
    @pl.when(pl.program_id(0) < nb_s[0])
    def _():
        dff = wd_ref.shape[1]
        gu = _dot(xs_ref[...], wgu_ref[0]) + bgu_ref[0]
        gate = jnp.minimum(gu[:, :dff], SWIGLU_LIMIT)
        up = jnp.clip(gu[:, dff:], -SWIGLU_LIMIT, SWIGLU_LIMIT)
        h = (up + 1.0) * (gate * jax.nn.sigmoid(SWIGLU_ALPHA * gate))
        ys_ref[...] = (_dot(h.astype(BF16), wd_ref[0]) + bd_ref[0]).astype(BF16)

    @pl.when(pl.program_id(0) >= nb_s[0])
    def _():
        ys_ref[...] = jnp.zeros_like(ys_ref)


def _experts(xs, block_expert, n_blocks, wgu_bf, bgu, wd_bf, bdn):
    n_rows, d = xs.shape
    bm = EXPERT_BLOCK
    f2 = wgu_bf.shape[2]
    dff = wd_bf.shape[1]
    blk = lambda b, be, nb: (jnp.minimum(b, nb[0] - 1), 0)
    wsel = lambda b, be, nb: (be[b], 0, 0)
    grid_spec = pltpu.PrefetchScalarGridSpec(
        num_scalar_prefetch=2,
        grid=(n_rows // bm,),
        in_specs=[pl.BlockSpec((bm, d), blk),
                  pl.BlockSpec((1, d, f2), wsel), pl.BlockSpec((1, 1, f2), wsel),
                  pl.BlockSpec((1, dff, d), wsel), pl.BlockSpec((1, 1, d), wsel)],
        out_specs=pl.BlockSpec((bm, d), lambda b, be, nb: (b, 0)),
    )
    return pl.pallas_call(
        _expert_kernel,
        grid_spec=grid_spec,
        out_shape=jax.ShapeDtypeStruct((n_rows, d), BF16),
        compiler_params=_params(1, "arbitrary"),
        name="moe_experts",
    )(block_expert, n_blocks, xs, wgu_bf, bgu, wd_bf, bdn)


def _routed_moe(x1, w_router, b_router, w_gate_up, b_gate_up, w_down, b_down, g2, b2, alpha):
    t, d = x1.shape
    e = w_router.shape[1]
    tt = TOKEN_TILE
    nt = t // tt
    bm = EXPERT_BLOCK
    idx, gate, cnt = _route(x1, w_router.T, b_router.reshape(e, 1))

    cnt = cnt.reshape(nt, e)
    seg = (cnt + SEG_ALIGN - 1) // SEG_ALIGN * SEG_ALIGN
    loff = jnp.cumsum(seg, axis=1) - seg
    per_expert = jnp.sum(seg, axis=0)
    padded = (per_expert + bm - 1) // bm * bm
    blocks_end = jnp.cumsum(padded) // bm
    start = jnp.cumsum(padded) - padded
    goff = start[None, :] + jnp.cumsum(seg, axis=0) - seg
    local_rows = TOP_K * tt + e * SEG_ALIGN
    n_rows_max = -(-(TOP_K * t + (SEG_ALIGN - 1) * e * nt) // bm) * bm + e * bm
    n_blocks = blocks_end[-1:].astype(I32)
    bidx = jnp.minimum(jnp.arange(n_rows_max // bm, dtype=I32), n_blocks[0] - 1)
    block_expert = jnp.minimum(jnp.searchsorted(blocks_end, bidx, side="right"), e - 1).astype(I32)
    flat = lambda a: a.reshape(-1).astype(I32)

    xs, pos = _dispatch(x1, idx, flat(loff), flat(seg), flat(goff), n_rows_max, local_rows)
    ys = _experts(xs, block_expert, n_blocks, w_gate_up.astype(BF16), b_gate_up.reshape(e, 1, -1),
                  w_down.astype(BF16), b_down.reshape(e, 1, -1))
    pos_tm = jnp.swapaxes(pos, 1, 2).reshape(t, 8)
    gate_tm = jnp.swapaxes(gate, 1, 2).reshape(t, 8)
    return _combine(ys, pos_tm, gate_tm, x1, g2, b2, flat(loff), flat(seg), flat(goff), local_rows, alpha)


def _mixer(x, apply_ln, lng, lnb, lay, consts, alpha):
    batch, seq, d = x.shape
    t = batch * seq
    fw = consts["bd"].shape[0]
    n2 = MINOR
    xn, z, x0, w = _in_proj(x.reshape(t, d), batch, seq, lng, lnb, lay["w_in"], lay["b_in"], consts["bd"],
                            lay["w_short"], lay["b_short"], apply_ln)
    c = x0.shape[1]

    n1 = seq // n2
    g = jnp.asarray(_major_matrix(n1, n1, n1, -1)).astype(BF16)
    y = _dft_major(g, z.reshape(batch, 2 * n1, n2, fw))
    yf = _dft_minor(y.reshape(batch, 2, n1, n2, fw), _twiddles(n1, n2), consts["wf"], "real",
                    scale=1.0 / math.sqrt(seq * FOURIER_GROUP_DIM))

    n = 2 * seq
    m1 = n // n2
    tw = _twiddles(m1, n2)
    kf, asum = _hyena_filter(seq, consts["bands"], lay["filt_w1"], lay["filt_b1"], lay["filt_w2"],
                             lay["filt_b2"], lay["filt_w3"], lay["filt_b3"], lay["filt_w_out"],
                             lay["filt_freq"], lay["filt_decay"])
    gk = jnp.asarray(_major_matrix(m1, m1, m1, -1, real_in=True)).astype(BF16)
    ky = _dft_major(gk, kf.reshape(1, m1, n2, c))
    kspec = _dft_minor(ky.reshape(1, 2, m1, n2, c), tw, consts["wf"], "fwd", scale=1.0 / n)
    gfw = jnp.asarray(_major_matrix(m1, m1 // 2, m1, -1)).astype(BF16)
    gin = jnp.asarray(_major_matrix(m1, m1, m1 // 2, +1)).astype(BF16)
    pairs = batch // 2
    u = _dft_major(gfw, w.reshape(pairs, m1, n2, c))
    v = _dft_minor(u.reshape(pairs, 2, m1, n2, c), tw, consts["wf"], "conv", wi_bf=consts["wi"], kf5=kspec)
    yc = _dft_major(gin, v.reshape(pairs, 2 * m1, n2, c))

    inv_norm = 1.0 / asum
    return _out_proj(yf.reshape(t, fw), yc.reshape(t, c), x0, w, xn, inv_norm,
                     lay["hyena_skip"], lay["g_fourier"], lay["g_hyena"], lay["w_out"], lay["b_out"],
                     lay["ln1_g"], lay["ln1_b"], alpha)


def kernel(x_prompt, x_sample, ln_in_g, ln_in_b, w_in, b_in, w_short, b_short, filt_w1, filt_b1, filt_w2, filt_b2, filt_w3, filt_b3, filt_w_out, filt_freq, filt_decay, hyena_skip, g_fourier, g_hyena, w_out, b_out, ln1_g, ln1_b, w_router, b_router, w_gate_up, b_gate_up, w_down, b_down, ln2_g, ln2_b):
    depth, d, in_width = w_in.shape
    fw = g_fourier.shape[1]
    alpha = (2.0 * depth) ** 0.25
    row = lambda a: a.reshape(1, -1)

    gd = FOURIER_GROUP_DIM
    cg, sg = _cs(np.outer(np.arange(gd), np.arange(gd)), gd)
    eye = np.eye(fw // gd)
    bd = np.concatenate([np.kron(eye, cg), -np.kron(eye, sg)], axis=1).astype(np.float32)
    bands = jnp.linspace(1e-4, FILTER_BANDS - 1, FILTER_BANDS, dtype=F32)
    bands_row = jnp.zeros((1, LANES), F32).at[0, 1:1 + FILTER_BANDS].set(bands)
    bands_row = bands_row.at[0, 1 + FILTER_BANDS:1 + 2 * FILTER_BANDS].set(bands)
    consts = dict(bd=jnp.asarray(bd).astype(BF16), bands=bands_row,
                  wf=jnp.asarray(_minor_matrix(MINOR, -1)).astype(BF16),
                  wi=jnp.asarray(_minor_matrix(MINOR, +1)).astype(BF16))

    xs = [x_prompt, x_sample]
    for l in range(depth):
        emb = filt_w1.shape[1]
        lay = dict(
            w_in=w_in[l].astype(BF16), b_in=row(b_in[l]), w_short=w_short[l], b_short=row(b_short[l]),
            filt_w1=jnp.zeros((LANES, filt_w1.shape[2]), F32).at[:emb].set(filt_w1[l]),
            filt_b1=row(filt_b1[l]), filt_w2=filt_w2[l], filt_b2=row(filt_b2[l]), filt_w3=filt_w3[l],
            filt_b3=row(filt_b3[l]), filt_w_out=filt_w_out[l], filt_freq=row(filt_freq[l]),
            filt_decay=filt_decay[l], hyena_skip=row(hyena_skip[l]), g_fourier=row(g_fourier[l]),
            g_hyena=row(g_hyena[l]), w_out=w_out[l].astype(BF16), b_out=row(b_out[l]),
            ln1_g=row(ln1_g[l]), ln1_b=row(ln1_b[l]))
        mixed = [_mixer(x, l == 0, row(ln_in_g), row(ln_in_b), lay, consts, alpha) for x in xs]
        sizes = [m.shape[0] for m in mixed]
        out = _routed_moe(jnp.concatenate(mixed, axis=0), w_router[l], b_router[l], w_gate_up[l],
                          b_gate_up[l], w_down[l], b_down[l], row(ln2_g[l]), row(ln2_b[l]), alpha)
        xs = [out[:sizes[0]].reshape(xs[0].shape), out[sizes[0]:].reshape(xs[1].shape)]
    return (xs[0], xs[1])
```

```python
import functools
import math

import numpy as np
import jax
import jax.numpy as jnp
from jax import lax
from jax.experimental import pallas as pl
from jax.experimental.pallas import tpu as pltpu

F32 = jnp.float32
BF16 = jnp.bfloat16
I32 = jnp.int32

LN_EPS = 1e-5
RMS_EPS = 1e-6
SWIGLU_LIMIT = 7.0
SWIGLU_ALPHA = 1.702
TOP_K = 4
FOURIER_GROUP_DIM = 64
FILTER_BANDS = 16

LANES = 128
MINOR = 128
ROW_TILE = 512
IN_TILE = 256
MAJOR_BLOCK_BYTES = 3 * 1024 * 1024
TOKEN_TILE = 512
SEG_ALIGN = 16
EXPERT_BLOCK = 512
VMEM_LIMIT = 56 * 1024 * 1024


def _params(n_grid, semantics="parallel"):
    return pltpu.CompilerParams(dimension_semantics=(semantics,) * n_grid,
                                vmem_limit_bytes=VMEM_LIMIT)


def _dot(a, b):
    return jnp.dot(a, b, preferred_element_type=F32)


def _split(a):
    hi = a.astype(BF16)
    lo = (a - hi.astype(F32)).astype(BF16)
    return hi, lo


def _dot3(a, b):
    ah, al = _split(a)
    bh, bl = _split(b)
    return _dot(ah, bh) + _dot(ah, bl) + _dot(al, bh)


def _layer_norm(x, g, b):
    mu = jnp.mean(x, axis=-1, keepdims=True)
    xc = x - mu
    var = jnp.mean(xc * xc, axis=-1, keepdims=True)
    return xc * lax.rsqrt(var + LN_EPS) * g + b


def _rms_norm(x, g):
    return x * lax.rsqrt(jnp.mean(x * x, axis=-1, keepdims=True) + RMS_EPS) * g


def _in_kernel(x_ref, xp_ref, xq_ref, lng_ref, lnb_ref, win_ref, bin_ref, bd_ref, wsh_ref, bsh_ref,
               xn_ref, z_ref, x0_ref, w_ref, *, tiles_per_seq, apply_ln, fw):
    it = pl.program_id(0) % tiles_per_seq
    tm = x_ref.shape[0]

    def norm(x):
        return _layer_norm(x, lng_ref[...], lnb_ref[...]) if apply_ln else x

    xn = norm(x_ref[...])
    xn_ref[...] = xn
    halo = norm(jnp.concatenate([xp_ref[...], xq_ref[...]], axis=0))
    proj = _dot(xn.astype(BF16), win_ref[...]) + bin_ref[...]
    projh = _dot(halo.astype(BF16), win_ref[:, fw:]) + bin_ref[:, fw:]

    z = _dot(proj[:, :fw].astype(BF16), bd_ref[...])
    z_ref[0, 0] = z[:, :fw]
    z_ref[0, 1] = z[:, fw:]

    ph = proj[:, fw:]
    prev_row = jnp.where(it > 0, projh[7:8], 0.0)
    next_row = jnp.where(it < tiles_per_seq - 1, projh[8:9], 0.0)
    rows = lax.broadcasted_iota(I32, ph.shape, 0)
    up = jnp.where(rows == 0, prev_row, pltpu.roll(ph, 1, 0))
    dn = jnp.where(rows == tm - 1, next_row, pltpu.roll(ph, tm - 1, 0))
    wsh = wsh_ref[...]
    uc = up * wsh[0:1] + ph * wsh[1:2] + dn * wsh[2:3] + bsh_ref[...]
    hw = uc.shape[1] // 3
    x0_ref[...] = uc[:, :hw]
    w_ref[...] = uc[:, 2 * hw:] * uc[:, hw:2 * hw]


def _in_proj(x2d, batch, seq, lng, lnb, win_bf, b_in, bd_bf, w_short, b_short, apply_ln):
    t, d = x2d.shape
    fw = bd_bf.shape[0]
    inw = win_bf.shape[1]
    hw = (inw - fw) // 3
    tm = IN_TILE
    tps = seq // tm
    r8 = tm // 8
    nblk8 = t // 8
    const = lambda i: (0, 0)
    kern = functools.partial(_in_kernel, tiles_per_seq=tps, apply_ln=apply_ln, fw=fw)
    return pl.pallas_call(
        kern,
        grid=(t // tm,),
        in_specs=[
            pl.BlockSpec((tm, d), lambda i: (i, 0)),
            pl.BlockSpec((8, d), lambda i: (jnp.maximum(i * r8 - 1, 0), 0)),
            pl.BlockSpec((8, d), lambda i: (jnp.minimum((i + 1) * r8, nblk8 - 1), 0)),
            pl.BlockSpec((1, d), const), pl.BlockSpec((1, d), const),
            pl.BlockSpec((d, inw), const), pl.BlockSpec((1, inw), const),
            pl.BlockSpec((fw, 2 * fw), const),
            pl.BlockSpec((3, 3 * hw), const), pl.BlockSpec((1, 3 * hw), const),
        ],
        out_specs=[
            pl.BlockSpec((tm, d), lambda i: (i, 0)),
            pl.BlockSpec((1, 2, tm, fw), lambda i: (i // tps, 0, i % tps, 0)),
            pl.BlockSpec((tm, hw), lambda i: (i, 0)),
            pl.BlockSpec((tm, hw), lambda i: (i, 0)),
        ],
        out_shape=[
            jax.ShapeDtypeStruct((t, d), F32),
            jax.ShapeDtypeStruct((batch, 2, seq, fw), F32),
            jax.ShapeDtypeStruct((t, hw), F32),
            jax.ShapeDtypeStruct((t, hw), F32),
        ],
        compiler_params=_params(1),
        name="in_proj",
    )(x2d, x2d, x2d, lng, lnb, win_bf, b_in, bd_bf, w_short, b_short)


def _filter_kernel(bands_ref, w1_ref, b1_ref, w2_ref, b2_ref, w3_ref, b3_ref, wl_ref, fr_ref, dec_ref,
                   k_ref, asum_ref, *, seq):
    i = pl.program_id(0)
    tile, c = k_ref.shape
    rep = c // LANES
    r = i * tile + lax.broadcasted_iota(I32, (tile, LANES), 0)
    j = jnp.where(r < seq, r, 2 * seq - r).astype(F32)
    t = j * (1.0 / (seq - 1))
    wj = (2.0 * math.pi / seq) * j
    lane = lax.broadcasted_iota(I32, (tile, LANES), 1)
    arg = bands_ref[...] * wj
    feat = jnp.where(lane == 0, t,
                     jnp.where(lane <= FILTER_BANDS, jnp.cos(arg),
                               jnp.where(lane <= 2 * FILTER_BANDS, -jnp.sin(arg), 0.0)))
    fr = fr_ref[...]
    h = jnp.sin(fr * (_dot3(feat, w1_ref[...]) + b1_ref[...]))
    h = jnp.sin(fr * (_dot3(h, w2_ref[...]) + b2_ref[...]))
    h = jnp.sin(fr * (_dot3(h, w3_ref[...]) + b3_ref[...]))
    hl = _dot3(h, wl_ref[...])
    t4 = jnp.concatenate([t] * rep, axis=1)
    r4 = jnp.concatenate([r] * rep, axis=1)
    dec = jnp.abs(dec_ref[...])
    hf = hl[:, :c] * jnp.exp(-t4 * dec[0:1])
    hb = hl[:, c:] * jnp.exp(-t4 * dec[1:2])
    fwd = r4 < seq
    k = jnp.where(fwd, hf, hb)
    k = jnp.where(r4 == 0, hf + hb, k)
    k = jnp.where(r4 == seq, 0.0, k)
    a = jnp.where(fwd, jnp.abs(hf), jnp.abs(hb))
    a = jnp.where(r4 == 0, jnp.abs(hf) + jnp.abs(hb), a)
    a = jnp.where(r4 == seq, 0.0, a)
    k_ref[...] = k

    @pl.when(i == 0)
    def _():
        asum_ref[...] = jnp.zeros_like(asum_ref)

    asum_ref[...] += jnp.sum(a, axis=0, keepdims=True)


def _hyena_filter(seq, bands_row, w1p, b1, w2, b2, w3, b3, wl, freq, decay):
    c = decay.shape[1]
    tile = ROW_TILE
    n = 2 * seq
    full = lambda a: pl.BlockSpec(a.shape, lambda i: (0,) * a.ndim)
    args = (bands_row, w1p, b1, w2, b2, w3, b3, wl, freq, decay)
    return pl.pallas_call(
        functools.partial(_filter_kernel, seq=seq),
        grid=(n // tile,),
        in_specs=[full(a) for a in args],
        out_specs=[pl.BlockSpec((tile, c), lambda i: (i, 0)), pl.BlockSpec((1, c), lambda i: (0, 0))],
        out_shape=[jax.ShapeDtypeStruct((n, c), F32), jax.ShapeDtypeStruct((1, c), F32)],
        compiler_params=_params(1, "arbitrary"),
        name="hyena_filter",
    )(*args)


def _major_kernel(g_ref, x_ref, o_ref):
    g = g_ref[...]
    bsz, k, sub, lanes = x_ref.shape
    m = o_ref.shape[1]
    x2 = x_ref.reshape(bsz * k * sub, lanes)
    o2 = o_ref.reshape(bsz * m * sub, lanes)
    for i in range(bsz):
        for b in range(sub):
            xb = x2[pl.ds(i * k * sub + b, k, stride=sub), :]
            o2[pl.ds(i * m * sub + b, m, stride=sub), :] = _dot(g, xb.astype(BF16))


def _dft_major(g_bf, x4):
    b, k, n2, c = x4.shape
    m = g_bf.shape[0]
    sub = 8
    bsz = max(1, min(b, MAJOR_BLOCK_BYTES // ((k + m) * sub * LANES * 4)))
    while b % bsz:
        bsz -= 1
    return pl.pallas_call(
        _major_kernel,
        grid=(b // bsz, n2 // sub, c // LANES),
        in_specs=[pl.BlockSpec((m, k), lambda i, j, l: (0, 0)),
                  pl.BlockSpec((bsz, k, sub, LANES), lambda i, j, l: (i, 0, j, l))],
        out_specs=pl.BlockSpec((bsz, m, sub, LANES), lambda i, j, l: (i, 0, j, l)),
        out_shape=jax.ShapeDtypeStruct((b, m, n2, c), F32),
        compiler_params=_params(3),
        name="dft_major",
    )(g_bf, x4)


def _minor_kernel(*refs, mode, scale, k1b):
    if mode == "conv":
        y_ref, tw_ref, wf_ref, wi_ref, kf_ref, o_ref = refs
    else:
        y_ref, tw_ref, wf_ref, o_ref = refs
    n2 = y_ref.shape[3]
    ct = y_ref.shape[4]
    rep = ct // LANES
    for j in range(k1b):
        yr = y_ref[0, 0, j]
        yi = y_ref[0, 1, j]
        cs = jnp.concatenate([tw_ref[0, j]] * rep, axis=1)
        sn = jnp.concatenate([tw_ref[1, j]] * rep, axis=1)
        a = jnp.concatenate([yr * cs + yi * sn, yi * cs - yr * sn], axis=0).astype(BF16)
        if mode == "real":
            o_ref[0, :, j, :] = _dot(wf_ref[:n2], a) * scale
            continue
        z = _dot(wf_ref[...], a)
        zr, zi = z[:n2], z[n2:]
        if mode == "fwd":
            o_ref[0, 0, j] = zr * scale
            o_ref[0, 1, j] = zi * scale
            continue
        kr = kf_ref[0, 0, j]
        ki = kf_ref[0, 1, j]
        p = jnp.concatenate([zr * kr - zi * ki, zr * ki + zi * kr], axis=0).astype(BF16)
        q = _dot(wi_ref[...], p)
        qr, qi = q[:n2], q[n2:]
        o_ref[0, 0, j] = qr * cs - qi * sn
        o_ref[0, 1, j] = qi * cs + qr * sn


def _dft_minor(y5, tw, wf_bf, mode, scale=1.0, wi_bf=None, kf5=None):
    b, _, n1, n2, c = y5.shape
    k1b = min(8, n1)
    ct = c
    grid = (b, n1 // k1b)
    dspec = pl.BlockSpec((1, 2, k1b, n2, ct), lambda i, j: (i, 0, j, 0, 0))
    in_specs = [dspec,
                pl.BlockSpec((2, k1b, n2, LANES), lambda i, j: (0, j, 0, 0)),
                pl.BlockSpec(wf_bf.shape, lambda i, j: (0, 0))]
    args = [y5, tw, wf_bf]
    if mode == "conv":
        in_specs += [pl.BlockSpec(wi_bf.shape, lambda i, j: (0, 0)),
                     pl.BlockSpec((1, 2, k1b, n2, ct), lambda i, j: (0, 0, j, 0, 0))]
        args += [wi_bf, kf5]
    if mode == "real":
        out_spec = pl.BlockSpec((1, n2, k1b, ct), lambda i, j: (i, 0, j, 0))
        out_shape = jax.ShapeDtypeStruct((b, n2, n1, c), F32)
    else:
        out_spec = dspec
        out_shape = jax.ShapeDtypeStruct((b, 2, n1, n2, c), F32)
    return pl.pallas_call(
        functools.partial(_minor_kernel, mode=mode, scale=scale, k1b=k1b),
        grid=grid, in_specs=in_specs, out_specs=out_spec, out_shape=out_shape,
        compiler_params=_params(2),
        name="dft_minor_" + mode,
    )(*args)


def _cs(num, den):
    ang = 2.0 * np.pi * (num % den) / den
    return np.cos(ang), np.sin(ang)


def _complex_block(c, s, sign):
    return np.block([[c, -sign * s], [sign * s, c]]).astype(np.float32)


def _major_matrix(n1, a_in, a_out, sign, real_in=False):
    c, s = _cs(np.outer(np.arange(a_out), np.arange(a_in)), n1)
    g = _complex_block(c, s, sign)
    return g[:, :a_in] if real_in else g


def _minor_matrix(n2, sign):
    c, s = _cs(np.outer(np.arange(n2), np.arange(n2)), n2)
    return _complex_block(c, s, sign)


def _twiddles(n1, n2):
    c, s = _cs(np.outer(np.arange(n1), np.arange(n2)), n1 * n2)
    tw = jnp.asarray(np.stack([c, s]).astype(np.float32))
    return jnp.broadcast_to(tw[..., None], (2, n1, n2, LANES))


def _out_kernel(yf_ref, yc_ref, x0_ref, w_ref, xn_ref, inv_ref, skip_ref, gf_ref, gh_ref, wo_ref, bo_ref,
                g1_ref, b1_ref, x1_ref, *, alpha):
    c = yc_ref.shape[1]
    w = w_ref[...]
    yh = x0_ref[...] * (yc_ref[...] * inv_ref[...] + skip_ref[...] * w)
    mf = _rms_norm(yf_ref[...], gf_ref[...]).astype(BF16)
    mh = _rms_norm(yh, gh_ref[...]).astype(BF16)
    m = _dot(mf, wo_ref[:c]) + _dot(mh, wo_ref[c:]) + bo_ref[...]
    x1_ref[...] = _layer_norm(alpha * xn_ref[...] + m, g1_ref[...], b1_ref[...])


def _out_proj(yf, yc, x0, w, xn, inv_norm, skip, gf, gh, wo_bf, bo, g1, b1, alpha):
    t, d = xn.shape
    c = yc.shape[1]
    tm = ROW_TILE
    row = lambda i: (i, 0)
    const = lambda i: (0, 0)
    return pl.pallas_call(
        functools.partial(_out_kernel, alpha=alpha),
        grid=(t // tm,),
        in_specs=[
            pl.BlockSpec((tm, c), row),
            pl.BlockSpec((tm, c), row), pl.BlockSpec((tm, c), row), pl.BlockSpec((tm, c), row),
            pl.BlockSpec((tm, d), row),
            pl.BlockSpec((1, c), const), pl.BlockSpec((1, c), const),
            pl.BlockSpec((1, c), const), pl.BlockSpec((1, c), const),
            pl.BlockSpec(wo_bf.shape, const), pl.BlockSpec((1, d), const),
            pl.BlockSpec((1, d), const), pl.BlockSpec((1, d), const),
        ],
        out_specs=pl.BlockSpec((tm, d), row),
        out_shape=jax.ShapeDtypeStruct((t, d), F32),
        compiler_params=_params(1),
        name="out_proj",
    )(yf, yc, x0, w, xn, inv_norm, skip, gf, gh, wo_bf, bo, g1, b1)


def _route_kernel(xa_ref, xb_ref, wr_ref, br_ref, idx_ref, gate_ref, cnt_ref, *, top_k, nt0):
    tt = xa_ref.shape[0]
    e = wr_ref.shape[0]
    nt = (((1,), (1,)), ((), ()))
    xh, xl = _split(jnp.where(pl.program_id(0) < nt0, xa_ref[...], xb_ref[...]))
    wh, wl = _split(wr_ref[...])
    logits = (lax.dot_general(wh, xh, nt, preferred_element_type=F32)
              + lax.dot_general(wh, xl, nt, preferred_element_type=F32)
              + lax.dot_general(wl, xh, nt, preferred_element_type=F32)) + br_ref[...]
    iota_e = lax.broadcasted_iota(I32, (e, tt), 0).astype(F32)
    l = logits
    tops, idxs, hots = [], [], []
    for _ in range(top_k):
        m = jnp.max(l, axis=0, keepdims=True)
        idx = jnp.min(jnp.where(l == m, iota_e, float(e)), axis=0, keepdims=True)
        hot = iota_e == idx
        l = jnp.where(hot, -jnp.inf, l)
        tops.append(m)
        idxs.append(idx)
        hots.append(hot)
    ex = [jnp.exp(m - tops[0]) for m in tops]
    den = ex[0]
    for v in ex[1:]:
        den = den + v
    gates = [v / den for v in ex]
    hot_all = jnp.zeros((e, tt), F32)
    for hot in hots:
        hot_all = jnp.where(hot, 1.0, hot_all)
    upper = (lax.broadcasted_iota(I32, (tt, tt), 0) < lax.broadcasted_iota(I32, (tt, tt), 1))
    before = _dot(hot_all.astype(BF16), jnp.where(upper, 1.0, 0.0).astype(BF16))
    ranks = [jnp.sum(jnp.where(hot, before, 0.0), axis=0, keepdims=True).astype(I32) for hot in hots]
    pad = 8 - top_k
    idx_ref[0] = jnp.concatenate([v.astype(I32) for v in idxs] + ranks, axis=0)
    gate_ref[0] = jnp.concatenate(gates + [jnp.zeros((pad, tt), F32)], axis=0)
    cnt_ref[0] = jnp.sum(hot_all, axis=1, keepdims=True).astype(I32)


def _dual_specs(shape, nt0):
    first = pl.BlockSpec(shape, lambda i, *_: (jnp.minimum(i, nt0 - 1),) + (0,) * (len(shape) - 1))
    second = pl.BlockSpec(shape, lambda i, *_: (jnp.maximum(i - nt0, 0),) + (0,) * (len(shape) - 1))
    return first, second


def _route(x1s, wr_t, br_col):
    d = x1s[0].shape[1]
    e = wr_t.shape[0]
    tt = TOKEN_TILE
    nt0 = x1s[0].shape[0] // tt
    nt = nt0 + x1s[1].shape[0] // tt
    xa, xb = _dual_specs((tt, d), nt0)
    return pl.pallas_call(
        functools.partial(_route_kernel, top_k=TOP_K, nt0=nt0),
        grid=(nt,),
        in_specs=[xa, xb,
                  pl.BlockSpec((e, d), lambda i: (0, 0)),
                  pl.BlockSpec((e, 1), lambda i: (0, 0))],
        out_specs=[pl.BlockSpec((1, 2 * TOP_K, tt), lambda i: (i, 0, 0)),
                   pl.BlockSpec((1, 8, tt), lambda i: (i, 0, 0)),
                   pl.BlockSpec((1, e, 1), lambda i: (i, 0, 0))],
        out_shape=[jax.ShapeDtypeStruct((nt, 2 * TOP_K, tt), I32),
                   jax.ShapeDtypeStruct((nt, 8, tt), F32),
                   jax.ShapeDtypeStruct((nt, e, 1), I32)],
        compiler_params=_params(1),
        name="moe_route",
    )(x1s[0], x1s[1], wr_t, br_col)


SEG_PIECES = int(math.log2(TOKEN_TILE // SEG_ALIGN)) + 1


def _segment_copies(first, count, loff_s, len_s, goff_s, local, remote, sem, to_remote, wait):
    def body(e, carry):
        s = first + e
        n = len_s[s] // SEG_ALIGN
        lo = 0 if loff_s is None else loff_s[s]
        go = goff_s[s]
        for p in reversed(range(SEG_PIECES)):
            size = SEG_ALIGN << p
            start = ((n >> (p + 1)) << (p + 1)) * SEG_ALIGN

            @pl.when(((n >> p) & 1) == 1)
            def _():
                lref = local.at[pl.ds(pl.multiple_of(lo + start, SEG_ALIGN), size)]
                rref = remote.at[pl.ds(pl.multiple_of(go + start, SEG_ALIGN), size)]
                cp = (pltpu.make_async_copy(lref, rref, sem) if to_remote
                      else pltpu.make_async_copy(rref, lref, sem))
                if wait:
                    cp.wait()
                else:
                    cp.start()
        return carry

    lax.fori_loop(0, count, body, 0)


def _dispatch_kernel(loff_s, seg_s, goff_s, plen_s, poff_s, nb_s, xa_ref, xb_ref, idx_ref, xs_ref, pos_ref,
                     buf, sems, *, n_experts, top_k, nt0):
    j = pl.program_id(0)
    nt = pl.num_programs(0)
    slot = j % 2
    tt = xa_ref.shape[0]
    lr = buf.shape[1]
    idx = idx_ref[0]
    tope = idx[:top_k]
    base = jnp.zeros_like(tope)
    for e in range(n_experts):
        base = jnp.where(tope == e, loff_s[j * n_experts + e], base)
    pos = base + idx[top_k:]
    pos_ref[0] = jnp.concatenate([pos, jnp.zeros((8 - top_k, tt), I32)], axis=0)
    rows = lax.broadcasted_iota(I32, (lr, tt), 0)
    perm = jnp.zeros((lr, tt), F32)
    for k in range(top_k):
        perm = jnp.where(rows == pos[k:k + 1], 1.0, perm)
    x = jnp.where(j < nt0, xa_ref[...], xb_ref[...])
    buf[slot] = _dot(perm.astype(BF16), x.astype(BF16)).astype(BF16)

    def copies(step, s, wait):
        _segment_copies(step * n_experts, n_experts, loff_s, seg_s, goff_s, buf.at[s], xs_ref, sems.at[s],
                        True, wait)

    copies(j, slot, False)

    @pl.when(j > 0)
    def _():
        copies(j - 1, 1 - slot, True)

    @pl.when(j == nt - 1)
    def _():
        copies(j, slot, True)
        zrows = EXPERT_BLOCK
        buf[0, :zrows] = jnp.zeros((zrows, buf.shape[2]), BF16)
        _segment_copies(0, n_experts, None, plen_s, poff_s, buf.at[0], xs_ref, sems.at[0], True, False)
        _segment_copies(0, n_experts, None, plen_s, poff_s, buf.at[0], xs_ref, sems.at[0], True, True)
        n_tail = xs_ref.shape[0] // zrows - nb_s[0]

        def tail_copy(i):
            dst = xs_ref.at[pl.ds(pl.multiple_of((nb_s[0] + i) * zrows, zrows), zrows)]
            return pltpu.make_async_copy(buf.at[0, pl.ds(0, zrows)], dst, sems.at[0])

        lax.fori_loop(0, n_tail, lambda i, c: (tail_copy(i).start(), c)[1], 0)
        lax.fori_loop(0, n_tail, lambda i, c: (tail_copy(i).wait(), c)[1], 0)


def _dispatch(x1s, idx, loff, seg, goff, plen, poff, n_blocks, n_rows, local_rows):
    d = x1s[0].shape[1]
    tt = TOKEN_TILE
    nt0 = x1s[0].shape[0] // tt
    nt = nt0 + x1s[1].shape[0] // tt
    e = loff.shape[0] // nt
    xa, xb = _dual_specs((tt, d), nt0)
    grid_spec = pltpu.PrefetchScalarGridSpec(
        num_scalar_prefetch=6,
        grid=(nt,),
        in_specs=[xa, xb, pl.BlockSpec((1, 2 * TOP_K, tt), lambda i, *_: (i, 0, 0))],
        out_specs=[pl.BlockSpec(memory_space=pl.ANY),
                   pl.BlockSpec((1, 8, tt), lambda i, *_: (i, 0, 0))],
        scratch_shapes=[pltpu.VMEM((2, local_rows, d), BF16), pltpu.SemaphoreType.DMA((2,))],
    )
    return pl.pallas_call(
        functools.partial(_dispatch_kernel, n_experts=e, top_k=TOP_K, nt0=nt0),
        grid_spec=grid_spec,
        out_shape=[jax.ShapeDtypeStruct((n_rows, d), BF16), jax.ShapeDtypeStruct((nt, 8, tt), I32)],
        compiler_params=_params(1, "arbitrary"),
        name="moe_dispatch",
    )(loff, seg, goff, plen, poff, n_blocks, x1s[0], x1s[1], idx)


def _combine_kernel(loff_s, seg_s, goff_s, ys_ref, pos_ref, gate_ref, xa_ref, xb_ref, g2_ref, b2_ref,
                    oa_ref, ob_ref, buf, sems, *, n_experts, top_k, alpha, nt0):
    j = pl.program_id(0)
    nt = pl.num_programs(0)
    slot = j % 2
    tt = xa_ref.shape[0]
    lr = buf.shape[1]

    def gather(step, s, wait):
        _segment_copies(step * n_experts, n_experts, loff_s, seg_s, goff_s, buf.at[s], ys_ref, sems.at[s],
                        False, wait)

    def prefetch(step, s):
        buf[s] = jnp.zeros((lr, buf.shape[2]), BF16)
        gather(step, s, False)

    @pl.when(j == 0)
    def _():
        prefetch(0, 0)

    @pl.when(j + 1 < nt)
    def _():
        prefetch(j + 1, 1 - slot)

    gather(j, slot, True)
    pos = pos_ref[...]
    gate = gate_ref[...]
    cols = lax.broadcasted_iota(I32, (tt, lr), 1)
    wsel = jnp.zeros((tt, lr), F32)
    for k in range(top_k):
        wsel = jnp.where(cols == pos[:, k:k + 1], gate[:, k:k + 1], wsel)
    moe = _dot(wsel.astype(BF16), buf[slot])
    x1 = jnp.where(j < nt0, xa_ref[...], xb_ref[...])
    out = _layer_norm(alpha * x1 + moe, g2_ref[...], b2_ref[...])

    @pl.when(j < nt0)
    def _():
        oa_ref[...] = out

    @pl.when(j >= nt0)
    def _():
        ob_ref[...] = out


def _combine(ys, pos_tm, gate_tm, x1s, g2, b2, loff, seg, goff, local_rows, alpha):
    d = x1s[0].shape[1]
    tt = TOKEN_TILE
    nt0 = x1s[0].shape[0] // tt
    nt = nt0 + x1s[1].shape[0] // tt
    e = loff.shape[0] // nt
    xa, xb = _dual_specs((tt, d), nt0)
    grid_spec = pltpu.PrefetchScalarGridSpec(
        num_scalar_prefetch=3,
        grid=(nt,),
        in_specs=[pl.BlockSpec(memory_space=pl.ANY),
                  pl.BlockSpec((tt, 8), lambda i, *_: (i, 0)),
                  pl.BlockSpec((tt, 8), lambda i, *_: (i, 0)),
                  xa, xb,
                  pl.BlockSpec((1, d), lambda i, *_: (0, 0)),
                  pl.BlockSpec((1, d), lambda i, *_: (0, 0))],
        out_specs=list(_dual_specs((tt, d), nt0)),
        scratch_shapes=[pltpu.VMEM((2, local_rows, d), BF16), pltpu.SemaphoreType.DMA((2,))],
    )
    return pl.pallas_call(
        functools.partial(_combine_kernel, n_experts=e, top_k=TOP_K, alpha=alpha, nt0=nt0),
        grid_spec=grid_spec,
        out_shape=[jax.ShapeDtypeStruct(x.shape, F32) for x in x1s],
        compiler_params=_params(1, "arbitrary"),
        name="moe_combine",
    )(loff, seg, goff, ys, pos_tm, gate_tm, x1s[0], x1s[1], g2, b2)


def _expert_kernel(be_s, nb_s, xs_ref, wgu_ref, bgu_ref, wd_ref, bd_ref, ys_ref, wgu_bf, wd_bf):
    b = pl.program_id(0)
    active = b < nb_s[0]
    fresh = jnp.logical_or(b == 0, be_s[b] != be_s[jnp.maximum(b - 1, 0)])

    @pl.when(jnp.logical_and(active, fresh))
    def _():
        wgu_bf[...] = wgu_ref[0].astype(BF16)
        wd_bf[...] = wd_ref[0].astype(BF16)

    @pl.when(active)
    def _():
        dff = wd_ref.shape[1]
        gu = _dot(xs_ref[...], wgu_bf[...]) + bgu_ref[0]
        gate = jnp.minimum(gu[:, :dff], SWIGLU_LIMIT)
        up = jnp.clip(gu[:, dff:], -SWIGLU_LIMIT, SWIGLU_LIMIT)
        h = (up + 1.0) * (gate * jax.nn.sigmoid(SWIGLU_ALPHA * gate))
        ys_ref[...] = (_dot(h.astype(BF16), wd_bf[...]) + bd_ref[0]).astype(BF16)

    @pl.when(jnp.logical_not(active))
    def _():
        ys_ref[...] = jnp.zeros_like(ys_ref)


def _experts(xs, block_expert, n_blocks, wgu, bgu, wd, bdn):
    n_rows, d = xs.shape
    bm = EXPERT_BLOCK
    f2 = wgu.shape[2]
    dff = wd.shape[1]
    blk = lambda b, be, nb: (jnp.minimum(b, nb[0] - 1), 0)
    wsel = lambda b, be, nb: (be[b], 0, 0)
    grid_spec = pltpu.PrefetchScalarGridSpec(
        num_scalar_prefetch=2,
        grid=(n_rows // bm,),
        in_specs=[pl.BlockSpec((bm, d), blk),
                  pl.BlockSpec((1, d, f2), wsel), pl.BlockSpec((1, 1, f2), wsel),
                  pl.BlockSpec((1, dff, d), wsel), pl.BlockSpec((1, 1, d), wsel)],
        out_specs=pl.BlockSpec((bm, d), lambda b, be, nb: (b, 0)),
        scratch_shapes=[pltpu.VMEM((d, f2), BF16), pltpu.VMEM((dff, d), BF16)],
    )
    return pl.pallas_call(
        _expert_kernel,
        grid_spec=grid_spec,
        out_shape=jax.ShapeDtypeStruct((n_rows, d), BF16),
        compiler_params=_params(1, "arbitrary"),
        name="moe_experts",
    )(block_expert, n_blocks, xs, wgu, bgu, wd, bdn)


def _routed_moe(x1s, w_router, b_router, w_gate_up, b_gate_up, w_down, b_down, g2, b2, alpha):
    t = x1s[0].shape[0] + x1s[1].shape[0]
    e = w_router.shape[1]
    tt = TOKEN_TILE
    nt = t // tt
    bm = EXPERT_BLOCK
    idx, gate, cnt = _route(x1s, w_router.T, b_router.reshape(e, 1))

    cnt = cnt.reshape(nt, e)
    seg = (cnt + SEG_ALIGN - 1) // SEG_ALIGN * SEG_ALIGN
    loff = jnp.cumsum(seg, axis=1) - seg
    per_expert = jnp.sum(seg, axis=0)
    padded = (per_expert + bm - 1) // bm * bm
    blocks_end = jnp.cumsum(padded) // bm
    start = jnp.cumsum(padded) - padded
    goff = start[None, :] + jnp.cumsum(seg, axis=0) - seg
    local_rows = TOP_K * tt + e * SEG_ALIGN
    n_rows_max = -(-(TOP_K * t + (SEG_ALIGN - 1) * e * nt) // bm) * bm + e * bm
    n_blocks = blocks_end[-1:].astype(I32)
    bidx = jnp.minimum(jnp.arange(n_rows_max // bm, dtype=I32), n_blocks[0] - 1)
    block_expert = jnp.minimum(jnp.sum(bidx[:, None] >= blocks_end[None, :], axis=1), e - 1).astype(I32)
    flat = lambda a: a.reshape(-1).astype(I32)
    plen, poff = flat(padded - per_expert), flat(start + per_expert)

    xs, pos = _dispatch(x1s, idx, flat(loff), flat(seg), flat(goff), plen, poff, n_blocks, n_rows_max,
                        local_rows)
    ys = _experts(xs, block_expert, n_blocks, w_gate_up, b_gate_up.reshape(e, 1, -1),
                  w_down, b_down.reshape(e, 1, -1))
    pos_tm = jnp.swapaxes(pos, 1, 2).reshape(t, 8)
    gate_tm = jnp.swapaxes(gate, 1, 2).reshape(t, 8)
    return _combine(ys, pos_tm, gate_tm, x1s, g2, b2, flat(loff), flat(seg), flat(goff), local_rows, alpha)


def _mixer(x, apply_ln, lng, lnb, lay, consts, alpha):
    batch, seq, d = x.shape
    t = batch * seq
    fw = consts["bd"].shape[0]
    n2 = MINOR
    xn, z, x0, w = _in_proj(x.reshape(t, d), batch, seq, lng, lnb, lay["w_in"], lay["b_in"], consts["bd"],
                            lay["w_short"], lay["b_short"], apply_ln)
    c = x0.shape[1]

    n1 = seq // n2
    g = jnp.asarray(_major_matrix(n1, n1, n1, -1)).astype(BF16)
    y = _dft_major(g, z.reshape(batch, 2 * n1, n2, fw))
    yf = _dft_minor(y.reshape(batch, 2, n1, n2, fw), _twiddles(n1, n2), consts["wf"], "real",
                    scale=1.0 / math.sqrt(seq * FOURIER_GROUP_DIM))

    n = 2 * seq
    m1 = n // n2
    tw = _twiddles(m1, n2)
    kf, asum = _hyena_filter(seq, consts["bands"], lay["filt_w1"], lay["filt_b1"], lay["filt_w2"],
                             lay["filt_b2"], lay["filt_w3"], lay["filt_b3"], lay["filt_w_out"],
                             lay["filt_freq"], lay["filt_decay"])
    gk = jnp.asarray(_major_matrix(m1, m1, m1, -1, real_in=True)).astype(BF16)
    ky = _dft_major(gk, kf.reshape(1, m1, n2, c))
    kspec = _dft_minor(ky.reshape(1, 2, m1, n2, c), tw, consts["wf"], "fwd", scale=1.0 / n)
    gfw = jnp.asarray(_major_matrix(m1, m1 // 2, m1, -1)).astype(BF16)
    gin = jnp.asarray(_major_matrix(m1, m1, m1 // 2, +1)).astype(BF16)
    pairs = batch // 2
    u = _dft_major(gfw, w.reshape(pairs, m1, n2, c))
    v = _dft_minor(u.reshape(pairs, 2, m1, n2, c), tw, consts["wf"], "conv", wi_bf=consts["wi"], kf5=kspec)
    yc = _dft_major(gin, v.reshape(pairs, 2 * m1, n2, c))

    inv_norm = 1.0 / asum
    return _out_proj(yf.reshape(t, fw), yc.reshape(t, c), x0, w, xn, inv_norm,
                     lay["hyena_skip"], lay["g_fourier"], lay["g_hyena"], lay["w_out"], lay["b_out"],
                     lay["ln1_g"], lay["ln1_b"], alpha)


def kernel(x_prompt, x_sample, ln_in_g, ln_in_b, w_in, b_in, w_short, b_short, filt_w1, filt_b1, filt_w2, filt_b2, filt_w3, filt_b3, filt_w_out, filt_freq, filt_decay, hyena_skip, g_fourier, g_hyena, w_out, b_out, ln1_g, ln1_b, w_router, b_router, w_gate_up, b_gate_up, w_down, b_down, ln2_g, ln2_b):
    depth, d, in_width = w_in.shape
    fw = g_fourier.shape[1]
    alpha = (2.0 * depth) ** 0.25
    row = lambda a: a.reshape(1, -1)

    gd = FOURIER_GROUP_DIM
    cg, sg = _cs(np.outer(np.arange(gd), np.arange(gd)), gd)
    eye = np.eye(fw // gd)
    bd = np.concatenate([np.kron(eye, cg), -np.kron(eye, sg)], axis=1).astype(np.float32)
    bands = jnp.linspace(1e-4, FILTER_BANDS - 1, FILTER_BANDS, dtype=F32)
    bands_row = jnp.zeros((1, LANES), F32).at[0, 1:1 + FILTER_BANDS].set(bands)
    bands_row = bands_row.at[0, 1 + FILTER_BANDS:1 + 2 * FILTER_BANDS].set(bands)
    consts = dict(bd=jnp.asarray(bd).astype(BF16), bands=bands_row,
                  wf=jnp.asarray(_minor_matrix(MINOR, -1)).astype(BF16),
                  wi=jnp.asarray(_minor_matrix(MINOR, +1)).astype(BF16))

    xs = [x_prompt, x_sample]
    for l in range(depth):
        emb = filt_w1.shape[1]
        lay = dict(
            w_in=w_in[l].astype(BF16), b_in=row(b_in[l]), w_short=w_short[l], b_short=row(b_short[l]),
            filt_w1=jnp.zeros((LANES, filt_w1.shape[2]), F32).at[:emb].set(filt_w1[l]),
            filt_b1=row(filt_b1[l]), filt_w2=filt_w2[l], filt_b2=row(filt_b2[l]), filt_w3=filt_w3[l],
            filt_b3=row(filt_b3[l]), filt_w_out=filt_w_out[l], filt_freq=row(filt_freq[l]),
            filt_decay=filt_decay[l], hyena_skip=row(hyena_skip[l]), g_fourier=row(g_fourier[l]),
            g_hyena=row(g_hyena[l]), w_out=w_out[l].astype(BF16), b_out=row(b_out[l]),
            ln1_g=row(ln1_g[l]), ln1_b=row(ln1_b[l]))
        mixed = [_mixer(x, l == 0, row(ln_in_g), row(ln_in_b), lay, consts, alpha) for x in xs]
        outs = _routed_moe(mixed, w_router[l], b_router[l], w_gate_up[l], b_gate_up[l], w_down[l],
                           b_down[l], row(ln2_g[l]), row(ln2_b[l]), alpha)
        xs = [o.reshape(x.shape) for o, x in zip(outs, xs)]
    return (xs[0], xs[1])
```

```python
import functools
import math

import numpy as np
import jax
import jax.numpy as jnp
from jax import lax
from jax.experimental import pallas as pl
from jax.experimental.pallas import tpu as pltpu

F32 = jnp.float32
BF16 = jnp.bfloat16
I32 = jnp.int32

LN_EPS = 1e-5
RMS_EPS = 1e-6
SWIGLU_LIMIT = 7.0
SWIGLU_ALPHA = 1.702
TOP_K = 4
FOURIER_GROUP_DIM = 64
FILTER_BANDS = 16

LANES = 128
MINOR = 128
ROW_TILE = 512
IN_TILE = 256
MAJOR_BLOCK_BYTES = 3 * 1024 * 1024
TOKEN_TILE = 512
SEG_ALIGN = 16
EXPERT_BLOCK = 512
EXPERT_SPLIT = 2
VMEM_LIMIT = 56 * 1024 * 1024


def _params(n_grid, semantics="parallel"):
    return pltpu.CompilerParams(dimension_semantics=(semantics,) * n_grid,
                                vmem_limit_bytes=VMEM_LIMIT)


def _dot(a, b):
    return jnp.dot(a, b, preferred_element_type=F32)


def _split(a):
    hi = a.astype(BF16)
    lo = (a - hi.astype(F32)).astype(BF16)
    return hi, lo


def _dot3(a, b):
    ah, al = _split(a)
    bh, bl = _split(b)
    return _dot(ah, bh) + _dot(ah, bl) + _dot(al, bh)


def _layer_norm(x, g, b):
    mu = jnp.mean(x, axis=-1, keepdims=True)
    xc = x - mu
    var = jnp.mean(xc * xc, axis=-1, keepdims=True)
    return xc * lax.rsqrt(var + LN_EPS) * g + b


def _rms_norm(x, g):
    return x * lax.rsqrt(jnp.mean(x * x, axis=-1, keepdims=True) + RMS_EPS) * g


def _in_kernel(x_ref, xp_ref, xq_ref, lng_ref, lnb_ref, win_ref, bin_ref, bd_ref, wsh_ref, bsh_ref,
               xn_ref, z_ref, x0_ref, w_ref, *, tiles_per_seq, apply_ln, fw):
    it = pl.program_id(0) % tiles_per_seq
    tm = x_ref.shape[0]

    def norm(x):
        return _layer_norm(x, lng_ref[...], lnb_ref[...]) if apply_ln else x

    xn = norm(x_ref[...])
    xn_ref[...] = xn
    halo = norm(jnp.concatenate([xp_ref[...], xq_ref[...]], axis=0))
    proj = _dot(xn.astype(BF16), win_ref[...]) + bin_ref[...]
    projh = _dot(halo.astype(BF16), win_ref[:, fw:]) + bin_ref[:, fw:]

    z = _dot(proj[:, :fw].astype(BF16), bd_ref[...])
    z_ref[0, 0] = z[:, :fw]
    z_ref[0, 1] = z[:, fw:]

    ph = proj[:, fw:]
    prev_row = jnp.where(it > 0, projh[7:8], 0.0)
    next_row = jnp.where(it < tiles_per_seq - 1, projh[8:9], 0.0)
    rows = lax.broadcasted_iota(I32, ph.shape, 0)
    up = jnp.where(rows == 0, prev_row, pltpu.roll(ph, 1, 0))
    dn = jnp.where(rows == tm - 1, next_row, pltpu.roll(ph, tm - 1, 0))
    wsh = wsh_ref[...]
    uc = up * wsh[0:1] + ph * wsh[1:2] + dn * wsh[2:3] + bsh_ref[...]
    hw = uc.shape[1] // 3
    x0_ref[...] = uc[:, :hw]
    w_ref[...] = uc[:, 2 * hw:] * uc[:, hw:2 * hw]


def _in_proj(x2d, batch, seq, lng, lnb, win_bf, b_in, bd_bf, w_short, b_short, apply_ln):
    t, d = x2d.shape
    fw = bd_bf.shape[0]
    inw = win_bf.shape[1]
    hw = (inw - fw) // 3
    tm = IN_TILE
    tps = seq // tm
    r8 = tm // 8
    nblk8 = t // 8
    const = lambda i: (0, 0)
    kern = functools.partial(_in_kernel, tiles_per_seq=tps, apply_ln=apply_ln, fw=fw)
    return pl.pallas_call(
        kern,
        grid=(t // tm,),
        in_specs=[
            pl.BlockSpec((tm, d), lambda i: (i, 0)),
            pl.BlockSpec((8, d), lambda i: (jnp.maximum(i * r8 - 1, 0), 0)),
            pl.BlockSpec((8, d), lambda i: (jnp.minimum((i + 1) * r8, nblk8 - 1), 0)),
            pl.BlockSpec((1, d), const), pl.BlockSpec((1, d), const),
            pl.BlockSpec((d, inw), const), pl.BlockSpec((1, inw), const),
            pl.BlockSpec((fw, 2 * fw), const),
            pl.BlockSpec((3, 3 * hw), const), pl.BlockSpec((1, 3 * hw), const),
        ],
        out_specs=[
            pl.BlockSpec((tm, d), lambda i: (i, 0)),
            pl.BlockSpec((1, 2, tm, fw), lambda i: (i // tps, 0, i % tps, 0)),
            pl.BlockSpec((tm, hw), lambda i: (i, 0)),
            pl.BlockSpec((tm, hw), lambda i: (i, 0)),
        ],
        out_shape=[
            jax.ShapeDtypeStruct((t, d), F32),
            jax.ShapeDtypeStruct((batch, 2, seq, fw), F32),
            jax.ShapeDtypeStruct((t, hw), F32),
            jax.ShapeDtypeStruct((t, hw), F32),
        ],
        compiler_params=_params(1),
        name="in_proj",
    )(x2d, x2d, x2d, lng, lnb, win_bf, b_in, bd_bf, w_short, b_short)


FILTER_HALO = 8


def _filter_kernel(bands_ref, w1_ref, b1_ref, w2_ref, b2_ref, w3_ref, b3_ref, wl_ref, fr_ref, dec_ref,
                   k_ref, asum_ref, tc_ref, ts_ref, kf_buf, kb_buf, sems, *, seq):
    i = pl.program_id(0)
    nsteps = pl.num_programs(0)
    tile, c = kf_buf.shape
    ext = tc_ref.shape[0]
    rep = c // LANES
    w_unit = 2.0 * math.pi / seq
    bands = bands_ref[...]

    @pl.when(i == 0)
    def _():
        arg = bands * (w_unit * lax.broadcasted_iota(I32, (ext, LANES), 0).astype(F32))
        tc_ref[...] = jnp.cos(arg)
        ts_ref[...] = jnp.sin(arg)
        asum_ref[...] = jnp.zeros_like(asum_ref)

    j0 = i * tile
    base = bands * (w_unit * j0.astype(F32))
    c0, s0 = jnp.cos(base), jnp.sin(base)
    tc, ts = tc_ref[...], ts_ref[...]
    cosf = c0 * tc - s0 * ts
    sinf = s0 * tc + c0 * ts
    j = j0 + lax.broadcasted_iota(I32, (ext, LANES), 0)
    t = j.astype(F32) * (1.0 / (seq - 1))
    lane = lax.broadcasted_iota(I32, (ext, LANES), 1)
    feat = jnp.where(lane == 0, t,
                     jnp.where(lane <= FILTER_BANDS, cosf,
                               jnp.where(lane <= 2 * FILTER_BANDS, -sinf, 0.0)))
    fr = fr_ref[...]
    h = jnp.sin(fr * (_dot3(feat, w1_ref[...]) + b1_ref[...]))
    h = jnp.sin(fr * (_dot3(h, w2_ref[...]) + b2_ref[...]))
    h = jnp.sin(fr * (_dot3(h, w3_ref[...]) + b3_ref[...]))
    hl = _dot3(h, wl_ref[...])
    t4 = jnp.concatenate([t] * rep, axis=1)
    j4 = jnp.concatenate([j] * rep, axis=1)
    dec = jnp.abs(dec_ref[...])
    hf = hl[:, :c] * jnp.exp(-t4 * dec[0:1])
    hb = hl[:, c:] * jnp.exp(-t4 * dec[1:2])
    hb = jnp.where(j4 < seq, hb, 0.0)
    asum_ref[...] += jnp.sum(jnp.abs(hf[:tile]) + jnp.abs(hb[:tile]), axis=0, keepdims=True)

    rev = (lax.broadcasted_iota(I32, (tile, ext), 1) == tile - lax.broadcasted_iota(I32, (tile, ext), 0))
    rev = jnp.where(rev, 1.0, 0.0).astype(BF16)
    b1 = hb.astype(BF16)
    r1 = hb - b1.astype(F32)
    b2 = r1.astype(BF16)
    b3 = (r1 - b2.astype(F32)).astype(BF16)
    kb = _dot(rev, b1) + _dot(rev, b2) + _dot(rev, b3)

    def copies():
        return (pltpu.make_async_copy(kf_buf, k_ref.at[pl.ds(pl.multiple_of(j0, tile), tile)], sems.at[0]),
                pltpu.make_async_copy(kb_buf, k_ref.at[pl.ds(pl.multiple_of(2 * seq - j0 - tile, tile), tile)],
                                      sems.at[1]))

    @pl.when(i > 0)
    def _():
        for cp in copies():
            cp.wait()

    kf_buf[...] = jnp.where(j4[:tile] == 0, hf[:tile] + hb[:tile], hf[:tile])
    kb_buf[...] = kb
    for cp in copies():
        cp.start()

    @pl.when(i == nsteps - 1)
    def _():
        for cp in copies():
            cp.wait()


def _hyena_filter(seq, bands_row, w1p, b1, w2, b2, w3, b3, wl, freq, decay):
    c = decay.shape[1]
    tile = ROW_TILE
    ext = tile + FILTER_HALO
    full = lambda a: pl.BlockSpec(a.shape, lambda i: (0,) * a.ndim)
    args = (bands_row, w1p, b1, w2, b2, w3, b3, wl, freq, decay)
    return pl.pallas_call(
        functools.partial(_filter_kernel, seq=seq),
        grid=(seq // tile,),
        in_specs=[full(a) for a in args],
        out_specs=[pl.BlockSpec(memory_space=pl.ANY), pl.BlockSpec((1, c), lambda i: (0, 0))],
        out_shape=[jax.ShapeDtypeStruct((2 * seq, c), F32), jax.ShapeDtypeStruct((1, c), F32)],
        scratch_shapes=[pltpu.VMEM((ext, LANES), F32), pltpu.VMEM((ext, LANES), F32),
                        pltpu.VMEM((tile, c), F32), pltpu.VMEM((tile, c), F32),
                        pltpu.SemaphoreType.DMA((2,))],
        compiler_params=_params(1, "arbitrary"),
        name="hyena_filter",
    )(*args)


def _major_kernel(g_ref, x_ref, o_ref):
    g = g_ref[...]
    bsz, k, sub, lanes = x_ref.shape
    m = o_ref.shape[1]
    x2 = x_ref.reshape(bsz * k * sub, lanes)
    o2 = o_ref.reshape(bsz * m * sub, lanes)
    for i in range(bsz):
        for b in range(sub):
            xb = x2[pl.ds(i * k * sub + b, k, stride=sub), :]
            o2[pl.ds(i * m * sub + b, m, stride=sub), :] = _dot(g, xb.astype(BF16))


def _dft_major(g_bf, x4):
    b, k, n2, c = x4.shape
    m = g_bf.shape[0]
    sub = 8
    bsz = max(1, min(b, MAJOR_BLOCK_BYTES // ((k + m) * sub * LANES * 4)))
    while b % bsz:
        bsz -= 1
    return pl.pallas_call(
        _major_kernel,
        grid=(b // bsz, n2 // sub, c // LANES),
        in_specs=[pl.BlockSpec((m, k), lambda i, j, l: (0, 0)),
                  pl.BlockSpec((bsz, k, sub, LANES), lambda i, j, l: (i, 0, j, l))],
        out_specs=pl.BlockSpec((bsz, m, sub, LANES), lambda i, j, l: (i, 0, j, l)),
        out_shape=jax.ShapeDtypeStruct((b, m, n2, c), F32),
        compiler_params=_params(3),
        name="dft_major",
    )(g_bf, x4)


def _minor_kernel(*refs, mode, scale, k1b):
    if mode == "conv":
        y_ref, tw_ref, wf_ref, wi_ref, kf_ref, o_ref = refs
    else:
        y_ref, tw_ref, wf_ref, o_ref = refs
    n2 = y_ref.shape[3]
    ct = y_ref.shape[4]
    rep = ct // LANES
    for j in range(k1b):
        yr = y_ref[0, 0, j]
        yi = y_ref[0, 1, j]
        cs = jnp.concatenate([tw_ref[0, j]] * rep, axis=1)
        sn = jnp.concatenate([tw_ref[1, j]] * rep, axis=1)
        a = jnp.concatenate([yr * cs + yi * sn, yi * cs - yr * sn], axis=0).astype(BF16)
        if mode == "real":
            o_ref[0, :, j, :] = _dot(wf_ref[:n2], a) * scale
            continue
        z = _dot(wf_ref[...], a)
        zr, zi = z[:n2], z[n2:]
        if mode == "fwd":
            o_ref[0, 0, j] = zr * scale
            o_ref[0, 1, j] = zi * scale
            continue
        kr = kf_ref[0, 0, j]
        ki = kf_ref[0, 1, j]
        p = jnp.concatenate([zr * kr - zi * ki, zr * ki + zi * kr], axis=0).astype(BF16)
        q = _dot(wi_ref[...], p)
        qr, qi = q[:n2], q[n2:]
        o_ref[0, 0, j] = qr * cs - qi * sn
        o_ref[0, 1, j] = qi * cs + qr * sn


def _dft_minor(y5, tw, wf_bf, mode, scale=1.0, wi_bf=None, kf5=None):
    b, _, n1, n2, c = y5.shape
    k1b = min(8, n1)
    ct = c
    grid = (b, n1 // k1b)
    dspec = pl.BlockSpec((1, 2, k1b, n2, ct), lambda i, j: (i, 0, j, 0, 0))
    in_specs = [dspec,
                pl.BlockSpec((2, k1b, n2, LANES), lambda i, j: (0, j, 0, 0)),
                pl.BlockSpec(wf_bf.shape, lambda i, j: (0, 0))]
    args = [y5, tw, wf_bf]
    if mode == "conv":
        in_specs += [pl.BlockSpec(wi_bf.shape, lambda i, j: (0, 0)),
                     pl.BlockSpec((1, 2, k1b, n2, ct), lambda i, j: (0, 0, j, 0, 0))]
        args += [wi_bf, kf5]
    if mode == "real":
        out_spec = pl.BlockSpec((1, n2, k1b, ct), lambda i, j: (i, 0, j, 0))
        out_shape = jax.ShapeDtypeStruct((b, n2, n1, c), F32)
    else:
        out_spec = dspec
        out_shape = jax.ShapeDtypeStruct((b, 2, n1, n2, c), F32)
    return pl.pallas_call(
        functools.partial(_minor_kernel, mode=mode, scale=scale, k1b=k1b),
        grid=grid, in_specs=in_specs, out_specs=out_spec, out_shape=out_shape,
        compiler_params=_params(2),
        name="dft_minor_" + mode,
    )(*args)


def _cs(num, den):
    ang = 2.0 * np.pi * (num % den) / den
    return np.cos(ang), np.sin(ang)


def _complex_block(c, s, sign):
    return np.block([[c, -sign * s], [sign * s, c]]).astype(np.float32)


def _major_matrix(n1, a_in, a_out, sign, real_in=False):
    c, s = _cs(np.outer(np.arange(a_out), np.arange(a_in)), n1)
    g = _complex_block(c, s, sign)
    return g[:, :a_in] if real_in else g


def _minor_matrix(n2, sign):
    c, s = _cs(np.outer(np.arange(n2), np.arange(n2)), n2)
    return _complex_block(c, s, sign)


def _twiddles(n1, n2):
    c, s = _cs(np.outer(np.arange(n1), np.arange(n2)), n1 * n2)
    tw = jnp.asarray(np.stack([c, s]).astype(np.float32))
    return jnp.broadcast_to(tw[..., None], (2, n1, n2, LANES))


def _out_kernel(yf_ref, yc_ref, x0_ref, w_ref, xn_ref, inv_ref, skip_ref, gf_ref, gh_ref, wo_ref, bo_ref,
                g1_ref, b1_ref, x1_ref, *, alpha):
    c = yc_ref.shape[1]
    w = w_ref[...]
    yh = x0_ref[...] * (yc_ref[...] * inv_ref[...] + skip_ref[...] * w)
    mf = _rms_norm(yf_ref[...], gf_ref[...]).astype(BF16)
    mh = _rms_norm(yh, gh_ref[...]).astype(BF16)
    m = _dot(mf, wo_ref[:c]) + _dot(mh, wo_ref[c:]) + bo_ref[...]
    x1_ref[...] = _layer_norm(alpha * xn_ref[...] + m, g1_ref[...], b1_ref[...])


def _out_proj(yf, yc, x0, w, xn, inv_norm, skip, gf, gh, wo_bf, bo, g1, b1, alpha):
    t, d = xn.shape
    c = yc.shape[1]
    tm = ROW_TILE
    row = lambda i: (i, 0)
    const = lambda i: (0, 0)
    return pl.pallas_call(
        functools.partial(_out_kernel, alpha=alpha),
        grid=(t // tm,),
        in_specs=[
            pl.BlockSpec((tm, c), row),
            pl.BlockSpec((tm, c), row), pl.BlockSpec((tm, c), row), pl.BlockSpec((tm, c), row),
            pl.BlockSpec((tm, d), row),
            pl.BlockSpec((1, c), const), pl.BlockSpec((1, c), const),
            pl.BlockSpec((1, c), const), pl.BlockSpec((1, c), const),
            pl.BlockSpec(wo_bf.shape, const), pl.BlockSpec((1, d), const),
            pl.BlockSpec((1, d), const), pl.BlockSpec((1, d), const),
        ],
        out_specs=pl.BlockSpec((tm, d), row),
        out_shape=jax.ShapeDtypeStruct((t, d), F32),
        compiler_params=_params(1),
        name="out_proj",
    )(yf, yc, x0, w, xn, inv_norm, skip, gf, gh, wo_bf, bo, g1, b1)


def _route_kernel(xa_ref, xb_ref, wr_ref, br_ref, idx_ref, gate_ref, cnt_ref, *, top_k, nt0):
    tt = xa_ref.shape[0]
    e = wr_ref.shape[0]
    nt = (((1,), (1,)), ((), ()))
    xh, xl = _split(jnp.where(pl.program_id(0) < nt0, xa_ref[...], xb_ref[...]))
    wh, wl = _split(wr_ref[...])
    logits = (lax.dot_general(wh, xh, nt, preferred_element_type=F32)
              + lax.dot_general(wh, xl, nt, preferred_element_type=F32)
              + lax.dot_general(wl, xh, nt, preferred_element_type=F32)) + br_ref[...]
    iota_e = lax.broadcasted_iota(I32, (e, tt), 0).astype(F32)
    l = logits
    tops, idxs, hots = [], [], []
    for _ in range(top_k):
        m = jnp.max(l, axis=0, keepdims=True)
        idx = jnp.min(jnp.where(l == m, iota_e, float(e)), axis=0, keepdims=True)
        hot = iota_e == idx
        l = jnp.where(hot, -jnp.inf, l)
        tops.append(m)
        idxs.append(idx)
        hots.append(hot)
    ex = [jnp.exp(m - tops[0]) for m in tops]
    den = ex[0]
    for v in ex[1:]:
        den = den + v
    gates = [v / den for v in ex]
    hot_all = jnp.zeros((e, tt), F32)
    for hot in hots:
        hot_all = jnp.where(hot, 1.0, hot_all)
    upper = (lax.broadcasted_iota(I32, (tt, tt), 0) < lax.broadcasted_iota(I32, (tt, tt), 1))
    before = _dot(hot_all.astype(BF16), jnp.where(upper, 1.0, 0.0).astype(BF16))
    ranks = [jnp.sum(jnp.where(hot, before, 0.0), axis=0, keepdims=True).astype(I32) for hot in hots]
    pad = 8 - top_k
    idx_ref[0] = jnp.concatenate([v.astype(I32) for v in idxs] + ranks, axis=0)
    gate_ref[0] = jnp.concatenate(gates + [jnp.zeros((pad, tt), F32)], axis=0)
    cnt_ref[0] = jnp.sum(hot_all, axis=1, keepdims=True).astype(I32)


def _dual_specs(shape, nt0):
    first = pl.BlockSpec(shape, lambda i, *_: (jnp.minimum(i, nt0 - 1),) + (0,) * (len(shape) - 1))
    second = pl.BlockSpec(shape, lambda i, *_: (jnp.maximum(i - nt0, 0),) + (0,) * (len(shape) - 1))
    return first, second


def _route(x1s, wr_t, br_col):
    d = x1s[0].shape[1]
    e = wr_t.shape[0]
    tt = TOKEN_TILE
    nt0 = x1s[0].shape[0] // tt
    nt = nt0 + x1s[1].shape[0] // tt
    xa, xb = _dual_specs((tt, d), nt0)
    return pl.pallas_call(
        functools.partial(_route_kernel, top_k=TOP_K, nt0=nt0),
        grid=(nt,),
        in_specs=[xa, xb,
                  pl.BlockSpec((e, d), lambda i: (0, 0)),
                  pl.BlockSpec((e, 1), lambda i: (0, 0))],
        out_specs=[pl.BlockSpec((1, 2 * TOP_K, tt), lambda i: (i, 0, 0)),
                   pl.BlockSpec((1, 8, tt), lambda i: (i, 0, 0)),
                   pl.BlockSpec((1, e, 1), lambda i: (i, 0, 0))],
        out_shape=[jax.ShapeDtypeStruct((nt, 2 * TOP_K, tt), I32),
                   jax.ShapeDtypeStruct((nt, 8, tt), F32),
                   jax.ShapeDtypeStruct((nt, e, 1), I32)],
        compiler_params=_params(1),
        name="moe_route",
    )(x1s[0], x1s[1], wr_t, br_col)


SEG_PIECES = int(math.log2(TOKEN_TILE // SEG_ALIGN)) + 1


def _segment_copies(first, count, loff_s, len_s, goff_s, local, remote, sem, to_remote, wait):
    def body(e, carry):
        s = first + e
        n = len_s[s] // SEG_ALIGN
        lo = 0 if loff_s is None else loff_s[s]
        go = goff_s[s]
        for p in reversed(range(SEG_PIECES)):
            size = SEG_ALIGN << p
            start = ((n >> (p + 1)) << (p + 1)) * SEG_ALIGN

            @pl.when(((n >> p) & 1) == 1)
            def _():
                lref = local.at[pl.ds(pl.multiple_of(lo + start, SEG_ALIGN), size)]
                rref = remote.at[pl.ds(pl.multiple_of(go + start, SEG_ALIGN), size)]
                cp = (pltpu.make_async_copy(lref, rref, sem) if to_remote
                      else pltpu.make_async_copy(rref, lref, sem))
                if wait:
                    cp.wait()
                else:
                    cp.start()
        return carry

    lax.fori_loop(0, count, body, 0)


def _wait_tile(step, n_experts, loff_s, len_s, local, remote, sem, to_remote):
    last = step * n_experts + n_experts - 1
    n = (loff_s[last] + len_s[last]) // SEG_ALIGN
    for p in range((local.shape[0] // SEG_ALIGN).bit_length()):
        size = SEG_ALIGN << p

        @pl.when(((n >> p) & 1) == 1)
        def _():
            lref = local.at[pl.ds(0, size)]
            rref = remote.at[pl.ds(0, size)]
            cp = (pltpu.make_async_copy(lref, rref, sem) if to_remote
                  else pltpu.make_async_copy(rref, lref, sem))
            cp.wait()


def _dispatch_kernel(loff_s, seg_s, goff_s, plen_s, poff_s, nb_s, xa_ref, xb_ref, idx_ref, xs_ref, pos_ref,
                     buf, sems, *, n_experts, top_k, nt0):
    j = pl.program_id(0)
    nt = pl.num_programs(0)
    slot = j % 2
    tt = xa_ref.shape[0]
    lr = buf.shape[1]
    idx = idx_ref[0]
    tope = idx[:top_k]
    base = jnp.zeros_like(tope)
    for e in range(n_experts):
        base = jnp.where(tope == e, loff_s[j * n_experts + e], base)
    pos = base + idx[top_k:]
    pos_ref[0] = jnp.concatenate([pos, jnp.zeros((8 - top_k, tt), I32)], axis=0)
    rows = lax.broadcasted_iota(I32, (lr, tt), 0)
    perm = jnp.zeros((lr, tt), F32)
    for k in range(top_k):
        perm = jnp.where(rows == pos[k:k + 1], 1.0, perm)
    x = jnp.where(j < nt0, xa_ref[...], xb_ref[...])
    buf[slot] = _dot(perm.astype(BF16), x.astype(BF16)).astype(BF16)

    def drained(step, s):
        _wait_tile(step, n_experts, loff_s, seg_s, buf.at[s], xs_ref, sems.at[s], True)

    _segment_copies(j * n_experts, n_experts, loff_s, seg_s, goff_s, buf.at[slot], xs_ref, sems.at[slot],
                    True, False)

    @pl.when(j > 0)
    def _():
        drained(j - 1, 1 - slot)

    @pl.when(j == nt - 1)
    def _():
        drained(j, slot)
        zrows = EXPERT_BLOCK
        buf[0, :zrows] = jnp.zeros((zrows, buf.shape[2]), BF16)
        _segment_copies(0, n_experts, None, plen_s, poff_s, buf.at[0], xs_ref, sems.at[0], True, False)
        _segment_copies(0, n_experts, None, plen_s, poff_s, buf.at[0], xs_ref, sems.at[0], True, True)
        n_tail = xs_ref.shape[0] // zrows - nb_s[0]

        def tail_copy(i):
            dst = xs_ref.at[pl.ds(pl.multiple_of((nb_s[0] + i) * zrows, zrows), zrows)]
            return pltpu.make_async_copy(buf.at[0, pl.ds(0, zrows)], dst, sems.at[0])

        lax.fori_loop(0, n_tail, lambda i, c: (tail_copy(i).start(), c)[1], 0)
        lax.fori_loop(0, n_tail, lambda i, c: (tail_copy(i).wait(), c)[1], 0)


def _dispatch(x1s, idx, loff, seg, goff, plen, poff, n_blocks, n_rows, local_rows):
    d = x1s[0].shape[1]
    tt = TOKEN_TILE
    nt0 = x1s[0].shape[0] // tt
    nt = nt0 + x1s[1].shape[0] // tt
    e = loff.shape[0] // nt
    xa, xb = _dual_specs((tt, d), nt0)
    grid_spec = pltpu.PrefetchScalarGridSpec(
        num_scalar_prefetch=6,
        grid=(nt,),
        in_specs=[xa, xb, pl.BlockSpec((1, 2 * TOP_K, tt), lambda i, *_: (i, 0, 0))],
        out_specs=[pl.BlockSpec(memory_space=pl.ANY),
                   pl.BlockSpec((1, 8, tt), lambda i, *_: (i, 0, 0))],
        scratch_shapes=[pltpu.VMEM((2, local_rows, d), BF16), pltpu.SemaphoreType.DMA((2,))],
    )
    return pl.pallas_call(
        functools.partial(_dispatch_kernel, n_experts=e, top_k=TOP_K, nt0=nt0),
        grid_spec=grid_spec,
        out_shape=[jax.ShapeDtypeStruct((n_rows, d), BF16), jax.ShapeDtypeStruct((nt, 8, tt), I32)],
        compiler_params=_params(1, "arbitrary"),
        name="moe_dispatch",
    )(loff, seg, goff, plen, poff, n_blocks, x1s[0], x1s[1], idx)


def _combine_kernel(loff_s, seg_s, goff_s, ys_ref, pos_ref, gate_ref, xa_ref, xb_ref, g2_ref, b2_ref,
                    oa_ref, ob_ref, buf, sems, *, n_experts, top_k, alpha, nt0):
    j = pl.program_id(0)
    nt = pl.num_programs(0)
    slot = j % 2
    tt = xa_ref.shape[0]
    lr = buf.shape[1]

    def prefetch(step, s):
        buf[s] = jnp.zeros((lr, buf.shape[2]), BF16)
        _segment_copies(step * n_experts, n_experts, loff_s, seg_s, goff_s, buf.at[s], ys_ref, sems.at[s],
                        False, False)

    @pl.when(j == 0)
    def _():
        prefetch(0, 0)

    @pl.when(j + 1 < nt)
    def _():
        prefetch(j + 1, 1 - slot)

    _wait_tile(j, n_experts, loff_s, seg_s, buf.at[slot], ys_ref, sems.at[slot], False)
    pos = pos_ref[...]
    gate = gate_ref[...]
    cols = lax.broadcasted_iota(I32, (tt, lr), 1)
    wsel = jnp.zeros((tt, lr), F32)
    for k in range(top_k):
        wsel = jnp.where(cols == pos[:, k:k + 1], gate[:, k:k + 1], wsel)
    moe = _dot(wsel.astype(BF16), buf[slot])
    x1 = jnp.where(j < nt0, xa_ref[...], xb_ref[...])
    out = _layer_norm(alpha * x1 + moe, g2_ref[...], b2_ref[...])

    @pl.when(j < nt0)
    def _():
        oa_ref[...] = out

    @pl.when(j >= nt0)
    def _():
        ob_ref[...] = out


def _combine(ys, pos_tm, gate_tm, x1s, g2, b2, loff, seg, goff, local_rows, alpha):
    d = x1s[0].shape[1]
    tt = TOKEN_TILE
    nt0 = x1s[0].shape[0] // tt
    nt = nt0 + x1s[1].shape[0] // tt
    e = loff.shape[0] // nt
    xa, xb = _dual_specs((tt, d), nt0)
    grid_spec = pltpu.PrefetchScalarGridSpec(
        num_scalar_prefetch=3,
        grid=(nt,),
        in_specs=[pl.BlockSpec(memory_space=pl.ANY),
                  pl.BlockSpec((tt, 8), lambda i, *_: (i, 0)),
                  pl.BlockSpec((tt, 8), lambda i, *_: (i, 0)),
                  xa, xb,
                  pl.BlockSpec((1, d), lambda i, *_: (0, 0)),
                  pl.BlockSpec((1, d), lambda i, *_: (0, 0))],
        out_specs=list(_dual_specs((tt, d), nt0)),
        scratch_shapes=[pltpu.VMEM((2, local_rows, d), BF16), pltpu.SemaphoreType.DMA((2,))],
    )
    return pl.pallas_call(
        functools.partial(_combine_kernel, n_experts=e, top_k=TOP_K, alpha=alpha, nt0=nt0),
        grid_spec=grid_spec,
        out_shape=[jax.ShapeDtypeStruct(x.shape, F32) for x in x1s],
        compiler_params=_params(1, "arbitrary"),
        name="moe_combine",
    )(loff, seg, goff, ys, pos_tm, gate_tm, x1s[0], x1s[1], g2, b2)


def _expert_kernel(be_s, nb_s, xs_ref, wgu_ref, bgu_ref, wd_ref, bd_ref, ys_ref, wgu_bf, wd_bf):
    b = pl.program_id(0)
    active = b < nb_s[0]
    fresh = jnp.logical_or(b == 0, be_s[b] != be_s[jnp.maximum(b - 1, 0)])

    @pl.when(jnp.logical_and(active, fresh))
    def _():
        wgu_bf[...] = wgu_ref[0].astype(BF16)
        wd_bf[...] = wd_ref[0].astype(BF16)

    @pl.when(active)
    def _():
        dff = wd_ref.shape[1]
        rows = xs_ref.shape[0] // EXPERT_SPLIT
        for r in range(EXPERT_SPLIT):
            sl = pl.ds(r * rows, rows)
            gu = _dot(xs_ref[sl, :], wgu_bf[...]) + bgu_ref[0]
            gate = jnp.minimum(gu[:, :dff], SWIGLU_LIMIT)
            up = jnp.clip(gu[:, dff:], -SWIGLU_LIMIT, SWIGLU_LIMIT)
            h = (up + 1.0) * (gate * jax.nn.sigmoid(SWIGLU_ALPHA * gate))
            ys_ref[sl, :] = (_dot(h.astype(BF16), wd_bf[...]) + bd_ref[0]).astype(BF16)

    @pl.when(jnp.logical_not(active))
    def _():
        ys_ref[...] = jnp.zeros_like(ys_ref)


def _experts(xs, block_expert, n_blocks, wgu, bgu, wd, bdn):
    n_rows, d = xs.shape
    bm = EXPERT_BLOCK
    f2 = wgu.shape[2]
    dff = wd.shape[1]
    blk = lambda b, be, nb: (jnp.minimum(b, nb[0] - 1), 0)
    wsel = lambda b, be, nb: (be[b], 0, 0)
    grid_spec = pltpu.PrefetchScalarGridSpec(
        num_scalar_prefetch=2,
        grid=(n_rows // bm,),
        in_specs=[pl.BlockSpec((bm, d), blk),
                  pl.BlockSpec((1, d, f2), wsel), pl.BlockSpec((1, 1, f2), wsel),
                  pl.BlockSpec((1, dff, d), wsel), pl.BlockSpec((1, 1, d), wsel)],
        out_specs=pl.BlockSpec((bm, d), lambda b, be, nb: (b, 0)),
        scratch_shapes=[pltpu.VMEM((d, f2), BF16), pltpu.VMEM((dff, d), BF16)],
    )
    return pl.pallas_call(
        _expert_kernel,
        grid_spec=grid_spec,
        out_shape=jax.ShapeDtypeStruct((n_rows, d), BF16),
        compiler_params=_params(1, "arbitrary"),
        name="moe_experts",
    )(block_expert, n_blocks, xs, wgu, bgu, wd, bdn)


def _routed_moe(x1s, w_router, b_router, w_gate_up, b_gate_up, w_down, b_down, g2, b2, alpha):
    t = x1s[0].shape[0] + x1s[1].shape[0]
    e = w_router.shape[1]
    tt = TOKEN_TILE
    nt = t // tt
    bm = EXPERT_BLOCK
    idx, gate, cnt = _route(x1s, w_router.T, b_router.reshape(e, 1))

    cnt = cnt.reshape(nt, e)
    seg = (cnt + SEG_ALIGN - 1) // SEG_ALIGN * SEG_ALIGN
    loff = jnp.cumsum(seg, axis=1) - seg
    per_expert = jnp.sum(seg, axis=0)
    padded = (per_expert + bm - 1) // bm * bm
    blocks_end = jnp.cumsum(padded) // bm
    start = jnp.cumsum(padded) - padded
    goff = start[None, :] + jnp.cumsum(seg, axis=0) - seg
    local_rows = TOP_K * tt + e * SEG_ALIGN
    n_rows_max = -(-(TOP_K * t + (SEG_ALIGN - 1) * e * nt) // bm) * bm + e * bm
    n_blocks = blocks_end[-1:].astype(I32)
    bidx = jnp.minimum(jnp.arange(n_rows_max // bm, dtype=I32), n_blocks[0] - 1)
    block_expert = jnp.minimum(jnp.sum(bidx[:, None] >= blocks_end[None, :], axis=1), e - 1).astype(I32)
    flat = lambda a: a.reshape(-1).astype(I32)
    plen, poff = flat(padded - per_expert), flat(start + per_expert)

    xs, pos = _dispatch(x1s, idx, flat(loff), flat(seg), flat(goff), plen, poff, n_blocks, n_rows_max,
                        local_rows)
    ys = _experts(xs, block_expert, n_blocks, w_gate_up, b_gate_up.reshape(e, 1, -1),
                  w_down, b_down.reshape(e, 1, -1))
    pos_tm = jnp.swapaxes(pos, 1, 2).reshape(t, 8)
    gate_tm = jnp.swapaxes(gate, 1, 2).reshape(t, 8)
    return _combine(ys, pos_tm, gate_tm, x1s, g2, b2, flat(loff), flat(seg), flat(goff), local_rows, alpha)


def _mixer(x, apply_ln, lng, lnb, lay, consts, alpha):
    batch, seq, d = x.shape
    t = batch * seq
    fw = consts["bd"].shape[0]
    n2 = MINOR
    xn, z, x0, w = _in_proj(x.reshape(t, d), batch, seq, lng, lnb, lay["w_in"], lay["b_in"], consts["bd"],
                            lay["w_short"], lay["b_short"], apply_ln)
    c = x0.shape[1]

    n1 = seq // n2
    g = jnp.asarray(_major_matrix(n1, n1, n1, -1)).astype(BF16)
    y = _dft_major(g, z.reshape(batch, 2 * n1, n2, fw))
    yf = _dft_minor(y.reshape(batch, 2, n1, n2, fw), _twiddles(n1, n2), consts["wf"], "real",
                    scale=1.0 / math.sqrt(seq * FOURIER_GROUP_DIM))

    n = 2 * seq
    m1 = n // n2
    tw = _twiddles(m1, n2)
    kf, asum = _hyena_filter(seq, consts["bands"], lay["filt_w1"], lay["filt_b1"], lay["filt_w2"],
                             lay["filt_b2"], lay["filt_w3"], lay["filt_b3"], lay["filt_w_out"],
                             lay["filt_freq"], lay["filt_decay"])
    gk = jnp.asarray(_major_matrix(m1, m1, m1, -1, real_in=True)).astype(BF16)
    ky = _dft_major(gk, kf.reshape(1, m1, n2, c))
    kspec = _dft_minor(ky.reshape(1, 2, m1, n2, c), tw, consts["wf"], "fwd", scale=1.0 / n)
    gfw = jnp.asarray(_major_matrix(m1, m1 // 2, m1, -1)).astype(BF16)
    gin = jnp.asarray(_major_matrix(m1, m1, m1 // 2, +1)).astype(BF16)
    pairs = batch // 2
    u = _dft_major(gfw, w.reshape(pairs, m1, n2, c))
    v = _dft_minor(u.reshape(pairs, 2, m1, n2, c), tw, consts["wf"], "conv", wi_bf=consts["wi"], kf5=kspec)
    yc = _dft_major(gin, v.reshape(pairs, 2 * m1, n2, c))

    inv_norm = 1.0 / asum
    return _out_proj(yf.reshape(t, fw), yc.reshape(t, c), x0, w, xn, inv_norm,
                     lay["hyena_skip"], lay["g_fourier"], lay["g_hyena"], lay["w_out"], lay["b_out"],
                     lay["ln1_g"], lay["ln1_b"], alpha)


def kernel(x_prompt, x_sample, ln_in_g, ln_in_b, w_in, b_in, w_short, b_short, filt_w1, filt_b1, filt_w2, filt_b2, filt_w3, filt_b3, filt_w_out, filt_freq, filt_decay, hyena_skip, g_fourier, g_hyena, w_out, b_out, ln1_g, ln1_b, w_router, b_router, w_gate_up, b_gate_up, w_down, b_down, ln2_g, ln2_b):
    depth, d, in_width = w_in.shape
    fw = g_fourier.shape[1]
    alpha = (2.0 * depth) ** 0.25
    row = lambda a: a.reshape(1, -1)

    gd = FOURIER_GROUP_DIM
    cg, sg = _cs(np.outer(np.arange(gd), np.arange(gd)), gd)
    eye = np.eye(fw // gd)
    bd = np.concatenate([np.kron(eye, cg), -np.kron(eye, sg)], axis=1).astype(np.float32)
    bands = jnp.linspace(1e-4, FILTER_BANDS - 1, FILTER_BANDS, dtype=F32)
    bands_row = jnp.zeros((1, LANES), F32).at[0, 1:1 + FILTER_BANDS].set(bands)
    bands_row = bands_row.at[0, 1 + FILTER_BANDS:1 + 2 * FILTER_BANDS].set(bands)
    consts = dict(bd=jnp.asarray(bd).astype(BF16), bands=bands_row,
                  wf=jnp.asarray(_minor_matrix(MINOR, -1)).astype(BF16),
                  wi=jnp.asarray(_minor_matrix(MINOR, +1)).astype(BF16))

    xs = [x_prompt, x_sample]
    for l in range(depth):
        emb = filt_w1.shape[1]
        lay = dict(
            w_in=w_in[l].astype(BF16), b_in=row(b_in[l]), w_short=w_short[l], b_short=row(b_short[l]),
            filt_w1=jnp.zeros((LANES, filt_w1.shape[2]), F32).at[:emb].set(filt_w1[l]),
            filt_b1=row(filt_b1[l]), filt_w2=filt_w2[l], filt_b2=row(filt_b2[l]), filt_w3=filt_w3[l],
            filt_b3=row(filt_b3[l]), filt_w_out=filt_w_out[l], filt_freq=row(filt_freq[l]),
            filt_decay=filt_decay[l], hyena_skip=row(hyena_skip[l]), g_fourier=row(g_fourier[l]),
            g_hyena=row(g_hyena[l]), w_out=w_out[l].astype(BF16), b_out=row(b_out[l]),
            ln1_g=row(ln1_g[l]), ln1_b=row(ln1_b[l]))
        mixed = [_mixer(x, l == 0, row(ln_in_g), row(ln_in_b), lay, consts, alpha) for x in xs]
        outs = _routed_moe(mixed, w_router[l], b_router[l], w_gate_up[l], b_gate_up[l], w_down[l],
                           b_down[l], row(ln2_g[l]), row(ln2_b[l]), alpha)
        xs = [o.reshape(x.shape) for o, x in zip(outs, xs)]
    return (xs[0], xs[1])
```

```python
import functools
import math

import numpy as np
import jax
import jax.numpy as jnp
from jax import lax
from jax.experimental import pallas as pl
from jax.experimental.pallas import tpu as pltpu

F32 = jnp.float32
BF16 = jnp.bfloat16
I32 = jnp.int32

LN_EPS = 1e-5
RMS_EPS = 1e-6
SWIGLU_LIMIT = 7.0
SWIGLU_ALPHA = 1.702
TOP_K = 4
FOURIER_GROUP_DIM = 64
FILTER_BANDS = 16

LANES = 128
MINOR = 128
ROW_TILE = 512
IN_TILE = 512
MAJOR_BLOCK_BYTES = 3 * 1024 * 1024
TOKEN_TILE = 512
SEG_ALIGN = 16
EXPERT_BLOCK = 512
VMEM_LIMIT = 56 * 1024 * 1024


def _params(n_grid, semantics="parallel"):
    return pltpu.CompilerParams(dimension_semantics=(semantics,) * n_grid,
                                vmem_limit_bytes=VMEM_LIMIT)


def _dot(a, b):
    return jnp.dot(a, b, preferred_element_type=F32)


def _split(a):
    hi = a.astype(BF16)
    lo = (a - hi.astype(F32)).astype(BF16)
    return hi, lo


def _dot3(a, b):
    ah, al = _split(a)
    bh, bl = _split(b)
    return _dot(ah, bh) + _dot(ah, bl) + _dot(al, bh)


def _layer_norm(x, g, b):
    mu = jnp.mean(x, axis=-1, keepdims=True)
    xc = x - mu
    var = jnp.mean(xc * xc, axis=-1, keepdims=True)
    return xc * lax.rsqrt(var + LN_EPS) * g + b


def _rms_norm(x, g):
    return x * lax.rsqrt(jnp.mean(x * x, axis=-1, keepdims=True) + RMS_EPS) * g


def _in_kernel(x_ref, xp_ref, xq_ref, lng_ref, lnb_ref, win_ref, bin_ref, bd_ref, wsh_ref, bsh_ref,
               z_ref, x0_ref, w_ref, *, tiles_per_seq, apply_ln, fw):
    it = pl.program_id(0) % tiles_per_seq
    tm = x_ref.shape[0]

    def norm(x):
        return _layer_norm(x, lng_ref[...], lnb_ref[...]) if apply_ln else x

    xn = norm(x_ref[...])
    halo = norm(jnp.concatenate([xp_ref[...], xq_ref[...]], axis=0))
    proj = _dot(xn.astype(BF16), win_ref[...]) + bin_ref[...]
    projh = _dot(halo.astype(BF16), win_ref[:, fw:]) + bin_ref[:, fw:]

    z = _dot(proj[:, :fw].astype(BF16), bd_ref[...])
    z_ref[0, 0] = z[:, :fw]
    z_ref[0, 1] = z[:, fw:]

    ph = proj[:, fw:]
    prev_row = jnp.where(it > 0, projh[7:8], 0.0)
    next_row = jnp.where(it < tiles_per_seq - 1, projh[8:9], 0.0)
    rows = lax.broadcasted_iota(I32, ph.shape, 0)
    up = jnp.where(rows == 0, prev_row, pltpu.roll(ph, 1, 0))
    dn = jnp.where(rows == tm - 1, next_row, pltpu.roll(ph, tm - 1, 0))
    wsh = wsh_ref[...]
    uc = up * wsh[0:1] + ph * wsh[1:2] + dn * wsh[2:3] + bsh_ref[...]
    hw = uc.shape[1] // 3
    x0_ref[...] = uc[:, :hw]
    w_ref[...] = uc[:, 2 * hw:] * uc[:, hw:2 * hw]


def _in_proj(x2d, batch, seq, lng, lnb, win_bf, b_in, bd_bf, w_short, b_short, apply_ln):
    t, d = x2d.shape
    fw = bd_bf.shape[0]
    inw = win_bf.shape[1]
    hw = (inw - fw) // 3
    tm = IN_TILE
    tps = seq // tm
    r8 = tm // 8
    nblk8 = t // 8
    const = lambda i: (0, 0)
    kern = functools.partial(_in_kernel, tiles_per_seq=tps, apply_ln=apply_ln, fw=fw)
    return pl.pallas_call(
        kern,
        grid=(t // tm,),
        in_specs=[
            pl.BlockSpec((tm, d), lambda i: (i, 0)),
            pl.BlockSpec((8, d), lambda i: (jnp.maximum(i * r8 - 1, 0), 0)),
            pl.BlockSpec((8, d), lambda i: (jnp.minimum((i + 1) * r8, nblk8 - 1), 0)),
            pl.BlockSpec((1, d), const), pl.BlockSpec((1, d), const),
            pl.BlockSpec((d, inw), const), pl.BlockSpec((1, inw), const),
            pl.BlockSpec((fw, 2 * fw), const),
            pl.BlockSpec((3, 3 * hw), const), pl.BlockSpec((1, 3 * hw), const),
        ],
        out_specs=[
            pl.BlockSpec((1, 2, tm, fw), lambda i: (i // tps, 0, i % tps, 0)),
            pl.BlockSpec((tm, hw), lambda i: (i, 0)),
            pl.BlockSpec((tm, hw), lambda i: (i, 0)),
        ],
        out_shape=[
            jax.ShapeDtypeStruct((batch, 2, seq, fw), F32),
            jax.ShapeDtypeStruct((t, hw), F32),
            jax.ShapeDtypeStruct((t, hw), F32),
        ],
        compiler_params=_params(1),
        name="in_proj",
    )(x2d, x2d, x2d, lng, lnb, win_bf, b_in, bd_bf, w_short, b_short)


FILTER_HALO = 8


def _filter_kernel(bands_ref, w1_ref, b1_ref, w2_ref, b2_ref, w3_ref, b3_ref, wl_ref, fr_ref, dec_ref,
                   k_ref, asum_ref, tc_ref, ts_ref, kf_buf, kb_buf, sems, *, seq):
    i = pl.program_id(0)
    nsteps = pl.num_programs(0)
    tile, c = kf_buf.shape
    ext = tc_ref.shape[0]
    rep = c // LANES
    w_unit = 2.0 * math.pi / seq
    bands = bands_ref[...]

    @pl.when(i == 0)
    def _():
        arg = bands * (w_unit * lax.broadcasted_iota(I32, (ext, LANES), 0).astype(F32))
        tc_ref[...] = jnp.cos(arg)
        ts_ref[...] = jnp.sin(arg)
        asum_ref[...] = jnp.zeros_like(asum_ref)

    j0 = i * tile
    base = bands * (w_unit * j0.astype(F32))
    c0, s0 = jnp.cos(base), jnp.sin(base)
    tc, ts = tc_ref[...], ts_ref[...]
    cosf = c0 * tc - s0 * ts
    sinf = s0 * tc + c0 * ts
    j = j0 + lax.broadcasted_iota(I32, (ext, LANES), 0)
    t = j.astype(F32) * (1.0 / (seq - 1))
    lane = lax.broadcasted_iota(I32, (ext, LANES), 1)
    feat = jnp.where(lane == 0, t,
                     jnp.where(lane <= FILTER_BANDS, cosf,
                               jnp.where(lane <= 2 * FILTER_BANDS, -sinf, 0.0)))
    fr = fr_ref[...]
    h = jnp.sin(fr * (_dot3(feat, w1_ref[...]) + b1_ref[...]))
    h = jnp.sin(fr * (_dot3(h, w2_ref[...]) + b2_ref[...]))
    h = jnp.sin(fr * (_dot3(h, w3_ref[...]) + b3_ref[...]))
    hl = _dot3(h, wl_ref[...])
    t4 = jnp.concatenate([t] * rep, axis=1)
    j4 = jnp.concatenate([j] * rep, axis=1)
    dec = jnp.abs(dec_ref[...])
    hf = hl[:, :c] * jnp.exp(-t4 * dec[0:1])
    hb = hl[:, c:] * jnp.exp(-t4 * dec[1:2])
    hb = jnp.where(j4 < seq, hb, 0.0)
    asum_ref[...] += jnp.sum(jnp.abs(hf[:tile]) + jnp.abs(hb[:tile]), axis=0, keepdims=True)

    rev = (lax.broadcasted_iota(I32, (tile, ext), 1) == tile - lax.broadcasted_iota(I32, (tile, ext), 0))
    rev = jnp.where(rev, 1.0, 0.0).astype(BF16)
    b1 = hb.astype(BF16)
    r1 = hb - b1.astype(F32)
    b2 = r1.astype(BF16)
    b3 = (r1 - b2.astype(F32)).astype(BF16)
    kb = _dot(rev, b1) + _dot(rev, b2) + _dot(rev, b3)

    def copies():
        return (pltpu.make_async_copy(kf_buf, k_ref.at[pl.ds(pl.multiple_of(j0, tile), tile)], sems.at[0]),
                pltpu.make_async_copy(kb_buf, k_ref.at[pl.ds(pl.multiple_of(2 * seq - j0 - tile, tile), tile)],
                                      sems.at[1]))

    @pl.when(i > 0)
    def _():
        for cp in copies():
            cp.wait()

    kf_buf[...] = jnp.where(j4[:tile] == 0, hf[:tile] + hb[:tile], hf[:tile])
    kb_buf[...] = kb
    for cp in copies():
        cp.start()

    @pl.when(i == nsteps - 1)
    def _():
        for cp in copies():
            cp.wait()


def _hyena_filter(seq, bands_row, w1p, b1, w2, b2, w3, b3, wl, freq, decay):
    c = decay.shape[1]
    tile = ROW_TILE
    ext = tile + FILTER_HALO
    full = lambda a: pl.BlockSpec(a.shape, lambda i: (0,) * a.ndim)
    args = (bands_row, w1p, b1, w2, b2, w3, b3, wl, freq, decay)
    return pl.pallas_call(
        functools.partial(_filter_kernel, seq=seq),
        grid=(seq // tile,),
        in_specs=[full(a) for a in args],
        out_specs=[pl.BlockSpec(memory_space=pl.ANY), pl.BlockSpec((1, c), lambda i: (0, 0))],
        out_shape=[jax.ShapeDtypeStruct((2 * seq, c), F32), jax.ShapeDtypeStruct((1, c), F32)],
        scratch_shapes=[pltpu.VMEM((ext, LANES), F32), pltpu.VMEM((ext, LANES), F32),
                        pltpu.VMEM((tile, c), F32), pltpu.VMEM((tile, c), F32),
                        pltpu.SemaphoreType.DMA((2,))],
        compiler_params=_params(1, "arbitrary"),
        name="hyena_filter",
    )(*args)


def _major_kernel(g_ref, x_ref, o_ref):
    g = g_ref[...]
    bsz, k, sub, lanes = x_ref.shape
    m = o_ref.shape[1]
    x2 = x_ref.reshape(bsz * k * sub, lanes)
    o2 = o_ref.reshape(bsz * m * sub, lanes)
    for i in range(bsz):
        for b in range(sub):
            xb = x2[pl.ds(i * k * sub + b, k, stride=sub), :]
            o2[pl.ds(i * m * sub + b, m, stride=sub), :] = _dot(g, xb.astype(BF16))


def _dft_major(g_bf, x4):
    b, k, n2, c = x4.shape
    m = g_bf.shape[0]
    sub = 8
    bsz = max(1, min(b, MAJOR_BLOCK_BYTES // ((k + m) * sub * LANES * 4)))
    while b % bsz:
        bsz -= 1
    return pl.pallas_call(
        _major_kernel,
        grid=(b // bsz, n2 // sub, c // LANES),
        in_specs=[pl.BlockSpec((m, k), lambda i, j, l: (0, 0)),
                  pl.BlockSpec((bsz, k, sub, LANES), lambda i, j, l: (i, 0, j, l))],
        out_specs=pl.BlockSpec((bsz, m, sub, LANES), lambda i, j, l: (i, 0, j, l)),
        out_shape=jax.ShapeDtypeStruct((b, m, n2, c), F32),
        compiler_params=_params(3),
        name="dft_major",
    )(g_bf, x4)


def _minor_kernel(*refs, mode, scale, k1b):
    if mode == "conv":
        y_ref, tw_ref, wf_ref, wi_ref, kf_ref, o_ref = refs
    else:
        y_ref, tw_ref, wf_ref, o_ref = refs
    n2 = y_ref.shape[3]
    ct = y_ref.shape[4]
    rep = ct // LANES
    for j in range(k1b):
        yr = y_ref[0, 0, j]
        yi = y_ref[0, 1, j]
        cs = jnp.concatenate([tw_ref[0, j]] * rep, axis=1)
        sn = jnp.concatenate([tw_ref[1, j]] * rep, axis=1)
        a = jnp.concatenate([yr * cs + yi * sn, yi * cs - yr * sn], axis=0).astype(BF16)
        if mode == "real":
            o_ref[0, :, j, :] = _dot(wf_ref[:n2], a) * scale
            continue
        z = _dot(wf_ref[...], a)
        zr, zi = z[:n2], z[n2:]
        if mode == "fwd":
            o_ref[0, 0, j] = zr * scale
            o_ref[0, 1, j] = zi * scale
            continue
        kr = kf_ref[0, 0, j]
        ki = kf_ref[0, 1, j]
        p = jnp.concatenate([zr * kr - zi * ki, zr * ki + zi * kr], axis=0).astype(BF16)
        q = _dot(wi_ref[...], p)
        qr, qi = q[:n2], q[n2:]
        o_ref[0, 0, j] = qr * cs - qi * sn
        o_ref[0, 1, j] = qi * cs + qr * sn


def _dft_minor(y5, tw, wf_bf, mode, scale=1.0, wi_bf=None, kf5=None):
    b, _, n1, n2, c = y5.shape
    k1b = min(8, n1)
    ct = c
    grid = (b, n1 // k1b)
    dspec = pl.BlockSpec((1, 2, k1b, n2, ct), lambda i, j: (i, 0, j, 0, 0))
    in_specs = [dspec,
                pl.BlockSpec((2, k1b, n2, LANES), lambda i, j: (0, j, 0, 0)),
                pl.BlockSpec(wf_bf.shape, lambda i, j: (0, 0))]
    args = [y5, tw, wf_bf]
    if mode == "conv":
        in_specs += [pl.BlockSpec(wi_bf.shape, lambda i, j: (0, 0)),
                     pl.BlockSpec((1, 2, k1b, n2, ct), lambda i, j: (0, 0, j, 0, 0))]
        args += [wi_bf, kf5]
    if mode == "real":
        out_spec = pl.BlockSpec((1, n2, k1b, ct), lambda i, j: (i, 0, j, 0))
        out_shape = jax.ShapeDtypeStruct((b, n2, n1, c), F32)
    else:
        out_spec = dspec
        out_shape = jax.ShapeDtypeStruct((b, 2, n1, n2, c), F32)
    return pl.pallas_call(
        functools.partial(_minor_kernel, mode=mode, scale=scale, k1b=k1b),
        grid=grid, in_specs=in_specs, out_specs=out_spec, out_shape=out_shape,
        compiler_params=_params(2),
        name="dft_minor_" + mode,
    )(*args)


def _cs(num, den):
    ang = 2.0 * np.pi * (num % den) / den
    return np.cos(ang), np.sin(ang)


def _complex_block(c, s, sign):
    return np.block([[c, -sign * s], [sign * s, c]]).astype(np.float32)


def _major_matrix(n1, a_in, a_out, sign, real_in=False):
    c, s = _cs(np.outer(np.arange(a_out), np.arange(a_in)), n1)
    g = _complex_block(c, s, sign)
    return g[:, :a_in] if real_in else g


def _minor_matrix(n2, sign):
    c, s = _cs(np.outer(np.arange(n2), np.arange(n2)), n2)
    return _complex_block(c, s, sign)


def _twiddles(n1, n2):
    c, s = _cs(np.outer(np.arange(n1), np.arange(n2)), n1 * n2)
    tw = jnp.asarray(np.stack([c, s]).astype(np.float32))
    return jnp.broadcast_to(tw[..., None], (2, n1, n2, LANES))


def _out_kernel(yf_ref, yc_ref, x0_ref, w_ref, x_ref, lng_ref, lnb_ref, inv_ref, skip_ref, gf_ref, gh_ref,
                wo_ref, bo_ref, g1_ref, b1_ref, wr_ref, br_ref, x1_ref, idx_ref, gate_ref, cnt_ref,
                *, alpha, apply_ln, top_k):
    c = yc_ref.shape[1]
    w = w_ref[...]
    yh = x0_ref[...] * (yc_ref[...] * inv_ref[...] + skip_ref[...] * w)
    mf = _rms_norm(yf_ref[...], gf_ref[...]).astype(BF16)
    mh = _rms_norm(yh, gh_ref[...]).astype(BF16)
    m = _dot(mf, wo_ref[:c]) + _dot(mh, wo_ref[c:]) + bo_ref[...]
    xn = _layer_norm(x_ref[...], lng_ref[...], lnb_ref[...]) if apply_ln else x_ref[...]
    x1 = _layer_norm(alpha * xn + m, g1_ref[...], b1_ref[...])
    x1_ref[...] = x1
    _route_tile(x1, wr_ref[...], br_ref[...], idx_ref, gate_ref, cnt_ref, top_k)


def _out_proj(yf, yc, x0, w, x, lng, lnb, inv_norm, skip, gf, gh, wo_bf, bo, g1, b1, wr_t, br_col, alpha,
              apply_ln):
    t, d = x.shape
    c = yc.shape[1]
    e = wr_t.shape[0]
    tm = TOKEN_TILE
    nt = t // tm
    row = lambda i: (i, 0)
    const = lambda i: (0, 0)
    tile3 = lambda i: (i, 0, 0)
    return pl.pallas_call(
        functools.partial(_out_kernel, alpha=alpha, apply_ln=apply_ln, top_k=TOP_K),
        grid=(nt,),
        in_specs=[
            pl.BlockSpec((tm, c), row),
            pl.BlockSpec((tm, c), row), pl.BlockSpec((tm, c), row), pl.BlockSpec((tm, c), row),
            pl.BlockSpec((tm, d), row),
            pl.BlockSpec((1, d), const), pl.BlockSpec((1, d), const),
            pl.BlockSpec((1, c), const), pl.BlockSpec((1, c), const),
            pl.BlockSpec((1, c), const), pl.BlockSpec((1, c), const),
            pl.BlockSpec(wo_bf.shape, const), pl.BlockSpec((1, d), const),
            pl.BlockSpec((1, d), const), pl.BlockSpec((1, d), const),
            pl.BlockSpec((e, d), const), pl.BlockSpec((e, 1), const),
        ],
        out_specs=[pl.BlockSpec((tm, d), row),
                   pl.BlockSpec((1, 2 * TOP_K, tm), tile3),
                   pl.BlockSpec((1, 8, tm), tile3),
                   pl.BlockSpec((1, e, 1), tile3)],
        out_shape=[jax.ShapeDtypeStruct((t, d), F32),
                   jax.ShapeDtypeStruct((nt, 2 * TOP_K, tm), I32),
                   jax.ShapeDtypeStruct((nt, 8, tm), F32),
                   jax.ShapeDtypeStruct((nt, e, 1), I32)],
        compiler_params=_params(1),
        name="out_proj",
    )(yf, yc, x0, w, x, lng, lnb, inv_norm, skip, gf, gh, wo_bf, bo, g1, b1, wr_t, br_col)


def _route_tile(x1, wr, br, idx_ref, gate_ref, cnt_ref, top_k):
    tt = x1.shape[0]
    e = wr.shape[0]
    nt = (((1,), (1,)), ((), ()))
    xh, xl = _split(x1)
    wh, wl = _split(wr)
    logits = (lax.dot_general(wh, xh, nt, preferred_element_type=F32)
              + lax.dot_general(wh, xl, nt, preferred_element_type=F32)
              + lax.dot_general(wl, xh, nt, preferred_element_type=F32)) + br
    iota_e = lax.broadcasted_iota(I32, (e, tt), 0).astype(F32)
    l = logits
    tops, idxs, hots = [], [], []
    for _ in range(top_k):
        m = jnp.max(l, axis=0, keepdims=True)
        idx = jnp.min(jnp.where(l == m, iota_e, float(e)), axis=0, keepdims=True)
        hot = iota_e == idx
        l = jnp.where(hot, -jnp.inf, l)
        tops.append(m)
        idxs.append(idx)
        hots.append(hot)
    ex = [jnp.exp(m - tops[0]) for m in tops]
    den = ex[0]
    for v in ex[1:]:
        den = den + v
    gates = [v / den for v in ex]
    hot_all = jnp.zeros((e, tt), F32)
    for hot in hots:
        hot_all = jnp.where(hot, 1.0, hot_all)
    upper = (lax.broadcasted_iota(I32, (tt, tt), 0) < lax.broadcasted_iota(I32, (tt, tt), 1))
    before = _dot(hot_all.astype(BF16), jnp.where(upper, 1.0, 0.0).astype(BF16))
    ranks = [jnp.sum(jnp.where(hot, before, 0.0), axis=0, keepdims=True).astype(I32) for hot in hots]
    pad = 8 - top_k
    idx_ref[0] = jnp.concatenate([v.astype(I32) for v in idxs] + ranks, axis=0)
    gate_ref[0] = jnp.concatenate(gates + [jnp.zeros((pad, tt), F32)], axis=0)
    cnt_ref[0] = jnp.sum(hot_all, axis=1, keepdims=True).astype(I32)


def _dual_specs(shape, nt0):
    first = pl.BlockSpec(shape, lambda i, *_: (jnp.minimum(i, nt0 - 1),) + (0,) * (len(shape) - 1))
    second = pl.BlockSpec(shape, lambda i, *_: (jnp.maximum(i - nt0, 0),) + (0,) * (len(shape) - 1))
    return first, second


SEG_PIECES = int(math.log2(TOKEN_TILE // SEG_ALIGN)) + 1


def _segment_copies(first, count, loff_s, len_s, goff_s, local, remote, sem, to_remote, wait):
    def body(e, carry):
        s = first + e
        n = len_s[s] // SEG_ALIGN
        lo = 0 if loff_s is None else loff_s[s]
        go = goff_s[s]
        for p in reversed(range(SEG_PIECES)):
            size = SEG_ALIGN << p
            start = ((n >> (p + 1)) << (p + 1)) * SEG_ALIGN

            @pl.when(((n >> p) & 1) == 1)
            def _():
                lref = local.at[pl.ds(pl.multiple_of(lo + start, SEG_ALIGN), size)]
                rref = remote.at[pl.ds(pl.multiple_of(go + start, SEG_ALIGN), size)]
                cp = (pltpu.make_async_copy(lref, rref, sem) if to_remote
                      else pltpu.make_async_copy(rref, lref, sem))
                if wait:
                    cp.wait()
                else:
                    cp.start()
        return carry

    lax.fori_loop(0, count, body, 0)


def _wait_tile(step, n_experts, loff_s, len_s, local, remote, sem, to_remote):
    last = step * n_experts + n_experts - 1
    n = (loff_s[last] + len_s[last]) // SEG_ALIGN
    for p in range((local.shape[0] // SEG_ALIGN).bit_length()):
        size = SEG_ALIGN << p

        @pl.when(((n >> p) & 1) == 1)
        def _():
            lref = local.at[pl.ds(0, size)]
            rref = remote.at[pl.ds(0, size)]
            cp = (pltpu.make_async_copy(lref, rref, sem) if to_remote
                  else pltpu.make_async_copy(rref, lref, sem))
            cp.wait()


def _dispatch_kernel(loff_s, seg_s, goff_s, plen_s, poff_s, nb_s, xa_ref, xb_ref, idx_ref, xs_ref, pos_ref,
                     buf, sems, *, n_experts, top_k, nt0):
    j = pl.program_id(0)
    nt = pl.num_programs(0)
    slot = j % 2
    tt = xa_ref.shape[0]
    lr = buf.shape[1]
    idx = idx_ref[0]
    tope = idx[:top_k]
    base = jnp.zeros_like(tope)
    for e in range(n_experts):
        base = jnp.where(tope == e, loff_s[j * n_experts + e], base)
    pos = base + idx[top_k:]
    pos_ref[0] = jnp.concatenate([pos, jnp.zeros((8 - top_k, tt), I32)], axis=0)
    rows = lax.broadcasted_iota(I32, (lr, tt), 0)
    perm = jnp.zeros((lr, tt), F32)
    for k in range(top_k):
        perm = jnp.where(rows == pos[k:k + 1], 1.0, perm)
    x = jnp.where(j < nt0, xa_ref[...], xb_ref[...])
    buf[slot] = _dot(perm.astype(BF16), x.astype(BF16)).astype(BF16)

    def drained(step, s):
        _wait_tile(step, n_experts, loff_s, seg_s, buf.at[s], xs_ref, sems.at[s], True)

    _segment_copies(j * n_experts, n_experts, loff_s, seg_s, goff_s, buf.at[slot], xs_ref, sems.at[slot],
                    True, False)

    @pl.when(j > 0)
    def _():
        drained(j - 1, 1 - slot)

    @pl.when(j == nt - 1)
    def _():
        drained(j, slot)
        zrows = EXPERT_BLOCK
        buf[0, :zrows] = jnp.zeros((zrows, buf.shape[2]), BF16)
        _segment_copies(0, n_experts, None, plen_s, poff_s, buf.at[0], xs_ref, sems.at[0], True, False)
        _segment_copies(0, n_experts, None, plen_s, poff_s, buf.at[0], xs_ref, sems.at[0], True, True)
        n_tail = xs_ref.shape[0] // zrows - nb_s[0]

        def tail_copy(i):
            dst = xs_ref.at[pl.ds(pl.multiple_of((nb_s[0] + i) * zrows, zrows), zrows)]
            return pltpu.make_async_copy(buf.at[0, pl.ds(0, zrows)], dst, sems.at[0])

        lax.fori_loop(0, n_tail, lambda i, c: (tail_copy(i).start(), c)[1], 0)
        lax.fori_loop(0, n_tail, lambda i, c: (tail_copy(i).wait(), c)[1], 0)


def _dispatch(x1s, idx, loff, seg, goff, plen, poff, n_blocks, n_rows, local_rows):
    d = x1s[0].shape[1]
    tt = TOKEN_TILE
    nt0 = x1s[0].shape[0] // tt
    nt = nt0 + x1s[1].shape[0] // tt
    e = loff.shape[0] // nt
    xa, xb = _dual_specs((tt, d), nt0)
    grid_spec = pltpu.PrefetchScalarGridSpec(
        num_scalar_prefetch=6,
        grid=(nt,),
        in_specs=[xa, xb, pl.BlockSpec((1, 2 * TOP_K, tt), lambda i, *_: (i, 0, 0))],
        out_specs=[pl.BlockSpec(memory_space=pl.ANY),
                   pl.BlockSpec((1, 8, tt), lambda i, *_: (i, 0, 0))],
        scratch_shapes=[pltpu.VMEM((2, local_rows, d), BF16), pltpu.SemaphoreType.DMA((2,))],
    )
    return pl.pallas_call(
        functools.partial(_dispatch_kernel, n_experts=e, top_k=TOP_K, nt0=nt0),
        grid_spec=grid_spec,
        out_shape=[jax.ShapeDtypeStruct((n_rows, d), BF16), jax.ShapeDtypeStruct((nt, 8, tt), I32)],
        compiler_params=_params(1, "arbitrary"),
        name="moe_dispatch",
    )(loff, seg, goff, plen, poff, n_blocks, x1s[0], x1s[1], idx)


def _combine_kernel(loff_s, seg_s, goff_s, ys_ref, pos_ref, gate_ref, xa_ref, xb_ref, g2_ref, b2_ref,
                    oa_ref, ob_ref, buf, sems, *, n_experts, top_k, alpha, nt0):
    j = pl.program_id(0)
    nt = pl.num_programs(0)
    slot = j % 2
    tt = xa_ref.shape[0]
    lr = buf.shape[1]

    def prefetch(step, s):
        buf[s] = jnp.zeros((lr, buf.shape[2]), BF16)
        _segment_copies(step * n_experts, n_experts, loff_s, seg_s, goff_s, buf.at[s], ys_ref, sems.at[s],
                        False, False)

    @pl.when(j == 0)
    def _():
        prefetch(0, 0)

    @pl.when(j + 1 < nt)
    def _():
        prefetch(j + 1, 1 - slot)

    _wait_tile(j, n_experts, loff_s, seg_s, buf.at[slot], ys_ref, sems.at[slot], False)
    pos = pos_ref[...]
    gate = gate_ref[...]
    cols = lax.broadcasted_iota(I32, (tt, lr), 1)
    wsel = jnp.zeros((tt, lr), F32)
    for k in range(top_k):
        wsel = jnp.where(cols == pos[:, k:k + 1], gate[:, k:k + 1], wsel)
    moe = _dot(wsel.astype(BF16), buf[slot])
    x1 = jnp.where(j < nt0, xa_ref[...], xb_ref[...])
    out = _layer_norm(alpha * x1 + moe, g2_ref[...], b2_ref[...])

    @pl.when(j < nt0)
    def _():
        oa_ref[...] = out

    @pl.when(j >= nt0)
    def _():
        ob_ref[...] = out


def _combine(ys, pos_tm, gate_tm, x1s, g2, b2, loff, seg, goff, local_rows, alpha):
    d = x1s[0].shape[1]
    tt = TOKEN_TILE
    nt0 = x1s[0].shape[0] // tt
    nt = nt0 + x1s[1].shape[0] // tt
    e = loff.shape[0] // nt
    xa, xb = _dual_specs((tt, d), nt0)
    grid_spec = pltpu.PrefetchScalarGridSpec(
        num_scalar_prefetch=3,
        grid=(nt,),
        in_specs=[pl.BlockSpec(memory_space=pl.ANY),
                  pl.BlockSpec((tt, 8), lambda i, *_: (i, 0)),
                  pl.BlockSpec((tt, 8), lambda i, *_: (i, 0)),
                  xa, xb,
                  pl.BlockSpec((1, d), lambda i, *_: (0, 0)),
                  pl.BlockSpec((1, d), lambda i, *_: (0, 0))],
        out_specs=list(_dual_specs((tt, d), nt0)),
        scratch_shapes=[pltpu.VMEM((2, local_rows, d), BF16), pltpu.SemaphoreType.DMA((2,))],
    )
    return pl.pallas_call(
        functools.partial(_combine_kernel, n_experts=e, top_k=TOP_K, alpha=alpha, nt0=nt0),
        grid_spec=grid_spec,
        out_shape=[jax.ShapeDtypeStruct(x.shape, F32) for x in x1s],
        compiler_params=_params(1, "arbitrary"),
        name="moe_combine",
    )(loff, seg, goff, ys, pos_tm, gate_tm, x1s[0], x1s[1], g2, b2)


def _expert_kernel(be_s, nb_s, xs_ref, wgu_ref, bgu_ref, wd_ref, bd_ref, ys_ref, wgu_bf, wd_bf):
    b = pl.program_id(0)
    active = b < nb_s[0]
    fresh = jnp.logical_or(b == 0, be_s[b] != be_s[jnp.maximum(b - 1, 0)])

    @pl.when(jnp.logical_and(active, fresh))
    def _():
        wgu_bf[...] = wgu_ref[0].astype(BF16)
        wd_bf[...] = wd_ref[0].astype(BF16)

    @pl.when(active)
    def _():
        dff = wd_ref.shape[1]
        gu = _dot(xs_ref[...], wgu_bf[...]) + bgu_ref[0]
        gate = jnp.minimum(gu[:, :dff], SWIGLU_LIMIT)
        up = jnp.clip(gu[:, dff:], -SWIGLU_LIMIT, SWIGLU_LIMIT)
        h = (up + 1.0) * (gate * jax.nn.sigmoid(SWIGLU_ALPHA * gate))
        ys_ref[...] = (_dot(h.astype(BF16), wd_bf[...]) + bd_ref[0]).astype(BF16)

    @pl.when(jnp.logical_not(active))
    def _():
        ys_ref[...] = jnp.zeros_like(ys_ref)


def _experts(xs, block_expert, n_blocks, wgu, bgu, wd, bdn):
    n_rows, d = xs.shape
    bm = EXPERT_BLOCK
    f2 = wgu.shape[2]
    dff = wd.shape[1]
    blk = lambda b, be, nb: (jnp.minimum(b, nb[0] - 1), 0)
    wsel = lambda b, be, nb: (be[b], 0, 0)
    grid_spec = pltpu.PrefetchScalarGridSpec(
        num_scalar_prefetch=2,
        grid=(n_rows // bm,),
        in_specs=[pl.BlockSpec((bm, d), blk),
                  pl.BlockSpec((1, d, f2), wsel), pl.BlockSpec((1, 1, f2), wsel),
                  pl.BlockSpec((1, dff, d), wsel), pl.BlockSpec((1, 1, d), wsel)],
        out_specs=pl.BlockSpec((bm, d), lambda b, be, nb: (b, 0)),
        scratch_shapes=[pltpu.VMEM((d, f2), BF16), pltpu.VMEM((dff, d), BF16)],
    )
    return pl.pallas_call(
        _expert_kernel,
        grid_spec=grid_spec,
        out_shape=jax.ShapeDtypeStruct((n_rows, d), BF16),
        compiler_params=_params(1, "arbitrary"),
        name="moe_experts",
    )(block_expert, n_blocks, xs, wgu, bgu, wd, bdn)


def _routed_moe(routed, w_gate_up, b_gate_up, w_down, b_down, g2, b2, alpha):
    x1s = [r[0] for r in routed]
    idx, gate, cnt = (jnp.concatenate([r[i] for r in routed], axis=0) for i in (1, 2, 3))
    t = x1s[0].shape[0] + x1s[1].shape[0]
    e = cnt.shape[1]
    tt = TOKEN_TILE
    nt = t // tt
    bm = EXPERT_BLOCK

    cnt = cnt.reshape(nt, e)
    seg = (cnt + SEG_ALIGN - 1) // SEG_ALIGN * SEG_ALIGN
    loff = jnp.cumsum(seg, axis=1) - seg
    per_expert = jnp.sum(seg, axis=0)
    padded = (per_expert + bm - 1) // bm * bm
    blocks_end = jnp.cumsum(padded) // bm
    start = jnp.cumsum(padded) - padded
    goff = start[None, :] + jnp.cumsum(seg, axis=0) - seg
    local_rows = TOP_K * tt + e * SEG_ALIGN
    n_rows_max = -(-(TOP_K * t + (SEG_ALIGN - 1) * e * nt) // bm) * bm + e * bm
    n_blocks = blocks_end[-1:].astype(I32)
    bidx = jnp.minimum(jnp.arange(n_rows_max // bm, dtype=I32), n_blocks[0] - 1)
    block_expert = jnp.minimum(jnp.sum(bidx[:, None] >= blocks_end[None, :], axis=1), e - 1).astype(I32)
    flat = lambda a: a.reshape(-1).astype(I32)
    plen, poff = flat(padded - per_expert), flat(start + per_expert)

    xs, pos = _dispatch(x1s, idx, flat(loff), flat(seg), flat(goff), plen, poff, n_blocks, n_rows_max,
                        local_rows)
    ys = _experts(xs, block_expert, n_blocks, w_gate_up, b_gate_up.reshape(e, 1, -1),
                  w_down, b_down.reshape(e, 1, -1))
    pos_tm = jnp.swapaxes(pos, 1, 2).reshape(t, 8)
    gate_tm = jnp.swapaxes(gate, 1, 2).reshape(t, 8)
    return _combine(ys, pos_tm, gate_tm, x1s, g2, b2, flat(loff), flat(seg), flat(goff), local_rows, alpha)


def _mixer(x, apply_ln, lng, lnb, lay, consts, alpha):
    batch, seq, d = x.shape
    t = batch * seq
    fw = consts["bd"].shape[0]
    n2 = MINOR
    x2d = x.reshape(t, d)
    z, x0, w = _in_proj(x2d, batch, seq, lng, lnb, lay["w_in"], lay["b_in"], consts["bd"],
                        lay["w_short"], lay["b_short"], apply_ln)
    c = x0.shape[1]

    n1 = seq // n2
    g = jnp.asarray(_major_matrix(n1, n1, n1, -1)).astype(BF16)
    y = _dft_major(g, z.reshape(batch, 2 * n1, n2, fw))
    yf = _dft_minor(y.reshape(batch, 2, n1, n2, fw), _twiddles(n1, n2), consts["wf"], "real",
                    scale=1.0 / math.sqrt(seq * FOURIER_GROUP_DIM))

    n = 2 * seq
    m1 = n // n2
    tw = _twiddles(m1, n2)
    kf, asum = _hyena_filter(seq, consts["bands"], lay["filt_w1"], lay["filt_b1"], lay["filt_w2"],
                             lay["filt_b2"], lay["filt_w3"], lay["filt_b3"], lay["filt_w_out"],
                             lay["filt_freq"], lay["filt_decay"])
    gk = jnp.asarray(_major_matrix(m1, m1, m1, -1, real_in=True)).astype(BF16)
    ky = _dft_major(gk, kf.reshape(1, m1, n2, c))
    kspec = _dft_minor(ky.reshape(1, 2, m1, n2, c), tw, consts["wf"], "fwd", scale=1.0 / n)
    gfw = jnp.asarray(_major_matrix(m1, m1 // 2, m1, -1)).astype(BF16)
    gin = jnp.asarray(_major_matrix(m1, m1, m1 // 2, +1)).astype(BF16)
    pairs = batch // 2
    u = _dft_major(gfw, w.reshape(pairs, m1, n2, c))
    v = _dft_minor(u.reshape(pairs, 2, m1, n2, c), tw, consts["wf"], "conv", wi_bf=consts["wi"], kf5=kspec)
    yc = _dft_major(gin, v.reshape(pairs, 2 * m1, n2, c))

    inv_norm = 1.0 / asum
    return _out_proj(yf.reshape(t, fw), yc.reshape(t, c), x0, w, x2d, lng, lnb, inv_norm,
                     lay["hyena_skip"], lay["g_fourier"], lay["g_hyena"], lay["w_out"], lay["b_out"],
                     lay["ln1_g"], lay["ln1_b"], lay["w_router_t"], lay["b_router"], alpha, apply_ln)


def kernel(x_prompt, x_sample, ln_in_g, ln_in_b, w_in, b_in, w_short, b_short, filt_w1, filt_b1, filt_w2, filt_b2, filt_w3, filt_b3, filt_w_out, filt_freq, filt_decay, hyena_skip, g_fourier, g_hyena, w_out, b_out, ln1_g, ln1_b, w_router, b_router, w_gate_up, b_gate_up, w_down, b_down, ln2_g, ln2_b):
    depth, d, in_width = w_in.shape
    fw = g_fourier.shape[1]
    alpha = (2.0 * depth) ** 0.25
    row = lambda a: a.reshape(1, -1)

    gd = FOURIER_GROUP_DIM
    cg, sg = _cs(np.outer(np.arange(gd), np.arange(gd)), gd)
    eye = np.eye(fw // gd)
    bd = np.concatenate([np.kron(eye, cg), -np.kron(eye, sg)], axis=1).astype(np.float32)
    bands = jnp.linspace(1e-4, FILTER_BANDS - 1, FILTER_BANDS, dtype=F32)
    bands_row = jnp.zeros((1, LANES), F32).at[0, 1:1 + FILTER_BANDS].set(bands)
    bands_row = bands_row.at[0, 1 + FILTER_BANDS:1 + 2 * FILTER_BANDS].set(bands)
    consts = dict(bd=jnp.asarray(bd).astype(BF16), bands=bands_row,
                  wf=jnp.asarray(_minor_matrix(MINOR, -1)).astype(BF16),
                  wi=jnp.asarray(_minor_matrix(MINOR, +1)).astype(BF16))

    xs = [x_prompt, x_sample]
    for l in range(depth):
        emb = filt_w1.shape[1]
        lay = dict(
            w_in=w_in[l].astype(BF16), b_in=row(b_in[l]), w_short=w_short[l], b_short=row(b_short[l]),
            filt_w1=jnp.zeros((LANES, filt_w1.shape[2]), F32).at[:emb].set(filt_w1[l]),
            filt_b1=row(filt_b1[l]), filt_w2=filt_w2[l], filt_b2=row(filt_b2[l]), filt_w3=filt_w3[l],
            filt_b3=row(filt_b3[l]), filt_w_out=filt_w_out[l], filt_freq=row(filt_freq[l]),
            filt_decay=filt_decay[l], hyena_skip=row(hyena_skip[l]), g_fourier=row(g_fourier[l]),
            g_hyena=row(g_hyena[l]), w_out=w_out[l].astype(BF16), b_out=row(b_out[l]),
            ln1_g=row(ln1_g[l]), ln1_b=row(ln1_b[l]),
            w_router_t=w_router[l].T, b_router=b_router[l].reshape(-1, 1))
        routed = [_mixer(x, l == 0, row(ln_in_g), row(ln_in_b), lay, consts, alpha) for x in xs]
        outs = _routed_moe(routed, w_gate_up[l], b_gate_up[l], w_down[l], b_down[l],
                           row(ln2_g[l]), row(ln2_b[l]), alpha)
        xs = [o.reshape(x.shape) for o, x in zip(outs, xs)]
    return (xs[0], xs[1])
```

```python
import functools
import math

import numpy as np
import jax
import jax.numpy as jnp
from jax import lax
from jax.experimental import pallas as pl
from jax.experimental.pallas import tpu as pltpu

F32 = jnp.float32
BF16 = jnp.bfloat16
I32 = jnp.int32

LN_EPS = 1e-5
RMS_EPS = 1e-6
SWIGLU_LIMIT = 7.0
SWIGLU_ALPHA = 1.702
TOP_K = 4
FOURIER_GROUP_DIM = 64
FILTER_BANDS = 16

LANES = 128
MINOR = 128
ROW_TILE = 512
IN_TILE = 512
MAJOR_BLOCK_BYTES = 12 * 1024 * 1024
TOKEN_TILE = 512
SEG_ALIGN = 16
EXPERT_BLOCK = 512
VMEM_LIMIT = 56 * 1024 * 1024


def _params(n_grid, semantics="parallel"):
    return pltpu.CompilerParams(dimension_semantics=(semantics,) * n_grid,
                                vmem_limit_bytes=VMEM_LIMIT)


def _dot(a, b):
    return jnp.dot(a, b, preferred_element_type=F32)


def _split(a):
    hi = a.astype(BF16)
    lo = (a - hi.astype(F32)).astype(BF16)
    return hi, lo


def _dot3(a, b):
    ah, al = _split(a)
    bh, bl = _split(b)
    return _dot(ah, bh) + _dot(ah, bl) + _dot(al, bh)


def _layer_norm(x, g, b):
    mu = jnp.mean(x, axis=-1, keepdims=True)
    xc = x - mu
    var = jnp.mean(xc * xc, axis=-1, keepdims=True)
    return xc * lax.rsqrt(var + LN_EPS) * g + b


def _rms_norm(x, g):
    return x * lax.rsqrt(jnp.mean(x * x, axis=-1, keepdims=True) + RMS_EPS) * g


def _in_kernel(x_ref, xp_ref, xq_ref, lng_ref, lnb_ref, win_ref, bin_ref, bd_ref, wsh_ref, bsh_ref,
               z_ref, x0_ref, w_ref, *, tiles_per_seq, apply_ln, fw):
    it = pl.program_id(0) % tiles_per_seq
    tm = x_ref.shape[0]

    def norm(x):
        return _layer_norm(x, lng_ref[...], lnb_ref[...]) if apply_ln else x

    xn = norm(x_ref[...])
    halo = norm(jnp.concatenate([xp_ref[...], xq_ref[...]], axis=0))
    proj = _dot(xn.astype(BF16), win_ref[...]) + bin_ref[...]
    projh = _dot(halo.astype(BF16), win_ref[:, fw:]) + bin_ref[:, fw:]

    z = _dot(proj[:, :fw].astype(BF16), bd_ref[...])
    z_ref[0, 0] = z[:, :fw]
    z_ref[0, 1] = z[:, fw:]

    ph = proj[:, fw:]
    prev_row = jnp.where(it > 0, projh[7:8], 0.0)
    next_row = jnp.where(it < tiles_per_seq - 1, projh[8:9], 0.0)
    rows = lax.broadcasted_iota(I32, ph.shape, 0)
    up = jnp.where(rows == 0, prev_row, pltpu.roll(ph, 1, 0))
    dn = jnp.where(rows == tm - 1, next_row, pltpu.roll(ph, tm - 1, 0))
    wsh = wsh_ref[...]
    uc = up * wsh[0:1] + ph * wsh[1:2] + dn * wsh[2:3] + bsh_ref[...]
    hw = uc.shape[1] // 3
    x0_ref[...] = uc[:, :hw]
    w_ref[...] = uc[:, 2 * hw:] * uc[:, hw:2 * hw]


def _in_proj(x2d, batch, seq, lng, lnb, win_bf, b_in, bd_bf, w_short, b_short, apply_ln):
    t, d = x2d.shape
    fw = bd_bf.shape[0]
    inw = win_bf.shape[1]
    hw = (inw - fw) // 3
    tm = IN_TILE
    tps = seq // tm
    r8 = tm // 8
    nblk8 = t // 8
    const = lambda i: (0, 0)
    kern = functools.partial(_in_kernel, tiles_per_seq=tps, apply_ln=apply_ln, fw=fw)
    return pl.pallas_call(
        kern,
        grid=(t // tm,),
        in_specs=[
            pl.BlockSpec((tm, d), lambda i: (i, 0)),
            pl.BlockSpec((8, d), lambda i: (jnp.maximum(i * r8 - 1, 0), 0)),
            pl.BlockSpec((8, d), lambda i: (jnp.minimum((i + 1) * r8, nblk8 - 1), 0)),
            pl.BlockSpec((1, d), const), pl.BlockSpec((1, d), const),
            pl.BlockSpec((d, inw), const), pl.BlockSpec((1, inw), const),
            pl.BlockSpec((fw, 2 * fw), const),
            pl.BlockSpec((3, 3 * hw), const), pl.BlockSpec((1, 3 * hw), const),
        ],
        out_specs=[
            pl.BlockSpec((1, 2, tm, fw), lambda i: (i // tps, 0, i % tps, 0)),
            pl.BlockSpec((tm, hw), lambda i: (i, 0)),
            pl.BlockSpec((tm, hw), lambda i: (i, 0)),
        ],
        out_shape=[
            jax.ShapeDtypeStruct((batch, 2, seq, fw), F32),
            jax.ShapeDtypeStruct((t, hw), F32),
            jax.ShapeDtypeStruct((t, hw), F32),
        ],
        compiler_params=_params(1),
        name="in_proj",
    )(x2d, x2d, x2d, lng, lnb, win_bf, b_in, bd_bf, w_short, b_short)


FILTER_HALO = 8


def _filter_kernel(bands_ref, w1_ref, b1_ref, w2_ref, b2_ref, w3_ref, b3_ref, wl_ref, fr_ref, dec_ref,
                   k_ref, asum_ref, tc_ref, ts_ref, kf_buf, kb_buf, sems, *, seq):
    i = pl.program_id(0)
    nsteps = pl.num_programs(0)
    tile, c = kf_buf.shape
    ext = tc_ref.shape[0]
    rep = c // LANES
    w_unit = 2.0 * math.pi / seq
    bands = bands_ref[...]

    @pl.when(i == 0)
    def _():
        arg = bands * (w_unit * lax.broadcasted_iota(I32, (ext, LANES), 0).astype(F32))
        tc_ref[...] = jnp.cos(arg)
        ts_ref[...] = jnp.sin(arg)
        asum_ref[...] = jnp.zeros_like(asum_ref)

    j0 = i * tile
    base = bands * (w_unit * j0.astype(F32))
    c0, s0 = jnp.cos(base), jnp.sin(base)
    tc, ts = tc_ref[...], ts_ref[...]
    cosf = c0 * tc - s0 * ts
    sinf = s0 * tc + c0 * ts
    j = j0 + lax.broadcasted_iota(I32, (ext, LANES), 0)
    t = j.astype(F32) * (1.0 / (seq - 1))
    lane = lax.broadcasted_iota(I32, (ext, LANES), 1)
    feat = jnp.where(lane == 0, t,
                     jnp.where(lane <= FILTER_BANDS, cosf,
                               jnp.where(lane <= 2 * FILTER_BANDS, -sinf, 0.0)))
    fr = fr_ref[...]
    h = jnp.sin(fr * (_dot3(feat, w1_ref[...]) + b1_ref[...]))
    h = jnp.sin(fr * (_dot3(h, w2_ref[...]) + b2_ref[...]))
    h = jnp.sin(fr * (_dot3(h, w3_ref[...]) + b3_ref[...]))
    hl = _dot3(h, wl_ref[...])
    t4 = jnp.concatenate([t] * rep, axis=1)
    j4 = jnp.concatenate([j] * rep, axis=1)
    dec = jnp.abs(dec_ref[...])
    hf = hl[:, :c] * jnp.exp(-t4 * dec[0:1])
    hb = hl[:, c:] * jnp.exp(-t4 * dec[1:2])
    hb = jnp.where(j4 < seq, hb, 0.0)
    asum_ref[...] += jnp.sum(jnp.abs(hf[:tile]) + jnp.abs(hb[:tile]), axis=0, keepdims=True)

    rev = (lax.broadcasted_iota(I32, (tile, ext), 1) == tile - lax.broadcasted_iota(I32, (tile, ext), 0))
    rev = jnp.where(rev, 1.0, 0.0).astype(BF16)
    b1 = hb.astype(BF16)
    r1 = hb - b1.astype(F32)
    b2 = r1.astype(BF16)
    b3 = (r1 - b2.astype(F32)).astype(BF16)
    kb = _dot(rev, b1) + _dot(rev, b2) + _dot(rev, b3)

    def copies():
        return (pltpu.make_async_copy(kf_buf, k_ref.at[pl.ds(pl.multiple_of(j0, tile), tile)], sems.at[0]),
                pltpu.make_async_copy(kb_buf, k_ref.at[pl.ds(pl.multiple_of(2 * seq - j0 - tile, tile), tile)],
                                      sems.at[1]))

    @pl.when(i > 0)
    def _():
        for cp in copies():
            cp.wait()

    kf_buf[...] = jnp.where(j4[:tile] == 0, hf[:tile] + hb[:tile], hf[:tile])
    kb_buf[...] = kb
    for cp in copies():
        cp.start()

    @pl.when(i == nsteps - 1)
    def _():
        for cp in copies():
            cp.wait()


def _hyena_filter(seq, bands_row, w1p, b1, w2, b2, w3, b3, wl, freq, decay):
    c = decay.shape[1]
    tile = ROW_TILE
    ext = tile + FILTER_HALO
    full = lambda a: pl.BlockSpec(a.shape, lambda i: (0,) * a.ndim)
    args = (bands_row, w1p, b1, w2, b2, w3, b3, wl, freq, decay)
    return pl.pallas_call(
        functools.partial(_filter_kernel, seq=seq),
        grid=(seq // tile,),
        in_specs=[full(a) for a in args],
        out_specs=[pl.BlockSpec(memory_space=pl.ANY), pl.BlockSpec((1, c), lambda i: (0, 0))],
        out_shape=[jax.ShapeDtypeStruct((2 * seq, c), F32), jax.ShapeDtypeStruct((1, c), F32)],
        scratch_shapes=[pltpu.VMEM((ext, LANES), F32), pltpu.VMEM((ext, LANES), F32),
                        pltpu.VMEM((tile, c), F32), pltpu.VMEM((tile, c), F32),
                        pltpu.SemaphoreType.DMA((2,))],
        compiler_params=_params(1, "arbitrary"),
        name="hyena_filter",
    )(*args)


def _major_compute(g, xbuf, obuf, slot):
    _, n_ct, bsz, k, sub, lanes = xbuf.shape
    m = obuf.shape[3]
    x2 = xbuf.reshape(2 * n_ct * bsz * k * sub, lanes)
    o2 = obuf.reshape(2 * n_ct * bsz * m * sub, lanes)
    for t in range(n_ct * bsz):
        xbase = (slot * n_ct * bsz + t) * k * sub
        obase = (slot * n_ct * bsz + t) * m * sub
        for b in range(sub):
            xb = x2[pl.ds(xbase + b, k, stride=sub), :]
            o2[pl.ds(obase + b, m, stride=sub), :] = _dot(g, xb.astype(BF16))


def _major_kernel(g_ref, x_hbm, o_hbm, xbuf, obuf, isem, osem, *, nj):
    s = pl.program_id(0)
    ns = pl.num_programs(0)
    slot = s % 2
    _, n_ct, bsz, k, sub, lanes = xbuf.shape
    m = obuf.shape[3]

    def copies(step, sl, hbm, buf, sem, inbound):
        i = step // nj
        j = step % nj
        out = []
        for ct in range(n_ct):
            tile = hbm.at[pl.ds(i * bsz, bsz), :, pl.ds(pl.multiple_of(j * sub, sub), sub),
                          pl.ds(ct * lanes, lanes)]
            out.append(pltpu.make_async_copy(tile, buf.at[sl, ct], sem.at[sl]) if inbound
                       else pltpu.make_async_copy(buf.at[sl, ct], tile, sem.at[sl]))
        return out

    def start(cps):
        for cp in cps:
            cp.start()

    def wait(cps):
        for cp in cps:
            cp.wait()

    @pl.when(s == 0)
    def _():
        start(copies(0, 0, x_hbm, xbuf, isem, True))

    @pl.when(s + 1 < ns)
    def _():
        start(copies(s + 1, 1 - slot, x_hbm, xbuf, isem, True))

    wait(copies(s, slot, x_hbm, xbuf, isem, True))

    @pl.when(s >= 2)
    def _():
        wait(copies(s - 2, slot, o_hbm, obuf, osem, False))

    _major_compute(g_ref[...], xbuf, obuf, slot)
    start(copies(s, slot, o_hbm, obuf, osem, False))

    @pl.when(s == ns - 1)
    def _():
        wait(copies(s, slot, o_hbm, obuf, osem, False))

        @pl.when(s >= 1)
        def _():
            wait(copies(s - 1, 1 - slot, o_hbm, obuf, osem, False))


def _dft_major(g_bf, x4):
    b, k, n2, c = x4.shape
    m = g_bf.shape[0]
    sub = 8
    n_ct = c // LANES
    bsz = max(1, min(b, MAJOR_BLOCK_BYTES // ((k + m) * sub * c * 4)))
    while b % bsz:
        bsz -= 1
    nj = n2 // sub
    return pl.pallas_call(
        functools.partial(_major_kernel, nj=nj),
        grid=((b // bsz) * nj,),
        in_specs=[pl.BlockSpec((m, k), lambda s: (0, 0)), pl.BlockSpec(memory_space=pl.ANY)],
        out_specs=pl.BlockSpec(memory_space=pl.ANY),
        out_shape=jax.ShapeDtypeStruct((b, m, n2, c), F32),
        scratch_shapes=[pltpu.VMEM((2, n_ct, bsz, k, sub, LANES), F32),
                        pltpu.VMEM((2, n_ct, bsz, m, sub, LANES), F32),
                        pltpu.SemaphoreType.DMA((2,)), pltpu.SemaphoreType.DMA((2,))],
        compiler_params=_params(1, "arbitrary"),
        name="dft_major",
    )(g_bf, x4)


def _minor_kernel(*refs, mode, scale, k1b):
    if mode == "conv":
        y_ref, tw_ref, wf_ref, wi_ref, kf_ref, o_ref, ks_ref = refs
    else:
        y_ref, tw_ref, wf_ref, o_ref = refs
    n2 = y_ref.shape[3]
    ct = y_ref.shape[4]
    rep = ct // LANES

    def forward(re, im, cs, sn, w):
        return _dot(w, jnp.concatenate([re * cs + im * sn, im * cs - re * sn], axis=0).astype(BF16))

    for j in range(k1b):
        cs = jnp.concatenate([tw_ref[0, j]] * rep, axis=1)
        sn = jnp.concatenate([tw_ref[1, j]] * rep, axis=1)
        if mode == "real":
            o_ref[0, :, j, :] = forward(y_ref[0, 0, j], y_ref[0, 1, j], cs, sn, wf_ref[:n2]) * scale
            continue

        @pl.when(pl.program_id(1) == 0)
        def _():
            ks_ref[j] = forward(kf_ref[0, 0, j], kf_ref[0, 1, j], cs, sn, wf_ref[...]) * scale

        z = forward(y_ref[0, 0, j], y_ref[0, 1, j], cs, sn, wf_ref[...])
        zr, zi = z[:n2], z[n2:]
        kr = ks_ref[j, :n2]
        ki = ks_ref[j, n2:]
        p = jnp.concatenate([zr * kr - zi * ki, zr * ki + zi * kr], axis=0).astype(BF16)
        q = _dot(wi_ref[...], p)
        qr, qi = q[:n2], q[n2:]
        o_ref[0, 0, j] = qr * cs - qi * sn
        o_ref[0, 1, j] = qi * cs + qr * sn


def _dft_minor(y5, tw, wf_bf, mode, scale=1.0, wi_bf=None, kf5=None):
    b, _, n1, n2, c = y5.shape
    k1b = min(8, n1)
    ct = c
    grid = (n1 // k1b, b)
    dspec = pl.BlockSpec((1, 2, k1b, n2, ct), lambda j, i: (i, 0, j, 0, 0))
    in_specs = [dspec,
                pl.BlockSpec((2, k1b, n2, LANES), lambda j, i: (0, j, 0, 0)),
                pl.BlockSpec(wf_bf.shape, lambda j, i: (0, 0))]
    args = [y5, tw, wf_bf]
    scratch = []
    if mode == "conv":
        in_specs += [pl.BlockSpec(wi_bf.shape, lambda j, i: (0, 0)),
                     pl.BlockSpec((1, 2, k1b, n2, ct), lambda j, i: (0, 0, j, 0, 0))]
        args += [wi_bf, kf5]
        scratch = [pltpu.VMEM((k1b, 2 * n2, ct), F32)]
    if mode == "real":
        out_spec = pl.BlockSpec((1, n2, k1b, ct), lambda j, i: (i, 0, j, 0))
        out_shape = jax.ShapeDtypeStruct((b, n2, n1, c), F32)
    else:
        out_spec = dspec
        out_shape = jax.ShapeDtypeStruct((b, 2, n1, n2, c), F32)
    return pl.pallas_call(
        functools.partial(_minor_kernel, mode=mode, scale=scale, k1b=k1b),
        grid=grid, in_specs=in_specs, out_specs=out_spec, out_shape=out_shape,
        scratch_shapes=scratch,
        compiler_params=pltpu.CompilerParams(dimension_semantics=("parallel", "arbitrary"),
                                             vmem_limit_bytes=VMEM_LIMIT),
        name="dft_minor_" + mode,
    )(*args)


def _cs(num, den):
    ang = 2.0 * np.pi * (num % den) / den
    return np.cos(ang), np.sin(ang)


def _complex_block(c, s, sign):
    return np.block([[c, -sign * s], [sign * s, c]]).astype(np.float32)


def _major_matrix(n1, a_in, a_out, sign, real_in=False):
    c, s = _cs(np.outer(np.arange(a_out), np.arange(a_in)), n1)
    g = _complex_block(c, s, sign)
    return g[:, :a_in] if real_in else g


def _minor_matrix(n2, sign):
    c, s = _cs(np.outer(np.arange(n2), np.arange(n2)), n2)
    return _complex_block(c, s, sign)


def _twiddles(n1, n2):
    c, s = _cs(np.outer(np.arange(n1), np.arange(n2)), n1 * n2)
    tw = jnp.asarray(np.stack([c, s]).astype(np.float32))
    return jnp.broadcast_to(tw[..., None], (2, n1, n2, LANES))


def _out_kernel(yf_ref, yc_ref, x0_ref, w_ref, x_ref, lng_ref, lnb_ref, inv_ref, skip_ref, gf_ref, gh_ref,
                wo_ref, bo_ref, g1_ref, b1_ref, wr_ref, br_ref, x1_ref, idx_ref, gate_ref, cnt_ref,
                *, alpha, apply_ln, top_k):
    c = yc_ref.shape[1]
    w = w_ref[...]
    yh = x0_ref[...] * (yc_ref[...] * inv_ref[...] + skip_ref[...] * w)
    mf = _rms_norm(yf_ref[...], gf_ref[...]).astype(BF16)
    mh = _rms_norm(yh, gh_ref[...]).astype(BF16)
    m = _dot(mf, wo_ref[:c]) + _dot(mh, wo_ref[c:]) + bo_ref[...]
    xn = _layer_norm(x_ref[...], lng_ref[...], lnb_ref[...]) if apply_ln else x_ref[...]
    x1 = _layer_norm(alpha * xn + m, g1_ref[...], b1_ref[...])
    x1_ref[...] = x1
    _route_tile(x1, wr_ref[...], br_ref[...], idx_ref, gate_ref, cnt_ref, top_k)


def _out_proj(yf, yc, x0, w, x, lng, lnb, inv_norm, skip, gf, gh, wo_bf, bo, g1, b1, wr_t, br_col, alpha,
              apply_ln):
    t, d = x.shape
    c = yc.shape[1]
    e = wr_t.shape[0]
    tm = TOKEN_TILE
    nt = t // tm
    row = lambda i: (i, 0)
    const = lambda i: (0, 0)
    tile3 = lambda i: (i, 0, 0)
    return pl.pallas_call(
        functools.partial(_out_kernel, alpha=alpha, apply_ln=apply_ln, top_k=TOP_K),
        grid=(nt,),
        in_specs=[
            pl.BlockSpec((tm, c), row),
            pl.BlockSpec((tm, c), row), pl.BlockSpec((tm, c), row), pl.BlockSpec((tm, c), row),
            pl.BlockSpec((tm, d), row),
            pl.BlockSpec((1, d), const), pl.BlockSpec((1, d), const),
            pl.BlockSpec((1, c), const), pl.BlockSpec((1, c), const),
            pl.BlockSpec((1, c), const), pl.BlockSpec((1, c), const),
            pl.BlockSpec(wo_bf.shape, const), pl.BlockSpec((1, d), const),
            pl.BlockSpec((1, d), const), pl.BlockSpec((1, d), const),
            pl.BlockSpec((e, d), const), pl.BlockSpec((e, 1), const),
        ],
        out_specs=[pl.BlockSpec((tm, d), row),
                   pl.BlockSpec((1, 2 * TOP_K, tm), tile3),
                   pl.BlockSpec((1, 8, tm), tile3),
                   pl.BlockSpec((1, e, 1), tile3)],
        out_shape=[jax.ShapeDtypeStruct((t, d), F32),
                   jax.ShapeDtypeStruct((nt, 2 * TOP_K, tm), I32),
                   jax.ShapeDtypeStruct((nt, 8, tm), F32),
                   jax.ShapeDtypeStruct((nt, e, 1), I32)],
        compiler_params=_params(1),
        name="out_proj",
    )(yf, yc, x0, w, x, lng, lnb, inv_norm, skip, gf, gh, wo_bf, bo, g1, b1, wr_t, br_col)


def _route_tile(x1, wr, br, idx_ref, gate_ref, cnt_ref, top_k):
    tt = x1.shape[0]
    e = wr.shape[0]
    nt = (((1,), (1,)), ((), ()))
    xh, xl = _split(x1)
    wh, wl = _split(wr)
    logits = (lax.dot_general(wh, xh, nt, preferred_element_type=F32)
              + lax.dot_general(wh, xl, nt, preferred_element_type=F32)
              + lax.dot_general(wl, xh, nt, preferred_element_type=F32)) + br
    iota_e = lax.broadcasted_iota(I32, (e, tt), 0).astype(F32)
    l = logits
    tops, idxs, hots = [], [], []
    for _ in range(top_k):
        m = jnp.max(l, axis=0, keepdims=True)
        idx = jnp.min(jnp.where(l == m, iota_e, float(e)), axis=0, keepdims=True)
        hot = iota_e == idx
        l = jnp.where(hot, -jnp.inf, l)
        tops.append(m)
        idxs.append(idx)
        hots.append(hot)
    ex = [jnp.exp(m - tops[0]) for m in tops]
    den = ex[0]
    for v in ex[1:]:
        den = den + v
    gates = [v / den for v in ex]
    hot_all = jnp.zeros((e, tt), F32)
    for hot in hots:
        hot_all = jnp.where(hot, 1.0, hot_all)
    upper = (lax.broadcasted_iota(I32, (tt, tt), 0) < lax.broadcasted_iota(I32, (tt, tt), 1))
    before = _dot(hot_all.astype(BF16), jnp.where(upper, 1.0, 0.0).astype(BF16))
    ranks = [jnp.sum(jnp.where(hot, before, 0.0), axis=0, keepdims=True).astype(I32) for hot in hots]
    pad = 8 - top_k
    idx_ref[0] = jnp.concatenate([v.astype(I32) for v in idxs] + ranks, axis=0)
    gate_ref[0] = jnp.concatenate(gates + [jnp.zeros((pad, tt), F32)], axis=0)
    cnt_ref[0] = jnp.sum(hot_all, axis=1, keepdims=True).astype(I32)


def _dual_specs(shape, nt0):
    first = pl.BlockSpec(shape, lambda i, *_: (jnp.minimum(i, nt0 - 1),) + (0,) * (len(shape) - 1))
    second = pl.BlockSpec(shape, lambda i, *_: (jnp.maximum(i - nt0, 0),) + (0,) * (len(shape) - 1))
    return first, second


SEG_PIECES = int(math.log2(TOKEN_TILE // SEG_ALIGN)) + 1


def _segment_copies(first, count, loff_s, len_s, goff_s, local, remote, sem, to_remote, wait):
    def body(e, carry):
        s = first + e
        n = len_s[s] // SEG_ALIGN
        lo = 0 if loff_s is None else loff_s[s]
        go = goff_s[s]
        for p in reversed(range(SEG_PIECES)):
            size = SEG_ALIGN << p
            start = ((n >> (p + 1)) << (p + 1)) * SEG_ALIGN

            @pl.when(((n >> p) & 1) == 1)
            def _():
                lref = local.at[pl.ds(pl.multiple_of(lo + start, SEG_ALIGN), size)]
                rref = remote.at[pl.ds(pl.multiple_of(go + start, SEG_ALIGN), size)]
                cp = (pltpu.make_async_copy(lref, rref, sem) if to_remote
                      else pltpu.make_async_copy(rref, lref, sem))
                if wait:
                    cp.wait()
                else:
                    cp.start()
        return carry

    lax.fori_loop(0, count, body, 0)


def _wait_tile(step, n_experts, loff_s, len_s, local, remote, sem, to_remote):
    last = step * n_experts + n_experts - 1
    n = (loff_s[last] + len_s[last]) // SEG_ALIGN
    for p in range((local.shape[0] // SEG_ALIGN).bit_length()):
        size = SEG_ALIGN << p

        @pl.when(((n >> p) & 1) == 1)
        def _():
            lref = local.at[pl.ds(0, size)]
            rref = remote.at[pl.ds(0, size)]
            cp = (pltpu.make_async_copy(lref, rref, sem) if to_remote
                  else pltpu.make_async_copy(rref, lref, sem))
            cp.wait()


def _dispatch_kernel(loff_s, seg_s, goff_s, plen_s, poff_s, nb_s, xa_ref, xb_ref, idx_ref, xs_ref, pos_ref,
                     buf, sems, *, n_experts, top_k, nt0):
    j = pl.program_id(0)
    nt = pl.num_programs(0)
    slot = j % 2
    tt = xa_ref.shape[0]
    lr = buf.shape[1]
    idx = idx_ref[0]
    tope = idx[:top_k]
    base = jnp.zeros_like(tope)
    for e in range(n_experts):
        base = jnp.where(tope == e, loff_s[j * n_experts + e], base)
    pos = base + idx[top_k:]
    pos_ref[0] = jnp.concatenate([pos, jnp.zeros((8 - top_k, tt), I32)], axis=0)
    rows = lax.broadcasted_iota(I32, (lr, tt), 0)
    perm = jnp.zeros((lr, tt), F32)
    for k in range(top_k):
        perm = jnp.where(rows == pos[k:k + 1], 1.0, perm)
    x = jnp.where(j < nt0, xa_ref[...], xb_ref[...])
    buf[slot] = _dot(perm.astype(BF16), x.astype(BF16)).astype(BF16)

    def drained(step, s):
        _wait_tile(step, n_experts, loff_s, seg_s, buf.at[s], xs_ref, sems.at[s], True)

    _segment_copies(j * n_experts, n_experts, loff_s, seg_s, goff_s, buf.at[slot], xs_ref, sems.at[slot],
                    True, False)

    @pl.when(j > 0)
    def _():
        drained(j - 1, 1 - slot)

    @pl.when(j == nt - 1)
    def _():
        drained(j, slot)
        zrows = EXPERT_BLOCK
        buf[0, :zrows] = jnp.zeros((zrows, buf.shape[2]), BF16)
        _segment_copies(0, n_experts, None, plen_s, poff_s, buf.at[0], xs_ref, sems.at[0], True, False)
        _segment_copies(0, n_experts, None, plen_s, poff_s, buf.at[0], xs_ref, sems.at[0], True, True)
        n_tail = xs_ref.shape[0] // zrows - nb_s[0]

        def tail_copy(i):
            dst = xs_ref.at[pl.ds(pl.multiple_of((nb_s[0] + i) * zrows, zrows), zrows)]
            return pltpu.make_async_copy(buf.at[0, pl.ds(0, zrows)], dst, sems.at[0])

        lax.fori_loop(0, n_tail, lambda i, c: (tail_copy(i).start(), c)[1], 0)
        lax.fori_loop(0, n_tail, lambda i, c: (tail_copy(i).wait(), c)[1], 0)


def _dispatch(x1s, idx, loff, seg, goff, plen, poff, n_blocks, n_rows, local_rows):
    d = x1s[0].shape[1]
    tt = TOKEN_TILE
    nt0 = x1s[0].shape[0] // tt
    nt = nt0 + x1s[1].shape[0] // tt
    e = loff.shape[0] // nt
    xa, xb = _dual_specs((tt, d), nt0)
    grid_spec = pltpu.PrefetchScalarGridSpec(
        num_scalar_prefetch=6,
        grid=(nt,),
        in_specs=[xa, xb, pl.BlockSpec((1, 2 * TOP_K, tt), lambda i, *_: (i, 0, 0))],
        out_specs=[pl.BlockSpec(memory_space=pl.ANY),
                   pl.BlockSpec((1, 8, tt), lambda i, *_: (i, 0, 0))],
        scratch_shapes=[pltpu.VMEM((2, local_rows, d), BF16), pltpu.SemaphoreType.DMA((2,))],
    )
    return pl.pallas_call(
        functools.partial(_dispatch_kernel, n_experts=e, top_k=TOP_K, nt0=nt0),
        grid_spec=grid_spec,
        out_shape=[jax.ShapeDtypeStruct((n_rows, d), BF16), jax.ShapeDtypeStruct((nt, 8, tt), I32)],
        compiler_params=_params(1, "arbitrary"),
        name="moe_dispatch",
    )(loff, seg, goff, plen, poff, n_blocks, x1s[0], x1s[1], idx)


def _combine_kernel(loff_s, seg_s, goff_s, ys_ref, pos_ref, gate_ref, xa_ref, xb_ref, g2_ref, b2_ref,
                    oa_ref, ob_ref, buf, sems, *, n_experts, top_k, alpha, nt0):
    j = pl.program_id(0)
    nt = pl.num_programs(0)
    slot = j % 2
    tt = xa_ref.shape[0]
    lr = buf.shape[1]

    def prefetch(step, s):
        buf[s] = jnp.zeros((lr, buf.shape[2]), BF16)
        _segment_copies(step * n_experts, n_experts, loff_s, seg_s, goff_s, buf.at[s], ys_ref, sems.at[s],
                        False, False)

    @pl.when(j == 0)
    def _():
        prefetch(0, 0)

    @pl.when(j + 1 < nt)
    def _():
        prefetch(j + 1, 1 - slot)

    _wait_tile(j, n_experts, loff_s, seg_s, buf.at[slot], ys_ref, sems.at[slot], False)
    pos = pos_ref[...]
    gate = gate_ref[...]
    cols = lax.broadcasted_iota(I32, (tt, lr), 1)
    wsel = jnp.zeros((tt, lr), F32)
    for k in range(top_k):
        wsel = jnp.where(cols == pos[:, k:k + 1], gate[:, k:k + 1], wsel)
    moe = _dot(wsel.astype(BF16), buf[slot])
    x1 = jnp.where(j < nt0, xa_ref[...], xb_ref[...])
    out = _layer_norm(alpha * x1 + moe, g2_ref[...], b2_ref[...])

    @pl.when(j < nt0)
    def _():
        oa_ref[...] = out

    @pl.when(j >= nt0)
    def _():
        ob_ref[...] = out


def _combine(ys, pos_tm, gate_tm, x1s, g2, b2, loff, seg, goff, local_rows, alpha):
    d = x1s[0].shape[1]
    tt = TOKEN_TILE
    nt0 = x1s[0].shape[0] // tt
    nt = nt0 + x1s[1].shape[0] // tt
    e = loff.shape[0] // nt
    xa, xb = _dual_specs((tt, d), nt0)
    grid_spec = pltpu.PrefetchScalarGridSpec(
        num_scalar_prefetch=3,
        grid=(nt,),
        in_specs=[pl.BlockSpec(memory_space=pl.ANY),
                  pl.BlockSpec((tt, 8), lambda i, *_: (i, 0)),
                  pl.BlockSpec((tt, 8), lambda i, *_: (i, 0)),
                  xa, xb,
                  pl.BlockSpec((1, d), lambda i, *_: (0, 0)),
                  pl.BlockSpec((1, d), lambda i, *_: (0, 0))],
        out_specs=list(_dual_specs((tt, d), nt0)),
        scratch_shapes=[pltpu.VMEM((2, local_rows, d), BF16), pltpu.SemaphoreType.DMA((2,))],
    )
    return pl.pallas_call(
        functools.partial(_combine_kernel, n_experts=e, top_k=TOP_K, alpha=alpha, nt0=nt0),
        grid_spec=grid_spec,
        out_shape=[jax.ShapeDtypeStruct(x.shape, F32) for x in x1s],
        compiler_params=_params(1, "arbitrary"),
        name="moe_combine",
    )(loff, seg, goff, ys, pos_tm, gate_tm, x1s[0], x1s[1], g2, b2)


def _expert_kernel(be_s, nb_s, xs_ref, wgu_ref, bgu_ref, wd_ref, bd_ref, ys_ref, wgu_bf, wd_bf):
    b = pl.program_id(0)
    active = b < nb_s[0]
    fresh = jnp.logical_or(b == 0, be_s[b] != be_s[jnp.maximum(b - 1, 0)])

    @pl.when(jnp.logical_and(active, fresh))
    def _():
        wgu_bf[...] = wgu_ref[0].astype(BF16)
        wd_bf[...] = wd_ref[0].astype(BF16)

    @pl.when(active)
    def _():
        dff = wd_ref.shape[1]
        gu = _dot(xs_ref[...], wgu_bf[...]) + bgu_ref[0]
        gate = jnp.minimum(gu[:, :dff], SWIGLU_LIMIT)
        up = jnp.clip(gu[:, dff:], -SWIGLU_LIMIT, SWIGLU_LIMIT)
        h = (up + 1.0) * (gate * jax.nn.sigmoid(SWIGLU_ALPHA * gate))
        ys_ref[...] = (_dot(h.astype(BF16), wd_bf[...]) + bd_ref[0]).astype(BF16)

    @pl.when(jnp.logical_not(active))
    def _():
        ys_ref[...] = jnp.zeros_like(ys_ref)


def _experts(xs, block_expert, n_blocks, wgu, bgu, wd, bdn):
    n_rows, d = xs.shape
    bm = EXPERT_BLOCK
    f2 = wgu.shape[2]
    dff = wd.shape[1]
    blk = lambda b, be, nb: (jnp.minimum(b, nb[0] - 1), 0)
    wsel = lambda b, be, nb: (be[b], 0, 0)
    grid_spec = pltpu.PrefetchScalarGridSpec(
        num_scalar_prefetch=2,
        grid=(n_rows // bm,),
        in_specs=[pl.BlockSpec((bm, d), blk),
                  pl.BlockSpec((1, d, f2), wsel), pl.BlockSpec((1, 1, f2), wsel),
                  pl.BlockSpec((1, dff, d), wsel), pl.BlockSpec((1, 1, d), wsel)],
        out_specs=pl.BlockSpec((bm, d), lambda b, be, nb: (b, 0)),
        scratch_shapes=[pltpu.VMEM((d, f2), BF16), pltpu.VMEM((dff, d), BF16)],
    )
    return pl.pallas_call(
        _expert_kernel,
        grid_spec=grid_spec,
        out_shape=jax.ShapeDtypeStruct((n_rows, d), BF16),
        compiler_params=_params(1, "arbitrary"),
        name="moe_experts",
    )(block_expert, n_blocks, xs, wgu, bgu, wd, bdn)


def _routed_moe(routed, w_gate_up, b_gate_up, w_down, b_down, g2, b2, alpha):
    x1s = [r[0] for r in routed]
    idx, gate, cnt = (jnp.concatenate([r[i] for r in routed], axis=0) for i in (1, 2, 3))
    t = x1s[0].shape[0] + x1s[1].shape[0]
    e = cnt.shape[1]
    tt = TOKEN_TILE
    nt = t // tt
    bm = EXPERT_BLOCK

    cnt = cnt.reshape(nt, e)
    seg = (cnt + SEG_ALIGN - 1) // SEG_ALIGN * SEG_ALIGN
    loff = jnp.cumsum(seg, axis=1) - seg
    per_expert = jnp.sum(seg, axis=0)
    padded = (per_expert + bm - 1) // bm * bm
    blocks_end = jnp.cumsum(padded) // bm
    start = jnp.cumsum(padded) - padded
    goff = start[None, :] + jnp.cumsum(seg, axis=0) - seg
    local_rows = TOP_K * tt + e * SEG_ALIGN
    n_rows_max = -(-(TOP_K * t + (SEG_ALIGN - 1) * e * nt) // bm) * bm + e * bm
    n_blocks = blocks_end[-1:].astype(I32)
    bidx = jnp.minimum(jnp.arange(n_rows_max // bm, dtype=I32), n_blocks[0] - 1)
    block_expert = jnp.minimum(jnp.sum(bidx[:, None] >= blocks_end[None, :], axis=1), e - 1).astype(I32)
    flat = lambda a: a.reshape(-1).astype(I32)
    plen, poff = flat(padded - per_expert), flat(start + per_expert)

    xs, pos = _dispatch(x1s, idx, flat(loff), flat(seg), flat(goff), plen, poff, n_blocks, n_rows_max,
                        local_rows)
    ys = _experts(xs, block_expert, n_blocks, w_gate_up, b_gate_up.reshape(e, 1, -1),
                  w_down, b_down.reshape(e, 1, -1))
    pos_tm = jnp.swapaxes(pos, 1, 2).reshape(t, 8)
    gate_tm = jnp.swapaxes(gate, 1, 2).reshape(t, 8)
    return _combine(ys, pos_tm, gate_tm, x1s, g2, b2, flat(loff), flat(seg), flat(goff), local_rows, alpha)


def _mixer(x, apply_ln, lng, lnb, lay, consts, alpha):
    batch, seq, d = x.shape
    t = batch * seq
    fw = consts["bd"].shape[0]
    n2 = MINOR
    x2d = x.reshape(t, d)
    z, x0, w = _in_proj(x2d, batch, seq, lng, lnb, lay["w_in"], lay["b_in"], consts["bd"],
                        lay["w_short"], lay["b_short"], apply_ln)
    c = x0.shape[1]

    n1 = seq // n2
    g = jnp.asarray(_major_matrix(n1, n1, n1, -1)).astype(BF16)
    y = _dft_major(g, z.reshape(batch, 2 * n1, n2, fw))
    yf = _dft_minor(y.reshape(batch, 2, n1, n2, fw), _twiddles(n1, n2), consts["wf"], "real",
                    scale=1.0 / math.sqrt(seq * FOURIER_GROUP_DIM))

    n = 2 * seq
    m1 = n // n2
    tw = _twiddles(m1, n2)
    kf, asum = _hyena_filter(seq, consts["bands"], lay["filt_w1"], lay["filt_b1"], lay["filt_w2"],
                             lay["filt_b2"], lay["filt_w3"], lay["filt_b3"], lay["filt_w_out"],
                             lay["filt_freq"], lay["filt_decay"])
    gk = jnp.asarray(_major_matrix(m1, m1, m1, -1, real_in=True)).astype(BF16)
    ky = _dft_major(gk, kf.reshape(1, m1, n2, c))
    gfw = jnp.asarray(_major_matrix(m1, m1 // 2, m1, -1)).astype(BF16)
    gin = jnp.asarray(_major_matrix(m1, m1, m1 // 2, +1)).astype(BF16)
    pairs = batch // 2
    u = _dft_major(gfw, w.reshape(pairs, m1, n2, c))
    v = _dft_minor(u.reshape(pairs, 2, m1, n2, c), tw, consts["wf"], "conv", scale=1.0 / n,
                   wi_bf=consts["wi"], kf5=ky.reshape(1, 2, m1, n2, c))
    yc = _dft_major(gin, v.reshape(pairs, 2 * m1, n2, c))

    inv_norm = 1.0 / asum
    return _out_proj(yf.reshape(t, fw), yc.reshape(t, c), x0, w, x2d, lng, lnb, inv_norm,
                     lay["hyena_skip"], lay["g_fourier"], lay["g_hyena"], lay["w_out"], lay["b_out"],
                     lay["ln1_g"], lay["ln1_b"], lay["w_router_t"], lay["b_router"], alpha, apply_ln)


def kernel(x_prompt, x_sample, ln_in_g, ln_in_b, w_in, b_in, w_short, b_short, filt_w1, filt_b1, filt_w2, filt_b2, filt_w3, filt_b3, filt_w_out, filt_freq, filt_decay, hyena_skip, g_fourier, g_hyena, w_out, b_out, ln1_g, ln1_b, w_router, b_router, w_gate_up, b_gate_up, w_down, b_down, ln2_g, ln2_b):
    depth, d, in_width = w_in.shape
    fw = g_fourier.shape[1]
    alpha = (2.0 * depth) ** 0.25
    row = lambda a: a.reshape(1, -1)

    gd = FOURIER_GROUP_DIM
    cg, sg = _cs(np.outer(np.arange(gd), np.arange(gd)), gd)
    eye = np.eye(fw // gd)
    bd = np.concatenate([np.kron(eye, cg), -np.kron(eye, sg)], axis=1).astype(np.float32)
    bands = jnp.linspace(1e-4, FILTER_BANDS - 1, FILTER_BANDS, dtype=F32)
    bands_row = jnp.zeros((1, LANES), F32).at[0, 1:1 + FILTER_BANDS].set(bands)
    bands_row = bands_row.at[0, 1 + FILTER_BANDS:1 + 2 * FILTER_BANDS].set(bands)
    consts = dict(bd=jnp.asarray(bd).astype(BF16), bands=bands_row,
                  wf=jnp.asarray(_minor_matrix(MINOR, -1)).astype(BF16),
                  wi=jnp.asarray(_minor_matrix(MINOR, +1)).astype(BF16))

    xs = [x_prompt, x_sample]
    for l in range(depth):
        emb = filt_w1.shape[1]
        lay = dict(
            w_in=w_in[l].astype(BF16), b_in=row(b_in[l]), w_short=w_short[l], b_short=row(b_short[l]),
            filt_w1=jnp.zeros((LANES, filt_w1.shape[2]), F32).at[:emb].set(filt_w1[l]),
            filt_b1=row(filt_b1[l]), filt_w2=filt_w2[l], filt_b2=row(filt_b2[l]), filt_w3=filt_w3[l],
            filt_b3=row(filt_b3[l]), filt_w_out=filt_w_out[l], filt_freq=row(filt_freq[l]),
            filt_decay=filt_decay[l], hyena_skip=row(hyena_skip[l]), g_fourier=row(g_fourier[l]),
            g_hyena=row(g_hyena[l]), w_out=w_out[l].astype(BF16), b_out=row(b_out[l]),
            ln1_g=row(ln1_g[l]), ln1_b=row(ln1_b[l]),
            w_router_t=w_router[l].T, b_router=b_router[l].reshape(-1, 1))
        routed = [_mixer(x, l == 0, row(ln_in_g), row(ln_in_b), lay, consts, alpha) for x in xs]
        outs = _routed_moe(routed, w_gate_up[l], b_gate_up[l], w_down[l], b_down[l],
                           row(ln2_g[l]), row(ln2_b[l]), alpha)
        xs = [o.reshape(x.shape) for o, x in zip(outs, xs)]
    return (xs[0], xs[1])
```

```python
import functools
import math

import numpy as np
import jax
import jax.numpy as jnp
from jax import lax
from jax.experimental import pallas as pl
from jax.experimental.pallas import tpu as pltpu

F32 = jnp.float32
BF16 = jnp.bfloat16
I32 = jnp.int32

LN_EPS = 1e-5
RMS_EPS = 1e-6
SWIGLU_LIMIT = 7.0
SWIGLU_ALPHA = 1.702
TOP_K = 4
FOURIER_GROUP_DIM = 64
FILTER_BANDS = 16

LANES = 128
MINOR = 128
ROW_TILE = 512
IN_TILE = 512
MAJOR_BLOCK_BYTES = 12 * 1024 * 1024
TOKEN_TILE = 512
SEG_ALIGN = 16
EXPERT_BLOCK = 512
VMEM_LIMIT = 56 * 1024 * 1024


def _params(n_grid, semantics="parallel"):
    return pltpu.CompilerParams(dimension_semantics=(semantics,) * n_grid,
                                vmem_limit_bytes=VMEM_LIMIT)


def _dot(a, b):
    return jnp.dot(a, b, preferred_element_type=F32)


def _split(a):
    hi = a.astype(BF16)
    lo = (a - hi.astype(F32)).astype(BF16)
    return hi, lo


def _dot3(a, b):
    ah, al = _split(a)
    bh, bl = _split(b)
    return _dot(ah, bh) + _dot(ah, bl) + _dot(al, bh)


def _layer_norm(x, g, b):
    mu = jnp.mean(x, axis=-1, keepdims=True)
    xc = x - mu
    var = jnp.mean(xc * xc, axis=-1, keepdims=True)
    return xc * lax.rsqrt(var + LN_EPS) * g + b


def _rms_norm(x, g):
    return x * lax.rsqrt(jnp.mean(x * x, axis=-1, keepdims=True) + RMS_EPS) * g


def _in_kernel(x_ref, xp_ref, xq_ref, lng_ref, lnb_ref, win_ref, bin_ref, bd_ref, wsh_ref, bsh_ref,
               z_ref, x0_ref, w_ref, *, tiles_per_seq, apply_ln, fw):
    it = pl.program_id(0) % tiles_per_seq
    tm = x_ref.shape[0]

    def norm(x):
        return _layer_norm(x, lng_ref[...], lnb_ref[...]) if apply_ln else x

    xn = norm(x_ref[...])
    halo = norm(jnp.concatenate([xp_ref[...], xq_ref[...]], axis=0))
    proj = _dot(xn.astype(BF16), win_ref[...]) + bin_ref[...]
    projh = _dot(halo.astype(BF16), win_ref[:, fw:]) + bin_ref[:, fw:]

    pf = proj[:, :fw].astype(BF16)
    zs = [_dot(pf[:, c0:c0 + LANES], bd_ref[...]) for c0 in range(0, fw, LANES)]
    z_ref[0, 0] = jnp.concatenate([z[:, :LANES] for z in zs], axis=1)
    z_ref[0, 1] = jnp.concatenate([z[:, LANES:] for z in zs], axis=1)

    ph = proj[:, fw:]
    prev_row = jnp.where(it > 0, projh[7:8], 0.0)
    next_row = jnp.where(it < tiles_per_seq - 1, projh[8:9], 0.0)
    rows = lax.broadcasted_iota(I32, ph.shape, 0)
    up = jnp.where(rows == 0, prev_row, pltpu.roll(ph, 1, 0))
    dn = jnp.where(rows == tm - 1, next_row, pltpu.roll(ph, tm - 1, 0))
    wsh = wsh_ref[...]
    uc = up * wsh[0:1] + ph * wsh[1:2] + dn * wsh[2:3] + bsh_ref[...]
    hw = uc.shape[1] // 3
    x0_ref[...] = uc[:, :hw]
    w_ref[...] = uc[:, 2 * hw:] * uc[:, hw:2 * hw]


def _in_proj(x2d, batch, seq, fw, lng, lnb, win_bf, b_in, bd_bf, w_short, b_short, apply_ln):
    t, d = x2d.shape
    inw = win_bf.shape[1]
    hw = (inw - fw) // 3
    tm = IN_TILE
    tps = seq // tm
    r8 = tm // 8
    nblk8 = t // 8
    const = lambda i: (0, 0)
    kern = functools.partial(_in_kernel, tiles_per_seq=tps, apply_ln=apply_ln, fw=fw)
    return pl.pallas_call(
        kern,
        grid=(t // tm,),
        in_specs=[
            pl.BlockSpec((tm, d), lambda i: (i, 0)),
            pl.BlockSpec((8, d), lambda i: (jnp.maximum(i * r8 - 1, 0), 0)),
            pl.BlockSpec((8, d), lambda i: (jnp.minimum((i + 1) * r8, nblk8 - 1), 0)),
            pl.BlockSpec((1, d), const), pl.BlockSpec((1, d), const),
            pl.BlockSpec((d, inw), const), pl.BlockSpec((1, inw), const),
            pl.BlockSpec(bd_bf.shape, const),
            pl.BlockSpec((3, 3 * hw), const), pl.BlockSpec((1, 3 * hw), const),
        ],
        out_specs=[
            pl.BlockSpec((1, 2, tm, fw), lambda i: (i // tps, 0, i % tps, 0)),
            pl.BlockSpec((tm, hw), lambda i: (i, 0)),
            pl.BlockSpec((tm, hw), lambda i: (i, 0)),
        ],
        out_shape=[
            jax.ShapeDtypeStruct((batch, 2, seq, fw), F32),
            jax.ShapeDtypeStruct((t, hw), F32),
            jax.ShapeDtypeStruct((t, hw), F32),
        ],
        compiler_params=_params(1),
        name="in_proj",
    )(x2d, x2d, x2d, lng, lnb, win_bf, b_in, bd_bf, w_short, b_short)


FILTER_HALO = 8


def _filter_kernel(bands_ref, w1_ref, b1_ref, w2_ref, b2_ref, w3_ref, b3_ref, wl_ref, fr_ref, dec_ref,
                   k_ref, asum_ref, tc_ref, ts_ref, kf_buf, kb_buf, sems, *, seq):
    i = pl.program_id(0)
    nsteps = pl.num_programs(0)
    tile, c = kf_buf.shape
    ext = tc_ref.shape[0]
    rep = c // LANES
    w_unit = 2.0 * math.pi / seq
    bands = bands_ref[...]

    @pl.when(i == 0)
    def _():
        arg = bands * (w_unit * lax.broadcasted_iota(I32, (ext, LANES), 0).astype(F32))
        tc_ref[...] = jnp.cos(arg)
        ts_ref[...] = jnp.sin(arg)
        asum_ref[...] = jnp.zeros_like(asum_ref)

    j0 = i * tile
    base = bands * (w_unit * j0.astype(F32))
    c0, s0 = jnp.cos(base), jnp.sin(base)
    tc, ts = tc_ref[...], ts_ref[...]
    cosf = c0 * tc - s0 * ts
    sinf = s0 * tc + c0 * ts
    j = j0 + lax.broadcasted_iota(I32, (ext, LANES), 0)
    t = j.astype(F32) * (1.0 / (seq - 1))
    lane = lax.broadcasted_iota(I32, (ext, LANES), 1)
    feat = jnp.where(lane == 0, t,
                     jnp.where(lane <= FILTER_BANDS, cosf,
                               jnp.where(lane <= 2 * FILTER_BANDS, -sinf, 0.0)))
    fr = fr_ref[...]
    h = jnp.sin(fr * (_dot3(feat, w1_ref[...]) + b1_ref[...]))
    h = jnp.sin(fr * (_dot3(h, w2_ref[...]) + b2_ref[...]))
    h = jnp.sin(fr * (_dot3(h, w3_ref[...]) + b3_ref[...]))
    hl = _dot3(h, wl_ref[...])
    t4 = jnp.concatenate([t] * rep, axis=1)
    j4 = jnp.concatenate([j] * rep, axis=1)
    dec = jnp.abs(dec_ref[...])
    hf = hl[:, :c] * jnp.exp(-t4 * dec[0:1])
    hb = hl[:, c:] * jnp.exp(-t4 * dec[1:2])
    hb = jnp.where(j4 < seq, hb, 0.0)
    asum_ref[...] += jnp.sum(jnp.abs(hf[:tile]) + jnp.abs(hb[:tile]), axis=0, keepdims=True)

    rev = (lax.broadcasted_iota(I32, (tile, ext), 1) == tile - lax.broadcasted_iota(I32, (tile, ext), 0))
    rev = jnp.where(rev, 1.0, 0.0).astype(BF16)
    b1 = hb.astype(BF16)
    r1 = hb - b1.astype(F32)
    b2 = r1.astype(BF16)
    b3 = (r1 - b2.astype(F32)).astype(BF16)
    kb = _dot(rev, b1) + _dot(rev, b2) + _dot(rev, b3)

    def copies():
        return (pltpu.make_async_copy(kf_buf, k_ref.at[pl.ds(pl.multiple_of(j0, tile), tile)], sems.at[0]),
                pltpu.make_async_copy(kb_buf, k_ref.at[pl.ds(pl.multiple_of(2 * seq - j0 - tile, tile), tile)],
                                      sems.at[1]))

    @pl.when(i > 0)
    def _():
        for cp in copies():
            cp.wait()

    kf_buf[...] = jnp.where(j4[:tile] == 0, hf[:tile] + hb[:tile], hf[:tile])
    kb_buf[...] = kb
    for cp in copies():
        cp.start()

    @pl.when(i == nsteps - 1)
    def _():
        for cp in copies():
            cp.wait()


def _hyena_filter(seq, bands_row, w1p, b1, w2, b2, w3, b3, wl, freq, decay):
    c = decay.shape[1]
    tile = ROW_TILE
    ext = tile + FILTER_HALO
    full = lambda a: pl.BlockSpec(a.shape, lambda i: (0,) * a.ndim)
    args = (bands_row, w1p, b1, w2, b2, w3, b3, wl, freq, decay)
    return pl.pallas_call(
        functools.partial(_filter_kernel, seq=seq),
        grid=(seq // tile,),
        in_specs=[full(a) for a in args],
        out_specs=[pl.BlockSpec(memory_space=pl.ANY), pl.BlockSpec((1, c), lambda i: (0, 0))],
        out_shape=[jax.ShapeDtypeStruct((2 * seq, c), F32), jax.ShapeDtypeStruct((1, c), F32)],
        scratch_shapes=[pltpu.VMEM((ext, LANES), F32), pltpu.VMEM((ext, LANES), F32),
                        pltpu.VMEM((tile, c), F32), pltpu.VMEM((tile, c), F32),
                        pltpu.SemaphoreType.DMA((2,))],
        compiler_params=_params(1, "arbitrary"),
        name="hyena_filter",
    )(*args)


def _major_compute(g, xbuf, obuf, slot):
    _, n_ct, bsz, k, sub, lanes = xbuf.shape
    m = obuf.shape[3]
    x2 = xbuf.reshape(2 * n_ct * bsz * k * sub, lanes)
    o2 = obuf.reshape(2 * n_ct * bsz * m * sub, lanes)
    for t in range(n_ct * bsz):
        xbase = (slot * n_ct * bsz + t) * k * sub
        obase = (slot * n_ct * bsz + t) * m * sub
        for b in range(sub):
            xb = x2[pl.ds(xbase + b, k, stride=sub), :]
            o2[pl.ds(obase + b, m, stride=sub), :] = _dot(g, xb.astype(BF16))


def _major_kernel(g_ref, x_hbm, o_hbm, xbuf, obuf, isem, osem, *, nj):
    s = pl.program_id(0)
    ns = pl.num_programs(0)
    slot = s % 2
    _, n_ct, bsz, k, sub, lanes = xbuf.shape
    m = obuf.shape[3]

    def copies(step, sl, hbm, buf, sem, inbound):
        i = step // nj
        j = step % nj
        out = []
        for ct in range(n_ct):
            tile = hbm.at[pl.ds(i * bsz, bsz), :, pl.ds(pl.multiple_of(j * sub, sub), sub),
                          pl.ds(ct * lanes, lanes)]
            out.append(pltpu.make_async_copy(tile, buf.at[sl, ct], sem.at[sl]) if inbound
                       else pltpu.make_async_copy(buf.at[sl, ct], tile, sem.at[sl]))
        return out

    def start(cps):
        for cp in cps:
            cp.start()

    def wait(cps):
        for cp in cps:
            cp.wait()

    @pl.when(s == 0)
    def _():
        start(copies(0, 0, x_hbm, xbuf, isem, True))

    @pl.when(s + 1 < ns)
    def _():
        start(copies(s + 1, 1 - slot, x_hbm, xbuf, isem, True))

    wait(copies(s, slot, x_hbm, xbuf, isem, True))

    @pl.when(s >= 2)
    def _():
        wait(copies(s - 2, slot, o_hbm, obuf, osem, False))

    _major_compute(g_ref[...], xbuf, obuf, slot)
    start(copies(s, slot, o_hbm, obuf, osem, False))

    @pl.when(s == ns - 1)
    def _():
        wait(copies(s, slot, o_hbm, obuf, osem, False))

        @pl.when(s >= 1)
        def _():
            wait(copies(s - 1, 1 - slot, o_hbm, obuf, osem, False))


def _dft_major(g_bf, x4):
    b, k, n2, c = x4.shape
    m = g_bf.shape[0]
    sub = 8
    n_ct = c // LANES
    bsz = max(1, min(b, MAJOR_BLOCK_BYTES // ((k + m) * sub * c * 4)))
    while b % bsz:
        bsz -= 1
    nj = n2 // sub
    return pl.pallas_call(
        functools.partial(_major_kernel, nj=nj),
        grid=((b // bsz) * nj,),
        in_specs=[pl.BlockSpec((m, k), lambda s: (0, 0)), pl.BlockSpec(memory_space=pl.ANY)],
        out_specs=pl.BlockSpec(memory_space=pl.ANY),
        out_shape=jax.ShapeDtypeStruct((b, m, n2, c), F32),
        scratch_shapes=[pltpu.VMEM((2, n_ct, bsz, k, sub, LANES), F32),
                        pltpu.VMEM((2, n_ct, bsz, m, sub, LANES), F32),
                        pltpu.SemaphoreType.DMA((2,)), pltpu.SemaphoreType.DMA((2,))],
        compiler_params=_params(1, "arbitrary"),
        name="dft_major",
    )(g_bf, x4)


def _minor_kernel(*refs, mode, scale, k1b):
    if mode == "conv":
        y_ref, twb_ref, twr_ref, wf_ref, wi_ref, kf_ref, o_ref, ks_ref = refs
    else:
        y_ref, twb_ref, twr_ref, wf_ref, o_ref = refs
    n2 = y_ref.shape[3]
    ct = y_ref.shape[4]
    rep = ct // LANES

    def forward(re, im, cs, sn, w):
        return _dot(w, jnp.concatenate([re * cs + im * sn, im * cs - re * sn], axis=0).astype(BF16))

    cb, sb = twb_ref[0, 0], twb_ref[1, 0]
    for j in range(k1b):
        cr, sr = twr_ref[0, j], twr_ref[1, j]
        cs = jnp.concatenate([cb * cr - sb * sr] * rep, axis=1)
        sn = jnp.concatenate([sb * cr + cb * sr] * rep, axis=1)
        if mode == "real":
            o_ref[0, :, j, :] = forward(y_ref[0, 0, j], y_ref[0, 1, j], cs, sn, wf_ref[:n2]) * scale
            continue

        @pl.when(pl.program_id(1) == 0)
        def _():
            ks_ref[j] = forward(kf_ref[0, 0, j], kf_ref[0, 1, j], cs, sn, wf_ref[...]) * scale

        z = forward(y_ref[0, 0, j], y_ref[0, 1, j], cs, sn, wf_ref[...])
        zr, zi = z[:n2], z[n2:]
        kr = ks_ref[j, :n2]
        ki = ks_ref[j, n2:]
        p = jnp.concatenate([zr * kr - zi * ki, zr * ki + zi * kr], axis=0).astype(BF16)
        q = _dot(wi_ref[...], p)
        qr, qi = q[:n2], q[n2:]
        o_ref[0, 0, j] = qr * cs - qi * sn
        o_ref[0, 1, j] = qi * cs + qr * sn


def _dft_minor(y5, wf_bf, mode, scale=1.0, wi_bf=None, kf5=None):
    b, _, n1, n2, c = y5.shape
    k1b = min(8, n1)
    ct = c
    tw_block, tw_rem = _twiddles(n1, n2, k1b)
    grid = (n1 // k1b, b)
    dspec = pl.BlockSpec((1, 2, k1b, n2, ct), lambda j, i: (i, 0, j, 0, 0))
    in_specs = [dspec,
                pl.BlockSpec((2, 1, n2, LANES), lambda j, i: (0, j, 0, 0)),
                pl.BlockSpec((2, k1b, n2, LANES), lambda j, i: (0, 0, 0, 0)),
                pl.BlockSpec(wf_bf.shape, lambda j, i: (0, 0))]
    args = [y5, tw_block, tw_rem, wf_bf]
    scratch = []
    if mode == "conv":
        in_specs += [pl.BlockSpec(wi_bf.shape, lambda j, i: (0, 0)),
                     pl.BlockSpec((1, 2, k1b, n2, ct), lambda j, i: (0, 0, j, 0, 0))]
        args += [wi_bf, kf5]
        scratch = [pltpu.VMEM((k1b, 2 * n2, ct), F32)]
    if mode == "real":
        out_spec = pl.BlockSpec((1, n2, k1b, ct), lambda j, i: (i, 0, j, 0))
        out_shape = jax.ShapeDtypeStruct((b, n2, n1, c), F32)
    else:
        out_spec = dspec
        out_shape = jax.ShapeDtypeStruct((b, 2, n1, n2, c), F32)
    return pl.pallas_call(
        functools.partial(_minor_kernel, mode=mode, scale=scale, k1b=k1b),
        grid=grid, in_specs=in_specs, out_specs=out_spec, out_shape=out_shape,
        scratch_shapes=scratch,
        compiler_params=pltpu.CompilerParams(dimension_semantics=("parallel", "arbitrary"),
                                             vmem_limit_bytes=VMEM_LIMIT),
        name="dft_minor_" + mode,
    )(*args)


def _cs(num, den):
    ang = 2.0 * np.pi * (num % den) / den
    return np.cos(ang), np.sin(ang)


def _complex_block(c, s, sign):
    return np.block([[c, -sign * s], [sign * s, c]]).astype(np.float32)


def _major_matrix(n1, a_in, a_out, sign, real_in=False):
    c, s = _cs(np.outer(np.arange(a_out), np.arange(a_in)), n1)
    g = _complex_block(c, s, sign)
    return g[:, :a_in] if real_in else g


def _minor_matrix(n2, sign):
    c, s = _cs(np.outer(np.arange(n2), np.arange(n2)), n2)
    return _complex_block(c, s, sign)


def _twiddles(n1, n2, k1b):
    def table(k1):
        c, s = _cs(np.outer(k1, np.arange(n2)), n1 * n2)
        tw = jnp.asarray(np.stack([c, s]).astype(np.float32))
        return jnp.broadcast_to(tw[..., None], (2, len(k1), n2, LANES))
    return table(np.arange(0, n1, k1b)), table(np.arange(k1b))


def _out_kernel(yf_ref, yc_ref, x0_ref, w_ref, x_ref, lng_ref, lnb_ref, inv_ref, skip_ref, gf_ref, gh_ref,
                wo_ref, bo_ref, g1_ref, b1_ref, wr_ref, br_ref, x1_ref, idx_ref, gate_ref, cnt_ref,
                *, alpha, apply_ln, top_k):
    c = yc_ref.shape[1]
    w = w_ref[...]
    yh = x0_ref[...] * (yc_ref[...] * inv_ref[...] + skip_ref[...] * w)
    mf = _rms_norm(yf_ref[...], gf_ref[...]).astype(BF16)
    mh = _rms_norm(yh, gh_ref[...]).astype(BF16)
    m = _dot(mf, wo_ref[:c]) + _dot(mh, wo_ref[c:]) + bo_ref[...]
    xn = _layer_norm(x_ref[...], lng_ref[...], lnb_ref[...]) if apply_ln else x_ref[...]
    x1 = _layer_norm(alpha * xn + m, g1_ref[...], b1_ref[...])
    x1_ref[...] = x1
    _route_tile(x1, wr_ref[...], br_ref[...], idx_ref, gate_ref, cnt_ref, top_k)


def _out_proj(yf, yc, x0, w, x, lng, lnb, inv_norm, skip, gf, gh, wo_bf, bo, g1, b1, wr_t, br_col, alpha,
              apply_ln):
    t, d = x.shape
    c = yc.shape[1]
    e = wr_t.shape[0]
    tm = TOKEN_TILE
    nt = t // tm
    row = lambda i: (i, 0)
    const = lambda i: (0, 0)
    tile3 = lambda i: (i, 0, 0)
    return pl.pallas_call(
        functools.partial(_out_kernel, alpha=alpha, apply_ln=apply_ln, top_k=TOP_K),
        grid=(nt,),
        in_specs=[
            pl.BlockSpec((tm, c), row),
            pl.BlockSpec((tm, c), row), pl.BlockSpec((tm, c), row), pl.BlockSpec((tm, c), row),
            pl.BlockSpec((tm, d), row),
            pl.BlockSpec((1, d), const), pl.BlockSpec((1, d), const),
            pl.BlockSpec((1, c), const), pl.BlockSpec((1, c), const),
            pl.BlockSpec((1, c), const), pl.BlockSpec((1, c), const),
            pl.BlockSpec(wo_bf.shape, const), pl.BlockSpec((1, d), const),
            pl.BlockSpec((1, d), const), pl.BlockSpec((1, d), const),
            pl.BlockSpec((e, d), const), pl.BlockSpec((e, 1), const),
        ],
        out_specs=[pl.BlockSpec((tm, d), row),
                   pl.BlockSpec((1, 2 * TOP_K, tm), tile3),
                   pl.BlockSpec((1, 8, tm), tile3),
                   pl.BlockSpec((1, e, 1), tile3)],
        out_shape=[jax.ShapeDtypeStruct((t, d), F32),
                   jax.ShapeDtypeStruct((nt, 2 * TOP_K, tm), I32),
                   jax.ShapeDtypeStruct((nt, 8, tm), F32),
                   jax.ShapeDtypeStruct((nt, e, 1), I32)],
        compiler_params=_params(1),
        name="out_proj",
    )(yf, yc, x0, w, x, lng, lnb, inv_norm, skip, gf, gh, wo_bf, bo, g1, b1, wr_t, br_col)


def _route_tile(x1, wr, br, idx_ref, gate_ref, cnt_ref, top_k):
    tt = x1.shape[0]
    e = wr.shape[0]
    nt = (((1,), (1,)), ((), ()))
    xh, xl = _split(x1)
    wh, wl = _split(wr)
    logits = (lax.dot_general(wh, xh, nt, preferred_element_type=F32)
              + lax.dot_general(wh, xl, nt, preferred_element_type=F32)
              + lax.dot_general(wl, xh, nt, preferred_element_type=F32)) + br
    iota_e = lax.broadcasted_iota(I32, (e, tt), 0).astype(F32)
    l = logits
    tops, idxs, hots = [], [], []
    for _ in range(top_k):
        m = jnp.max(l, axis=0, keepdims=True)
        idx = jnp.min(jnp.where(l == m, iota_e, float(e)), axis=0, keepdims=True)
        hot = iota_e == idx
        l = jnp.where(hot, -jnp.inf, l)
        tops.append(m)
        idxs.append(idx)
        hots.append(hot)
    ex = [jnp.exp(m - tops[0]) for m in tops]
    den = ex[0]
    for v in ex[1:]:
        den = den + v
    gates = [v / den for v in ex]
    hot_all = jnp.zeros((e, tt), F32)
    for hot in hots:
        hot_all = jnp.where(hot, 1.0, hot_all)
    upper = (lax.broadcasted_iota(I32, (tt, tt), 0) < lax.broadcasted_iota(I32, (tt, tt), 1))
    before = _dot(hot_all.astype(BF16), jnp.where(upper, 1.0, 0.0).astype(BF16))
    ranks = [jnp.sum(jnp.where(hot, before, 0.0), axis=0, keepdims=True).astype(I32) for hot in hots]
    pad = 8 - top_k
    idx_ref[0] = jnp.concatenate([v.astype(I32) for v in idxs] + ranks, axis=0)
    gate_ref[0] = jnp.concatenate(gates + [jnp.zeros((pad, tt), F32)], axis=0)
    cnt_ref[0] = jnp.sum(hot_all, axis=1, keepdims=True).astype(I32)


def _dual_specs(shape, nt0):
    first = pl.BlockSpec(shape, lambda i, *_: (jnp.minimum(i, nt0 - 1),) + (0,) * (len(shape) - 1))
    second = pl.BlockSpec(shape, lambda i, *_: (jnp.maximum(i - nt0, 0),) + (0,) * (len(shape) - 1))
    return first, second


SEG_PIECES = 3
SEG_CHUNK = SEG_ALIGN << SEG_PIECES


def _segment_copies(first, count, loff_s, len_s, goff_s, local, remote, sem, to_remote, wait):
    def copy(lstart, gstart, size):
        lref = local.at[pl.ds(pl.multiple_of(lstart, SEG_ALIGN), size)]
        rref = remote.at[pl.ds(pl.multiple_of(gstart, SEG_ALIGN), size)]
        cp = (pltpu.make_async_copy(lref, rref, sem) if to_remote
              else pltpu.make_async_copy(rref, lref, sem))
        if wait:
            cp.wait()
        else:
            cp.start()

    def body(e, carry):
        s = first + e
        n = len_s[s] // SEG_ALIGN
        lo = 0 if loff_s is None else loff_s[s]
        go = goff_s[s]
        chunks = n >> SEG_PIECES

        def chunk(i, c):
            copy(lo + i * SEG_CHUNK, go + i * SEG_CHUNK, SEG_CHUNK)
            return c

        lax.fori_loop(0, chunks, chunk, 0)
        rem = n & ((1 << SEG_PIECES) - 1)
        base = chunks * SEG_CHUNK
        for p in reversed(range(SEG_PIECES)):
            start = base + ((rem >> (p + 1)) << (p + 1)) * SEG_ALIGN

            @pl.when(((rem >> p) & 1) == 1)
            def _():
                copy(lo + start, go + start, SEG_ALIGN << p)
        return carry

    lax.fori_loop(0, count, body, 0)


def _wait_tile(step, n_experts, loff_s, len_s, local, remote, sem, to_remote):
    last = step * n_experts + n_experts - 1
    n = (loff_s[last] + len_s[last]) // SEG_ALIGN
    for p in range((local.shape[0] // SEG_ALIGN).bit_length()):
        size = SEG_ALIGN << p

        @pl.when(((n >> p) & 1) == 1)
        def _():
            lref = local.at[pl.ds(0, size)]
            rref = remote.at[pl.ds(0, size)]
            cp = (pltpu.make_async_copy(lref, rref, sem) if to_remote
                  else pltpu.make_async_copy(rref, lref, sem))
            cp.wait()


def _dispatch_kernel(loff_s, seg_s, goff_s, plen_s, poff_s, nb_s, xa_ref, xb_ref, idx_ref, xs_ref, pos_ref,
                     buf, sems, *, n_experts, top_k, nt0):
    j = pl.program_id(0)
    nt = pl.num_programs(0)
    slot = j % 2
    tt = xa_ref.shape[0]
    lr = buf.shape[1]
    idx = idx_ref[0]
    tope = idx[:top_k]
    base = jnp.zeros_like(tope)
    for e in range(n_experts):
        base = jnp.where(tope == e, loff_s[j * n_experts + e], base)
    pos = base + idx[top_k:]
    pos_ref[0] = jnp.concatenate([pos, jnp.zeros((8 - top_k, tt), I32)], axis=0)
    rows = lax.broadcasted_iota(I32, (lr, tt), 0)
    perm = jnp.zeros((lr, tt), F32)
    for k in range(top_k):
        perm = jnp.where(rows == pos[k:k + 1], 1.0, perm)
    x = jnp.where(j < nt0, xa_ref[...], xb_ref[...])
    buf[slot] = _dot(perm.astype(BF16), x.astype(BF16)).astype(BF16)

    def drained(step, s):
        _wait_tile(step, n_experts, loff_s, seg_s, buf.at[s], xs_ref, sems.at[s], True)

    _segment_copies(j * n_experts, n_experts, loff_s, seg_s, goff_s, buf.at[slot], xs_ref, sems.at[slot],
                    True, False)

    @pl.when(j > 0)
    def _():
        drained(j - 1, 1 - slot)

    @pl.when(j == nt - 1)
    def _():
        drained(j, slot)
        zrows = EXPERT_BLOCK
        buf[0, :zrows] = jnp.zeros((zrows, buf.shape[2]), BF16)
        _segment_copies(0, n_experts, None, plen_s, poff_s, buf.at[0], xs_ref, sems.at[0], True, False)
        _segment_copies(0, n_experts, None, plen_s, poff_s, buf.at[0], xs_ref, sems.at[0], True, True)
        n_tail = xs_ref.shape[0] // zrows - nb_s[0]

        def tail_copy(i):
            dst = xs_ref.at[pl.ds(pl.multiple_of((nb_s[0] + i) * zrows, zrows), zrows)]
            return pltpu.make_async_copy(buf.at[0, pl.ds(0, zrows)], dst, sems.at[0])

        lax.fori_loop(0, n_tail, lambda i, c: (tail_copy(i).start(), c)[1], 0)
        lax.fori_loop(0, n_tail, lambda i, c: (tail_copy(i).wait(), c)[1], 0)


def _dispatch(x1s, idx, loff, seg, goff, plen, poff, n_blocks, n_rows, local_rows):
    d = x1s[0].shape[1]
    tt = TOKEN_TILE
    nt0 = x1s[0].shape[0] // tt
    nt = nt0 + x1s[1].shape[0] // tt
    e = loff.shape[0] // nt
    xa, xb = _dual_specs((tt, d), nt0)
    grid_spec = pltpu.PrefetchScalarGridSpec(
        num_scalar_prefetch=6,
        grid=(nt,),
        in_specs=[xa, xb, pl.BlockSpec((1, 2 * TOP_K, tt), lambda i, *_: (i, 0, 0))],
        out_specs=[pl.BlockSpec(memory_space=pl.ANY),
                   pl.BlockSpec((1, 8, tt), lambda i, *_: (i, 0, 0))],
        scratch_shapes=[pltpu.VMEM((2, local_rows, d), BF16), pltpu.SemaphoreType.DMA((2,))],
    )
    return pl.pallas_call(
        functools.partial(_dispatch_kernel, n_experts=e, top_k=TOP_K, nt0=nt0),
        grid_spec=grid_spec,
        out_shape=[jax.ShapeDtypeStruct((n_rows, d), BF16), jax.ShapeDtypeStruct((nt, 8, tt), I32)],
        compiler_params=_params(1, "arbitrary"),
        name="moe_dispatch",
    )(loff, seg, goff, plen, poff, n_blocks, x1s[0], x1s[1], idx)


def _combine_kernel(loff_s, seg_s, goff_s, ys_ref, pos_ref, gate_ref, xa_ref, xb_ref, g2_ref, b2_ref,
                    oa_ref, ob_ref, buf, sems, *, n_experts, top_k, alpha, nt0):
    j = pl.program_id(0)
    nt = pl.num_programs(0)
    slot = j % 2
    tt = xa_ref.shape[0]
    lr = buf.shape[1]

    def prefetch(step, s):
        buf[s] = jnp.zeros((lr, buf.shape[2]), BF16)
        _segment_copies(step * n_experts, n_experts, loff_s, seg_s, goff_s, buf.at[s], ys_ref, sems.at[s],
                        False, False)

    @pl.when(j == 0)
    def _():
        prefetch(0, 0)

    @pl.when(j + 1 < nt)
    def _():
        prefetch(j + 1, 1 - slot)

    _wait_tile(j, n_experts, loff_s, seg_s, buf.at[slot], ys_ref, sems.at[slot], False)
    pos = pos_ref[...]
    gate = gate_ref[...]
    cols = lax.broadcasted_iota(I32, (tt, lr), 1)
    wsel = jnp.zeros((tt, lr), F32)
    for k in range(top_k):
        wsel = jnp.where(cols == pos[:, k:k + 1], gate[:, k:k + 1], wsel)
    moe = _dot(wsel.astype(BF16), buf[slot])
    x1 = jnp.where(j < nt0, xa_ref[...], xb_ref[...])
    out = _layer_norm(alpha * x1 + moe, g2_ref[...], b2_ref[...])

    @pl.when(j < nt0)
    def _():
        oa_ref[...] = out

    @pl.when(j >= nt0)
    def _():
        ob_ref[...] = out


def _combine(ys, pos_tm, gate_tm, x1s, g2, b2, loff, seg, goff, local_rows, alpha):
    d = x1s[0].shape[1]
    tt = TOKEN_TILE
    nt0 = x1s[0].shape[0] // tt
    nt = nt0 + x1s[1].shape[0] // tt
    e = loff.shape[0] // nt
    xa, xb = _dual_specs((tt, d), nt0)
    grid_spec = pltpu.PrefetchScalarGridSpec(
        num_scalar_prefetch=3,
        grid=(nt,),
        in_specs=[pl.BlockSpec(memory_space=pl.ANY),
                  pl.BlockSpec((tt, 8), lambda i, *_: (i, 0)),
                  pl.BlockSpec((tt, 8), lambda i, *_: (i, 0)),
                  xa, xb,
                  pl.BlockSpec((1, d), lambda i, *_: (0, 0)),
                  pl.BlockSpec((1, d), lambda i, *_: (0, 0))],
        out_specs=list(_dual_specs((tt, d), nt0)),
        scratch_shapes=[pltpu.VMEM((2, local_rows, d), BF16), pltpu.SemaphoreType.DMA((2,))],
    )
    return pl.pallas_call(
        functools.partial(_combine_kernel, n_experts=e, top_k=TOP_K, alpha=alpha, nt0=nt0),
        grid_spec=grid_spec,
        out_shape=[jax.ShapeDtypeStruct(x.shape, F32) for x in x1s],
        compiler_params=_params(1, "arbitrary"),
        name="moe_combine",
    )(loff, seg, goff, ys, pos_tm, gate_tm, x1s[0], x1s[1], g2, b2)


def _expert_kernel(be_s, nb_s, xs_ref, wgu_ref, bgu_ref, wd_ref, bd_ref, ys_ref, wgu_bf, wd_bf):
    b = pl.program_id(0)
    active = b < nb_s[0]
    fresh = jnp.logical_or(b == 0, be_s[b] != be_s[jnp.maximum(b - 1, 0)])

    @pl.when(jnp.logical_and(active, fresh))
    def _():
        wgu_bf[...] = wgu_ref[0].astype(BF16)
        wd_bf[...] = wd_ref[0].astype(BF16)

    @pl.when(active)
    def _():
        dff = wd_ref.shape[1]
        gu = _dot(xs_ref[...], wgu_bf[...]) + bgu_ref[0]
        gate = jnp.minimum(gu[:, :dff], SWIGLU_LIMIT)
        up = jnp.clip(gu[:, dff:], -SWIGLU_LIMIT, SWIGLU_LIMIT)
        h = (up + 1.0) * (gate * jax.nn.sigmoid(SWIGLU_ALPHA * gate))
        ys_ref[...] = (_dot(h.astype(BF16), wd_bf[...]) + bd_ref[0]).astype(BF16)

    @pl.when(jnp.logical_not(active))
    def _():
        ys_ref[...] = jnp.zeros_like(ys_ref)


def _experts(xs, block_expert, n_blocks, wgu, bgu, wd, bdn):
    n_rows, d = xs.shape
    bm = EXPERT_BLOCK
    f2 = wgu.shape[2]
    dff = wd.shape[1]
    blk = lambda b, be, nb: (jnp.minimum(b, nb[0] - 1), 0)
    wsel = lambda b, be, nb: (be[b], 0, 0)
    grid_spec = pltpu.PrefetchScalarGridSpec(
        num_scalar_prefetch=2,
        grid=(n_rows // bm,),
        in_specs=[pl.BlockSpec((bm, d), blk),
                  pl.BlockSpec((1, d, f2), wsel), pl.BlockSpec((1, 1, f2), wsel),
                  pl.BlockSpec((1, dff, d), wsel), pl.BlockSpec((1, 1, d), wsel)],
        out_specs=pl.BlockSpec((bm, d), lambda b, be, nb: (b, 0)),
        scratch_shapes=[pltpu.VMEM((d, f2), BF16), pltpu.VMEM((dff, d), BF16)],
    )
    return pl.pallas_call(
        _expert_kernel,
        grid_spec=grid_spec,
        out_shape=jax.ShapeDtypeStruct((n_rows, d), BF16),
        compiler_params=_params(1, "arbitrary"),
        name="moe_experts",
    )(block_expert, n_blocks, xs, wgu, bgu, wd, bdn)


def _routed_moe(routed, w_gate_up, b_gate_up, w_down, b_down, g2, b2, alpha):
    x1s = [r[0] for r in routed]
    idx, gate, cnt = (jnp.concatenate([r[i] for r in routed], axis=0) for i in (1, 2, 3))
    t = x1s[0].shape[0] + x1s[1].shape[0]
    e = cnt.shape[1]
    tt = TOKEN_TILE
    nt = t // tt
    bm = EXPERT_BLOCK

    cnt = cnt.reshape(nt, e)
    seg = (cnt + SEG_ALIGN - 1) // SEG_ALIGN * SEG_ALIGN
    loff = jnp.cumsum(seg, axis=1) - seg
    per_expert = jnp.sum(seg, axis=0)
    padded = (per_expert + bm - 1) // bm * bm
    blocks_end = jnp.cumsum(padded) // bm
    start = jnp.cumsum(padded) - padded
    goff = start[None, :] + jnp.cumsum(seg, axis=0) - seg
    local_rows = TOP_K * tt + e * SEG_ALIGN
    n_rows_max = -(-(TOP_K * t + (SEG_ALIGN - 1) * e * nt) // bm) * bm + e * bm
    n_blocks = blocks_end[-1:].astype(I32)
    bidx = jnp.minimum(jnp.arange(n_rows_max // bm, dtype=I32), n_blocks[0] - 1)
    block_expert = jnp.minimum(jnp.sum(bidx[:, None] >= blocks_end[None, :], axis=1), e - 1).astype(I32)
    flat = lambda a: a.reshape(-1).astype(I32)
    plen, poff = flat(padded - per_expert), flat(start + per_expert)

    xs, pos = _dispatch(x1s, idx, flat(loff), flat(seg), flat(goff), plen, poff, n_blocks, n_rows_max,
                        local_rows)
    ys = _experts(xs, block_expert, n_blocks, w_gate_up, b_gate_up.reshape(e, 1, -1),
                  w_down, b_down.reshape(e, 1, -1))
    pos_tm = jnp.swapaxes(pos, 1, 2).reshape(t, 8)
    gate_tm = jnp.swapaxes(gate, 1, 2).reshape(t, 8)
    return _combine(ys, pos_tm, gate_tm, x1s, g2, b2, flat(loff), flat(seg), flat(goff), local_rows, alpha)


def _mixer(x, apply_ln, lng, lnb, lay, consts, alpha):
    batch, seq, d = x.shape
    t = batch * seq
    fw = lay["g_fourier"].shape[1]
    n2 = MINOR
    x2d = x.reshape(t, d)
    z, x0, w = _in_proj(x2d, batch, seq, fw, lng, lnb, lay["w_in"], lay["b_in"], consts["bd"],
                        lay["w_short"], lay["b_short"], apply_ln)
    c = x0.shape[1]

    n1 = seq // n2
    g = jnp.asarray(_major_matrix(n1, n1, n1, -1)).astype(BF16)
    y = _dft_major(g, z.reshape(batch, 2 * n1, n2, fw))
    yf = _dft_minor(y.reshape(batch, 2, n1, n2, fw), consts["wf"], "real",
                    scale=1.0 / math.sqrt(seq * FOURIER_GROUP_DIM))

    n = 2 * seq
    m1 = n // n2
    kf, asum = _hyena_filter(seq, consts["bands"], lay["filt_w1"], lay["filt_b1"], lay["filt_w2"],
                             lay["filt_b2"], lay["filt_w3"], lay["filt_b3"], lay["filt_w_out"],
                             lay["filt_freq"], lay["filt_decay"])
    gk = jnp.asarray(_major_matrix(m1, m1, m1, -1, real_in=True)).astype(BF16)
    ky = _dft_major(gk, kf.reshape(1, m1, n2, c))
    gfw = jnp.asarray(_major_matrix(m1, m1 // 2, m1, -1)).astype(BF16)
    gin = jnp.asarray(_major_matrix(m1, m1, m1 // 2, +1)).astype(BF16)
    pairs = batch // 2
    u = _dft_major(gfw, w.reshape(pairs, m1, n2, c))
    v = _dft_minor(u.reshape(pairs, 2, m1, n2, c), consts["wf"], "conv", scale=1.0 / n,
                   wi_bf=consts["wi"], kf5=ky.reshape(1, 2, m1, n2, c))
    yc = _dft_major(gin, v.reshape(pairs, 2 * m1, n2, c))

    inv_norm = 1.0 / asum
    return _out_proj(yf.reshape(t, fw), yc.reshape(t, c), x0, w, x2d, lng, lnb, inv_norm,
                     lay["hyena_skip"], lay["g_fourier"], lay["g_hyena"], lay["w_out"], lay["b_out"],
                     lay["ln1_g"], lay["ln1_b"], lay["w_router_t"], lay["b_router"], alpha, apply_ln)


def kernel(x_prompt, x_sample, ln_in_g, ln_in_b, w_in, b_in, w_short, b_short, filt_w1, filt_b1, filt_w2, filt_b2, filt_w3, filt_b3, filt_w_out, filt_freq, filt_decay, hyena_skip, g_fourier, g_hyena, w_out, b_out, ln1_g, ln1_b, w_router, b_router, w_gate_up, b_gate_up, w_down, b_down, ln2_g, ln2_b):
    depth, d, in_width = w_in.shape
    fw = g_fourier.shape[1]
    alpha = (2.0 * depth) ** 0.25
    row = lambda a: a.reshape(1, -1)

    gd = FOURIER_GROUP_DIM
    cg, sg = _cs(np.outer(np.arange(gd), np.arange(gd)), gd)
    eye = np.eye(LANES // gd)
    bd = np.concatenate([np.kron(eye, cg), -np.kron(eye, sg)], axis=1).astype(np.float32)
    bands = jnp.linspace(1e-4, FILTER_BANDS - 1, FILTER_BANDS, dtype=F32)
    bands_row = jnp.zeros((1, LANES), F32).at[0, 1:1 + FILTER_BANDS].set(bands)
    bands_row = bands_row.at[0, 1 + FILTER_BANDS:1 + 2 * FILTER_BANDS].set(bands)
    consts = dict(bd=jnp.asarray(bd).astype(BF16), bands=bands_row,
                  wf=jnp.asarray(_minor_matrix(MINOR, -1)).astype(BF16),
                  wi=jnp.asarray(_minor_matrix(MINOR, +1)).astype(BF16))

    xs = [x_prompt, x_sample]
    for l in range(depth):
        emb = filt_w1.shape[1]
        lay = dict(
            w_in=w_in[l].astype(BF16), b_in=row(b_in[l]), w_short=w_short[l], b_short=row(b_short[l]),
            filt_w1=jnp.zeros((LANES, filt_w1.shape[2]), F32).at[:emb].set(filt_w1[l]),
            filt_b1=row(filt_b1[l]), filt_w2=filt_w2[l], filt_b2=row(filt_b2[l]), filt_w3=filt_w3[l],
            filt_b3=row(filt_b3[l]), filt_w_out=filt_w_out[l], filt_freq=row(filt_freq[l]),
            filt_decay=filt_decay[l], hyena_skip=row(hyena_skip[l]), g_fourier=row(g_fourier[l]),
            g_hyena=row(g_hyena[l]), w_out=w_out[l].astype(BF16), b_out=row(b_out[l]),
            ln1_g=row(ln1_g[l]), ln1_b=row(ln1_b[l]),
            w_router_t=w_router[l].T, b_router=b_router[l].reshape(-1, 1))
        routed = [_mixer(x, l == 0, row(ln_in_g), row(ln_in_b), lay, consts, alpha) for x in xs]
        outs = _routed_moe(routed, w_gate_up[l], b_gate_up[l], w_down[l], b_down[l],
                           row(ln2_g[l]), row(ln2_b[l]), alpha)
        xs = [o.reshape(x.shape) for o, x in zip(outs, xs)]
    return (xs[0], xs[1])
```

```python
import functools
import math

import numpy as np
import jax
import jax.numpy as jnp
from jax import lax
from jax.experimental import pallas as pl
from jax.experimental.pallas import tpu as pltpu

F32 = jnp.float32
BF16 = jnp.bfloat16
I32 = jnp.int32

LN_EPS = 1e-5
RMS_EPS = 1e-6
SWIGLU_LIMIT = 7.0
SWIGLU_ALPHA = 1.702
TOP_K = 4
FOURIER_GROUP_DIM = 64
FILTER_BANDS = 16

LANES = 128
MINOR = 128
ROW_TILE = 512
IN_TILE = 512
MAJOR_BLOCK_BYTES = 12 * 1024 * 1024
TOKEN_TILE = 512
SEG_ALIGN = 16
EXPERT_BLOCK = 512
VMEM_LIMIT = 56 * 1024 * 1024


def _params(n_grid, semantics="parallel"):
    return pltpu.CompilerParams(dimension_semantics=(semantics,) * n_grid,
                                vmem_limit_bytes=VMEM_LIMIT)


def _dot(a, b):
    return jnp.dot(a, b, preferred_element_type=F32)


def _split(a):
    hi = a.astype(BF16)
    lo = (a - hi.astype(F32)).astype(BF16)
    return hi, lo


def _dot3(a, b):
    ah, al = _split(a)
    bh, bl = _split(b)
    return _dot(ah, bh) + _dot(ah, bl) + _dot(al, bh)


def _layer_norm(x, g, b):
    mu = jnp.mean(x, axis=-1, keepdims=True)
    xc = x - mu
    var = jnp.mean(xc * xc, axis=-1, keepdims=True)
    return xc * lax.rsqrt(var + LN_EPS) * g + b


def _rms_norm(x, g):
    return x * lax.rsqrt(jnp.mean(x * x, axis=-1, keepdims=True) + RMS_EPS) * g


def _in_kernel(x_ref, xp_ref, xq_ref, lng_ref, lnb_ref, win_ref, bin_ref, bd_ref, wsh_ref, bsh_ref,
               z_ref, x0_ref, w_ref, *, tiles_per_seq, apply_ln, fw):
    it = pl.program_id(0) % tiles_per_seq
    tm = x_ref.shape[0]

    def norm(x):
        return _layer_norm(x, lng_ref[...], lnb_ref[...]) if apply_ln else x

    xn = norm(x_ref[...])
    halo = norm(jnp.concatenate([xp_ref[...], xq_ref[...]], axis=0))
    proj = _dot(xn.astype(BF16), win_ref[...]) + bin_ref[...]
    projh = _dot(halo.astype(BF16), win_ref[:, fw:]) + bin_ref[:, fw:]

    pf = proj[:, :fw].astype(BF16)
    zs = [_dot(pf[:, c0:c0 + LANES], bd_ref[...]) for c0 in range(0, fw, LANES)]
    z_ref[0, 0] = jnp.concatenate([z[:, :LANES] for z in zs], axis=1)
    z_ref[0, 1] = jnp.concatenate([z[:, LANES:] for z in zs], axis=1)

    ph = proj[:, fw:]
    prev_row = jnp.where(it > 0, projh[7:8], 0.0)
    next_row = jnp.where(it < tiles_per_seq - 1, projh[8:9], 0.0)
    rows = lax.broadcasted_iota(I32, ph.shape, 0)
    up = jnp.where(rows == 0, prev_row, pltpu.roll(ph, 1, 0))
    dn = jnp.where(rows == tm - 1, next_row, pltpu.roll(ph, tm - 1, 0))
    wsh = wsh_ref[...]
    uc = up * wsh[0:1] + ph * wsh[1:2] + dn * wsh[2:3] + bsh_ref[...]
    hw = uc.shape[1] // 3
    x0_ref[...] = uc[:, :hw]
    w_ref[...] = uc[:, 2 * hw:] * uc[:, hw:2 * hw]


def _in_proj(x2d, batch, seq, fw, lng, lnb, win_bf, b_in, bd_bf, w_short, b_short, apply_ln):
    t, d = x2d.shape
    inw = win_bf.shape[1]
    hw = (inw - fw) // 3
    tm = IN_TILE
    tps = seq // tm
    r8 = tm // 8
    nblk8 = t // 8
    const = lambda i: (0, 0)
    kern = functools.partial(_in_kernel, tiles_per_seq=tps, apply_ln=apply_ln, fw=fw)
    return pl.pallas_call(
        kern,
        grid=(t // tm,),
        in_specs=[
            pl.BlockSpec((tm, d), lambda i: (i, 0)),
            pl.BlockSpec((8, d), lambda i: (jnp.maximum(i * r8 - 1, 0), 0)),
            pl.BlockSpec((8, d), lambda i: (jnp.minimum((i + 1) * r8, nblk8 - 1), 0)),
            pl.BlockSpec((1, d), const), pl.BlockSpec((1, d), const),
            pl.BlockSpec((d, inw), const), pl.BlockSpec((1, inw), const),
            pl.BlockSpec(bd_bf.shape, const),
            pl.BlockSpec((3, 3 * hw), const), pl.BlockSpec((1, 3 * hw), const),
        ],
        out_specs=[
            pl.BlockSpec((1, 2, tm, fw), lambda i: (i // tps, 0, i % tps, 0)),
            pl.BlockSpec((tm, hw), lambda i: (i, 0)),
            pl.BlockSpec((tm, hw), lambda i: (i, 0)),
        ],
        out_shape=[
            jax.ShapeDtypeStruct((batch, 2, seq, fw), F32),
            jax.ShapeDtypeStruct((t, hw), F32),
            jax.ShapeDtypeStruct((t, hw), F32),
        ],
        compiler_params=_params(1),
        name="in_proj",
    )(x2d, x2d, x2d, lng, lnb, win_bf, b_in, bd_bf, w_short, b_short)


FILTER_HALO = 8


def _filter_kernel(bands_ref, w1_ref, b1_ref, w2_ref, b2_ref, w3_ref, b3_ref, wl_ref, fr_ref, dec_ref,
                   k_ref, asum_ref, tc_ref, ts_ref, kf_buf, kb_buf, sems, *, seq):
    i = pl.program_id(0)
    nsteps = pl.num_programs(0)
    tile, c = kf_buf.shape
    ext = tc_ref.shape[0]
    rep = c // LANES
    w_unit = 2.0 * math.pi / seq
    bands = bands_ref[...]

    @pl.when(i == 0)
    def _():
        arg = bands * (w_unit * lax.broadcasted_iota(I32, (ext, LANES), 0).astype(F32))
        tc_ref[...] = jnp.cos(arg)
        ts_ref[...] = jnp.sin(arg)
        asum_ref[...] = jnp.zeros_like(asum_ref)

    j0 = i * tile
    base = bands * (w_unit * j0.astype(F32))
    c0, s0 = jnp.cos(base), jnp.sin(base)
    tc, ts = tc_ref[...], ts_ref[...]
    cosf = c0 * tc - s0 * ts
    sinf = s0 * tc + c0 * ts
    j = j0 + lax.broadcasted_iota(I32, (ext, LANES), 0)
    t = j.astype(F32) * (1.0 / (seq - 1))
    lane = lax.broadcasted_iota(I32, (ext, LANES), 1)
    feat = jnp.where(lane == 0, t,
                     jnp.where(lane <= FILTER_BANDS, cosf,
                               jnp.where(lane <= 2 * FILTER_BANDS, -sinf, 0.0)))
    fr = fr_ref[...]
    h = jnp.sin(fr * (_dot3(feat, w1_ref[...]) + b1_ref[...]))
    h = jnp.sin(fr * (_dot3(h, w2_ref[...]) + b2_ref[...]))
    h = jnp.sin(fr * (_dot3(h, w3_ref[...]) + b3_ref[...]))
    hl = _dot3(h, wl_ref[...])
    t4 = jnp.concatenate([t] * rep, axis=1)
    j4 = jnp.concatenate([j] * rep, axis=1)
    dec = jnp.abs(dec_ref[...])
    hf = hl[:, :c] * jnp.exp(-t4 * dec[0:1])
    hb = hl[:, c:] * jnp.exp(-t4 * dec[1:2])
    hb = jnp.where(j4 < seq, hb, 0.0)
    asum_ref[...] += jnp.sum(jnp.abs(hf[:tile]) + jnp.abs(hb[:tile]), axis=0, keepdims=True)

    rev = (lax.broadcasted_iota(I32, (tile, ext), 1) == tile - lax.broadcasted_iota(I32, (tile, ext), 0))
    rev = jnp.where(rev, 1.0, 0.0).astype(BF16)
    b1 = hb.astype(BF16)
    r1 = hb - b1.astype(F32)
    b2 = r1.astype(BF16)
    b3 = (r1 - b2.astype(F32)).astype(BF16)
    kb = _dot(rev, b1) + _dot(rev, b2) + _dot(rev, b3)

    def copies():
        return (pltpu.make_async_copy(kf_buf, k_ref.at[pl.ds(pl.multiple_of(j0, tile), tile)], sems.at[0]),
                pltpu.make_async_copy(kb_buf, k_ref.at[pl.ds(pl.multiple_of(2 * seq - j0 - tile, tile), tile)],
                                      sems.at[1]))

    @pl.when(i > 0)
    def _():
        for cp in copies():
            cp.wait()

    kf_buf[...] = jnp.where(j4[:tile] == 0, hf[:tile] + hb[:tile], hf[:tile])
    kb_buf[...] = kb
    for cp in copies():
        cp.start()

    @pl.when(i == nsteps - 1)
    def _():
        for cp in copies():
            cp.wait()


def _hyena_filter(seq, bands_row, w1p, b1, w2, b2, w3, b3, wl, freq, decay):
    c = decay.shape[1]
    tile = ROW_TILE
    ext = tile + FILTER_HALO
    full = lambda a: pl.BlockSpec(a.shape, lambda i: (0,) * a.ndim)
    args = (bands_row, w1p, b1, w2, b2, w3, b3, wl, freq, decay)
    return pl.pallas_call(
        functools.partial(_filter_kernel, seq=seq),
        grid=(seq // tile,),
        in_specs=[full(a) for a in args],
        out_specs=[pl.BlockSpec(memory_space=pl.ANY), pl.BlockSpec((1, c), lambda i: (0, 0))],
        out_shape=[jax.ShapeDtypeStruct((2 * seq, c), F32), jax.ShapeDtypeStruct((1, c), F32)],
        scratch_shapes=[pltpu.VMEM((ext, LANES), F32), pltpu.VMEM((ext, LANES), F32),
                        pltpu.VMEM((tile, c), F32), pltpu.VMEM((tile, c), F32),
                        pltpu.SemaphoreType.DMA((2,))],
        compiler_params=_params(1, "arbitrary"),
        name="hyena_filter",
    )(*args)


def _major_compute(g, xbuf, obuf, slot):
    _, n_ct, bsz, k, sub, lanes = xbuf.shape
    m = obuf.shape[3]
    x2 = xbuf.reshape(2 * n_ct * bsz * k * sub, lanes)
    o2 = obuf.reshape(2 * n_ct * bsz * m * sub, lanes)
    for t in range(n_ct * bsz):
        xbase = (slot * n_ct * bsz + t) * k * sub
        obase = (slot * n_ct * bsz + t) * m * sub
        for b in range(sub):
            xb = x2[pl.ds(xbase + b, k, stride=sub), :]
            o2[pl.ds(obase + b, m, stride=sub), :] = _dot(g, xb.astype(BF16))


def _major_kernel(g_ref, x_hbm, o_hbm, xbuf, obuf, isem, osem, *, nj):
    s = pl.program_id(0)
    ns = pl.num_programs(0)
    slot = s % 2
    _, n_ct, bsz, k, sub, lanes = xbuf.shape
    m = obuf.shape[3]

    def copies(step, sl, hbm, buf, sem, inbound):
        i = step // nj
        j = step % nj
        out = []
        for ct in range(n_ct):
            tile = hbm.at[pl.ds(i * bsz, bsz), :, pl.ds(pl.multiple_of(j * sub, sub), sub),
                          pl.ds(ct * lanes, lanes)]
            out.append(pltpu.make_async_copy(tile, buf.at[sl, ct], sem.at[sl]) if inbound
                       else pltpu.make_async_copy(buf.at[sl, ct], tile, sem.at[sl]))
        return out

    def start(cps):
        for cp in cps:
            cp.start()

    def wait(cps):
        for cp in cps:
            cp.wait()

    @pl.when(s == 0)
    def _():
        start(copies(0, 0, x_hbm, xbuf, isem, True))

    @pl.when(s + 1 < ns)
    def _():
        start(copies(s + 1, 1 - slot, x_hbm, xbuf, isem, True))

    wait(copies(s, slot, x_hbm, xbuf, isem, True))

    @pl.when(s >= 2)
    def _():
        wait(copies(s - 2, slot, o_hbm, obuf, osem, False))

    _major_compute(g_ref[...], xbuf, obuf, slot)
    start(copies(s, slot, o_hbm, obuf, osem, False))

    @pl.when(s == ns - 1)
    def _():
        wait(copies(s, slot, o_hbm, obuf, osem, False))

        @pl.when(s >= 1)
        def _():
            wait(copies(s - 1, 1 - slot, o_hbm, obuf, osem, False))


def _dft_major(g_bf, x4):
    b, k, n2, c = x4.shape
    m = g_bf.shape[0]
    sub = 8
    n_ct = c // LANES
    bsz = max(1, min(b, MAJOR_BLOCK_BYTES // ((k + m) * sub * c * 4)))
    while b % bsz:
        bsz -= 1
    nj = n2 // sub
    return pl.pallas_call(
        functools.partial(_major_kernel, nj=nj),
        grid=((b // bsz) * nj,),
        in_specs=[pl.BlockSpec((m, k), lambda s: (0, 0)), pl.BlockSpec(memory_space=pl.ANY)],
        out_specs=pl.BlockSpec(memory_space=pl.ANY),
        out_shape=jax.ShapeDtypeStruct((b, m, n2, c), F32),
        scratch_shapes=[pltpu.VMEM((2, n_ct, bsz, k, sub, LANES), F32),
                        pltpu.VMEM((2, n_ct, bsz, m, sub, LANES), F32),
                        pltpu.SemaphoreType.DMA((2,)), pltpu.SemaphoreType.DMA((2,))],
        compiler_params=_params(1, "arbitrary"),
        name="dft_major",
    )(g_bf, x4)


def _minor_kernel(*refs, mode, scale, k1b):
    if mode == "conv":
        y_ref, twb_ref, twr_ref, wf_ref, wi_ref, kf_ref, o_ref, ks_ref = refs
    else:
        y_ref, twb_ref, twr_ref, wf_ref, o_ref = refs
    n2 = y_ref.shape[3]
    ct = y_ref.shape[4]
    rep = ct // LANES

    def forward(re, im, cs, sn, w):
        return _dot(w, jnp.concatenate([re * cs + im * sn, im * cs - re * sn], axis=0).astype(BF16))

    cb, sb = twb_ref[0, 0], twb_ref[1, 0]
    for j in range(k1b):
        cr, sr = twr_ref[0, j], twr_ref[1, j]
        cs = jnp.concatenate([cb * cr - sb * sr] * rep, axis=1)
        sn = jnp.concatenate([sb * cr + cb * sr] * rep, axis=1)
        if mode == "real":
            o_ref[0, :, j, :] = forward(y_ref[0, 0, j], y_ref[0, 1, j], cs, sn, wf_ref[:n2]) * scale
            continue

        @pl.when(pl.program_id(1) == 0)
        def _():
            ks_ref[j] = forward(kf_ref[0, 0, j], kf_ref[0, 1, j], cs, sn, wf_ref[...]) * scale

        z = forward(y_ref[0, 0, j], y_ref[0, 1, j], cs, sn, wf_ref[...])
        zr, zi = z[:n2], z[n2:]
        kr = ks_ref[j, :n2]
        ki = ks_ref[j, n2:]
        p = jnp.concatenate([zr * kr - zi * ki, zr * ki + zi * kr], axis=0).astype(BF16)
        q = _dot(wi_ref[...], p)
        qr, qi = q[:n2], q[n2:]
        o_ref[0, 0, j] = qr * cs - qi * sn
        o_ref[0, 1, j] = qi * cs + qr * sn


def _dft_minor(y5, wf_bf, mode, scale=1.0, wi_bf=None, kf5=None):
    b, _, n1, n2, c = y5.shape
    k1b = min(8, n1)
    ct = c
    tw_block, tw_rem = _twiddles(n1, n2, k1b)
    grid = (n1 // k1b, b)
    dspec = pl.BlockSpec((1, 2, k1b, n2, ct), lambda j, i: (i, 0, j, 0, 0))
    in_specs = [dspec,
                pl.BlockSpec((2, 1, n2, LANES), lambda j, i: (0, j, 0, 0)),
                pl.BlockSpec((2, k1b, n2, LANES), lambda j, i: (0, 0, 0, 0)),
                pl.BlockSpec(wf_bf.shape, lambda j, i: (0, 0))]
    args = [y5, tw_block, tw_rem, wf_bf]
    scratch = []
    if mode == "conv":
        in_specs += [pl.BlockSpec(wi_bf.shape, lambda j, i: (0, 0)),
                     pl.BlockSpec((1, 2, k1b, n2, ct), lambda j, i: (0, 0, j, 0, 0))]
        args += [wi_bf, kf5]
        scratch = [pltpu.VMEM((k1b, 2 * n2, ct), F32)]
    if mode == "real":
        out_spec = pl.BlockSpec((1, n2, k1b, ct), lambda j, i: (i, 0, j, 0))
        out_shape = jax.ShapeDtypeStruct((b, n2, n1, c), F32)
    else:
        out_spec = dspec
        out_shape = jax.ShapeDtypeStruct((b, 2, n1, n2, c), F32)
    return pl.pallas_call(
        functools.partial(_minor_kernel, mode=mode, scale=scale, k1b=k1b),
        grid=grid, in_specs=in_specs, out_specs=out_spec, out_shape=out_shape,
        scratch_shapes=scratch,
        compiler_params=pltpu.CompilerParams(dimension_semantics=("parallel", "arbitrary"),
                                             vmem_limit_bytes=VMEM_LIMIT),
        name="dft_minor_" + mode,
    )(*args)


def _cs(num, den):
    ang = 2.0 * np.pi * (num % den) / den
    return np.cos(ang), np.sin(ang)


def _complex_block(c, s, sign):
    return np.block([[c, -sign * s], [sign * s, c]]).astype(np.float32)


def _major_matrix(n1, a_in, a_out, sign, real_in=False):
    c, s = _cs(np.outer(np.arange(a_out), np.arange(a_in)), n1)
    g = _complex_block(c, s, sign)
    return g[:, :a_in] if real_in else g


def _minor_matrix(n2, sign):
    c, s = _cs(np.outer(np.arange(n2), np.arange(n2)), n2)
    return _complex_block(c, s, sign)


def _twiddles(n1, n2, k1b):
    def table(k1):
        c, s = _cs(np.outer(k1, np.arange(n2)), n1 * n2)
        tw = jnp.asarray(np.stack([c, s]).astype(np.float32))
        return jnp.broadcast_to(tw[..., None], (2, len(k1), n2, LANES))
    return table(np.arange(0, n1, k1b)), table(np.arange(k1b))


def _out_kernel(yf_ref, yc_ref, x0_ref, w_ref, x_ref, lng_ref, lnb_ref, inv_ref, skip_ref, gf_ref, gh_ref,
                wo_ref, bo_ref, g1_ref, b1_ref, wr_ref, br_ref, x1_ref, idx_ref, gate_ref, cnt_ref,
                *, alpha, apply_ln, top_k):
    c = yc_ref.shape[1]
    w = w_ref[...]
    yh = x0_ref[...] * (yc_ref[...] * inv_ref[...] + skip_ref[...] * w)
    mf = _rms_norm(yf_ref[...], gf_ref[...]).astype(BF16)
    mh = _rms_norm(yh, gh_ref[...]).astype(BF16)
    m = _dot(mf, wo_ref[:c]) + _dot(mh, wo_ref[c:]) + bo_ref[...]
    xn = _layer_norm(x_ref[...], lng_ref[...], lnb_ref[...]) if apply_ln else x_ref[...]
    x1 = _layer_norm(alpha * xn + m, g1_ref[...], b1_ref[...])
    x1_ref[...] = x1
    _route_tile(x1, wr_ref[...], br_ref[...], idx_ref, gate_ref, cnt_ref, top_k)


def _out_proj(yf, yc, x0, w, x, lng, lnb, inv_norm, skip, gf, gh, wo_bf, bo, g1, b1, wr_t, br_col, alpha,
              apply_ln):
    t, d = x.shape
    c = yc.shape[1]
    e = wr_t.shape[0]
    tm = TOKEN_TILE
    nt = t // tm
    row = lambda i: (i, 0)
    const = lambda i: (0, 0)
    tile3 = lambda i: (i, 0, 0)
    return pl.pallas_call(
        functools.partial(_out_kernel, alpha=alpha, apply_ln=apply_ln, top_k=TOP_K),
        grid=(nt,),
        in_specs=[
            pl.BlockSpec((tm, c), row),
            pl.BlockSpec((tm, c), row), pl.BlockSpec((tm, c), row), pl.BlockSpec((tm, c), row),
            pl.BlockSpec((tm, d), row),
            pl.BlockSpec((1, d), const), pl.BlockSpec((1, d), const),
            pl.BlockSpec((1, c), const), pl.BlockSpec((1, c), const),
            pl.BlockSpec((1, c), const), pl.BlockSpec((1, c), const),
            pl.BlockSpec(wo_bf.shape, const), pl.BlockSpec((1, d), const),
            pl.BlockSpec((1, d), const), pl.BlockSpec((1, d), const),
            pl.BlockSpec((e, d), const), pl.BlockSpec((e, 1), const),
        ],
        out_specs=[pl.BlockSpec((tm, d), row),
                   pl.BlockSpec((1, 2 * TOP_K, tm), tile3),
                   pl.BlockSpec((1, 8, tm), tile3),
                   pl.BlockSpec((1, e, 1), tile3)],
        out_shape=[jax.ShapeDtypeStruct((t, d), F32),
                   jax.ShapeDtypeStruct((nt, 2 * TOP_K, tm), I32),
                   jax.ShapeDtypeStruct((nt, 8, tm), F32),
                   jax.ShapeDtypeStruct((nt, e, 1), I32)],
        compiler_params=_params(1),
        name="out_proj",
    )(yf, yc, x0, w, x, lng, lnb, inv_norm, skip, gf, gh, wo_bf, bo, g1, b1, wr_t, br_col)


def _route_tile(x1, wr, br, idx_ref, gate_ref, cnt_ref, top_k):
    tt = x1.shape[0]
    e = wr.shape[0]
    nt = (((1,), (1,)), ((), ()))
    xh, xl = _split(x1)
    wh, wl = _split(wr)
    logits = (lax.dot_general(wh, xh, nt, preferred_element_type=F32)
              + lax.dot_general(wh, xl, nt, preferred_element_type=F32)
              + lax.dot_general(wl, xh, nt, preferred_element_type=F32)) + br
    iota_e = lax.broadcasted_iota(I32, (e, tt), 0).astype(F32)
    l = logits
    tops, idxs, hots = [], [], []
    for _ in range(top_k):
        m = jnp.max(l, axis=0, keepdims=True)
        idx = jnp.min(jnp.where(l == m, iota_e, float(e)), axis=0, keepdims=True)
        hot = iota_e == idx
        l = jnp.where(hot, -jnp.inf, l)
        tops.append(m)
        idxs.append(idx)
        hots.append(hot)
    ex = [jnp.exp(m - tops[0]) for m in tops]
    den = ex[0]
    for v in ex[1:]:
        den = den + v
    gates = [v / den for v in ex]
    hot_all = jnp.zeros((e, tt), F32)
    for hot in hots:
        hot_all = jnp.where(hot, 1.0, hot_all)
    upper = (lax.broadcasted_iota(I32, (tt, tt), 0) < lax.broadcasted_iota(I32, (tt, tt), 1))
    before = _dot(hot_all.astype(BF16), jnp.where(upper, 1.0, 0.0).astype(BF16))
    ranks = [jnp.sum(jnp.where(hot, before, 0.0), axis=0, keepdims=True).astype(I32) for hot in hots]
    pad = 8 - top_k
    idx_ref[0] = jnp.concatenate([v.astype(I32) for v in idxs] + ranks, axis=0)
    gate_ref[0] = jnp.concatenate(gates + [jnp.zeros((pad, tt), F32)], axis=0)
    cnt_ref[0] = jnp.sum(hot_all, axis=1, keepdims=True).astype(I32)


def _dual_specs(shape, nt0):
    first = pl.BlockSpec(shape, lambda i, *_: (jnp.minimum(i, nt0 - 1),) + (0,) * (len(shape) - 1))
    second = pl.BlockSpec(shape, lambda i, *_: (jnp.maximum(i - nt0, 0),) + (0,) * (len(shape) - 1))
    return first, second


SEG_PIECES = 3
SEG_CHUNK = SEG_ALIGN << SEG_PIECES


def _segment_copies(first, count, loff_s, len_s, goff_s, local, remote, sem, to_remote, wait):
    def copy(lstart, gstart, size):
        lref = local.at[pl.ds(pl.multiple_of(lstart, SEG_ALIGN), size)]
        rref = remote.at[pl.ds(pl.multiple_of(gstart, SEG_ALIGN), size)]
        cp = (pltpu.make_async_copy(lref, rref, sem) if to_remote
              else pltpu.make_async_copy(rref, lref, sem))
        if wait:
            cp.wait()
        else:
            cp.start()

    def body(e, carry):
        s = first + e
        n = len_s[s] // SEG_ALIGN
        lo = 0 if loff_s is None else loff_s[s]
        go = goff_s[s]
        chunks = n >> SEG_PIECES

        def chunk(i, c):
            copy(lo + i * SEG_CHUNK, go + i * SEG_CHUNK, SEG_CHUNK)
            return c

        lax.fori_loop(0, chunks, chunk, 0)
        rem = n & ((1 << SEG_PIECES) - 1)
        base = chunks * SEG_CHUNK
        for p in reversed(range(SEG_PIECES)):
            start = base + ((rem >> (p + 1)) << (p + 1)) * SEG_ALIGN

            @pl.when(((rem >> p) & 1) == 1)
            def _():
                copy(lo + start, go + start, SEG_ALIGN << p)
        return carry

    lax.fori_loop(0, count, body, 0)


def _wait_tile(step, n_experts, loff_s, len_s, local, remote, sem, to_remote):
    last = step * n_experts + n_experts - 1
    n = (loff_s[last] + len_s[last]) // SEG_ALIGN
    for p in range((local.shape[0] // SEG_ALIGN).bit_length()):
        size = SEG_ALIGN << p

        @pl.when(((n >> p) & 1) == 1)
        def _():
            lref = local.at[pl.ds(0, size)]
            rref = remote.at[pl.ds(0, size)]
            cp = (pltpu.make_async_copy(lref, rref, sem) if to_remote
                  else pltpu.make_async_copy(rref, lref, sem))
            cp.wait()


def _dispatch_kernel(loff_s, seg_s, goff_s, plen_s, poff_s, nb_s, xa_ref, xb_ref, idx_ref, xs_ref, pos_ref,
                     buf, sems, *, n_experts, top_k, nt0):
    j = pl.program_id(0)
    nt = pl.num_programs(0)
    slot = j % 2
    tt = xa_ref.shape[0]
    lr = buf.shape[1]
    idx = idx_ref[0]
    tope = idx[:top_k]
    base = jnp.zeros_like(tope)
    for e in range(n_experts):
        base = jnp.where(tope == e, loff_s[j * n_experts + e], base)
    pos = base + idx[top_k:]
    pos_ref[0] = jnp.concatenate([pos, jnp.zeros((8 - top_k, tt), I32)], axis=0)
    rows = lax.broadcasted_iota(I32, (lr, tt), 0)
    perm = jnp.zeros((lr, tt), F32)
    for k in range(top_k):
        perm = jnp.where(rows == pos[k:k + 1], 1.0, perm)
    x = jnp.where(j < nt0, xa_ref[...], xb_ref[...])
    buf[slot] = _dot(perm.astype(BF16), x.astype(BF16)).astype(BF16)

    def drained(step, s):
        _wait_tile(step, n_experts, loff_s, seg_s, buf.at[s], xs_ref, sems.at[s], True)

    _segment_copies(j * n_experts, n_experts, loff_s, seg_s, goff_s, buf.at[slot], xs_ref, sems.at[slot],
                    True, False)

    @pl.when(j > 0)
    def _():
        drained(j - 1, 1 - slot)

    @pl.when(j == nt - 1)
    def _():
        drained(j, slot)
        zrows = EXPERT_BLOCK
        buf[0, :zrows] = jnp.zeros((zrows, buf.shape[2]), BF16)
        _segment_copies(0, n_experts, None, plen_s, poff_s, buf.at[0], xs_ref, sems.at[0], True, False)
        _segment_copies(0, n_experts, None, plen_s, poff_s, buf.at[0], xs_ref, sems.at[0], True, True)
        n_tail = xs_ref.shape[0] // zrows - nb_s[0]

        def tail_copy(i):
            dst = xs_ref.at[pl.ds(pl.multiple_of((nb_s[0] + i) * zrows, zrows), zrows)]
            return pltpu.make_async_copy(buf.at[0, pl.ds(0, zrows)], dst, sems.at[0])

        lax.fori_loop(0, n_tail, lambda i, c: (tail_copy(i).start(), c)[1], 0)
        lax.fori_loop(0, n_tail, lambda i, c: (tail_copy(i).wait(), c)[1], 0)


def _dispatch(x1s, idx, loff, seg, goff, plen, poff, n_blocks, n_rows, local_rows):
    d = x1s[0].shape[1]
    tt = TOKEN_TILE
    nt0 = x1s[0].shape[0] // tt
    nt = nt0 + x1s[1].shape[0] // tt
    e = loff.shape[0] // nt
    xa, xb = _dual_specs((tt, d), nt0)
    grid_spec = pltpu.PrefetchScalarGridSpec(
        num_scalar_prefetch=6,
        grid=(nt,),
        in_specs=[xa, xb, pl.BlockSpec((1, 2 * TOP_K, tt), lambda i, *_: (i, 0, 0))],
        out_specs=[pl.BlockSpec(memory_space=pl.ANY),
                   pl.BlockSpec((1, 8, tt), lambda i, *_: (i, 0, 0))],
        scratch_shapes=[pltpu.VMEM((2, local_rows, d), BF16), pltpu.SemaphoreType.DMA((2,))],
    )
    return pl.pallas_call(
        functools.partial(_dispatch_kernel, n_experts=e, top_k=TOP_K, nt0=nt0),
        grid_spec=grid_spec,
        out_shape=[jax.ShapeDtypeStruct((n_rows, d), BF16), jax.ShapeDtypeStruct((nt, 8, tt), I32)],
        compiler_params=_params(1, "arbitrary"),
        name="moe_dispatch",
    )(loff, seg, goff, plen, poff, n_blocks, x1s[0], x1s[1], idx)


def _combine_kernel(loff_s, seg_s, goff_s, ys_ref, pos_ref, gate_ref, xa_ref, xb_ref, g2_ref, b2_ref,
                    oa_ref, ob_ref, buf, sems, *, n_experts, top_k, alpha, nt0):
    j = pl.program_id(0)
    nt = pl.num_programs(0)
    slot = j % 2
    tt = xa_ref.shape[0]
    lr = buf.shape[1]

    def prefetch(step, s):
        buf[s] = jnp.zeros((lr, buf.shape[2]), BF16)
        _segment_copies(step * n_experts, n_experts, loff_s, seg_s, goff_s, buf.at[s], ys_ref, sems.at[s],
                        False, False)

    @pl.when(j == 0)
    def _():
        prefetch(0, 0)

    @pl.when(j + 1 < nt)
    def _():
        prefetch(j + 1, 1 - slot)

    _wait_tile(j, n_experts, loff_s, seg_s, buf.at[slot], ys_ref, sems.at[slot], False)
    pos = pos_ref[...]
    gate = gate_ref[...]
    cols = lax.broadcasted_iota(I32, (tt, lr), 1)
    wsel = jnp.zeros((tt, lr), F32)
    for k in range(top_k):
        wsel = jnp.where(cols == pos[:, k:k + 1], gate[:, k:k + 1], wsel)
    moe = _dot(wsel.astype(BF16), buf[slot])
    x1 = jnp.where(j < nt0, xa_ref[...], xb_ref[...])
    out = _layer_norm(alpha * x1 + moe, g2_ref[...], b2_ref[...])

    @pl.when(j < nt0)
    def _():
        oa_ref[...] = out

    @pl.when(j >= nt0)
    def _():
        ob_ref[...] = out


def _combine(ys, pos_tm, gate_tm, x1s, g2, b2, loff, seg, goff, local_rows, alpha):
    d = x1s[0].shape[1]
    tt = TOKEN_TILE
    nt0 = x1s[0].shape[0] // tt
    nt = nt0 + x1s[1].shape[0] // tt
    e = loff.shape[0] // nt
    xa, xb = _dual_specs((tt, d), nt0)
    grid_spec = pltpu.PrefetchScalarGridSpec(
        num_scalar_prefetch=3,
        grid=(nt,),
        in_specs=[pl.BlockSpec(memory_space=pl.ANY),
                  pl.BlockSpec((tt, 8), lambda i, *_: (i, 0)),
                  pl.BlockSpec((tt, 8), lambda i, *_: (i, 0)),
                  xa, xb,
                  pl.BlockSpec((1, d), lambda i, *_: (0, 0)),
                  pl.BlockSpec((1, d), lambda i, *_: (0, 0))],
        out_specs=list(_dual_specs((tt, d), nt0)),
        scratch_shapes=[pltpu.VMEM((2, local_rows, d), BF16), pltpu.SemaphoreType.DMA((2,))],
    )
    return pl.pallas_call(
        functools.partial(_combine_kernel, n_experts=e, top_k=TOP_K, alpha=alpha, nt0=nt0),
        grid_spec=grid_spec,
        out_shape=[jax.ShapeDtypeStruct(x.shape, F32) for x in x1s],
        compiler_params=_params(1, "arbitrary"),
        name="moe_combine",
    )(loff, seg, goff, ys, pos_tm, gate_tm, x1s[0], x1s[1], g2, b2)


def _expert_kernel(first_s, count_s, xs_ref, wgu_ref, bgu_ref, wd_ref, bd_ref, ys_ref,
                   wgu_bf, wd_bf, xbuf, ybuf, isem, osem):
    e = pl.program_id(0)
    bm = xbuf.shape[1]
    dff = wd_ref.shape[1]
    first = first_s[e]
    count = count_s[e]
    wgu_bf[...] = wgu_ref[0].astype(BF16)
    wd_bf[...] = wd_ref[0].astype(BF16)

    def rows(i):
        return pl.ds(pl.multiple_of((first + i) * bm, bm), bm)

    def load(i, slot):
        return pltpu.make_async_copy(xs_ref.at[rows(i)], xbuf.at[slot], isem.at[slot])

    def store(i, slot):
        return pltpu.make_async_copy(ybuf.at[slot], ys_ref.at[rows(i)], osem.at[slot])

    @pl.when(count > 0)
    def _():
        load(0, 0).start()

    def block(i, carry):
        slot = i % 2

        @pl.when(i + 1 < count)
        def _():
            load(i + 1, 1 - slot).start()

        load(i, slot).wait()

        @pl.when(i >= 2)
        def _():
            store(i - 2, slot).wait()

        gu = _dot(xbuf[slot], wgu_bf[...]) + bgu_ref[0]
        gate = jnp.minimum(gu[:, :dff], SWIGLU_LIMIT)
        up = jnp.clip(gu[:, dff:], -SWIGLU_LIMIT, SWIGLU_LIMIT)
        h = (up + 1.0) * (gate * jax.nn.sigmoid(SWIGLU_ALPHA * gate))
        ybuf[slot] = (_dot(h.astype(BF16), wd_bf[...]) + bd_ref[0]).astype(BF16)
        store(i, slot).start()
        return carry

    lax.fori_loop(0, count, block, 0)

    @pl.when(count >= 2)
    def _():
        store(count - 2, count % 2).wait()

    @pl.when(count >= 1)
    def _():
        store(count - 1, (count - 1) % 2).wait()

    @pl.when(e == pl.num_programs(0) - 1)
    def _():
        ybuf[0] = jnp.zeros(ybuf.shape[1:], BF16)
        n_tail = ys_ref.shape[0] // bm - (first + count)
        lax.fori_loop(0, n_tail, lambda i, c: (store(count + i, 0).start(), c)[1], 0)
        lax.fori_loop(0, n_tail, lambda i, c: (store(count + i, 0).wait(), c)[1], 0)


def _experts(xs, first_block, block_count, wgu, bgu, wd, bdn):
    n_rows, d = xs.shape
    bm = EXPERT_BLOCK
    e, _, f2 = wgu.shape
    dff = wd.shape[1]
    wsel = lambda i, *_: (i, 0, 0)
    grid_spec = pltpu.PrefetchScalarGridSpec(
        num_scalar_prefetch=2,
        grid=(e,),
        in_specs=[pl.BlockSpec(memory_space=pl.ANY),
                  pl.BlockSpec((1, d, f2), wsel), pl.BlockSpec((1, 1, f2), wsel),
                  pl.BlockSpec((1, dff, d), wsel), pl.BlockSpec((1, 1, d), wsel)],
        out_specs=pl.BlockSpec(memory_space=pl.ANY),
        scratch_shapes=[pltpu.VMEM((d, f2), BF16), pltpu.VMEM((dff, d), BF16),
                        pltpu.VMEM((2, bm, d), BF16), pltpu.VMEM((2, bm, d), BF16),
                        pltpu.SemaphoreType.DMA((2,)), pltpu.SemaphoreType.DMA((2,))],
    )
    return pl.pallas_call(
        _expert_kernel,
        grid_spec=grid_spec,
        out_shape=jax.ShapeDtypeStruct((n_rows, d), BF16),
        compiler_params=_params(1, "arbitrary"),
        name="moe_experts",
    )(first_block, block_count, xs, wgu, bgu, wd, bdn)


def _routed_moe(routed, w_gate_up, b_gate_up, w_down, b_down, g2, b2, alpha):
    x1s = [r[0] for r in routed]
    idx, gate, cnt = (jnp.concatenate([r[i] for r in routed], axis=0) for i in (1, 2, 3))
    t = x1s[0].shape[0] + x1s[1].shape[0]
    e = cnt.shape[1]
    tt = TOKEN_TILE
    nt = t // tt
    bm = EXPERT_BLOCK

    cnt = cnt.reshape(nt, e)
    seg = (cnt + SEG_ALIGN - 1) // SEG_ALIGN * SEG_ALIGN
    loff = jnp.cumsum(seg, axis=1) - seg
    per_expert = jnp.sum(seg, axis=0)
    padded = (per_expert + bm - 1) // bm * bm
    blocks_end = jnp.cumsum(padded) // bm
    start = jnp.cumsum(padded) - padded
    goff = start[None, :] + jnp.cumsum(seg, axis=0) - seg
    local_rows = TOP_K * tt + e * SEG_ALIGN
    n_rows_max = -(-(TOP_K * t + (SEG_ALIGN - 1) * e * nt) // bm) * bm + e * bm
    n_blocks = blocks_end[-1:].astype(I32)
    flat = lambda a: a.reshape(-1).astype(I32)
    plen, poff = flat(padded - per_expert), flat(start + per_expert)

    xs, pos = _dispatch(x1s, idx, flat(loff), flat(seg), flat(goff), plen, poff, n_blocks, n_rows_max,
                        local_rows)
    ys = _experts(xs, flat(start // bm), flat(padded // bm), w_gate_up, b_gate_up.reshape(e, 1, -1),
                  w_down, b_down.reshape(e, 1, -1))
    pos_tm = jnp.swapaxes(pos, 1, 2).reshape(t, 8)
    gate_tm = jnp.swapaxes(gate, 1, 2).reshape(t, 8)
    return _combine(ys, pos_tm, gate_tm, x1s, g2, b2, flat(loff), flat(seg), flat(goff), local_rows, alpha)


def _mixer(x, apply_ln, lng, lnb, lay, consts, alpha):
    batch, seq, d = x.shape
    t = batch * seq
    fw = lay["g_fourier"].shape[1]
    n2 = MINOR
    x2d = x.reshape(t, d)
    z, x0, w = _in_proj(x2d, batch, seq, fw, lng, lnb, lay["w_in"], lay["b_in"], consts["bd"],
                        lay["w_short"], lay["b_short"], apply_ln)
    c = x0.shape[1]

    n1 = seq // n2
    g = jnp.asarray(_major_matrix(n1, n1, n1, -1)).astype(BF16)
    y = _dft_major(g, z.reshape(batch, 2 * n1, n2, fw))
    yf = _dft_minor(y.reshape(batch, 2, n1, n2, fw), consts["wf"], "real",
                    scale=1.0 / math.sqrt(seq * FOURIER_GROUP_DIM))

    n = 2 * seq
    m1 = n // n2
    kf, asum = _hyena_filter(seq, consts["bands"], lay["filt_w1"], lay["filt_b1"], lay["filt_w2"],
                             lay["filt_b2"], lay["filt_w3"], lay["filt_b3"], lay["filt_w_out"],
                             lay["filt_freq"], lay["filt_decay"])
    gk = jnp.asarray(_major_matrix(m1, m1, m1, -1, real_in=True)).astype(BF16)
    ky = _dft_major(gk, kf.reshape(1, m1, n2, c))
    gfw = jnp.asarray(_major_matrix(m1, m1 // 2, m1, -1)).astype(BF16)
    gin = jnp.asarray(_major_matrix(m1, m1, m1 // 2, +1)).astype(BF16)
    pairs = batch // 2
    u = _dft_major(gfw, w.reshape(pairs, m1, n2, c))
    v = _dft_minor(u.reshape(pairs, 2, m1, n2, c), consts["wf"], "conv", scale=1.0 / n,
                   wi_bf=consts["wi"], kf5=ky.reshape(1, 2, m1, n2, c))
    yc = _dft_major(gin, v.reshape(pairs, 2 * m1, n2, c))

    inv_norm = 1.0 / asum
    return _out_proj(yf.reshape(t, fw), yc.reshape(t, c), x0, w, x2d, lng, lnb, inv_norm,
                     lay["hyena_skip"], lay["g_fourier"], lay["g_hyena"], lay["w_out"], lay["b_out"],
                     lay["ln1_g"], lay["ln1_b"], lay["w_router_t"], lay["b_router"], alpha, apply_ln)


def kernel(x_prompt, x_sample, ln_in_g, ln_in_b, w_in, b_in, w_short, b_short, filt_w1, filt_b1, filt_w2, filt_b2, filt_w3, filt_b3, filt_w_out, filt_freq, filt_decay, hyena_skip, g_fourier, g_hyena, w_out, b_out, ln1_g, ln1_b, w_router, b_router, w_gate_up, b_gate_up, w_down, b_down, ln2_g, ln2_b):
    depth, d, in_width = w_in.shape
    fw = g_fourier.shape[1]
    alpha = (2.0 * depth) ** 0.25
    row = lambda a: a.reshape(1, -1)

    gd = FOURIER_GROUP_DIM
    cg, sg = _cs(np.outer(np.arange(gd), np.arange(gd)), gd)
    eye = np.eye(LANES // gd)
    bd = np.concatenate([np.kron(eye, cg), -np.kron(eye, sg)], axis=1).astype(np.float32)
    bands = jnp.linspace(1e-4, FILTER_BANDS - 1, FILTER_BANDS, dtype=F32)
    bands_row = jnp.zeros((1, LANES), F32).at[0, 1:1 + FILTER_BANDS].set(bands)
    bands_row = bands_row.at[0, 1 + FILTER_BANDS:1 + 2 * FILTER_BANDS].set(bands)
    consts = dict(bd=jnp.asarray(bd).astype(BF16), bands=bands_row,
                  wf=jnp.asarray(_minor_matrix(MINOR, -1)).astype(BF16),
                  wi=jnp.asarray(_minor_matrix(MINOR, +1)).astype(BF16))

    xs = [x_prompt, x_sample]
    for l in range(depth):
        emb = filt_w1.shape[1]
        lay = dict(
            w_in=w_in[l].astype(BF16), b_in=row(b_in[l]), w_short=w_short[l], b_short=row(b_short[l]),
            filt_w1=jnp.zeros((LANES, filt_w1.shape[2]), F32).at[:emb].set(filt_w1[l]),
            filt_b1=row(filt_b1[l]), filt_w2=filt_w2[l], filt_b2=row(filt_b2[l]), filt_w3=filt_w3[l],
            filt_b3=row(filt_b3[l]), filt_w_out=filt_w_out[l], filt_freq=row(filt_freq[l]),
            filt_decay=filt_decay[l], hyena_skip=row(hyena_skip[l]), g_fourier=row(g_fourier[l]),
            g_hyena=row(g_hyena[l]), w_out=w_out[l].astype(BF16), b_out=row(b_out[l]),
            ln1_g=row(ln1_g[l]), ln1_b=row(ln1_b[l]),
            w_router_t=w_router[l].T, b_router=b_router[l].reshape(-1, 1))
        routed = [_mixer(x, l == 0, row(ln_in_g), row(ln_in_b), lay, consts, alpha) for x in xs]
        outs = _routed_moe(routed, w_gate_up[l], b_gate_up[l], w_down[l], b_down[l],
                           row(ln2_g[l]), row(ln2_b[l]), alpha)
        xs = [o.reshape(x.shape) for o, x in zip(outs, xs)]
    return (xs[0], xs[1])
```

```python
import functools
import math

import numpy as np
import jax
import jax.numpy as jnp
from jax import lax
from jax.experimental import pallas as pl
from jax.experimental.pallas import tpu as pltpu

F32 = jnp.float32
BF16 = jnp.bfloat16
I32 = jnp.int32
U32 = jnp.uint32

LN_EPS = 1e-5
RMS_EPS = 1e-6
SWIGLU_LIMIT = 7.0
SWIGLU_ALPHA = 1.702
TOP_K = 4
FOURIER_GROUP_DIM = 64
FILTER_BANDS = 16

LANES = 128
MINOR = 128
ROW_TILE = 512
IN_TILE = 512
IN_SPLIT = 2
MAJOR_BLOCK_BYTES = 12 * 1024 * 1024
TOKEN_TILE = 512
SEG_ALIGN = 16
EXPERT_BLOCK = 512
VMEM_LIMIT = 56 * 1024 * 1024


def _params(n_grid, semantics="parallel"):
    return pltpu.CompilerParams(dimension_semantics=(semantics,) * n_grid,
                                vmem_limit_bytes=VMEM_LIMIT)


def _dot(a, b):
    return jnp.dot(a, b, preferred_element_type=F32)


def _split(a):
    hi = a.astype(BF16)
    lo = (a - hi.astype(F32)).astype(BF16)
    return hi, lo


def _dot3(a, b):
    ah, al = _split(a)
    bh, bl = _split(b)
    return _dot(ah, bh) + _dot(ah, bl) + _dot(al, bh)


def _layer_norm(x, g, b):
    mu = jnp.mean(x, axis=-1, keepdims=True)
    xc = x - mu
    var = jnp.mean(xc * xc, axis=-1, keepdims=True)
    return xc * lax.rsqrt(var + LN_EPS) * g + b


def _rms_norm(x, g):
    return x * lax.rsqrt(jnp.mean(x * x, axis=-1, keepdims=True) + RMS_EPS) * g


def _in_kernel(x_ref, xp_ref, xq_ref, lng_ref, lnb_ref, win_ref, bin_ref, bd_ref, wsh_ref, bsh_ref,
               z_ref, x0_ref, w_ref, *, tiles_per_seq, apply_ln, fw):
    it = pl.program_id(0) % tiles_per_seq
    tm = x_ref.shape[0]

    def norm(x):
        return _layer_norm(x, lng_ref[...], lnb_ref[...]) if apply_ln else x

    halo = norm(jnp.concatenate([xp_ref[...], xq_ref[...]], axis=0))
    projh = _dot(halo.astype(BF16), win_ref[:, fw:]) + bin_ref[:, fw:]
    rp = tm // IN_SPLIT
    projs = [_dot(norm(x_ref[r * rp:(r + 1) * rp, :]).astype(BF16), win_ref[...]) + bin_ref[...]
             for r in range(IN_SPLIT)]
    wsh = wsh_ref[...]
    for r, proj in enumerate(projs):
        sl = slice(r * rp, (r + 1) * rp)
        pf = proj[:, :fw].astype(BF16)
        zs = [_dot(pf[:, c0:c0 + LANES], bd_ref[...]) for c0 in range(0, fw, LANES)]
        z_ref[0, sl, :] = _pack_pair(jnp.concatenate([z[:, :LANES] for z in zs], axis=1),
                                     jnp.concatenate([z[:, LANES:] for z in zs], axis=1))

        ph = proj[:, fw:]
        prev_row = projs[r - 1][rp - 1:rp, fw:] if r > 0 else jnp.where(it > 0, projh[7:8], 0.0)
        next_row = (projs[r + 1][0:1, fw:] if r + 1 < IN_SPLIT
                    else jnp.where(it < tiles_per_seq - 1, projh[8:9], 0.0))
        rows = lax.broadcasted_iota(I32, ph.shape, 0)
        up = jnp.where(rows == 0, prev_row, pltpu.roll(ph, 1, 0))
        dn = jnp.where(rows == rp - 1, next_row, pltpu.roll(ph, rp - 1, 0))
        uc = up * wsh[0:1] + ph * wsh[1:2] + dn * wsh[2:3] + bsh_ref[...]
        hw = uc.shape[1] // 3
        x0_ref[sl, :] = uc[:, :hw]
        w_ref[sl, :] = uc[:, 2 * hw:] * uc[:, hw:2 * hw]


def _in_proj(x2d, batch, seq, fw, lng, lnb, win_bf, b_in, bd_bf, w_short, b_short, apply_ln):
    t, d = x2d.shape
    inw = win_bf.shape[1]
    hw = (inw - fw) // 3
    tm = IN_TILE
    tps = seq // tm
    r8 = tm // 8
    nblk8 = t // 8
    const = lambda i: (0, 0)
    kern = functools.partial(_in_kernel, tiles_per_seq=tps, apply_ln=apply_ln, fw=fw)
    return pl.pallas_call(
        kern,
        grid=(t // tm,),
        in_specs=[
            pl.BlockSpec((tm, d), lambda i: (i, 0)),
            pl.BlockSpec((8, d), lambda i: (jnp.maximum(i * r8 - 1, 0), 0)),
            pl.BlockSpec((8, d), lambda i: (jnp.minimum((i + 1) * r8, nblk8 - 1), 0)),
            pl.BlockSpec((1, d), const), pl.BlockSpec((1, d), const),
            pl.BlockSpec((d, inw), const), pl.BlockSpec((1, inw), const),
            pl.BlockSpec(bd_bf.shape, const),
            pl.BlockSpec((3, 3 * hw), const), pl.BlockSpec((1, 3 * hw), const),
        ],
        out_specs=[
            pl.BlockSpec((1, tm, fw), lambda i: (i // tps, i % tps, 0)),
            pl.BlockSpec((tm, hw), lambda i: (i, 0)),
            pl.BlockSpec((tm, hw), lambda i: (i, 0)),
        ],
        out_shape=[
            jax.ShapeDtypeStruct((batch, seq, fw), U32),
            jax.ShapeDtypeStruct((t, hw), F32),
            jax.ShapeDtypeStruct((t, hw), F32),
        ],
        compiler_params=_params(1),
        name="in_proj",
    )(x2d, x2d, x2d, lng, lnb, win_bf, b_in, bd_bf, w_short, b_short)


FILTER_HALO = 8


def _filter_kernel(bands_ref, w1_ref, b1_ref, w2_ref, b2_ref, w3_ref, b3_ref, wl_ref, fr_ref, dec_ref,
                   k_ref, asum_ref, tc_ref, ts_ref, kf_buf, kb_buf, sems, *, seq):
    i = pl.program_id(0)
    nsteps = pl.num_programs(0)
    tile, c = kf_buf.shape
    ext = tc_ref.shape[0]
    rep = c // LANES
    w_unit = 2.0 * math.pi / seq
    bands = bands_ref[...]

    @pl.when(i == 0)
    def _():
        arg = bands * (w_unit * lax.broadcasted_iota(I32, (ext, LANES), 0).astype(F32))
        tc_ref[...] = jnp.cos(arg)
        ts_ref[...] = jnp.sin(arg)
        asum_ref[...] = jnp.zeros_like(asum_ref)

    j0 = i * tile
    base = bands * (w_unit * j0.astype(F32))
    c0, s0 = jnp.cos(base), jnp.sin(base)
    tc, ts = tc_ref[...], ts_ref[...]
    cosf = c0 * tc - s0 * ts
    sinf = s0 * tc + c0 * ts
    j = j0 + lax.broadcasted_iota(I32, (ext, LANES), 0)
    t = j.astype(F32) * (1.0 / (seq - 1))
    lane = lax.broadcasted_iota(I32, (ext, LANES), 1)
    feat = jnp.where(lane == 0, t,
                     jnp.where(lane <= FILTER_BANDS, cosf,
                               jnp.where(lane <= 2 * FILTER_BANDS, -sinf, 0.0)))
    fr = fr_ref[...]
    h = jnp.sin(fr * (_dot3(feat, w1_ref[...]) + b1_ref[...]))
    h = jnp.sin(fr * (_dot3(h, w2_ref[...]) + b2_ref[...]))
    h = jnp.sin(fr * (_dot3(h, w3_ref[...]) + b3_ref[...]))
    hl = _dot3(h, wl_ref[...])
    t4 = jnp.concatenate([t] * rep, axis=1)
    j4 = jnp.concatenate([j] * rep, axis=1)
    dec = jnp.abs(dec_ref[...])
    hf = hl[:, :c] * jnp.exp(-t4 * dec[0:1])
    hb = hl[:, c:] * jnp.exp(-t4 * dec[1:2])
    hb = jnp.where(j4 < seq, hb, 0.0)
    asum_ref[...] += jnp.sum(jnp.abs(hf[:tile]) + jnp.abs(hb[:tile]), axis=0, keepdims=True)

    rev = (lax.broadcasted_iota(I32, (tile, ext), 1) == tile - lax.broadcasted_iota(I32, (tile, ext), 0))
    rev = jnp.where(rev, 1.0, 0.0).astype(BF16)
    b1 = hb.astype(BF16)
    r1 = hb - b1.astype(F32)
    b2 = r1.astype(BF16)
    b3 = (r1 - b2.astype(F32)).astype(BF16)
    kb = _dot(rev, b1) + _dot(rev, b2) + _dot(rev, b3)

    def copies():
        return (pltpu.make_async_copy(kf_buf, k_ref.at[pl.ds(pl.multiple_of(j0, tile), tile)], sems.at[0]),
                pltpu.make_async_copy(kb_buf, k_ref.at[pl.ds(pl.multiple_of(2 * seq - j0 - tile, tile), tile)],
                                      sems.at[1]))

    @pl.when(i > 0)
    def _():
        for cp in copies():
            cp.wait()

    kf_buf[...] = jnp.where(j4[:tile] == 0, hf[:tile] + hb[:tile], hf[:tile])
    kb_buf[...] = kb
    for cp in copies():
        cp.start()

    @pl.when(i == nsteps - 1)
    def _():
        for cp in copies():
            cp.wait()


def _hyena_filter(seq, bands_row, w1p, b1, w2, b2, w3, b3, wl, freq, decay):
    c = decay.shape[1]
    tile = ROW_TILE
    ext = tile + FILTER_HALO
    full = lambda a: pl.BlockSpec(a.shape, lambda i: (0,) * a.ndim)
    args = (bands_row, w1p, b1, w2, b2, w3, b3, wl, freq, decay)
    return pl.pallas_call(
        functools.partial(_filter_kernel, seq=seq),
        grid=(seq // tile,),
        in_specs=[full(a) for a in args],
        out_specs=[pl.BlockSpec(memory_space=pl.ANY), pl.BlockSpec((1, c), lambda i: (0, 0))],
        out_shape=[jax.ShapeDtypeStruct((2 * seq, c), F32), jax.ShapeDtypeStruct((1, c), F32)],
        scratch_shapes=[pltpu.VMEM((ext, LANES), F32), pltpu.VMEM((ext, LANES), F32),
                        pltpu.VMEM((tile, c), F32), pltpu.VMEM((tile, c), F32),
                        pltpu.SemaphoreType.DMA((2,))],
        compiler_params=_params(1, "arbitrary"),
        name="hyena_filter",
    )(*args)


def _pack_pair(re, im):
    rb = lax.bitcast_convert_type(re.astype(BF16).astype(F32), U32)
    ib = lax.bitcast_convert_type(im.astype(BF16).astype(F32), U32)
    return (rb >> 16) | ib


def _unpack_pair(word):
    re = lax.bitcast_convert_type(word << 16, F32)
    im = lax.bitcast_convert_type(word & jnp.uint32(0xFFFF0000), F32)
    return re, im


def _major_compute(g, xbuf, obuf, slot):
    _, n_ct, bsz, k, sub, lanes = xbuf.shape
    m = obuf.shape[3]
    x2 = xbuf.reshape(2 * n_ct * bsz * k * sub, lanes)
    o2 = obuf.reshape(2 * n_ct * bsz * m * sub, lanes)
    for t in range(n_ct * bsz):
        xbase = (slot * n_ct * bsz + t) * k * sub
        obase = (slot * n_ct * bsz + t) * m * sub
        for b in range(sub):
            xb = x2[pl.ds(xbase + b, k, stride=sub), :]
            if xb.dtype == U32:
                xb = jnp.concatenate(_unpack_pair(xb), axis=0)
            res = _dot(g, xb.astype(BF16))
            if obuf.dtype == U32:
                res = _pack_pair(res[:m], res[m:])
            o2[pl.ds(obase + b, m, stride=sub), :] = res


def _major_kernel(g_ref, x_hbm, o_hbm, xbuf, obuf, isem, osem, *, nj):
    s = pl.program_id(0)
    ns = pl.num_programs(0)
    slot = s % 2
    _, n_ct, bsz, k, sub, lanes = xbuf.shape
    m = obuf.shape[3]

    def copies(step, sl, hbm, buf, sem, inbound):
        i = step // nj
        j = step % nj
        out = []
        for ct in range(n_ct):
            tile = hbm.at[pl.ds(i * bsz, bsz), :, pl.ds(pl.multiple_of(j * sub, sub), sub),
                          pl.ds(ct * lanes, lanes)]
            out.append(pltpu.make_async_copy(tile, buf.at[sl, ct], sem.at[sl]) if inbound
                       else pltpu.make_async_copy(buf.at[sl, ct], tile, sem.at[sl]))
        return out

    def start(cps):
        for cp in cps:
            cp.start()

    def wait(cps):
        for cp in cps:
            cp.wait()

    @pl.when(s == 0)
    def _():
        start(copies(0, 0, x_hbm, xbuf, isem, True))

    @pl.when(s + 1 < ns)
    def _():
        start(copies(s + 1, 1 - slot, x_hbm, xbuf, isem, True))

    wait(copies(s, slot, x_hbm, xbuf, isem, True))

    @pl.when(s >= 2)
    def _():
        wait(copies(s - 2, slot, o_hbm, obuf, osem, False))

    _major_compute(g_ref[...], xbuf, obuf, slot)
    start(copies(s, slot, o_hbm, obuf, osem, False))

    @pl.when(s == ns - 1)
    def _():
        wait(copies(s, slot, o_hbm, obuf, osem, False))

        @pl.when(s >= 1)
        def _():
            wait(copies(s - 1, 1 - slot, o_hbm, obuf, osem, False))


def _dft_major(g_bf, x4, pack_out):
    b, k, n2, c = x4.shape
    m = g_bf.shape[0] // 2 if pack_out else g_bf.shape[0]
    out_dtype = U32 if pack_out else F32
    sub = 8
    n_ct = c // LANES
    bsz = max(1, min(b, MAJOR_BLOCK_BYTES // ((k + m) * sub * c * 4)))
    while b % bsz:
        bsz -= 1
    nj = n2 // sub
    return pl.pallas_call(
        functools.partial(_major_kernel, nj=nj),
        grid=((b // bsz) * nj,),
        in_specs=[pl.BlockSpec(g_bf.shape, lambda s: (0, 0)), pl.BlockSpec(memory_space=pl.ANY)],
        out_specs=pl.BlockSpec(memory_space=pl.ANY),
        out_shape=jax.ShapeDtypeStruct((b, m, n2, c), out_dtype),
        scratch_shapes=[pltpu.VMEM((2, n_ct, bsz, k, sub, LANES), x4.dtype),
                        pltpu.VMEM((2, n_ct, bsz, m, sub, LANES), out_dtype),
                        pltpu.SemaphoreType.DMA((2,)), pltpu.SemaphoreType.DMA((2,))],
        compiler_params=_params(1, "arbitrary"),
        name="dft_major",
    )(g_bf, x4)


def _minor_kernel(*refs, mode, scale, k1b):
    if mode == "conv":
        y_ref, twb_ref, twr_ref, wf_ref, wi_ref, kf_ref, o_ref, ks_ref = refs
    else:
        y_ref, twb_ref, twr_ref, wf_ref, o_ref = refs
    n2 = y_ref.shape[2]
    ct = y_ref.shape[3]
    rep = ct // LANES

    def forward(packed, cs, sn, w):
        re, im = _unpack_pair(packed)
        return _dot(w, jnp.concatenate([re * cs + im * sn, im * cs - re * sn], axis=0).astype(BF16))

    cb, sb = twb_ref[0, 0], twb_ref[1, 0]
    for j in range(k1b):
        cr, sr = twr_ref[0, j], twr_ref[1, j]
        cs = jnp.concatenate([cb * cr - sb * sr] * rep, axis=1)
        sn = jnp.concatenate([sb * cr + cb * sr] * rep, axis=1)
        if mode == "real":
            o_ref[0, :, j, :] = forward(y_ref[0, j], cs, sn, wf_ref[:n2]) * scale
            continue

        @pl.when(pl.program_id(1) == 0)
        def _():
            ks_ref[j] = forward(kf_ref[0, j], cs, sn, wf_ref[...]) * scale

        z = forward(y_ref[0, j], cs, sn, wf_ref[...])
        zr, zi = z[:n2], z[n2:]
        kr = ks_ref[j, :n2]
        ki = ks_ref[j, n2:]
        p = jnp.concatenate([zr * kr - zi * ki, zr * ki + zi * kr], axis=0).astype(BF16)
        q = _dot(wi_ref[...], p)
        qr, qi = q[:n2], q[n2:]
        o_ref[0, j] = _pack_pair(qr * cs - qi * sn, qi * cs + qr * sn)


def _dft_minor(y4, wf_bf, mode, scale=1.0, wi_bf=None, kf4=None):
    b, n1, n2, c = y4.shape
    k1b = min(8, n1)
    ct = c
    tw_block, tw_rem = _twiddles(n1, n2, k1b)
    grid = (n1 // k1b, b)
    dspec = pl.BlockSpec((1, k1b, n2, ct), lambda j, i: (i, j, 0, 0))
    in_specs = [dspec,
                pl.BlockSpec((2, 1, n2, LANES), lambda j, i: (0, j, 0, 0)),
                pl.BlockSpec((2, k1b, n2, LANES), lambda j, i: (0, 0, 0, 0)),
                pl.BlockSpec(wf_bf.shape, lambda j, i: (0, 0))]
    args = [y4, tw_block, tw_rem, wf_bf]
    scratch = []
    if mode == "conv":
        in_specs += [pl.BlockSpec(wi_bf.shape, lambda j, i: (0, 0)),
                     pl.BlockSpec((1, k1b, n2, ct), lambda j, i: (0, j, 0, 0))]
        args += [wi_bf, kf4]
        scratch = [pltpu.VMEM((k1b, 2 * n2, ct), F32)]
    if mode == "real":
        out_spec = pl.BlockSpec((1, n2, k1b, ct), lambda j, i: (i, 0, j, 0))
        out_shape = jax.ShapeDtypeStruct((b, n2, n1, c), F32)
    else:
        out_spec = dspec
        out_shape = jax.ShapeDtypeStruct((b, n1, n2, c), U32)
    return pl.pallas_call(
        functools.partial(_minor_kernel, mode=mode, scale=scale, k1b=k1b),
        grid=grid, in_specs=in_specs, out_specs=out_spec, out_shape=out_shape,
        scratch_shapes=scratch,
        compiler_params=pltpu.CompilerParams(dimension_semantics=("parallel", "arbitrary"),
                                             vmem_limit_bytes=VMEM_LIMIT),
        name="dft_minor_" + mode,
    )(*args)


def _cs(num, den):
    ang = 2.0 * np.pi * (num % den) / den
    return np.cos(ang), np.sin(ang)


def _complex_block(c, s, sign):
    return np.block([[c, -sign * s], [sign * s, c]]).astype(np.float32)


def _major_matrix(n1, a_in, a_out, sign, real_in=False):
    c, s = _cs(np.outer(np.arange(a_out), np.arange(a_in)), n1)
    g = _complex_block(c, s, sign)
    return g[:, :a_in] if real_in else g


def _minor_matrix(n2, sign):
    c, s = _cs(np.outer(np.arange(n2), np.arange(n2)), n2)
    return _complex_block(c, s, sign)


def _twiddles(n1, n2, k1b):
    def table(k1):
        c, s = _cs(np.outer(k1, np.arange(n2)), n1 * n2)
        tw = jnp.asarray(np.stack([c, s]).astype(np.float32))
        return jnp.broadcast_to(tw[..., None], (2, len(k1), n2, LANES))
    return table(np.arange(0, n1, k1b)), table(np.arange(k1b))


def _out_kernel(yf_ref, yc_ref, x0_ref, w_ref, x_ref, lng_ref, lnb_ref, inv_ref, skip_ref, gf_ref, gh_ref,
                wo_ref, bo_ref, g1_ref, b1_ref, wr_ref, br_ref, x1_ref, idx_ref, gate_ref, cnt_ref,
                *, alpha, apply_ln, top_k):
    c = yc_ref.shape[1]
    w = w_ref[...]
    yh = x0_ref[...] * (yc_ref[...] * inv_ref[...] + skip_ref[...] * w)
    mf = _rms_norm(yf_ref[...], gf_ref[...]).astype(BF16)
    mh = _rms_norm(yh, gh_ref[...]).astype(BF16)
    m = _dot(mf, wo_ref[:c]) + _dot(mh, wo_ref[c:]) + bo_ref[...]
    xn = _layer_norm(x_ref[...], lng_ref[...], lnb_ref[...]) if apply_ln else x_ref[...]
    x1 = _layer_norm(alpha * xn + m, g1_ref[...], b1_ref[...])
    x1_ref[...] = x1
    _route_tile(x1, wr_ref[...], br_ref[...], idx_ref, gate_ref, cnt_ref, top_k)


def _out_proj(yf, yc, x0, w, x, lng, lnb, inv_norm, skip, gf, gh, wo_bf, bo, g1, b1, wr_t, br_col, alpha,
              apply_ln):
    t, d = x.shape
    c = yc.shape[1]
    e = wr_t.shape[0]
    tm = TOKEN_TILE
    nt = t // tm
    row = lambda i: (i, 0)
    const = lambda i: (0, 0)
    tile3 = lambda i: (i, 0, 0)
    return pl.pallas_call(
        functools.partial(_out_kernel, alpha=alpha, apply_ln=apply_ln, top_k=TOP_K),
        grid=(nt,),
        in_specs=[
            pl.BlockSpec((tm, c), row),
            pl.BlockSpec((tm, c), row), pl.BlockSpec((tm, c), row), pl.BlockSpec((tm, c), row),
            pl.BlockSpec((tm, d), row),
            pl.BlockSpec((1, d), const), pl.BlockSpec((1, d), const),
            pl.BlockSpec((1, c), const), pl.BlockSpec((1, c), const),
            pl.BlockSpec((1, c), const), pl.BlockSpec((1, c), const),
            pl.BlockSpec(wo_bf.shape, const), pl.BlockSpec((1, d), const),
            pl.BlockSpec((1, d), const), pl.BlockSpec((1, d), const),
            pl.BlockSpec((e, d), const), pl.BlockSpec((e, 1), const),
        ],
        out_specs=[pl.BlockSpec((tm, d), row),
                   pl.BlockSpec((1, 2 * TOP_K, tm), tile3),
                   pl.BlockSpec((1, 8, tm), tile3),
                   pl.BlockSpec((1, e, 1), tile3)],
        out_shape=[jax.ShapeDtypeStruct((t, d), F32),
                   jax.ShapeDtypeStruct((nt, 2 * TOP_K, tm), I32),
                   jax.ShapeDtypeStruct((nt, 8, tm), F32),
                   jax.ShapeDtypeStruct((nt, e, 1), I32)],
        compiler_params=_params(1),
        name="out_proj",
    )(yf, yc, x0, w, x, lng, lnb, inv_norm, skip, gf, gh, wo_bf, bo, g1, b1, wr_t, br_col)


def _route_tile(x1, wr, br, idx_ref, gate_ref, cnt_ref, top_k):
    tt = x1.shape[0]
    e = wr.shape[0]
    nt = (((1,), (1,)), ((), ()))
    xh, xl = _split(x1)
    wh, wl = _split(wr)
    logits = (lax.dot_general(wh, xh, nt, preferred_element_type=F32)
              + lax.dot_general(wh, xl, nt, preferred_element_type=F32)
              + lax.dot_general(wl, xh, nt, preferred_element_type=F32)) + br
    iota_e = lax.broadcasted_iota(I32, (e, tt), 0).astype(F32)
    l = logits
    tops, idxs, hots = [], [], []
    for _ in range(top_k):
        m = jnp.max(l, axis=0, keepdims=True)
        idx = jnp.min(jnp.where(l == m, iota_e, float(e)), axis=0, keepdims=True)
        hot = iota_e == idx
        l = jnp.where(hot, -jnp.inf, l)
        tops.append(m)
        idxs.append(idx)
        hots.append(hot)
    ex = [jnp.exp(m - tops[0]) for m in tops]
    den = ex[0]
    for v in ex[1:]:
        den = den + v
    gates = [v / den for v in ex]
    hot_all = jnp.zeros((e, tt), F32)
    for hot in hots:
        hot_all = jnp.where(hot, 1.0, hot_all)
    upper = (lax.broadcasted_iota(I32, (tt, tt), 0) < lax.broadcasted_iota(I32, (tt, tt), 1))
    before = _dot(hot_all.astype(BF16), jnp.where(upper, 1.0, 0.0).astype(BF16))
    ranks = [jnp.sum(jnp.where(hot, before, 0.0), axis=0, keepdims=True).astype(I32) for hot in hots]
    pad = 8 - top_k
    idx_ref[0] = jnp.concatenate([v.astype(I32) for v in idxs] + ranks, axis=0)
    gate_ref[0] = jnp.concatenate(gates + [jnp.zeros((pad, tt), F32)], axis=0)
    cnt_ref[0] = jnp.sum(hot_all, axis=1, keepdims=True).astype(I32)


def _dual_specs(shape, nt0):
    first = pl.BlockSpec(shape, lambda i, *_: (jnp.minimum(i, nt0 - 1),) + (0,) * (len(shape) - 1))
    second = pl.BlockSpec(shape, lambda i, *_: (jnp.maximum(i - nt0, 0),) + (0,) * (len(shape) - 1))
    return first, second


SEG_PIECES = 3
SEG_CHUNK = SEG_ALIGN << SEG_PIECES


def _segment_copies(first, count, loff_s, len_s, goff_s, local, remote, sem, to_remote, wait):
    def copy(lstart, gstart, size):
        lref = local.at[pl.ds(pl.multiple_of(lstart, SEG_ALIGN), size)]
        rref = remote.at[pl.ds(pl.multiple_of(gstart, SEG_ALIGN), size)]
        cp = (pltpu.make_async_copy(lref, rref, sem) if to_remote
              else pltpu.make_async_copy(rref, lref, sem))
        if wait:
            cp.wait()
        else:
            cp.start()

    def body(e, carry):
        s = first + e
        n = len_s[s] // SEG_ALIGN
        lo = 0 if loff_s is None else loff_s[s]
        go = goff_s[s]
        chunks = n >> SEG_PIECES

        def chunk(i, c):
            copy(lo + i * SEG_CHUNK, go + i * SEG_CHUNK, SEG_CHUNK)
            return c

        lax.fori_loop(0, chunks, chunk, 0)
        rem = n & ((1 << SEG_PIECES) - 1)
        base = chunks * SEG_CHUNK
        for p in reversed(range(SEG_PIECES)):
            start = base + ((rem >> (p + 1)) << (p + 1)) * SEG_ALIGN

            @pl.when(((rem >> p) & 1) == 1)
            def _():
                copy(lo + start, go + start, SEG_ALIGN << p)
        return carry

    lax.fori_loop(0, count, body, 0)


def _wait_tile(step, n_experts, loff_s, len_s, local, remote, sem, to_remote):
    last = step * n_experts + n_experts - 1
    n = (loff_s[last] + len_s[last]) // SEG_ALIGN
    for p in range((local.shape[0] // SEG_ALIGN).bit_length()):
        size = SEG_ALIGN << p

        @pl.when(((n >> p) & 1) == 1)
        def _():
            lref = local.at[pl.ds(0, size)]
            rref = remote.at[pl.ds(0, size)]
            cp = (pltpu.make_async_copy(lref, rref, sem) if to_remote
                  else pltpu.make_async_copy(rref, lref, sem))
            cp.wait()


def _dispatch_kernel(loff_s, seg_s, goff_s, plen_s, poff_s, nb_s, xa_ref, xb_ref, idx_ref, xs_ref, pos_ref,
                     buf, sems, *, n_experts, top_k, nt0):
    j = pl.program_id(0)
    nt = pl.num_programs(0)
    slot = j % 2
    tt = xa_ref.shape[0]
    lr = buf.shape[1]
    idx = idx_ref[0]
    tope = idx[:top_k]
    base = jnp.zeros_like(tope)
    for e in range(n_experts):
        base = jnp.where(tope == e, loff_s[j * n_experts + e], base)
    pos = base + idx[top_k:]
    pos_ref[0] = jnp.concatenate([pos, jnp.zeros((8 - top_k, tt), I32)], axis=0)
    rows = lax.broadcasted_iota(I32, (lr, tt), 0)
    perm = jnp.zeros((lr, tt), F32)
    for k in range(top_k):
        perm = jnp.where(rows == pos[k:k + 1], 1.0, perm)
    x = jnp.where(j < nt0, xa_ref[...], xb_ref[...])
    buf[slot] = _dot(perm.astype(BF16), x.astype(BF16)).astype(BF16)

    def drained(step, s):
        _wait_tile(step, n_experts, loff_s, seg_s, buf.at[s], xs_ref, sems.at[s], True)

    _segment_copies(j * n_experts, n_experts, loff_s, seg_s, goff_s, buf.at[slot], xs_ref, sems.at[slot],
                    True, False)

    @pl.when(j > 0)
    def _():
        drained(j - 1, 1 - slot)

    @pl.when(j == nt - 1)
    def _():
        drained(j, slot)
        zrows = EXPERT_BLOCK
        buf[0, :zrows] = jnp.zeros((zrows, buf.shape[2]), BF16)
        _segment_copies(0, n_experts, None, plen_s, poff_s, buf.at[0], xs_ref, sems.at[0], True, False)
        _segment_copies(0, n_experts, None, plen_s, poff_s, buf.at[0], xs_ref, sems.at[0], True, True)
        n_tail = xs_ref.shape[0] // zrows - nb_s[0]

        def tail_copy(i):
            dst = xs_ref.at[pl.ds(pl.multiple_of((nb_s[0] + i) * zrows, zrows), zrows)]
            return pltpu.make_async_copy(buf.at[0, pl.ds(0, zrows)], dst, sems.at[0])

        lax.fori_loop(0, n_tail, lambda i, c: (tail_copy(i).start(), c)[1], 0)
        lax.fori_loop(0, n_tail, lambda i, c: (tail_copy(i).wait(), c)[1], 0)


def _dispatch(x1s, idx, loff, seg, goff, plen, poff, n_blocks, n_rows, local_rows):
    d = x1s[0].shape[1]
    tt = TOKEN_TILE
    nt0 = x1s[0].shape[0] // tt
    nt = nt0 + x1s[1].shape[0] // tt
    e = loff.shape[0] // nt
    xa, xb = _dual_specs((tt, d), nt0)
    grid_spec = pltpu.PrefetchScalarGridSpec(
        num_scalar_prefetch=6,
        grid=(nt,),
        in_specs=[xa, xb, pl.BlockSpec((1, 2 * TOP_K, tt), lambda i, *_: (i, 0, 0))],
        out_specs=[pl.BlockSpec(memory_space=pl.ANY),
                   pl.BlockSpec((1, 8, tt), lambda i, *_: (i, 0, 0))],
        scratch_shapes=[pltpu.VMEM((2, local_rows, d), BF16), pltpu.SemaphoreType.DMA((2,))],
    )
    return pl.pallas_call(
        functools.partial(_dispatch_kernel, n_experts=e, top_k=TOP_K, nt0=nt0),
        grid_spec=grid_spec,
        out_shape=[jax.ShapeDtypeStruct((n_rows, d), BF16), jax.ShapeDtypeStruct((nt, 8, tt), I32)],
        compiler_params=_params(1, "arbitrary"),
        name="moe_dispatch",
    )(loff, seg, goff, plen, poff, n_blocks, x1s[0], x1s[1], idx)


def _combine_kernel(loff_s, seg_s, goff_s, ys_ref, pos_ref, gate_ref, xa_ref, xb_ref, g2_ref, b2_ref,
                    oa_ref, ob_ref, buf, sems, *, n_experts, top_k, alpha, nt0):
    j = pl.program_id(0)
    nt = pl.num_programs(0)
    slot = j % 2
    tt = xa_ref.shape[0]
    lr = buf.shape[1]

    def prefetch(step, s):
        buf[s] = jnp.zeros((lr, buf.shape[2]), BF16)
        _segment_copies(step * n_experts, n_experts, loff_s, seg_s, goff_s, buf.at[s], ys_ref, sems.at[s],
                        False, False)

    @pl.when(j == 0)
    def _():
        prefetch(0, 0)

    @pl.when(j + 1 < nt)
    def _():
        prefetch(j + 1, 1 - slot)

    _wait_tile(j, n_experts, loff_s, seg_s, buf.at[slot], ys_ref, sems.at[slot], False)
    pos = pos_ref[...]
    gate = gate_ref[...]
    cols = lax.broadcasted_iota(I32, (tt, lr), 1)
    wsel = jnp.zeros((tt, lr), F32)
    for k in range(top_k):
        wsel = jnp.where(cols == pos[:, k:k + 1], gate[:, k:k + 1], wsel)
    moe = _dot(wsel.astype(BF16), buf[slot])
    x1 = jnp.where(j < nt0, xa_ref[...], xb_ref[...])
    out = _layer_norm(alpha * x1 + moe, g2_ref[...], b2_ref[...])

    @pl.when(j < nt0)
    def _():
        oa_ref[...] = out

    @pl.when(j >= nt0)
    def _():
        ob_ref[...] = out


def _combine(ys, pos_tm, gate_tm, x1s, g2, b2, loff, seg, goff, local_rows, alpha):
    d = x1s[0].shape[1]
    tt = TOKEN_TILE
    nt0 = x1s[0].shape[0] // tt
    nt = nt0 + x1s[1].shape[0] // tt
    e = loff.shape[0] // nt
    xa, xb = _dual_specs((tt, d), nt0)
    grid_spec = pltpu.PrefetchScalarGridSpec(
        num_scalar_prefetch=3,
        grid=(nt,),
        in_specs=[pl.BlockSpec(memory_space=pl.ANY),
                  pl.BlockSpec((tt, 8), lambda i, *_: (i, 0)),
                  pl.BlockSpec((tt, 8), lambda i, *_: (i, 0)),
                  xa, xb,
                  pl.BlockSpec((1, d), lambda i, *_: (0, 0)),
                  pl.BlockSpec((1, d), lambda i, *_: (0, 0))],
        out_specs=list(_dual_specs((tt, d), nt0)),
        scratch_shapes=[pltpu.VMEM((2, local_rows, d), BF16), pltpu.SemaphoreType.DMA((2,))],
    )
    return pl.pallas_call(
        functools.partial(_combine_kernel, n_experts=e, top_k=TOP_K, alpha=alpha, nt0=nt0),
        grid_spec=grid_spec,
        out_shape=[jax.ShapeDtypeStruct(x.shape, F32) for x in x1s],
        compiler_params=_params(1, "arbitrary"),
        name="moe_combine",
    )(loff, seg, goff, ys, pos_tm, gate_tm, x1s[0], x1s[1], g2, b2)


def _expert_kernel(be_s, nb_s, xs_ref, wgu_ref, bgu_ref, wd_ref, bd_ref, ys_ref, wgu_bf, wd_bf):
    b = pl.program_id(0)
    active = b < nb_s[0]
    fresh = jnp.logical_or(b == 0, be_s[b] != be_s[jnp.maximum(b - 1, 0)])

    @pl.when(jnp.logical_and(active, fresh))
    def _():
        wgu_bf[...] = wgu_ref[0].astype(BF16)
        wd_bf[...] = wd_ref[0].astype(BF16)

    @pl.when(active)
    def _():
        dff = wd_ref.shape[1]
        gu = _dot(xs_ref[...], wgu_bf[...]) + bgu_ref[0]
        gate = jnp.minimum(gu[:, :dff], SWIGLU_LIMIT)
        up = jnp.clip(gu[:, dff:], -SWIGLU_LIMIT, SWIGLU_LIMIT)
        h = (up + 1.0) * (gate * jax.nn.sigmoid(SWIGLU_ALPHA * gate))
        ys_ref[...] = (_dot(h.astype(BF16), wd_bf[...]) + bd_ref[0]).astype(BF16)

    @pl.when(jnp.logical_not(active))
    def _():
        ys_ref[...] = jnp.zeros_like(ys_ref)


def _experts(xs, block_expert, n_blocks, wgu, bgu, wd, bdn):
    n_rows, d = xs.shape
    bm = EXPERT_BLOCK
    f2 = wgu.shape[2]
    dff = wd.shape[1]
    blk = lambda b, be, nb: (jnp.minimum(b, nb[0] - 1), 0)
    wsel = lambda b, be, nb: (be[b], 0, 0)
    grid_spec = pltpu.PrefetchScalarGridSpec(
        num_scalar_prefetch=2,
        grid=(n_rows // bm,),
        in_specs=[pl.BlockSpec((bm, d), blk),
                  pl.BlockSpec((1, d, f2), wsel), pl.BlockSpec((1, 1, f2), wsel),
                  pl.BlockSpec((1, dff, d), wsel), pl.BlockSpec((1, 1, d), wsel)],
        out_specs=pl.BlockSpec((bm, d), lambda b, be, nb: (b, 0)),
        scratch_shapes=[pltpu.VMEM((d, f2), BF16), pltpu.VMEM((dff, d), BF16)],
    )
    return pl.pallas_call(
        _expert_kernel,
        grid_spec=grid_spec,
        out_shape=jax.ShapeDtypeStruct((n_rows, d), BF16),
        compiler_params=_params(1, "arbitrary"),
        name="moe_experts",
    )(block_expert, n_blocks, xs, wgu, bgu, wd, bdn)


def _routed_moe(routed, w_gate_up, b_gate_up, w_down, b_down, g2, b2, alpha):
    x1s = [r[0] for r in routed]
    idx, gate, cnt = (jnp.concatenate([r[i] for r in routed], axis=0) for i in (1, 2, 3))
    t = x1s[0].shape[0] + x1s[1].shape[0]
    e = cnt.shape[1]
    tt = TOKEN_TILE
    nt = t // tt
    bm = EXPERT_BLOCK

    cnt = cnt.reshape(nt, e)
    seg = (cnt + SEG_ALIGN - 1) // SEG_ALIGN * SEG_ALIGN
    loff = jnp.cumsum(seg, axis=1) - seg
    per_expert = jnp.sum(seg, axis=0)
    padded = (per_expert + bm - 1) // bm * bm
    blocks_end = jnp.cumsum(padded) // bm
    start = jnp.cumsum(padded) - padded
    goff = start[None, :] + jnp.cumsum(seg, axis=0) - seg
    local_rows = TOP_K * tt + e * SEG_ALIGN
    n_rows_max = -(-(TOP_K * t + (SEG_ALIGN - 1) * e * nt) // bm) * bm + e * bm
    n_blocks = blocks_end[-1:].astype(I32)
    bidx = jnp.minimum(jnp.arange(n_rows_max // bm, dtype=I32), n_blocks[0] - 1)
    block_expert = jnp.minimum(jnp.sum(bidx[:, None] >= blocks_end[None, :], axis=1), e - 1).astype(I32)
    flat = lambda a: a.reshape(-1).astype(I32)
    plen, poff = flat(padded - per_expert), flat(start + per_expert)

    xs, pos = _dispatch(x1s, idx, flat(loff), flat(seg), flat(goff), plen, poff, n_blocks, n_rows_max,
                        local_rows)
    ys = _experts(xs, block_expert, n_blocks, w_gate_up, b_gate_up.reshape(e, 1, -1),
                  w_down, b_down.reshape(e, 1, -1))
    pos_tm = jnp.swapaxes(pos, 1, 2).reshape(t, 8)
    gate_tm = jnp.swapaxes(gate, 1, 2).reshape(t, 8)
    return _combine(ys, pos_tm, gate_tm, x1s, g2, b2, flat(loff), flat(seg), flat(goff), local_rows, alpha)


def _mixer(x, apply_ln, lng, lnb, lay, consts, alpha):
    batch, seq, d = x.shape
    t = batch * seq
    fw = lay["g_fourier"].shape[1]
    n2 = MINOR
    x2d = x.reshape(t, d)
    z, x0, w = _in_proj(x2d, batch, seq, fw, lng, lnb, lay["w_in"], lay["b_in"], consts["bd"],
                        lay["w_short"], lay["b_short"], apply_ln)
    c = x0.shape[1]

    n1 = seq // n2
    g = jnp.asarray(_major_matrix(n1, n1, n1, -1)).astype(BF16)
    y = _dft_major(g, z.reshape(batch, n1, n2, fw), True)
    yf = _dft_minor(y, consts["wf"], "real",
                    scale=1.0 / math.sqrt(seq * FOURIER_GROUP_DIM))

    n = 2 * seq
    m1 = n // n2
    kf, asum = _hyena_filter(seq, consts["bands"], lay["filt_w1"], lay["filt_b1"], lay["filt_w2"],
                             lay["filt_b2"], lay["filt_w3"], lay["filt_b3"], lay["filt_w_out"],
                             lay["filt_freq"], lay["filt_decay"])
    gk = jnp.asarray(_major_matrix(m1, m1, m1, -1, real_in=True)).astype(BF16)
    ky = _dft_major(gk, kf.reshape(1, m1, n2, c), True)
    gfw = jnp.asarray(_major_matrix(m1, m1 // 2, m1, -1)).astype(BF16)
    gin = jnp.asarray(_major_matrix(m1, m1, m1 // 2, +1)).astype(BF16)
    pairs = batch // 2
    u = _dft_major(gfw, w.reshape(pairs, m1, n2, c), True)
    v = _dft_minor(u, consts["wf"], "conv", scale=1.0 / n, wi_bf=consts["wi"], kf4=ky)
    yc = _dft_major(gin, v, False)

    inv_norm = 1.0 / asum
    return _out_proj(yf.reshape(t, fw), yc.reshape(t, c), x0, w, x2d, lng, lnb, inv_norm,
                     lay["hyena_skip"], lay["g_fourier"], lay["g_hyena"], lay["w_out"], lay["b_out"],
                     lay["ln1_g"], lay["ln1_b"], lay["w_router_t"], lay["b_router"], alpha, apply_ln)


def kernel(x_prompt, x_sample, ln_in_g, ln_in_b, w_in, b_in, w_short, b_short, filt_w1, filt_b1, filt_w2, filt_b2, filt_w3, filt_b3, filt_w_out, filt_freq, filt_decay, hyena_skip, g_fourier, g_hyena, w_out, b_out, ln1_g, ln1_b, w_router, b_router, w_gate_up, b_gate_up, w_down, b_down, ln2_g, ln2_b):
    depth, d, in_width = w_in.shape
    fw = g_fourier.shape[1]
    alpha = (2.0 * depth) ** 0.25
    row = lambda a: a.reshape(1, -1)

    gd = FOURIER_GROUP_DIM
    cg, sg = _cs(np.outer(np.arange(gd), np.arange(gd)), gd)
    eye = np.eye(LANES // gd)
    bd = np.concatenate([np.kron(eye, cg), -np.kron(eye, sg)], axis=1).astype(np.float32)
    bands = jnp.linspace(1e-4, FILTER_BANDS - 1, FILTER_BANDS, dtype=F32)
    bands_row = jnp.zeros((1, LANES), F32).at[0, 1:1 + FILTER_BANDS].set(bands)
    bands_row = bands_row.at[0, 1 + FILTER_BANDS:1 + 2 * FILTER_BANDS].set(bands)
    consts = dict(bd=jnp.asarray(bd).astype(BF16), bands=bands_row,
                  wf=jnp.asarray(_minor_matrix(MINOR, -1)).astype(BF16),
                  wi=jnp.asarray(_minor_matrix(MINOR, +1)).astype(BF16))

    xs = [x_prompt, x_sample]
    for l in range(depth):
        emb = filt_w1.shape[1]
        lay = dict(
            w_in=w_in[l].astype(BF16), b_in=row(b_in[l]), w_short=w_short[l], b_short=row(b_short[l]),
            filt_w1=jnp.zeros((LANES, filt_w1.shape[2]), F32).at[:emb].set(filt_w1[l]),
            filt_b1=row(filt_b1[l]), filt_w2=filt_w2[l], filt_b2=row(filt_b2[l]), filt_w3=filt_w3[l],
            filt_b3=row(filt_b3[l]), filt_w_out=filt_w_out[l], filt_freq=row(filt_freq[l]),
            filt_decay=filt_decay[l], hyena_skip=row(hyena_skip[l]), g_fourier=row(g_fourier[l]),
            g_hyena=row(g_hyena[l]), w_out=w_out[l].astype(BF16), b_out=row(b_out[l]),
            ln1_g=row(ln1_g[l]), ln1_b=row(ln1_b[l]),
            w_router_t=w_router[l].T, b_router=b_router[l].reshape(-1, 1))
        routed = [_mixer(x, l == 0, row(ln_in_g), row(ln_in_b), lay, consts, alpha) for x in xs]
        outs = _routed_moe(routed, w_gate_up[l], b_gate_up[l], w_down[l], b_down[l],
                           row(ln2_g[l]), row(ln2_b[l]), alpha)
        xs = [o.reshape(x.shape) for o, x in zip(outs, xs)]
    return (xs[0], xs[1])
```

```python
import functools
import math

import numpy as np
import jax
import jax.numpy as jnp
from jax import lax
from jax.experimental import pallas as pl
from jax.experimental.pallas import tpu as pltpu

F32 = jnp.float32
BF16 = jnp.bfloat16
I32 = jnp.int32
U32 = jnp.uint32

LN_EPS = 1e-5
RMS_EPS = 1e-6
SWIGLU_LIMIT = 7.0
SWIGLU_ALPHA = 1.702
TOP_K = 4
FOURIER_GROUP_DIM = 64
FILTER_BANDS = 16

LANES = 128
MINOR = 128
ROW_TILE = 512
IN_TILE = 512
IN_SPLIT = 2
MAJOR_BLOCK_BYTES = 12 * 1024 * 1024
TOKEN_TILE = 512
SEG_ALIGN = 16
EXPERT_BLOCK = 512
VMEM_LIMIT = 56 * 1024 * 1024


def _params(n_grid, semantics="parallel"):
    return pltpu.CompilerParams(dimension_semantics=(semantics,) * n_grid,
                                vmem_limit_bytes=VMEM_LIMIT)


def _dot(a, b):
    return jnp.dot(a, b, preferred_element_type=F32)


def _split(a):
    hi = a.astype(BF16)
    lo = (a - hi.astype(F32)).astype(BF16)
    return hi, lo


def _dot3(a, b):
    ah, al = _split(a)
    bh, bl = _split(b)
    return _dot(ah, bh) + _dot(ah, bl) + _dot(al, bh)


def _layer_norm(x, g, b):
    mu = jnp.mean(x, axis=-1, keepdims=True)
    xc = x - mu
    var = jnp.mean(xc * xc, axis=-1, keepdims=True)
    return xc * lax.rsqrt(var + LN_EPS) * g + b


def _rms_norm(x, g):
    return x * lax.rsqrt(jnp.mean(x * x, axis=-1, keepdims=True) + RMS_EPS) * g


def _in_kernel(x_ref, xp_ref, xq_ref, lng_ref, lnb_ref, win_ref, bin_ref, bd_ref, wsh_ref, bsh_ref,
               z_ref, x0_ref, w_ref, *, tiles_per_seq, apply_ln, fw):
    it = pl.program_id(0) % tiles_per_seq
    tm = x_ref.shape[0]

    def norm(x):
        return _layer_norm(x, lng_ref[...], lnb_ref[...]) if apply_ln else x

    halo = norm(jnp.concatenate([xp_ref[...], xq_ref[...]], axis=0))
    projh = _dot(halo.astype(BF16), win_ref[:, fw:]) + bin_ref[:, fw:]
    rp = tm // IN_SPLIT
    projs = [_dot(norm(x_ref[r * rp:(r + 1) * rp, :]).astype(BF16), win_ref[...]) + bin_ref[...]
             for r in range(IN_SPLIT)]
    wsh = wsh_ref[...]
    for r, proj in enumerate(projs):
        sl = slice(r * rp, (r + 1) * rp)
        pf = proj[:, :fw].astype(BF16)
        zs = [_dot(pf[:, c0:c0 + LANES], bd_ref[...]) for c0 in range(0, fw, LANES)]
        z_ref[0, sl, :] = _pack_pair(jnp.concatenate([z[:, :LANES] for z in zs], axis=1),
                                     jnp.concatenate([z[:, LANES:] for z in zs], axis=1))

        ph = proj[:, fw:]
        prev_row = projs[r - 1][rp - 1:rp, fw:] if r > 0 else jnp.where(it > 0, projh[7:8], 0.0)
        next_row = (projs[r + 1][0:1, fw:] if r + 1 < IN_SPLIT
                    else jnp.where(it < tiles_per_seq - 1, projh[8:9], 0.0))
        rows = lax.broadcasted_iota(I32, ph.shape, 0)
        up = jnp.where(rows == 0, prev_row, pltpu.roll(ph, 1, 0))
        dn = jnp.where(rows == rp - 1, next_row, pltpu.roll(ph, rp - 1, 0))
        uc = up * wsh[0:1] + ph * wsh[1:2] + dn * wsh[2:3] + bsh_ref[...]
        hw = uc.shape[1] // 3
        x0_ref[sl, :] = uc[:, :hw]
        w_ref[sl, :] = uc[:, 2 * hw:] * uc[:, hw:2 * hw]


def _in_proj(x2d, batch, seq, fw, lng, lnb, win_bf, b_in, bd_bf, w_short, b_short, apply_ln):
    t, d = x2d.shape
    inw = win_bf.shape[1]
    hw = (inw - fw) // 3
    tm = IN_TILE
    tps = seq // tm
    r8 = tm // 8
    nblk8 = t // 8
    const = lambda i: (0, 0)
    kern = functools.partial(_in_kernel, tiles_per_seq=tps, apply_ln=apply_ln, fw=fw)
    return pl.pallas_call(
        kern,
        grid=(t // tm,),
        in_specs=[
            pl.BlockSpec((tm, d), lambda i: (i, 0)),
            pl.BlockSpec((8, d), lambda i: (jnp.maximum(i * r8 - 1, 0), 0)),
            pl.BlockSpec((8, d), lambda i: (jnp.minimum((i + 1) * r8, nblk8 - 1), 0)),
            pl.BlockSpec((1, d), const), pl.BlockSpec((1, d), const),
            pl.BlockSpec((d, inw), const), pl.BlockSpec((1, inw), const),
            pl.BlockSpec(bd_bf.shape, const),
            pl.BlockSpec((3, 3 * hw), const), pl.BlockSpec((1, 3 * hw), const),
        ],
        out_specs=[
            pl.BlockSpec((1, tm, fw), lambda i: (i // tps, i % tps, 0)),
            pl.BlockSpec((tm, hw), lambda i: (i, 0)),
            pl.BlockSpec((tm, hw), lambda i: (i, 0)),
        ],
        out_shape=[
            jax.ShapeDtypeStruct((batch, seq, fw), U32),
            jax.ShapeDtypeStruct((t, hw), F32),
            jax.ShapeDtypeStruct((t, hw), F32),
        ],
        compiler_params=_params(1),
        name="in_proj",
    )(x2d, x2d, x2d, lng, lnb, win_bf, b_in, bd_bf, w_short, b_short)


FILTER_HALO = 8


def _filter_kernel(bands_ref, w1_ref, b1_ref, w2_ref, b2_ref, w3_ref, b3_ref, wl_ref, fr_ref, dec_ref,
                   k_ref, asum_ref, tc_ref, ts_ref, kf_buf, kb_buf, sems, *, seq):
    i = pl.program_id(0)
    nsteps = pl.num_programs(0)
    tile, c = kf_buf.shape
    ext = tc_ref.shape[0]
    rep = c // LANES
    w_unit = 2.0 * math.pi / seq
    bands = bands_ref[...]

    @pl.when(i == 0)
    def _():
        arg = bands * (w_unit * lax.broadcasted_iota(I32, (ext, LANES), 0).astype(F32))
        tc_ref[...] = jnp.cos(arg)
        ts_ref[...] = jnp.sin(arg)
        asum_ref[...] = jnp.zeros_like(asum_ref)

    j0 = i * tile
    base = bands * (w_unit * j0.astype(F32))
    c0, s0 = jnp.cos(base), jnp.sin(base)
    tc, ts = tc_ref[...], ts_ref[...]
    cosf = c0 * tc - s0 * ts
    sinf = s0 * tc + c0 * ts
    j = j0 + lax.broadcasted_iota(I32, (ext, LANES), 0)
    t = j.astype(F32) * (1.0 / (seq - 1))
    lane = lax.broadcasted_iota(I32, (ext, LANES), 1)
    feat = jnp.where(lane == 0, t,
                     jnp.where(lane <= FILTER_BANDS, cosf,
                               jnp.where(lane <= 2 * FILTER_BANDS, -sinf, 0.0)))
    fr = fr_ref[...]
    h = jnp.sin(fr * (_dot3(feat, w1_ref[...]) + b1_ref[...]))
    h = jnp.sin(fr * (_dot3(h, w2_ref[...]) + b2_ref[...]))
    h = jnp.sin(fr * (_dot3(h, w3_ref[...]) + b3_ref[...]))
    hl = _dot3(h, wl_ref[...])
    t4 = jnp.concatenate([t] * rep, axis=1)
    j4 = jnp.concatenate([j] * rep, axis=1)
    dec = jnp.abs(dec_ref[...])
    hf = hl[:, :c] * jnp.exp(-t4 * dec[0:1])
    hb = hl[:, c:] * jnp.exp(-t4 * dec[1:2])
    hb = jnp.where(j4 < seq, hb, 0.0)
    asum_ref[...] += jnp.sum(jnp.abs(hf[:tile]) + jnp.abs(hb[:tile]), axis=0, keepdims=True)

    rev = (lax.broadcasted_iota(I32, (tile, ext), 1) == tile - lax.broadcasted_iota(I32, (tile, ext), 0))
    rev = jnp.where(rev, 1.0, 0.0).astype(BF16)
    b1 = hb.astype(BF16)
    r1 = hb - b1.astype(F32)
    b2 = r1.astype(BF16)
    b3 = (r1 - b2.astype(F32)).astype(BF16)
    kb = _dot(rev, b1) + _dot(rev, b2) + _dot(rev, b3)

    def copies():
        return (pltpu.make_async_copy(kf_buf, k_ref.at[pl.ds(pl.multiple_of(j0, tile), tile)], sems.at[0]),
                pltpu.make_async_copy(kb_buf, k_ref.at[pl.ds(pl.multiple_of(2 * seq - j0 - tile, tile), tile)],
                                      sems.at[1]))

    @pl.when(i > 0)
    def _():
        for cp in copies():
            cp.wait()

    kf_buf[...] = jnp.where(j4[:tile] == 0, hf[:tile] + hb[:tile], hf[:tile])
    kb_buf[...] = kb
    for cp in copies():
        cp.start()

    @pl.when(i == nsteps - 1)
    def _():
        for cp in copies():
            cp.wait()


def _hyena_filter(seq, bands_row, w1p, b1, w2, b2, w3, b3, wl, freq, decay):
    c = decay.shape[1]
    tile = ROW_TILE
    ext = tile + FILTER_HALO
    full = lambda a: pl.BlockSpec(a.shape, lambda i: (0,) * a.ndim)
    args = (bands_row, w1p, b1, w2, b2, w3, b3, wl, freq, decay)
    return pl.pallas_call(
        functools.partial(_filter_kernel, seq=seq),
        grid=(seq // tile,),
        in_specs=[full(a) for a in args],
        out_specs=[pl.BlockSpec(memory_space=pl.ANY), pl.BlockSpec((1, c), lambda i: (0, 0))],
        out_shape=[jax.ShapeDtypeStruct((2 * seq, c), F32), jax.ShapeDtypeStruct((1, c), F32)],
        scratch_shapes=[pltpu.VMEM((ext, LANES), F32), pltpu.VMEM((ext, LANES), F32),
                        pltpu.VMEM((tile, c), F32), pltpu.VMEM((tile, c), F32),
                        pltpu.SemaphoreType.DMA((2,))],
        compiler_params=_params(1, "arbitrary"),
        name="hyena_filter",
    )(*args)


def _pack_pair(re, im):
    rb = lax.bitcast_convert_type(re.astype(BF16).astype(F32), U32)
    ib = lax.bitcast_convert_type(im.astype(BF16).astype(F32), U32)
    return (rb >> 16) | ib


def _unpack_pair(word):
    re = lax.bitcast_convert_type(word << 16, F32)
    im = lax.bitcast_convert_type(word & jnp.uint32(0xFFFF0000), F32)
    return re, im


def _major_compute(g, xbuf, obuf, slot):
    _, n_ct, bsz, k, sub, lanes = xbuf.shape
    m = obuf.shape[3]
    x2 = xbuf.reshape(2 * n_ct * bsz * k * sub, lanes)
    o2 = obuf.reshape(2 * n_ct * bsz * m * sub, lanes)
    for t in range(n_ct * bsz):
        xbase = (slot * n_ct * bsz + t) * k * sub
        obase = (slot * n_ct * bsz + t) * m * sub
        for b in range(sub):
            xb = x2[pl.ds(xbase + b, k, stride=sub), :]
            if xb.dtype == U32:
                xb = jnp.concatenate(_unpack_pair(xb), axis=0)
            res = _dot(g, xb.astype(BF16))
            if obuf.dtype == U32:
                res = _pack_pair(res[:m], res[m:])
            o2[pl.ds(obase + b, m, stride=sub), :] = res


def _major_kernel(g_ref, x_hbm, o_hbm, xbuf, obuf, isem, osem, *, nj):
    s = pl.program_id(0)
    ns = pl.num_programs(0)
    slot = s % 2
    _, n_ct, bsz, k, sub, lanes = xbuf.shape
    m = obuf.shape[3]

    def copies(step, sl, hbm, buf, sem, inbound):
        i = step // nj
        j = step % nj
        out = []
        for ct in range(n_ct):
            tile = hbm.at[pl.ds(i * bsz, bsz), :, pl.ds(pl.multiple_of(j * sub, sub), sub),
                          pl.ds(ct * lanes, lanes)]
            out.append(pltpu.make_async_copy(tile, buf.at[sl, ct], sem.at[sl]) if inbound
                       else pltpu.make_async_copy(buf.at[sl, ct], tile, sem.at[sl]))
        return out

    def start(cps):
        for cp in cps:
            cp.start()

    def wait(cps):
        for cp in cps:
            cp.wait()

    @pl.when(s == 0)
    def _():
        start(copies(0, 0, x_hbm, xbuf, isem, True))

    @pl.when(s + 1 < ns)
    def _():
        start(copies(s + 1, 1 - slot, x_hbm, xbuf, isem, True))

    wait(copies(s, slot, x_hbm, xbuf, isem, True))

    @pl.when(s >= 2)
    def _():
        wait(copies(s - 2, slot, o_hbm, obuf, osem, False))

    _major_compute(g_ref[...], xbuf, obuf, slot)
    start(copies(s, slot, o_hbm, obuf, osem, False))

    @pl.when(s == ns - 1)
    def _():
        wait(copies(s, slot, o_hbm, obuf, osem, False))

        @pl.when(s >= 1)
        def _():
            wait(copies(s - 1, 1 - slot, o_hbm, obuf, osem, False))


def _dft_major(g_bf, x4, pack_out):
    b, k, n2, c = x4.shape
    m = g_bf.shape[0] // 2 if pack_out else g_bf.shape[0]
    out_dtype = U32 if pack_out else F32
    sub = 8
    n_ct = c // LANES
    bsz = max(1, min(b, MAJOR_BLOCK_BYTES // ((k + m) * sub * c * 4)))
    while b % bsz:
        bsz -= 1
    nj = n2 // sub
    return pl.pallas_call(
        functools.partial(_major_kernel, nj=nj),
        grid=((b // bsz) * nj,),
        in_specs=[pl.BlockSpec(g_bf.shape, lambda s: (0, 0)), pl.BlockSpec(memory_space=pl.ANY)],
        out_specs=pl.BlockSpec(memory_space=pl.ANY),
        out_shape=jax.ShapeDtypeStruct((b, m, n2, c), out_dtype),
        scratch_shapes=[pltpu.VMEM((2, n_ct, bsz, k, sub, LANES), x4.dtype),
                        pltpu.VMEM((2, n_ct, bsz, m, sub, LANES), out_dtype),
                        pltpu.SemaphoreType.DMA((2,)), pltpu.SemaphoreType.DMA((2,))],
        compiler_params=_params(1, "arbitrary"),
        name="dft_major",
    )(g_bf, x4)


def _minor_kernel(*refs, mode, scale, k1b):
    if mode == "conv":
        y_ref, twb_ref, twr_ref, wf_ref, wi_ref, kf_ref, o_ref, ks_ref = refs
    else:
        y_ref, twb_ref, twr_ref, wf_ref, o_ref = refs
    n2 = y_ref.shape[2]
    ct = y_ref.shape[3]
    rep = ct // LANES

    def forward(packed, cs, sn, w):
        re, im = _unpack_pair(packed)
        return _dot(w, jnp.concatenate([re * cs + im * sn, im * cs - re * sn], axis=0).astype(BF16))

    cb, sb = twb_ref[0, 0], twb_ref[1, 0]
    for j in range(k1b):
        cr, sr = twr_ref[0, j], twr_ref[1, j]
        cs = jnp.concatenate([cb * cr - sb * sr] * rep, axis=1)
        sn = jnp.concatenate([sb * cr + cb * sr] * rep, axis=1)
        if mode == "real":
            o_ref[0, :, j, :] = forward(y_ref[0, j], cs, sn, wf_ref[:n2]) * scale
            continue

        @pl.when(pl.program_id(1) == 0)
        def _():
            ks_ref[j] = forward(kf_ref[0, j], cs, sn, wf_ref[...]) * scale

        z = forward(y_ref[0, j], cs, sn, wf_ref[...])
        zr, zi = z[:n2], z[n2:]
        kr = ks_ref[j, :n2]
        ki = ks_ref[j, n2:]
        p = jnp.concatenate([zr * kr - zi * ki, zr * ki + zi * kr], axis=0).astype(BF16)
        q = _dot(wi_ref[...], p)
        qr, qi = q[:n2], q[n2:]
        o_ref[0, j] = _pack_pair(qr * cs - qi * sn, qi * cs + qr * sn)


def _dft_minor(y4, wf_bf, mode, scale=1.0, wi_bf=None, kf4=None):
    b, n1, n2, c = y4.shape
    k1b = min(8, n1)
    ct = c
    tw_block, tw_rem = _twiddles(n1, n2, k1b)
    grid = (n1 // k1b, b)
    dspec = pl.BlockSpec((1, k1b, n2, ct), lambda j, i: (i, j, 0, 0))
    in_specs = [dspec,
                pl.BlockSpec((2, 1, n2, LANES), lambda j, i: (0, j, 0, 0)),
                pl.BlockSpec((2, k1b, n2, LANES), lambda j, i: (0, 0, 0, 0)),
                pl.BlockSpec(wf_bf.shape, lambda j, i: (0, 0))]
    args = [y4, tw_block, tw_rem, wf_bf]
    scratch = []
    if mode == "conv":
        in_specs += [pl.BlockSpec(wi_bf.shape, lambda j, i: (0, 0)),
                     pl.BlockSpec((1, k1b, n2, ct), lambda j, i: (0, j, 0, 0))]
        args += [wi_bf, kf4]
        scratch = [pltpu.VMEM((k1b, 2 * n2, ct), F32)]
    if mode == "real":
        out_spec = pl.BlockSpec((1, n2, k1b, ct), lambda j, i: (i, 0, j, 0))
        out_shape = jax.ShapeDtypeStruct((b, n2, n1, c), F32)
    else:
        out_spec = dspec
        out_shape = jax.ShapeDtypeStruct((b, n1, n2, c), U32)
    return pl.pallas_call(
        functools.partial(_minor_kernel, mode=mode, scale=scale, k1b=k1b),
        grid=grid, in_specs=in_specs, out_specs=out_spec, out_shape=out_shape,
        scratch_shapes=scratch,
        compiler_params=pltpu.CompilerParams(dimension_semantics=("parallel", "arbitrary"),
                                             vmem_limit_bytes=VMEM_LIMIT),
        name="dft_minor_" + mode,
    )(*args)


def _cs(num, den):
    ang = 2.0 * np.pi * (num % den) / den
    return np.cos(ang), np.sin(ang)


def _complex_block(c, s, sign):
    return np.block([[c, -sign * s], [sign * s, c]]).astype(np.float32)


def _major_matrix(n1, a_in, a_out, sign, real_in=False):
    c, s = _cs(np.outer(np.arange(a_out), np.arange(a_in)), n1)
    g = _complex_block(c, s, sign)
    return g[:, :a_in] if real_in else g


def _minor_matrix(n2, sign):
    c, s = _cs(np.outer(np.arange(n2), np.arange(n2)), n2)
    return _complex_block(c, s, sign)


def _twiddles(n1, n2, k1b):
    def table(k1):
        c, s = _cs(np.outer(k1, np.arange(n2)), n1 * n2)
        tw = jnp.asarray(np.stack([c, s]).astype(np.float32))
        return jnp.broadcast_to(tw[..., None], (2, len(k1), n2, LANES))
    return table(np.arange(0, n1, k1b)), table(np.arange(k1b))


def _out_kernel(yf_ref, yc_ref, x0_ref, w_ref, x_ref, lng_ref, lnb_ref, inv_ref, skip_ref, gf_ref, gh_ref,
                wo_ref, bo_ref, g1_ref, b1_ref, wr_ref, br_ref, x1_ref, idx_ref, gate_ref, cnt_ref,
                *, alpha, apply_ln, top_k):
    c = yc_ref.shape[1]
    w = w_ref[...]
    yh = x0_ref[...] * (yc_ref[...] * inv_ref[...] + skip_ref[...] * w)
    mf = _rms_norm(yf_ref[...], gf_ref[...]).astype(BF16)
    mh = _rms_norm(yh, gh_ref[...]).astype(BF16)
    m = _dot(mf, wo_ref[:c]) + _dot(mh, wo_ref[c:]) + bo_ref[...]
    xn = _layer_norm(x_ref[...], lng_ref[...], lnb_ref[...]) if apply_ln else x_ref[...]
    x1 = _layer_norm(alpha * xn + m, g1_ref[...], b1_ref[...])
    x1_ref[...] = x1
    _route_tile(x1, wr_ref[...], br_ref[...], idx_ref, gate_ref, cnt_ref, top_k)


def _out_proj(yf, yc, x0, w, x, lng, lnb, inv_norm, skip, gf, gh, wo_bf, bo, g1, b1, wr_t, br_col, alpha,
              apply_ln):
    t, d = x.shape
    c = yc.shape[1]
    e = wr_t.shape[0]
    tm = TOKEN_TILE
    nt = t // tm
    row = lambda i: (i, 0)
    const = lambda i: (0, 0)
    tile3 = lambda i: (i, 0, 0)
    return pl.pallas_call(
        functools.partial(_out_kernel, alpha=alpha, apply_ln=apply_ln, top_k=TOP_K),
        grid=(nt,),
        in_specs=[
            pl.BlockSpec((tm, c), row),
            pl.BlockSpec((tm, c), row), pl.BlockSpec((tm, c), row), pl.BlockSpec((tm, c), row),
            pl.BlockSpec((tm, d), row),
            pl.BlockSpec((1, d), const), pl.BlockSpec((1, d), const),
            pl.BlockSpec((1, c), const), pl.BlockSpec((1, c), const),
            pl.BlockSpec((1, c), const), pl.BlockSpec((1, c), const),
            pl.BlockSpec(wo_bf.shape, const), pl.BlockSpec((1, d), const),
            pl.BlockSpec((1, d), const), pl.BlockSpec((1, d), const),
            pl.BlockSpec((e, d), const), pl.BlockSpec((e, 1), const),
        ],
        out_specs=[pl.BlockSpec((tm, d), row),
                   pl.BlockSpec((1, 2 * TOP_K, tm), tile3),
                   pl.BlockSpec((1, 8, tm), tile3),
                   pl.BlockSpec((1, e, 1), tile3)],
        out_shape=[jax.ShapeDtypeStruct((t, d), F32),
                   jax.ShapeDtypeStruct((nt, 2 * TOP_K, tm), I32),
                   jax.ShapeDtypeStruct((nt, 8, tm), F32),
                   jax.ShapeDtypeStruct((nt, e, 1), I32)],
        compiler_params=_params(1),
        name="out_proj",
    )(yf, yc, x0, w, x, lng, lnb, inv_norm, skip, gf, gh, wo_bf, bo, g1, b1, wr_t, br_col)


def _route_tile(x1, wr, br, idx_ref, gate_ref, cnt_ref, top_k):
    tt = x1.shape[0]
    e = wr.shape[0]
    nt = (((1,), (1,)), ((), ()))
    xh, xl = _split(x1)
    wh, wl = _split(wr)
    logits = (lax.dot_general(wh, xh, nt, preferred_element_type=F32)
              + lax.dot_general(wh, xl, nt, preferred_element_type=F32)
              + lax.dot_general(wl, xh, nt, preferred_element_type=F32)) + br
    iota_e = lax.broadcasted_iota(I32, (e, tt), 0).astype(F32)
    l = logits
    tops, idxs, hots = [], [], []
    for _ in range(top_k):
        m = jnp.max(l, axis=0, keepdims=True)
        idx = jnp.min(jnp.where(l == m, iota_e, float(e)), axis=0, keepdims=True)
        hot = iota_e == idx
        l = jnp.where(hot, -jnp.inf, l)
        tops.append(m)
        idxs.append(idx)
        hots.append(hot)
    ex = [jnp.exp(m - tops[0]) for m in tops]
    den = ex[0]
    for v in ex[1:]:
        den = den + v
    gates = [v / den for v in ex]
    hot_all = jnp.zeros((e, tt), F32)
    for hot in hots:
        hot_all = jnp.where(hot, 1.0, hot_all)
    upper = (lax.broadcasted_iota(I32, (tt, tt), 0) < lax.broadcasted_iota(I32, (tt, tt), 1))
    before = _dot(hot_all.astype(BF16), jnp.where(upper, 1.0, 0.0).astype(BF16))
    ranks = [jnp.sum(jnp.where(hot, before, 0.0), axis=0, keepdims=True).astype(I32) for hot in hots]
    pad = 8 - top_k
    idx_ref[0] = jnp.concatenate([v.astype(I32) for v in idxs] + ranks, axis=0)
    gate_ref[0] = jnp.concatenate(gates + [jnp.zeros((pad, tt), F32)], axis=0)
    cnt_ref[0] = jnp.sum(hot_all, axis=1, keepdims=True).astype(I32)


def _dual_specs(shape, nt0):
    first = pl.BlockSpec(shape, lambda i, *_: (jnp.minimum(i, nt0 - 1),) + (0,) * (len(shape) - 1))
    second = pl.BlockSpec(shape, lambda i, *_: (jnp.maximum(i - nt0, 0),) + (0,) * (len(shape) - 1))
    return first, second


SEG_PIECES = 3
SEG_CHUNK = SEG_ALIGN << SEG_PIECES


def _segment_copies(first, count, loff_s, len_s, goff_s, local, remote, sem, to_remote, wait):
    def copy(lstart, gstart, size):
        lref = local.at[pl.ds(pl.multiple_of(lstart, SEG_ALIGN), size)]
        rref = remote.at[pl.ds(pl.multiple_of(gstart, SEG_ALIGN), size)]
        cp = (pltpu.make_async_copy(lref, rref, sem) if to_remote
              else pltpu.make_async_copy(rref, lref, sem))
        if wait:
            cp.wait()
        else:
            cp.start()

    def body(e, carry):
        s = first + e
        n = len_s[s] // SEG_ALIGN
        lo = 0 if loff_s is None else loff_s[s]
        go = goff_s[s]
        chunks = n >> SEG_PIECES

        def chunk(i, c):
            copy(lo + i * SEG_CHUNK, go + i * SEG_CHUNK, SEG_CHUNK)
            return c

        lax.fori_loop(0, chunks, chunk, 0)
        rem = n & ((1 << SEG_PIECES) - 1)
        base = chunks * SEG_CHUNK
        for p in reversed(range(SEG_PIECES)):
            start = base + ((rem >> (p + 1)) << (p + 1)) * SEG_ALIGN

            @pl.when(((rem >> p) & 1) == 1)
            def _():
                copy(lo + start, go + start, SEG_ALIGN << p)
        return carry

    lax.fori_loop(0, count, body, 0)


def _wait_tile(step, n_experts, loff_s, len_s, local, remote, sem, to_remote):
    last = step * n_experts + n_experts - 1
    n = (loff_s[last] + len_s[last]) // SEG_ALIGN
    for p in range((local.shape[0] // SEG_ALIGN).bit_length()):
        size = SEG_ALIGN << p

        @pl.when(((n >> p) & 1) == 1)
        def _():
            lref = local.at[pl.ds(0, size)]
            rref = remote.at[pl.ds(0, size)]
            cp = (pltpu.make_async_copy(lref, rref, sem) if to_remote
                  else pltpu.make_async_copy(rref, lref, sem))
            cp.wait()


def _dispatch_kernel(loff_s, seg_s, goff_s, plen_s, poff_s, nb_s, xa_ref, xb_ref, idx_ref, xs_ref, pos_ref,
                     buf, sems, *, n_experts, top_k, nt0):
    j = pl.program_id(0)
    nt = pl.num_programs(0)
    slot = j % 2
    tt = xa_ref.shape[0]
    lr = buf.shape[1]
    idx = idx_ref[0]
    tope = idx[:top_k]
    base = jnp.zeros_like(tope)
    for e in range(n_experts):
        base = jnp.where(tope == e, loff_s[j * n_experts + e], base)
    pos = base + idx[top_k:]
    pos_ref[0] = jnp.concatenate([pos, jnp.zeros((8 - top_k, tt), I32)], axis=0)
    rows = lax.broadcasted_iota(jnp.int16, (lr, tt), 0)
    pos16 = pos.astype(jnp.int16)
    perm = jnp.zeros((lr, tt), BF16)
    for k in range(top_k):
        perm = jnp.where(rows == pos16[k:k + 1], jnp.ones((lr, tt), BF16), perm)
    x = jnp.where(j < nt0, xa_ref[...], xb_ref[...])
    buf[slot] = _dot(perm, x.astype(BF16)).astype(BF16)

    def drained(step, s):
        _wait_tile(step, n_experts, loff_s, seg_s, buf.at[s], xs_ref, sems.at[s], True)

    _segment_copies(j * n_experts, n_experts, loff_s, seg_s, goff_s, buf.at[slot], xs_ref, sems.at[slot],
                    True, False)

    @pl.when(j > 0)
    def _():
        drained(j - 1, 1 - slot)

    @pl.when(j == nt - 1)
    def _():
        drained(j, slot)
        zrows = EXPERT_BLOCK
        buf[0, :zrows] = jnp.zeros((zrows, buf.shape[2]), BF16)
        _segment_copies(0, n_experts, None, plen_s, poff_s, buf.at[0], xs_ref, sems.at[0], True, False)
        _segment_copies(0, n_experts, None, plen_s, poff_s, buf.at[0], xs_ref, sems.at[0], True, True)
        n_tail = xs_ref.shape[0] // zrows - nb_s[0]

        def tail_copy(i):
            dst = xs_ref.at[pl.ds(pl.multiple_of((nb_s[0] + i) * zrows, zrows), zrows)]
            return pltpu.make_async_copy(buf.at[0, pl.ds(0, zrows)], dst, sems.at[0])

        lax.fori_loop(0, n_tail, lambda i, c: (tail_copy(i).start(), c)[1], 0)
        lax.fori_loop(0, n_tail, lambda i, c: (tail_copy(i).wait(), c)[1], 0)


def _dispatch(x1s, idx, loff, seg, goff, plen, poff, n_blocks, n_rows, local_rows):
    d = x1s[0].shape[1]
    tt = TOKEN_TILE
    nt0 = x1s[0].shape[0] // tt
    nt = nt0 + x1s[1].shape[0] // tt
    e = loff.shape[0] // nt
    xa, xb = _dual_specs((tt, d), nt0)
    grid_spec = pltpu.PrefetchScalarGridSpec(
        num_scalar_prefetch=6,
        grid=(nt,),
        in_specs=[xa, xb, pl.BlockSpec((1, 2 * TOP_K, tt), lambda i, *_: (i, 0, 0))],
        out_specs=[pl.BlockSpec(memory_space=pl.ANY),
                   pl.BlockSpec((1, 8, tt), lambda i, *_: (i, 0, 0))],
        scratch_shapes=[pltpu.VMEM((2, local_rows, d), BF16), pltpu.SemaphoreType.DMA((2,))],
    )
    return pl.pallas_call(
        functools.partial(_dispatch_kernel, n_experts=e, top_k=TOP_K, nt0=nt0),
        grid_spec=grid_spec,
        out_shape=[jax.ShapeDtypeStruct((n_rows, d), BF16), jax.ShapeDtypeStruct((nt, 8, tt), I32)],
        compiler_params=_params(1, "arbitrary"),
        name="moe_dispatch",
    )(loff, seg, goff, plen, poff, n_blocks, x1s[0], x1s[1], idx)


def _combine_kernel(loff_s, seg_s, goff_s, ys_ref, pos_ref, gate_ref, xa_ref, xb_ref, g2_ref, b2_ref,
                    oa_ref, ob_ref, buf, sems, *, n_experts, top_k, alpha, nt0):
    j = pl.program_id(0)
    nt = pl.num_programs(0)
    slot = j % 2
    tt = xa_ref.shape[0]
    lr = buf.shape[1]

    def prefetch(step, s):
        buf[s] = jnp.zeros((lr, buf.shape[2]), BF16)
        _segment_copies(step * n_experts, n_experts, loff_s, seg_s, goff_s, buf.at[s], ys_ref, sems.at[s],
                        False, False)

    @pl.when(j == 0)
    def _():
        prefetch(0, 0)

    @pl.when(j + 1 < nt)
    def _():
        prefetch(j + 1, 1 - slot)

    _wait_tile(j, n_experts, loff_s, seg_s, buf.at[slot], ys_ref, sems.at[slot], False)
    pos = pos_ref[0].astype(jnp.int16)
    gate = gate_ref[0].astype(BF16)
    rows = lax.broadcasted_iota(jnp.int16, (lr, tt), 0)
    wsel = jnp.zeros((lr, tt), BF16)
    for k in range(top_k):
        wsel = jnp.where(rows == pos[k:k + 1], gate[k:k + 1], wsel)
    moe = lax.dot_general(wsel, buf[slot], (((0,), (0,)), ((), ())), preferred_element_type=F32)
    x1 = jnp.where(j < nt0, xa_ref[...], xb_ref[...])
    out = _layer_norm(alpha * x1 + moe, g2_ref[...], b2_ref[...])

    @pl.when(j < nt0)
    def _():
        oa_ref[...] = out

    @pl.when(j >= nt0)
    def _():
        ob_ref[...] = out


def _combine(ys, pos_tm, gate_tm, x1s, g2, b2, loff, seg, goff, local_rows, alpha):
    d = x1s[0].shape[1]
    tt = TOKEN_TILE
    nt0 = x1s[0].shape[0] // tt
    nt = nt0 + x1s[1].shape[0] // tt
    e = loff.shape[0] // nt
    xa, xb = _dual_specs((tt, d), nt0)
    grid_spec = pltpu.PrefetchScalarGridSpec(
        num_scalar_prefetch=3,
        grid=(nt,),
        in_specs=[pl.BlockSpec(memory_space=pl.ANY),
                  pl.BlockSpec((1, 8, tt), lambda i, *_: (i, 0, 0)),
                  pl.BlockSpec((1, 8, tt), lambda i, *_: (i, 0, 0)),
                  xa, xb,
                  pl.BlockSpec((1, d), lambda i, *_: (0, 0)),
                  pl.BlockSpec((1, d), lambda i, *_: (0, 0))],
        out_specs=list(_dual_specs((tt, d), nt0)),
        scratch_shapes=[pltpu.VMEM((2, local_rows, d), BF16), pltpu.SemaphoreType.DMA((2,))],
    )
    return pl.pallas_call(
        functools.partial(_combine_kernel, n_experts=e, top_k=TOP_K, alpha=alpha, nt0=nt0),
        grid_spec=grid_spec,
        out_shape=[jax.ShapeDtypeStruct(x.shape, F32) for x in x1s],
        compiler_params=_params(1, "arbitrary"),
        name="moe_combine",
    )(loff, seg, goff, ys, pos_tm, gate_tm, x1s[0], x1s[1], g2, b2)


def _expert_kernel(be_s, nb_s, xs_ref, wgu_ref, bgu_ref, wd_ref, bd_ref, ys_ref, wgu_bf, wd_bf):
    b = pl.program_id(0)
    active = b < nb_s[0]
    fresh = jnp.logical_or(b == 0, be_s[b] != be_s[jnp.maximum(b - 1, 0)])

    @pl.when(jnp.logical_and(active, fresh))
    def _():
        wgu_bf[...] = wgu_ref[0].astype(BF16)
        wd_bf[...] = wd_ref[0].astype(BF16)

    @pl.when(active)
    def _():
        dff = wd_ref.shape[1]
        gu = _dot(xs_ref[...], wgu_bf[...]) + bgu_ref[0]
        gate = jnp.minimum(gu[:, :dff], SWIGLU_LIMIT)
        up = jnp.clip(gu[:, dff:], -SWIGLU_LIMIT, SWIGLU_LIMIT)
        h = (up + 1.0) * (gate * jax.nn.sigmoid(SWIGLU_ALPHA * gate))
        ys_ref[...] = (_dot(h.astype(BF16), wd_bf[...]) + bd_ref[0]).astype(BF16)

    @pl.when(jnp.logical_not(active))
    def _():
        ys_ref[...] = jnp.zeros_like(ys_ref)


def _experts(xs, block_expert, n_blocks, wgu, bgu, wd, bdn):
    n_rows, d = xs.shape
    bm = EXPERT_BLOCK
    f2 = wgu.shape[2]
    dff = wd.shape[1]
    blk = lambda b, be, nb: (jnp.minimum(b, nb[0] - 1), 0)
    wsel = lambda b, be, nb: (be[b], 0, 0)
    grid_spec = pltpu.PrefetchScalarGridSpec(
        num_scalar_prefetch=2,
        grid=(n_rows // bm,),
        in_specs=[pl.BlockSpec((bm, d), blk),
                  pl.BlockSpec((1, d, f2), wsel), pl.BlockSpec((1, 1, f2), wsel),
                  pl.BlockSpec((1, dff, d), wsel), pl.BlockSpec((1, 1, d), wsel)],
        out_specs=pl.BlockSpec((bm, d), lambda b, be, nb: (b, 0)),
        scratch_shapes=[pltpu.VMEM((d, f2), BF16), pltpu.VMEM((dff, d), BF16)],
    )
    return pl.pallas_call(
        _expert_kernel,
        grid_spec=grid_spec,
        out_shape=jax.ShapeDtypeStruct((n_rows, d), BF16),
        compiler_params=_params(1, "arbitrary"),
        name="moe_experts",
    )(block_expert, n_blocks, xs, wgu, bgu, wd, bdn)


def _routed_moe(routed, w_gate_up, b_gate_up, w_down, b_down, g2, b2, alpha):
    x1s = [r[0] for r in routed]
    idx, gate, cnt = (jnp.concatenate([r[i] for r in routed], axis=0) for i in (1, 2, 3))
    t = x1s[0].shape[0] + x1s[1].shape[0]
    e = cnt.shape[1]
    tt = TOKEN_TILE
    nt = t // tt
    bm = EXPERT_BLOCK

    cnt = cnt.reshape(nt, e)
    seg = (cnt + SEG_ALIGN - 1) // SEG_ALIGN * SEG_ALIGN
    loff = jnp.cumsum(seg, axis=1) - seg
    per_expert = jnp.sum(seg, axis=0)
    padded = (per_expert + bm - 1) // bm * bm
    blocks_end = jnp.cumsum(padded) // bm
    start = jnp.cumsum(padded) - padded
    goff = start[None, :] + jnp.cumsum(seg, axis=0) - seg
    local_rows = TOP_K * tt + e * SEG_ALIGN
    n_rows_max = -(-(TOP_K * t + (SEG_ALIGN - 1) * e * nt) // bm) * bm + e * bm
    n_blocks = blocks_end[-1:].astype(I32)
    bidx = jnp.minimum(jnp.arange(n_rows_max // bm, dtype=I32), n_blocks[0] - 1)
    block_expert = jnp.minimum(jnp.sum(bidx[:, None] >= blocks_end[None, :], axis=1), e - 1).astype(I32)
    flat = lambda a: a.reshape(-1).astype(I32)
    plen, poff = flat(padded - per_expert), flat(start + per_expert)

    xs, pos = _dispatch(x1s, idx, flat(loff), flat(seg), flat(goff), plen, poff, n_blocks, n_rows_max,
                        local_rows)
    ys = _experts(xs, block_expert, n_blocks, w_gate_up, b_gate_up.reshape(e, 1, -1),
                  w_down, b_down.reshape(e, 1, -1))
    return _combine(ys, pos, gate, x1s, g2, b2, flat(loff), flat(seg), flat(goff), local_rows, alpha)


def _mixer(x, apply_ln, lng, lnb, lay, consts, alpha):
    batch, seq, d = x.shape
    t = batch * seq
    fw = lay["g_fourier"].shape[1]
    n2 = MINOR
    x2d = x.reshape(t, d)
    z, x0, w = _in_proj(x2d, batch, seq, fw, lng, lnb, lay["w_in"], lay["b_in"], consts["bd"],
                        lay["w_short"], lay["b_short"], apply_ln)
    c = x0.shape[1]

    n1 = seq // n2
    g = jnp.asarray(_major_matrix(n1, n1, n1, -1)).astype(BF16)
    y = _dft_major(g, z.reshape(batch, n1, n2, fw), True)
    yf = _dft_minor(y, consts["wf"], "real",
                    scale=1.0 / math.sqrt(seq * FOURIER_GROUP_DIM))

    n = 2 * seq
    m1 = n // n2
    kf, asum = _hyena_filter(seq, consts["bands"], lay["filt_w1"], lay["filt_b1"], lay["filt_w2"],
                             lay["filt_b2"], lay["filt_w3"], lay["filt_b3"], lay["filt_w_out"],
                             lay["filt_freq"], lay["filt_decay"])
    gk = jnp.asarray(_major_matrix(m1, m1, m1, -1, real_in=True)).astype(BF16)
    ky = _dft_major(gk, kf.reshape(1, m1, n2, c), True)
    gfw = jnp.asarray(_major_matrix(m1, m1 // 2, m1, -1)).astype(BF16)
    gin = jnp.asarray(_major_matrix(m1, m1, m1 // 2, +1)).astype(BF16)
    pairs = batch // 2
    u = _dft_major(gfw, w.reshape(pairs, m1, n2, c), True)
    v = _dft_minor(u, consts["wf"], "conv", scale=1.0 / n, wi_bf=consts["wi"], kf4=ky)
    yc = _dft_major(gin, v, False)

    inv_norm = 1.0 / asum
    return _out_proj(yf.reshape(t, fw), yc.reshape(t, c), x0, w, x2d, lng, lnb, inv_norm,
                     lay["hyena_skip"], lay["g_fourier"], lay["g_hyena"], lay["w_out"], lay["b_out"],
                     lay["ln1_g"], lay["ln1_b"], lay["w_router_t"], lay["b_router"], alpha, apply_ln)


def kernel(x_prompt, x_sample, ln_in_g, ln_in_b, w_in, b_in, w_short, b_short, filt_w1, filt_b1, filt_w2, filt_b2, filt_w3, filt_b3, filt_w_out, filt_freq, filt_decay, hyena_skip, g_fourier, g_hyena, w_out, b_out, ln1_g, ln1_b, w_router, b_router, w_gate_up, b_gate_up, w_down, b_down, ln2_g, ln2_b):
    depth, d, in_width = w_in.shape
    fw = g_fourier.shape[1]
    alpha = (2.0 * depth) ** 0.25
    row = lambda a: a.reshape(1, -1)

    gd = FOURIER_GROUP_DIM
    cg, sg = _cs(np.outer(np.arange(gd), np.arange(gd)), gd)
    eye = np.eye(LANES // gd)
    bd = np.concatenate([np.kron(eye, cg), -np.kron(eye, sg)], axis=1).astype(np.float32)
    bands = jnp.linspace(1e-4, FILTER_BANDS - 1, FILTER_BANDS, dtype=F32)
    bands_row = jnp.zeros((1, LANES), F32).at[0, 1:1 + FILTER_BANDS].set(bands)
    bands_row = bands_row.at[0, 1 + FILTER_BANDS:1 + 2 * FILTER_BANDS].set(bands)
    consts = dict(bd=jnp.asarray(bd).astype(BF16), bands=bands_row,
                  wf=jnp.asarray(_minor_matrix(MINOR, -1)).astype(BF16),
                  wi=jnp.asarray(_minor_matrix(MINOR, +1)).astype(BF16))

    xs = [x_prompt, x_sample]
    for l in range(depth):
        emb = filt_w1.shape[1]
        lay = dict(
            w_in=w_in[l].astype(BF16), b_in=row(b_in[l]), w_short=w_short[l], b_short=row(b_short[l]),
            filt_w1=jnp.zeros((LANES, filt_w1.shape[2]), F32).at[:emb].set(filt_w1[l]),
            filt_b1=row(filt_b1[l]), filt_w2=filt_w2[l], filt_b2=row(filt_b2[l]), filt_w3=filt_w3[l],
            filt_b3=row(filt_b3[l]), filt_w_out=filt_w_out[l], filt_freq=row(filt_freq[l]),
            filt_decay=filt_decay[l], hyena_skip=row(hyena_skip[l]), g_fourier=row(g_fourier[l]),
            g_hyena=row(g_hyena[l]), w_out=w_out[l].astype(BF16), b_out=row(b_out[l]),
            ln1_g=row(ln1_g[l]), ln1_b=row(ln1_b[l]),
            w_router_t=w_router[l].T, b_router=b_router[l].reshape(-1, 1))
        routed = [_mixer(x, l == 0, row(ln_in_g), row(ln_in_b), lay, consts, alpha) for x in xs]
        outs = _routed_moe(routed, w_gate_up[l], b_gate_up[l], w_down[l], b_down[l],
                           row(ln2_g[l]), row(ln2_b[l]), alpha)
        xs = [o.reshape(x.shape) for o, x in zip(outs, xs)]
    return (xs[0], xs[1])
```

```python
import functools
import math

import numpy as np
import jax
import jax.numpy as jnp
from jax import lax
from jax.experimental import pallas as pl
from jax.experimental.pallas import tpu as pltpu

F32 = jnp.float32
BF16 = jnp.bfloat16
I32 = jnp.int32
U32 = jnp.uint32

LN_EPS = 1e-5
RMS_EPS = 1e-6
SWIGLU_LIMIT = 7.0
SWIGLU_ALPHA = 1.702
TOP_K = 4
FOURIER_GROUP_DIM = 64
FILTER_BANDS = 16

LANES = 128
MINOR = 128
ROW_TILE = 512
IN_TILE = 512
IN_SPLIT = 2
MAJOR_BLOCK_BYTES = 12 * 1024 * 1024
TOKEN_TILE = 512
SEG_ALIGN = 16
EXPERT_BLOCK = 512
VMEM_LIMIT = 56 * 1024 * 1024


def _params(n_grid, semantics="parallel"):
    return pltpu.CompilerParams(dimension_semantics=(semantics,) * n_grid,
                                vmem_limit_bytes=VMEM_LIMIT)


def _dot(a, b):
    return jnp.dot(a, b, preferred_element_type=F32)


def _split(a):
    hi = a.astype(BF16)
    lo = (a - hi.astype(F32)).astype(BF16)
    return hi, lo


def _dot3(a, b):
    ah, al = _split(a)
    bh, bl = _split(b)
    return _dot(ah, bh) + _dot(ah, bl) + _dot(al, bh)


def _layer_norm(x, g, b):
    mu = jnp.mean(x, axis=-1, keepdims=True)
    xc = x - mu
    var = jnp.mean(xc * xc, axis=-1, keepdims=True)
    return xc * lax.rsqrt(var + LN_EPS) * g + b


def _rms_norm(x, g):
    return x * lax.rsqrt(jnp.mean(x * x, axis=-1, keepdims=True) + RMS_EPS) * g


def _in_kernel(x_ref, xp_ref, xq_ref, lng_ref, lnb_ref, win_ref, bin_ref, bd_ref, wsh_ref, bsh_ref,
               z_ref, x0_ref, w_ref, *, tiles_per_seq, apply_ln, fw):
    it = pl.program_id(0) % tiles_per_seq
    tm = x_ref.shape[0]

    def norm(x):
        return _layer_norm(x, lng_ref[...], lnb_ref[...]) if apply_ln else x

    halo = norm(jnp.concatenate([xp_ref[...], xq_ref[...]], axis=0))
    projh = _dot(halo.astype(BF16), win_ref[:, fw:]) + bin_ref[:, fw:]
    rp = tm // IN_SPLIT
    projs = [_dot(norm(x_ref[r * rp:(r + 1) * rp, :]).astype(BF16), win_ref[...]) + bin_ref[...]
             for r in range(IN_SPLIT)]
    wsh = wsh_ref[...]
    for r, proj in enumerate(projs):
        sl = slice(r * rp, (r + 1) * rp)
        pf = proj[:, :fw].astype(BF16)
        zs = [_dot(pf[:, c0:c0 + LANES], bd_ref[...]) for c0 in range(0, fw, LANES)]
        z_ref[0, sl, :] = _pack_pair(jnp.concatenate([z[:, :LANES] for z in zs], axis=1),
                                     jnp.concatenate([z[:, LANES:] for z in zs], axis=1))

        ph = proj[:, fw:]
        prev_row = projs[r - 1][rp - 1:rp, fw:] if r > 0 else jnp.where(it > 0, projh[7:8], 0.0)
        next_row = (projs[r + 1][0:1, fw:] if r + 1 < IN_SPLIT
                    else jnp.where(it < tiles_per_seq - 1, projh[8:9], 0.0))
        rows = lax.broadcasted_iota(I32, ph.shape, 0)
        up = jnp.where(rows == 0, prev_row, pltpu.roll(ph, 1, 0))
        dn = jnp.where(rows == rp - 1, next_row, pltpu.roll(ph, rp - 1, 0))
        uc = up * wsh[0:1] + ph * wsh[1:2] + dn * wsh[2:3] + bsh_ref[...]
        hw = uc.shape[1] // 3
        x0_ref[sl, :] = uc[:, :hw]
        w_ref[sl, :] = uc[:, 2 * hw:] * uc[:, hw:2 * hw]


def _in_proj(x2d, batch, seq, fw, lng, lnb, win_bf, b_in, bd_bf, w_short, b_short, apply_ln):
    t, d = x2d.shape
    inw = win_bf.shape[1]
    hw = (inw - fw) // 3
    tm = IN_TILE
    tps = seq // tm
    r8 = tm // 8
    nblk8 = t // 8
    const = lambda i: (0, 0)
    kern = functools.partial(_in_kernel, tiles_per_seq=tps, apply_ln=apply_ln, fw=fw)
    return pl.pallas_call(
        kern,
        grid=(t // tm,),
        in_specs=[
            pl.BlockSpec((tm, d), lambda i: (i, 0)),
            pl.BlockSpec((8, d), lambda i: (jnp.maximum(i * r8 - 1, 0), 0)),
            pl.BlockSpec((8, d), lambda i: (jnp.minimum((i + 1) * r8, nblk8 - 1), 0)),
            pl.BlockSpec((1, d), const), pl.BlockSpec((1, d), const),
            pl.BlockSpec((d, inw), const), pl.BlockSpec((1, inw), const),
            pl.BlockSpec(bd_bf.shape, const),
            pl.BlockSpec((3, 3 * hw), const), pl.BlockSpec((1, 3 * hw), const),
        ],
        out_specs=[
            pl.BlockSpec((1, tm, fw), lambda i: (i // tps, i % tps, 0)),
            pl.BlockSpec((tm, hw), lambda i: (i, 0)),
            pl.BlockSpec((tm, hw), lambda i: (i, 0)),
        ],
        out_shape=[
            jax.ShapeDtypeStruct((batch, seq, fw), U32),
            jax.ShapeDtypeStruct((t, hw), F32),
            jax.ShapeDtypeStruct((t, hw), F32),
        ],
        compiler_params=_params(1),
        name="in_proj",
    )(x2d, x2d, x2d, lng, lnb, win_bf, b_in, bd_bf, w_short, b_short)


FILTER_HALO = 16


def _filter_kernel(bands_ref, w1_ref, b1_ref, w2_ref, b2_ref, w3_ref, b3_ref, wl_ref, fr_ref, dec_ref,
                   k_ref, asum_ref, tc_ref, ts_ref, kf_buf, kb_buf, sems, *, seq):
    i = pl.program_id(0)
    nsteps = pl.num_programs(0)
    tile, c = kf_buf.shape
    ext = tc_ref.shape[0]
    rep = c // LANES
    w_unit = 2.0 * math.pi / seq
    bands = bands_ref[...]

    @pl.when(i == 0)
    def _():
        arg = bands * (w_unit * lax.broadcasted_iota(I32, (ext, LANES), 0).astype(F32))
        tc_ref[...] = jnp.cos(arg)
        ts_ref[...] = jnp.sin(arg)
        asum_ref[...] = jnp.zeros_like(asum_ref)

    j0 = i * tile
    base = bands * (w_unit * j0.astype(F32))
    c0, s0 = jnp.cos(base), jnp.sin(base)
    tc, ts = tc_ref[...], ts_ref[...]
    cosf = c0 * tc - s0 * ts
    sinf = s0 * tc + c0 * ts
    j = j0 + lax.broadcasted_iota(I32, (ext, LANES), 0)
    t = j.astype(F32) * (1.0 / (seq - 1))
    lane = lax.broadcasted_iota(I32, (ext, LANES), 1)
    feat = jnp.where(lane == 0, t,
                     jnp.where(lane <= FILTER_BANDS, cosf,
                               jnp.where(lane <= 2 * FILTER_BANDS, -sinf, 0.0)))
    half = ext // 2
    fr = fr_ref[...]
    h = jnp.concatenate([feat[:half], feat[half:]], axis=1)
    h = jnp.sin(fr * (_dot3(h, w1_ref[...]) + b1_ref[...]))
    h = jnp.sin(fr * (_dot3(h, w2_ref[...]) + b2_ref[...]))
    h = jnp.sin(fr * (_dot3(h, w3_ref[...]) + b3_ref[...]))
    hl = jnp.concatenate([_dot3(h, wl_ref[0]), _dot3(h, wl_ref[1])], axis=0)
    t4 = jnp.concatenate([t] * rep, axis=1)
    j4 = jnp.concatenate([j] * rep, axis=1)
    dec = jnp.abs(dec_ref[...])
    hf = hl[:, :c] * jnp.exp(-t4 * dec[0:1])
    hb = hl[:, c:] * jnp.exp(-t4 * dec[1:2])
    hb = jnp.where(j4 < seq, hb, 0.0)
    asum_ref[...] += jnp.sum(jnp.abs(hf[:tile]) + jnp.abs(hb[:tile]), axis=0, keepdims=True)

    rev = (lax.broadcasted_iota(I32, (tile, ext), 1) == tile - lax.broadcasted_iota(I32, (tile, ext), 0))
    rev = jnp.where(rev, 1.0, 0.0).astype(BF16)
    b1 = hb.astype(BF16)
    r1 = hb - b1.astype(F32)
    b2 = r1.astype(BF16)
    b3 = (r1 - b2.astype(F32)).astype(BF16)
    kb = _dot(rev, b1) + _dot(rev, b2) + _dot(rev, b3)

    def copies():
        return (pltpu.make_async_copy(kf_buf, k_ref.at[pl.ds(pl.multiple_of(j0, tile), tile)], sems.at[0]),
                pltpu.make_async_copy(kb_buf, k_ref.at[pl.ds(pl.multiple_of(2 * seq - j0 - tile, tile), tile)],
                                      sems.at[1]))

    @pl.when(i > 0)
    def _():
        for cp in copies():
            cp.wait()

    kf_buf[...] = jnp.where(j4[:tile] == 0, hf[:tile] + hb[:tile], hf[:tile])
    kb_buf[...] = kb
    for cp in copies():
        cp.start()

    @pl.when(i == nsteps - 1)
    def _():
        for cp in copies():
            cp.wait()


def _hyena_filter(seq, bands_row, w1p, b1, w2, b2, w3, b3, wl, freq, decay):
    c = decay.shape[1]
    tile = ROW_TILE
    ext = tile + FILTER_HALO
    full = lambda a: pl.BlockSpec(a.shape, lambda i: (0,) * a.ndim)
    args = (bands_row, w1p, b1, w2, b2, w3, b3, wl, freq, decay)
    return pl.pallas_call(
        functools.partial(_filter_kernel, seq=seq),
        grid=(seq // tile,),
        in_specs=[full(a) for a in args],
        out_specs=[pl.BlockSpec(memory_space=pl.ANY), pl.BlockSpec((1, c), lambda i: (0, 0))],
        out_shape=[jax.ShapeDtypeStruct((2 * seq, c), F32), jax.ShapeDtypeStruct((1, c), F32)],
        scratch_shapes=[pltpu.VMEM((ext, LANES), F32), pltpu.VMEM((ext, LANES), F32),
                        pltpu.VMEM((tile, c), F32), pltpu.VMEM((tile, c), F32),
                        pltpu.SemaphoreType.DMA((2,))],
        compiler_params=_params(1, "arbitrary"),
        name="hyena_filter",
    )(*args)


def _pack_pair(re, im):
    rb = lax.bitcast_convert_type(re.astype(BF16).astype(F32), U32)
    ib = lax.bitcast_convert_type(im.astype(BF16).astype(F32), U32)
    return (rb >> 16) | ib


def _unpack_pair(word):
    re = lax.bitcast_convert_type(word << 16, F32)
    im = lax.bitcast_convert_type(word & jnp.uint32(0xFFFF0000), F32)
    return re, im


def _major_compute(g, xbuf, obuf, slot):
    _, n_ct, bsz, k, sub, lanes = xbuf.shape
    m = obuf.shape[3]
    x2 = xbuf.reshape(2 * n_ct * bsz * k * sub, lanes)
    o2 = obuf.reshape(2 * n_ct * bsz * m * sub, lanes)
    for t in range(n_ct * bsz):
        xbase = (slot * n_ct * bsz + t) * k * sub
        obase = (slot * n_ct * bsz + t) * m * sub
        for b in range(sub):
            xb = x2[pl.ds(xbase + b, k, stride=sub), :]
            if xb.dtype == U32:
                xb = jnp.concatenate(_unpack_pair(xb), axis=0)
            res = _dot(g, xb.astype(BF16))
            if obuf.dtype == U32:
                res = _pack_pair(res[:m], res[m:])
            o2[pl.ds(obase + b, m, stride=sub), :] = res


def _major_kernel(g_ref, x_hbm, o_hbm, xbuf, obuf, isem, osem, *, nj):
    s = pl.program_id(0)
    ns = pl.num_programs(0)
    slot = s % 2
    _, n_ct, bsz, k, sub, lanes = xbuf.shape
    m = obuf.shape[3]

    def copies(step, sl, hbm, buf, sem, inbound):
        i = step // nj
        j = step % nj
        out = []
        for ct in range(n_ct):
            tile = hbm.at[pl.ds(i * bsz, bsz), :, pl.ds(pl.multiple_of(j * sub, sub), sub),
                          pl.ds(ct * lanes, lanes)]
            out.append(pltpu.make_async_copy(tile, buf.at[sl, ct], sem.at[sl]) if inbound
                       else pltpu.make_async_copy(buf.at[sl, ct], tile, sem.at[sl]))
        return out

    def start(cps):
        for cp in cps:
            cp.start()

    def wait(cps):
        for cp in cps:
            cp.wait()

    @pl.when(s == 0)
    def _():
        start(copies(0, 0, x_hbm, xbuf, isem, True))

    @pl.when(s + 1 < ns)
    def _():
        start(copies(s + 1, 1 - slot, x_hbm, xbuf, isem, True))

    wait(copies(s, slot, x_hbm, xbuf, isem, True))

    @pl.when(s >= 2)
    def _():
        wait(copies(s - 2, slot, o_hbm, obuf, osem, False))

    _major_compute(g_ref[...], xbuf, obuf, slot)
    start(copies(s, slot, o_hbm, obuf, osem, False))

    @pl.when(s == ns - 1)
    def _():
        wait(copies(s, slot, o_hbm, obuf, osem, False))

        @pl.when(s >= 1)
        def _():
            wait(copies(s - 1, 1 - slot, o_hbm, obuf, osem, False))


def _dft_major(g_bf, x4, pack_out):
    b, k, n2, c = x4.shape
    m = g_bf.shape[0] // 2 if pack_out else g_bf.shape[0]
    out_dtype = U32 if pack_out else F32
    sub = 8
    n_ct = c // LANES
    bsz = max(1, min(b, MAJOR_BLOCK_BYTES // ((k + m) * sub * c * 4)))
    while b % bsz:
        bsz -= 1
    nj = n2 // sub
    return pl.pallas_call(
        functools.partial(_major_kernel, nj=nj),
        grid=((b // bsz) * nj,),
        in_specs=[pl.BlockSpec(g_bf.shape, lambda s: (0, 0)), pl.BlockSpec(memory_space=pl.ANY)],
        out_specs=pl.BlockSpec(memory_space=pl.ANY),
        out_shape=jax.ShapeDtypeStruct((b, m, n2, c), out_dtype),
        scratch_shapes=[pltpu.VMEM((2, n_ct, bsz, k, sub, LANES), x4.dtype),
                        pltpu.VMEM((2, n_ct, bsz, m, sub, LANES), out_dtype),
                        pltpu.SemaphoreType.DMA((2,)), pltpu.SemaphoreType.DMA((2,))],
        compiler_params=_params(1, "arbitrary"),
        name="dft_major",
    )(g_bf, x4)


def _minor_kernel(*refs, mode, k1b):
    if mode == "conv":
        y_ref, twb_ref, twr_ref, wf_ref, wi_ref, kf_ref, o_ref, ks_ref = refs
    else:
        y_ref, twb_ref, twr_ref, wf_ref, o_ref = refs
    n2 = y_ref.shape[2]
    ct = y_ref.shape[3]
    rep = ct // LANES

    def forward(packed, cs, sn, w):
        re, im = _unpack_pair(packed)
        return _dot(w, jnp.concatenate([re * cs + im * sn, im * cs - re * sn], axis=0).astype(BF16))

    cb, sb = twb_ref[0, 0], twb_ref[1, 0]
    for j in range(k1b):
        cr, sr = twr_ref[0, j], twr_ref[1, j]
        cs = jnp.concatenate([cb * cr - sb * sr] * rep, axis=1)
        sn = jnp.concatenate([sb * cr + cb * sr] * rep, axis=1)
        if mode == "real":
            o_ref[0, :, j, :] = forward(y_ref[0, j], cs, sn, wf_ref[...])
            continue

        @pl.when(pl.program_id(1) == 0)
        def _():
            ks_ref[j] = forward(kf_ref[0, j], cs, sn, wf_ref[...])

        z = forward(y_ref[0, j], cs, sn, wf_ref[...])
        zr, zi = z[:n2], z[n2:]
        kr = ks_ref[j, :n2]
        ki = ks_ref[j, n2:]
        p = jnp.concatenate([zr * kr - zi * ki, zr * ki + zi * kr], axis=0).astype(BF16)
        q = _dot(wi_ref[...], p)
        qr, qi = q[:n2], q[n2:]
        o_ref[0, j] = _pack_pair(qr * cs - qi * sn, qi * cs + qr * sn)


def _dft_minor(y4, wf_bf, mode, wi_bf=None, kf4=None):
    b, n1, n2, c = y4.shape
    k1b = min(8, n1)
    ct = c
    tw_block, tw_rem = _twiddles(n1, n2, k1b)
    grid = (n1 // k1b, b)
    dspec = pl.BlockSpec((1, k1b, n2, ct), lambda j, i: (i, j, 0, 0))
    in_specs = [dspec,
                pl.BlockSpec((2, 1, n2, LANES), lambda j, i: (0, j, 0, 0)),
                pl.BlockSpec((2, k1b, n2, LANES), lambda j, i: (0, 0, 0, 0)),
                pl.BlockSpec(wf_bf.shape, lambda j, i: (0, 0))]
    args = [y4, tw_block, tw_rem, wf_bf]
    scratch = []
    if mode == "conv":
        in_specs += [pl.BlockSpec(wi_bf.shape, lambda j, i: (0, 0)),
                     pl.BlockSpec((1, k1b, n2, ct), lambda j, i: (0, j, 0, 0))]
        args += [wi_bf, kf4]
        scratch = [pltpu.VMEM((k1b, 2 * n2, ct), F32)]
    if mode == "real":
        out_spec = pl.BlockSpec((1, n2, k1b, ct), lambda j, i: (i, 0, j, 0))
        out_shape = jax.ShapeDtypeStruct((b, n2, n1, c), F32)
    else:
        out_spec = dspec
        out_shape = jax.ShapeDtypeStruct((b, n1, n2, c), U32)
    return pl.pallas_call(
        functools.partial(_minor_kernel, mode=mode, k1b=k1b),
        grid=grid, in_specs=in_specs, out_specs=out_spec, out_shape=out_shape,
        scratch_shapes=scratch,
        compiler_params=pltpu.CompilerParams(dimension_semantics=("parallel", "arbitrary"),
                                             vmem_limit_bytes=VMEM_LIMIT),
        name="dft_minor_" + mode,
    )(*args)


def _cs(num, den):
    ang = 2.0 * np.pi * (num % den) / den
    return np.cos(ang), np.sin(ang)


def _complex_block(c, s, sign):
    return np.block([[c, -sign * s], [sign * s, c]]).astype(np.float32)


def _major_matrix(n1, a_in, a_out, sign, real_in=False):
    c, s = _cs(np.outer(np.arange(a_out), np.arange(a_in)), n1)
    g = _complex_block(c, s, sign)
    return g[:, :a_in] if real_in else g


def _minor_matrix(n2, sign):
    c, s = _cs(np.outer(np.arange(n2), np.arange(n2)), n2)
    return _complex_block(c, s, sign)


def _twiddles(n1, n2, k1b):
    def table(k1):
        c, s = _cs(np.outer(k1, np.arange(n2)), n1 * n2)
        tw = jnp.asarray(np.stack([c, s]).astype(np.float32))
        return jnp.broadcast_to(tw[..., None], (2, len(k1), n2, LANES))
    return table(np.arange(0, n1, k1b)), table(np.arange(k1b))


def _out_kernel(yf_ref, yc_ref, x0_ref, w_ref, x_ref, lng_ref, lnb_ref, inv_ref, skip_ref, gf_ref, gh_ref,
                wo_ref, bo_ref, g1_ref, b1_ref, wr_ref, br_ref, x1_ref, idx_ref, gate_ref, cnt_ref,
                *, alpha, apply_ln, top_k):
    c = yc_ref.shape[1]
    w = w_ref[...]
    yh = x0_ref[...] * (yc_ref[...] * inv_ref[...] + skip_ref[...] * w)
    mf = _rms_norm(yf_ref[...], gf_ref[...]).astype(BF16)
    mh = _rms_norm(yh, gh_ref[...]).astype(BF16)
    m = _dot(mf, wo_ref[:c]) + _dot(mh, wo_ref[c:]) + bo_ref[...]
    xn = _layer_norm(x_ref[...], lng_ref[...], lnb_ref[...]) if apply_ln else x_ref[...]
    x1 = _layer_norm(alpha * xn + m, g1_ref[...], b1_ref[...])
    x1_ref[...] = x1
    _route_tile(x1, wr_ref[...], br_ref[...], idx_ref, gate_ref, cnt_ref, top_k)


def _out_proj(yf, yc, x0, w, x, lng, lnb, inv_norm, skip, gf, gh, wo_bf, bo, g1, b1, wr_t, br_col, alpha,
              apply_ln):
    t, d = x.shape
    c = yc.shape[1]
    e = wr_t.shape[0]
    tm = TOKEN_TILE
    nt = t // tm
    row = lambda i: (i, 0)
    const = lambda i: (0, 0)
    tile3 = lambda i: (i, 0, 0)
    return pl.pallas_call(
        functools.partial(_out_kernel, alpha=alpha, apply_ln=apply_ln, top_k=TOP_K),
        grid=(nt,),
        in_specs=[
            pl.BlockSpec((tm, c), row),
            pl.BlockSpec((tm, c), row), pl.BlockSpec((tm, c), row), pl.BlockSpec((tm, c), row),
            pl.BlockSpec((tm, d), row),
            pl.BlockSpec((1, d), const), pl.BlockSpec((1, d), const),
            pl.BlockSpec((1, c), const), pl.BlockSpec((1, c), const),
            pl.BlockSpec((1, c), const), pl.BlockSpec((1, c), const),
            pl.BlockSpec(wo_bf.shape, const), pl.BlockSpec((1, d), const),
            pl.BlockSpec((1, d), const), pl.BlockSpec((1, d), const),
            pl.BlockSpec((e, d), const), pl.BlockSpec((e, 1), const),
        ],
        out_specs=[pl.BlockSpec((tm, d), row),
                   pl.BlockSpec((1, 2 * TOP_K, tm), tile3),
                   pl.BlockSpec((1, 8, tm), tile3),
                   pl.BlockSpec((1, e, 1), tile3)],
        out_shape=[jax.ShapeDtypeStruct((t, d), F32),
                   jax.ShapeDtypeStruct((nt, 2 * TOP_K, tm), I32),
                   jax.ShapeDtypeStruct((nt, 8, tm), F32),
                   jax.ShapeDtypeStruct((nt, e, 1), I32)],
        compiler_params=_params(1),
        name="out_proj",
    )(yf, yc, x0, w, x, lng, lnb, inv_norm, skip, gf, gh, wo_bf, bo, g1, b1, wr_t, br_col)


def _route_tile(x1, wr, br, idx_ref, gate_ref, cnt_ref, top_k):
    tt = x1.shape[0]
    e = wr.shape[0]
    nt = (((1,), (1,)), ((), ()))
    xh, xl = _split(x1)
    wh, wl = _split(wr)
    logits = (lax.dot_general(wh, xh, nt, preferred_element_type=F32)
              + lax.dot_general(wh, xl, nt, preferred_element_type=F32)
              + lax.dot_general(wl, xh, nt, preferred_element_type=F32)) + br
    iota_e = lax.broadcasted_iota(I32, (e, tt), 0).astype(F32)
    l = logits
    tops, idxs, hots = [], [], []
    for _ in range(top_k):
        m = jnp.max(l, axis=0, keepdims=True)
        idx = jnp.min(jnp.where(l == m, iota_e, float(e)), axis=0, keepdims=True)
        hot = iota_e == idx
        l = jnp.where(hot, -jnp.inf, l)
        tops.append(m)
        idxs.append(idx)
        hots.append(hot)
    ex = [jnp.exp(m - tops[0]) for m in tops]
    den = ex[0]
    for v in ex[1:]:
        den = den + v
    gates = [v / den for v in ex]
    hot_all = jnp.zeros((e, tt), F32)
    for hot in hots:
        hot_all = jnp.where(hot, 1.0, hot_all)
    upper = (lax.broadcasted_iota(I32, (tt, tt), 0) < lax.broadcasted_iota(I32, (tt, tt), 1))
    before = _dot(hot_all.astype(BF16), jnp.where(upper, 1.0, 0.0).astype(BF16))
    ranks = [jnp.sum(jnp.where(hot, before, 0.0), axis=0, keepdims=True).astype(I32) for hot in hots]
    pad = 8 - top_k
    idx_ref[0] = jnp.concatenate([v.astype(I32) for v in idxs] + ranks, axis=0)
    gate_ref[0] = jnp.concatenate(gates + [jnp.zeros((pad, tt), F32)], axis=0)
    cnt_ref[0] = jnp.sum(hot_all, axis=1, keepdims=True).astype(I32)


def _dual_specs(shape, nt0):
    first = pl.BlockSpec(shape, lambda i, *_: (jnp.minimum(i, nt0 - 1),) + (0,) * (len(shape) - 1))
    second = pl.BlockSpec(shape, lambda i, *_: (jnp.maximum(i - nt0, 0),) + (0,) * (len(shape) - 1))
    return first, second


SEG_PIECES = 3
SEG_CHUNK = SEG_ALIGN << SEG_PIECES


def _segment_copies(first, count, loff_s, len_s, goff_s, local, remote, sem, to_remote, wait):
    def copy(lstart, gstart, size):
        lref = local.at[pl.ds(pl.multiple_of(lstart, SEG_ALIGN), size)]
        rref = remote.at[pl.ds(pl.multiple_of(gstart, SEG_ALIGN), size)]
        cp = (pltpu.make_async_copy(lref, rref, sem) if to_remote
              else pltpu.make_async_copy(rref, lref, sem))
        if wait:
            cp.wait()
        else:
            cp.start()

    def body(e, carry):
        s = first + e
        n = len_s[s] // SEG_ALIGN
        lo = 0 if loff_s is None else loff_s[s]
        go = goff_s[s]
        chunks = n >> SEG_PIECES

        def chunk(i, c):
            copy(lo + i * SEG_CHUNK, go + i * SEG_CHUNK, SEG_CHUNK)
            return c

        lax.fori_loop(0, chunks, chunk, 0)
        rem = n & ((1 << SEG_PIECES) - 1)
        base = chunks * SEG_CHUNK
        for p in reversed(range(SEG_PIECES)):
            start = base + ((rem >> (p + 1)) << (p + 1)) * SEG_ALIGN

            @pl.when(((rem >> p) & 1) == 1)
            def _():
                copy(lo + start, go + start, SEG_ALIGN << p)
        return carry

    lax.fori_loop(0, count, body, 0)


def _wait_tile(step, n_experts, loff_s, len_s, local, remote, sem, to_remote):
    last = step * n_experts + n_experts - 1
    n = (loff_s[last] + len_s[last]) // SEG_ALIGN
    for p in range((local.shape[0] // SEG_ALIGN).bit_length()):
        size = SEG_ALIGN << p

        @pl.when(((n >> p) & 1) == 1)
        def _():
            lref = local.at[pl.ds(0, size)]
            rref = remote.at[pl.ds(0, size)]
            cp = (pltpu.make_async_copy(lref, rref, sem) if to_remote
                  else pltpu.make_async_copy(rref, lref, sem))
            cp.wait()


def _dispatch_kernel(loff_s, seg_s, goff_s, plen_s, poff_s, nb_s, xa_ref, xb_ref, idx_ref, xs_ref, pos_ref,
                     buf, sems, *, n_experts, top_k, nt0):
    j = pl.program_id(0)
    nt = pl.num_programs(0)
    slot = j % 2
    tt = xa_ref.shape[0]
    lr = buf.shape[1]
    idx = idx_ref[0]
    tope = idx[:top_k]
    base = jnp.zeros_like(tope)
    for e in range(n_experts):
        base = jnp.where(tope == e, loff_s[j * n_experts + e], base)
    pos = base + idx[top_k:]
    pos_ref[0] = jnp.concatenate([pos, jnp.zeros((8 - top_k, tt), I32)], axis=0)
    rows = lax.broadcasted_iota(jnp.int16, (lr, tt), 0)
    pos16 = pos.astype(jnp.int16)
    perm = jnp.zeros((lr, tt), BF16)
    for k in range(top_k):
        perm = jnp.where(rows == pos16[k:k + 1], jnp.ones((lr, tt), BF16), perm)
    x = jnp.where(j < nt0, xa_ref[...], xb_ref[...])
    buf[slot] = _dot(perm, x.astype(BF16)).astype(BF16)

    def drained(step, s):
        _wait_tile(step, n_experts, loff_s, seg_s, buf.at[s], xs_ref, sems.at[s], True)

    _segment_copies(j * n_experts, n_experts, loff_s, seg_s, goff_s, buf.at[slot], xs_ref, sems.at[slot],
                    True, False)

    @pl.when(j > 0)
    def _():
        drained(j - 1, 1 - slot)

    @pl.when(j == nt - 1)
    def _():
        drained(j, slot)
        zrows = EXPERT_BLOCK
        buf[0, :zrows] = jnp.zeros((zrows, buf.shape[2]), BF16)
        _segment_copies(0, n_experts, None, plen_s, poff_s, buf.at[0], xs_ref, sems.at[0], True, False)
        _segment_copies(0, n_experts, None, plen_s, poff_s, buf.at[0], xs_ref, sems.at[0], True, True)
        n_tail = xs_ref.shape[0] // zrows - nb_s[0]

        def tail_copy(i):
            dst = xs_ref.at[pl.ds(pl.multiple_of((nb_s[0] + i) * zrows, zrows), zrows)]
            return pltpu.make_async_copy(buf.at[0, pl.ds(0, zrows)], dst, sems.at[0])

        lax.fori_loop(0, n_tail, lambda i, c: (tail_copy(i).start(), c)[1], 0)
        lax.fori_loop(0, n_tail, lambda i, c: (tail_copy(i).wait(), c)[1], 0)


def _dispatch(x1s, idx, loff, seg, goff, plen, poff, n_blocks, n_rows, local_rows):
    d = x1s[0].shape[1]
    tt = TOKEN_TILE
    nt0 = x1s[0].shape[0] // tt
    nt = nt0 + x1s[1].shape[0] // tt
    e = loff.shape[0] // nt
    xa, xb = _dual_specs((tt, d), nt0)
    grid_spec = pltpu.PrefetchScalarGridSpec(
        num_scalar_prefetch=6,
        grid=(nt,),
        in_specs=[xa, xb, pl.BlockSpec((1, 2 * TOP_K, tt), lambda i, *_: (i, 0, 0))],
        out_specs=[pl.BlockSpec(memory_space=pl.ANY),
                   pl.BlockSpec((1, 8, tt), lambda i, *_: (i, 0, 0))],
        scratch_shapes=[pltpu.VMEM((2, local_rows, d), BF16), pltpu.SemaphoreType.DMA((2,))],
    )
    return pl.pallas_call(
        functools.partial(_dispatch_kernel, n_experts=e, top_k=TOP_K, nt0=nt0),
        grid_spec=grid_spec,
        out_shape=[jax.ShapeDtypeStruct((n_rows, d), BF16), jax.ShapeDtypeStruct((nt, 8, tt), I32)],
        compiler_params=_params(1, "arbitrary"),
        name="moe_dispatch",
    )(loff, seg, goff, plen, poff, n_blocks, x1s[0], x1s[1], idx)


def _combine_kernel(loff_s, seg_s, goff_s, ys_ref, pos_ref, gate_ref, xa_ref, xb_ref, g2_ref, b2_ref,
                    oa_ref, ob_ref, buf, sems, *, n_experts, top_k, alpha, nt0):
    j = pl.program_id(0)
    nt = pl.num_programs(0)
    slot = j % 2
    tt = xa_ref.shape[0]
    lr = buf.shape[1]

    def prefetch(step, s):
        buf[s] = jnp.zeros((lr, buf.shape[2]), BF16)
        _segment_copies(step * n_experts, n_experts, loff_s, seg_s, goff_s, buf.at[s], ys_ref, sems.at[s],
                        False, False)

    @pl.when(j == 0)
    def _():
        prefetch(0, 0)

    @pl.when(j + 1 < nt)
    def _():
        prefetch(j + 1, 1 - slot)

    _wait_tile(j, n_experts, loff_s, seg_s, buf.at[slot], ys_ref, sems.at[slot], False)
    pos = pos_ref[0].astype(jnp.int16)
    gate = gate_ref[0].astype(BF16)
    rows = lax.broadcasted_iota(jnp.int16, (lr, tt), 0)
    wsel = jnp.zeros((lr, tt), BF16)
    for k in range(top_k):
        wsel = jnp.where(rows == pos[k:k + 1], gate[k:k + 1], wsel)
    moe = lax.dot_general(wsel, buf[slot], (((0,), (0,)), ((), ())), preferred_element_type=F32)
    x1 = jnp.where(j < nt0, xa_ref[...], xb_ref[...])
    out = _layer_norm(alpha * x1 + moe, g2_ref[...], b2_ref[...])

    @pl.when(j < nt0)
    def _():
        oa_ref[...] = out

    @pl.when(j >= nt0)
    def _():
        ob_ref[...] = out


def _combine(ys, pos_tm, gate_tm, x1s, g2, b2, loff, seg, goff, local_rows, alpha):
    d = x1s[0].shape[1]
    tt = TOKEN_TILE
    nt0 = x1s[0].shape[0] // tt
    nt = nt0 + x1s[1].shape[0] // tt
    e = loff.shape[0] // nt
    xa, xb = _dual_specs((tt, d), nt0)
    grid_spec = pltpu.PrefetchScalarGridSpec(
        num_scalar_prefetch=3,
        grid=(nt,),
        in_specs=[pl.BlockSpec(memory_space=pl.ANY),
                  pl.BlockSpec((1, 8, tt), lambda i, *_: (i, 0, 0)),
                  pl.BlockSpec((1, 8, tt), lambda i, *_: (i, 0, 0)),
                  xa, xb,
                  pl.BlockSpec((1, d), lambda i, *_: (0, 0)),
                  pl.BlockSpec((1, d), lambda i, *_: (0, 0))],
        out_specs=list(_dual_specs((tt, d), nt0)),
        scratch_shapes=[pltpu.VMEM((2, local_rows, d), BF16), pltpu.SemaphoreType.DMA((2,))],
    )
    return pl.pallas_call(
        functools.partial(_combine_kernel, n_experts=e, top_k=TOP_K, alpha=alpha, nt0=nt0),
        grid_spec=grid_spec,
        out_shape=[jax.ShapeDtypeStruct(x.shape, F32) for x in x1s],
        compiler_params=_params(1, "arbitrary"),
        name="moe_combine",
    )(loff, seg, goff, ys, pos_tm, gate_tm, x1s[0], x1s[1], g2, b2)


def _expert_kernel(be_s, nb_s, xs_ref, wgu_ref, bgu_ref, wd_ref, bd_ref, ys_ref, wgu_bf, wd_bf):
    b = pl.program_id(0)
    active = b < nb_s[0]
    fresh = jnp.logical_or(b == 0, be_s[b] != be_s[jnp.maximum(b - 1, 0)])

    @pl.when(jnp.logical_and(active, fresh))
    def _():
        wgu_bf[...] = wgu_ref[0].astype(BF16)
        wd_bf[...] = wd_ref[0].astype(BF16)

    @pl.when(active)
    def _():
        dff = wd_ref.shape[1]
        gu = _dot(xs_ref[...], wgu_bf[...]) + bgu_ref[0]
        gate = jnp.minimum(gu[:, :dff], SWIGLU_LIMIT)
        up = jnp.clip(gu[:, dff:], -SWIGLU_LIMIT, SWIGLU_LIMIT)
        h = (up + 1.0) * (gate * jax.nn.sigmoid(SWIGLU_ALPHA * gate))
        ys_ref[...] = (_dot(h.astype(BF16), wd_bf[...]) + bd_ref[0]).astype(BF16)

    @pl.when(jnp.logical_not(active))
    def _():
        ys_ref[...] = jnp.zeros_like(ys_ref)


def _experts(xs, block_expert, n_blocks, wgu, bgu, wd, bdn):
    n_rows, d = xs.shape
    bm = EXPERT_BLOCK
    f2 = wgu.shape[2]
    dff = wd.shape[1]
    blk = lambda b, be, nb: (jnp.minimum(b, nb[0] - 1), 0)
    wsel = lambda b, be, nb: (be[b], 0, 0)
    grid_spec = pltpu.PrefetchScalarGridSpec(
        num_scalar_prefetch=2,
        grid=(n_rows // bm,),
        in_specs=[pl.BlockSpec((bm, d), blk),
                  pl.BlockSpec((1, d, f2), wsel), pl.BlockSpec((1, 1, f2), wsel),
                  pl.BlockSpec((1, dff, d), wsel), pl.BlockSpec((1, 1, d), wsel)],
        out_specs=pl.BlockSpec((bm, d), lambda b, be, nb: (b, 0)),
        scratch_shapes=[pltpu.VMEM((d, f2), BF16), pltpu.VMEM((dff, d), BF16)],
    )
    return pl.pallas_call(
        _expert_kernel,
        grid_spec=grid_spec,
        out_shape=jax.ShapeDtypeStruct((n_rows, d), BF16),
        compiler_params=_params(1, "arbitrary"),
        name="moe_experts",
    )(block_expert, n_blocks, xs, wgu, bgu, wd, bdn)


def _routed_moe(routed, w_gate_up, b_gate_up, w_down, b_down, g2, b2, alpha):
    x1s = [r[0] for r in routed]
    idx, gate, cnt = (jnp.concatenate([r[i] for r in routed], axis=0) for i in (1, 2, 3))
    t = x1s[0].shape[0] + x1s[1].shape[0]
    e = cnt.shape[1]
    tt = TOKEN_TILE
    nt = t // tt
    bm = EXPERT_BLOCK

    cnt = cnt.reshape(nt, e)
    seg = (cnt + SEG_ALIGN - 1) // SEG_ALIGN * SEG_ALIGN
    loff = jnp.cumsum(seg, axis=1) - seg
    per_expert = jnp.sum(seg, axis=0)
    padded = (per_expert + bm - 1) // bm * bm
    blocks_end = jnp.cumsum(padded) // bm
    start = jnp.cumsum(padded) - padded
    goff = start[None, :] + jnp.cumsum(seg, axis=0) - seg
    local_rows = TOP_K * tt + e * SEG_ALIGN
    n_rows_max = -(-(TOP_K * t + (SEG_ALIGN - 1) * e * nt) // bm) * bm + e * bm
    n_blocks = blocks_end[-1:].astype(I32)
    bidx = jnp.minimum(jnp.arange(n_rows_max // bm, dtype=I32), n_blocks[0] - 1)
    block_expert = jnp.minimum(jnp.sum(bidx[:, None] >= blocks_end[None, :], axis=1), e - 1).astype(I32)
    flat = lambda a: a.reshape(-1).astype(I32)
    plen, poff = flat(padded - per_expert), flat(start + per_expert)

    xs, pos = _dispatch(x1s, idx, flat(loff), flat(seg), flat(goff), plen, poff, n_blocks, n_rows_max,
                        local_rows)
    ys = _experts(xs, block_expert, n_blocks, w_gate_up, b_gate_up.reshape(e, 1, -1),
                  w_down, b_down.reshape(e, 1, -1))
    return _combine(ys, pos, gate, x1s, g2, b2, flat(loff), flat(seg), flat(goff), local_rows, alpha)


def _mixer(x, apply_ln, lng, lnb, lay, consts, alpha):
    batch, seq, d = x.shape
    t = batch * seq
    fw = lay["g_fourier"].shape[1]
    n2 = MINOR
    x2d = x.reshape(t, d)
    z, x0, w = _in_proj(x2d, batch, seq, fw, lng, lnb, lay["w_in"], lay["b_in"], consts["bd"],
                        lay["w_short"], lay["b_short"], apply_ln)
    c = x0.shape[1]

    n1 = seq // n2
    g = jnp.asarray(_major_matrix(n1, n1, n1, -1)).astype(BF16)
    y = _dft_major(g, z.reshape(batch, n1, n2, fw), True)
    ortho = 1.0 / math.sqrt(seq * FOURIER_GROUP_DIM)
    wf_real = jnp.asarray(_minor_matrix(n2, -1)[:n2] * ortho).astype(BF16)
    yf = _dft_minor(y, wf_real, "real")

    n = 2 * seq
    m1 = n // n2
    kf, asum = _hyena_filter(seq, consts["bands"], lay["filt_w1"], lay["filt_b1"], lay["filt_w2"],
                             lay["filt_b2"], lay["filt_w3"], lay["filt_b3"], lay["filt_w_out"],
                             lay["filt_freq"], lay["filt_decay"])
    gk = jnp.asarray(_major_matrix(m1, m1, m1, -1, real_in=True)).astype(BF16)
    ky = _dft_major(gk, kf.reshape(1, m1, n2, c), True)
    gfw = jnp.asarray(_major_matrix(m1, m1 // 2, m1, -1)).astype(BF16)
    gin = jnp.asarray(_major_matrix(m1, m1, m1 // 2, +1)).astype(BF16)
    pairs = batch // 2
    u = _dft_major(gfw, w.reshape(pairs, m1, n2, c), True)
    wi = jnp.asarray(_minor_matrix(n2, +1) * (1.0 / n)).astype(BF16)
    v = _dft_minor(u, consts["wf"], "conv", wi_bf=wi, kf4=ky)
    yc = _dft_major(gin, v, False)

    inv_norm = 1.0 / asum
    return _out_proj(yf.reshape(t, fw), yc.reshape(t, c), x0, w, x2d, lng, lnb, inv_norm,
                     lay["hyena_skip"], lay["g_fourier"], lay["g_hyena"], lay["w_out"], lay["b_out"],
                     lay["ln1_g"], lay["ln1_b"], lay["w_router_t"], lay["b_router"], alpha, apply_ln)


def kernel(x_prompt, x_sample, ln_in_g, ln_in_b, w_in, b_in, w_short, b_short, filt_w1, filt_b1, filt_w2, filt_b2, filt_w3, filt_b3, filt_w_out, filt_freq, filt_decay, hyena_skip, g_fourier, g_hyena, w_out, b_out, ln1_g, ln1_b, w_router, b_router, w_gate_up, b_gate_up, w_down, b_down, ln2_g, ln2_b):
    depth, d, in_width = w_in.shape
    fw = g_fourier.shape[1]
    alpha = (2.0 * depth) ** 0.25
    row = lambda a: a.reshape(1, -1)

    gd = FOURIER_GROUP_DIM
    cg, sg = _cs(np.outer(np.arange(gd), np.arange(gd)), gd)
    eye = np.eye(LANES // gd)
    bd = np.concatenate([np.kron(eye, cg), -np.kron(eye, sg)], axis=1).astype(np.float32)
    bands = jnp.linspace(1e-4, FILTER_BANDS - 1, FILTER_BANDS, dtype=F32)
    bands_row = jnp.zeros((1, LANES), F32).at[0, 1:1 + FILTER_BANDS].set(bands)
    bands_row = bands_row.at[0, 1 + FILTER_BANDS:1 + 2 * FILTER_BANDS].set(bands)
    consts = dict(bd=jnp.asarray(bd).astype(BF16), bands=bands_row,
                  wf=jnp.asarray(_minor_matrix(MINOR, -1)).astype(BF16))

    xs = [x_prompt, x_sample]
    for l in range(depth):
        emb = filt_w1.shape[1]
        twice = lambda a: jnp.concatenate([row(a), row(a)], axis=1)
        blockdiag = lambda a: jnp.kron(jnp.eye(2, dtype=F32), a)
        w1_pad = jnp.zeros((LANES, filt_w1.shape[2]), F32).at[:emb].set(filt_w1[l])
        wl = filt_w_out[l]
        wl_halves = jnp.stack([jnp.concatenate([wl, jnp.zeros_like(wl)], axis=0),
                               jnp.concatenate([jnp.zeros_like(wl), wl], axis=0)])
        lay = dict(
            w_in=w_in[l].astype(BF16), b_in=row(b_in[l]), w_short=w_short[l], b_short=row(b_short[l]),
            filt_w1=blockdiag(w1_pad),
            filt_b1=twice(filt_b1[l]), filt_w2=blockdiag(filt_w2[l]), filt_b2=twice(filt_b2[l]),
            filt_w3=blockdiag(filt_w3[l]),
            filt_b3=twice(filt_b3[l]), filt_w_out=wl_halves, filt_freq=twice(filt_freq[l]),
            filt_decay=filt_decay[l], hyena_skip=row(hyena_skip[l]), g_fourier=row(g_fourier[l]),
            g_hyena=row(g_hyena[l]), w_out=w_out[l].astype(BF16), b_out=row(b_out[l]),
            ln1_g=row(ln1_g[l]), ln1_b=row(ln1_b[l]),
            w_router_t=w_router[l].T, b_router=b_router[l].reshape(-1, 1))
        routed = [_mixer(x, l == 0, row(ln_in_g), row(ln_in_b), lay, consts, alpha) for x in xs]
        outs = _routed_moe(routed, w_gate_up[l], b_gate_up[l], w_down[l], b_down[l],
                           row(ln2_g[l]), row(ln2_b[l]), alpha)
        xs = [o.reshape(x.shape) for o, x in zip(outs, xs)]
    return (xs[0], xs[1])
```

```python
import functools
import math

import numpy as np
import jax
import jax.numpy as jnp
from jax import lax
from jax.experimental import pallas as pl
from jax.experimental.pallas import tpu as pltpu

F32 = jnp.float32
BF16 = jnp.bfloat16
I32 = jnp.int32
U32 = jnp.uint32

LN_EPS = 1e-5
RMS_EPS = 1e-6
SWIGLU_LIMIT = 7.0
SWIGLU_ALPHA = 1.702
TOP_K = 4
FOURIER_GROUP_DIM = 64
FILTER_BANDS = 16

LANES = 128
MINOR = 128
ROW_TILE = 512
IN_TILE = 512
IN_SPLIT = 2
MAJOR_BLOCK_BYTES = 12 * 1024 * 1024
TOKEN_TILE = 512
SEG_ALIGN = 16
EXPERT_BLOCK = 512
VMEM_LIMIT = 56 * 1024 * 1024


def _params(n_grid, semantics="parallel"):
    return pltpu.CompilerParams(dimension_semantics=(semantics,) * n_grid,
                                vmem_limit_bytes=VMEM_LIMIT)


def _dot(a, b):
    return jnp.dot(a, b, preferred_element_type=F32)


def _split(a):
    hi = a.astype(BF16)
    lo = (a - hi.astype(F32)).astype(BF16)
    return hi, lo


def _dot3(a, b):
    ah, al = _split(a)
    bh, bl = _split(b)
    return _dot(ah, bh) + _dot(ah, bl) + _dot(al, bh)


def _layer_norm(x, g, b):
    mu = jnp.mean(x, axis=-1, keepdims=True)
    xc = x - mu
    var = jnp.mean(xc * xc, axis=-1, keepdims=True)
    return xc * lax.rsqrt(var + LN_EPS) * g + b


def _rms_norm(x, g):
    return x * lax.rsqrt(jnp.mean(x * x, axis=-1, keepdims=True) + RMS_EPS) * g


def _in_kernel(x_ref, xp_ref, xq_ref, lng_ref, lnb_ref, win_ref, bin_ref, bd_ref, wsh_ref, bsh_ref,
               z_ref, x0_ref, w_ref, *, tiles_per_seq, apply_ln, fw):
    it = pl.program_id(0) % tiles_per_seq
    tm = x_ref.shape[0]

    def norm(x):
        return _layer_norm(x, lng_ref[...], lnb_ref[...]) if apply_ln else x

    halo = norm(jnp.concatenate([xp_ref[...], xq_ref[...]], axis=0))
    projh = _dot(halo.astype(BF16), win_ref[:, fw:]) + bin_ref[:, fw:]
    rp = tm // IN_SPLIT
    projs = [_dot(norm(x_ref[r * rp:(r + 1) * rp, :]).astype(BF16), win_ref[...]) + bin_ref[...]
             for r in range(IN_SPLIT)]
    wsh = wsh_ref[...]
    for r, proj in enumerate(projs):
        sl = slice(r * rp, (r + 1) * rp)
        pf = proj[:, :fw].astype(BF16)
        zs = [_dot(pf[:, c0:c0 + LANES], bd_ref[...]) for c0 in range(0, fw, LANES)]
        z_ref[0, sl, :] = _pack_pair(jnp.concatenate([z[:, :LANES] for z in zs], axis=1),
                                     jnp.concatenate([z[:, LANES:] for z in zs], axis=1))

        ph = proj[:, fw:]
        prev_row = projs[r - 1][rp - 1:rp, fw:] if r > 0 else jnp.where(it > 0, projh[7:8], 0.0)
        next_row = (projs[r + 1][0:1, fw:] if r + 1 < IN_SPLIT
                    else jnp.where(it < tiles_per_seq - 1, projh[8:9], 0.0))
        rows = lax.broadcasted_iota(I32, ph.shape, 0)
        up = jnp.where(rows == 0, prev_row, pltpu.roll(ph, 1, 0))
        dn = jnp.where(rows == rp - 1, next_row, pltpu.roll(ph, rp - 1, 0))
        uc = up * wsh[0:1] + ph * wsh[1:2] + dn * wsh[2:3] + bsh_ref[...]
        hw = uc.shape[1] // 3
        x0_ref[sl, :] = uc[:, :hw]
        w_ref[sl, :] = uc[:, 2 * hw:] * uc[:, hw:2 * hw]


def _in_proj(x2d, batch, seq, fw, lng, lnb, win_bf, b_in, bd_bf, w_short, b_short, apply_ln):
    t, d = x2d.shape
    inw = win_bf.shape[1]
    hw = (inw - fw) // 3
    tm = IN_TILE
    tps = seq // tm
    r8 = tm // 8
    nblk8 = t // 8
    const = lambda i: (0, 0)
    kern = functools.partial(_in_kernel, tiles_per_seq=tps, apply_ln=apply_ln, fw=fw)
    return pl.pallas_call(
        kern,
        grid=(t // tm,),
        in_specs=[
            pl.BlockSpec((tm, d), lambda i: (i, 0)),
            pl.BlockSpec((8, d), lambda i: (jnp.maximum(i * r8 - 1, 0), 0)),
            pl.BlockSpec((8, d), lambda i: (jnp.minimum((i + 1) * r8, nblk8 - 1), 0)),
            pl.BlockSpec((1, d), const), pl.BlockSpec((1, d), const),
            pl.BlockSpec((d, inw), const), pl.BlockSpec((1, inw), const),
            pl.BlockSpec(bd_bf.shape, const),
            pl.BlockSpec((3, 3 * hw), const), pl.BlockSpec((1, 3 * hw), const),
        ],
        out_specs=[
            pl.BlockSpec((1, tm, fw), lambda i: (i // tps, i % tps, 0)),
            pl.BlockSpec((tm, hw), lambda i: (i, 0)),
            pl.BlockSpec((tm, hw), lambda i: (i, 0)),
        ],
        out_shape=[
            jax.ShapeDtypeStruct((batch, seq, fw), U32),
            jax.ShapeDtypeStruct((t, hw), F32),
            jax.ShapeDtypeStruct((t, hw), F32),
        ],
        compiler_params=_params(1),
        name="in_proj",
    )(x2d, x2d, x2d, lng, lnb, win_bf, b_in, bd_bf, w_short, b_short)


FILTER_HALO = 16


def _filter_kernel(bands_ref, w1_ref, b1_ref, w2_ref, b2_ref, w3_ref, b3_ref, wl_ref, fr_ref, dec_ref,
                   k_ref, asum_ref, tc_ref, ts_ref, kf_buf, kb_buf, sems, *, seq):
    i = pl.program_id(0)
    nsteps = pl.num_programs(0)
    tile, c = kf_buf.shape
    ext = tc_ref.shape[0]
    rep = c // LANES
    w_unit = 2.0 * math.pi / seq
    bands = bands_ref[...]

    @pl.when(i == 0)
    def _():
        arg = bands * (w_unit * lax.broadcasted_iota(I32, (ext, LANES), 0).astype(F32))
        tc_ref[...] = jnp.cos(arg)
        ts_ref[...] = jnp.sin(arg)
        asum_ref[...] = jnp.zeros_like(asum_ref)

    j0 = i * tile
    base = bands * (w_unit * j0.astype(F32))
    c0, s0 = jnp.cos(base), jnp.sin(base)
    tc, ts = tc_ref[...], ts_ref[...]
    cosf = c0 * tc - s0 * ts
    sinf = s0 * tc + c0 * ts
    j = j0 + lax.broadcasted_iota(I32, (ext, LANES), 0)
    t = j.astype(F32) * (1.0 / (seq - 1))
    lane = lax.broadcasted_iota(I32, (ext, LANES), 1)
    feat = jnp.where(lane == 0, t,
                     jnp.where(lane <= FILTER_BANDS, cosf,
                               jnp.where(lane <= 2 * FILTER_BANDS, -sinf, 0.0)))
    half = ext // 2
    fr = fr_ref[...]
    h = jnp.concatenate([feat[:half], feat[half:]], axis=1)
    h = jnp.sin(fr * (_dot3(h, w1_ref[...]) + b1_ref[...]))
    h = jnp.sin(fr * (_dot3(h, w2_ref[...]) + b2_ref[...]))
    h = jnp.sin(fr * (_dot3(h, w3_ref[...]) + b3_ref[...]))
    hl = jnp.concatenate([_dot3(h, wl_ref[0]), _dot3(h, wl_ref[1])], axis=0)
    t4 = jnp.concatenate([t] * rep, axis=1)
    j4 = jnp.concatenate([j] * rep, axis=1)
    dec = jnp.abs(dec_ref[...])
    hf = hl[:, :c] * jnp.exp(-t4 * dec[0:1])
    hb = hl[:, c:] * jnp.exp(-t4 * dec[1:2])
    hb = jnp.where(j4 < seq, hb, 0.0)
    asum_ref[...] += jnp.sum(jnp.abs(hf[:tile]) + jnp.abs(hb[:tile]), axis=0, keepdims=True)

    rev = (lax.broadcasted_iota(I32, (tile, ext), 1) == tile - lax.broadcasted_iota(I32, (tile, ext), 0))
    rev = jnp.where(rev, 1.0, 0.0).astype(BF16)
    b1 = hb.astype(BF16)
    r1 = hb - b1.astype(F32)
    b2 = r1.astype(BF16)
    b3 = (r1 - b2.astype(F32)).astype(BF16)
    kb = _dot(rev, b1) + _dot(rev, b2) + _dot(rev, b3)

    def copies():
        return (pltpu.make_async_copy(kf_buf, k_ref.at[pl.ds(pl.multiple_of(j0, tile), tile)], sems.at[0]),
                pltpu.make_async_copy(kb_buf, k_ref.at[pl.ds(pl.multiple_of(2 * seq - j0 - tile, tile), tile)],
                                      sems.at[1]))

    @pl.when(i > 0)
    def _():
        for cp in copies():
            cp.wait()

    kf_buf[...] = jnp.where(j4[:tile] == 0, hf[:tile] + hb[:tile], hf[:tile])
    kb_buf[...] = kb
    for cp in copies():
        cp.start()

    @pl.when(i == nsteps - 1)
    def _():
        for cp in copies():
            cp.wait()


def _hyena_filter(seq, bands_row, w1p, b1, w2, b2, w3, b3, wl, freq, decay):
    c = decay.shape[1]
    tile = ROW_TILE
    ext = tile + FILTER_HALO
    full = lambda a: pl.BlockSpec(a.shape, lambda i: (0,) * a.ndim)
    args = (bands_row, w1p, b1, w2, b2, w3, b3, wl, freq, decay)
    return pl.pallas_call(
        functools.partial(_filter_kernel, seq=seq),
        grid=(seq // tile,),
        in_specs=[full(a) for a in args],
        out_specs=[pl.BlockSpec(memory_space=pl.ANY), pl.BlockSpec((1, c), lambda i: (0, 0))],
        out_shape=[jax.ShapeDtypeStruct((2 * seq, c), F32), jax.ShapeDtypeStruct((1, c), F32)],
        scratch_shapes=[pltpu.VMEM((ext, LANES), F32), pltpu.VMEM((ext, LANES), F32),
                        pltpu.VMEM((tile, c), F32), pltpu.VMEM((tile, c), F32),
                        pltpu.SemaphoreType.DMA((2,))],
        compiler_params=_params(1, "arbitrary"),
        name="hyena_filter",
    )(*args)


def _pack_pair(re, im):
    rb = lax.bitcast_convert_type(re.astype(BF16).astype(F32), U32)
    ib = lax.bitcast_convert_type(im.astype(BF16).astype(F32), U32)
    return (rb >> 16) | ib


def _unpack_pair(word):
    re = lax.bitcast_convert_type(word << 16, F32)
    im = lax.bitcast_convert_type(word & jnp.uint32(0xFFFF0000), F32)
    return re, im


def _major_compute(g, xbuf, obuf, slot):
    _, n_ct, bsz, k, sub, lanes = xbuf.shape
    m = obuf.shape[3]
    x2 = xbuf.reshape(2 * n_ct * bsz * k * sub, lanes)
    o2 = obuf.reshape(2 * n_ct * bsz * m * sub, lanes)
    for t in range(n_ct * bsz):
        xbase = (slot * n_ct * bsz + t) * k * sub
        obase = (slot * n_ct * bsz + t) * m * sub
        for b in range(sub):
            xb = x2[pl.ds(xbase + b, k, stride=sub), :]
            if xb.dtype == U32:
                xb = jnp.concatenate(_unpack_pair(xb), axis=0)
            res = _dot(g, xb.astype(BF16))
            if obuf.dtype == U32:
                res = _pack_pair(res[:m], res[m:])
            o2[pl.ds(obase + b, m, stride=sub), :] = res


def _major_kernel(g_ref, x_hbm, o_hbm, xbuf, obuf, isem, osem, *, nj):
    s = pl.program_id(0)
    ns = pl.num_programs(0)
    slot = s % 2
    _, n_ct, bsz, k, sub, lanes = xbuf.shape
    m = obuf.shape[3]

    def copies(step, sl, hbm, buf, sem, inbound):
        i = step // nj
        j = step % nj
        out = []
        for ct in range(n_ct):
            tile = hbm.at[pl.ds(i * bsz, bsz), :, pl.ds(pl.multiple_of(j * sub, sub), sub),
                          pl.ds(ct * lanes, lanes)]
            out.append(pltpu.make_async_copy(tile, buf.at[sl, ct], sem.at[sl]) if inbound
                       else pltpu.make_async_copy(buf.at[sl, ct], tile, sem.at[sl]))
        return out

    def start(cps):
        for cp in cps:
            cp.start()

    def wait(cps):
        for cp in cps:
            cp.wait()

    @pl.when(s == 0)
    def _():
        start(copies(0, 0, x_hbm, xbuf, isem, True))

    @pl.when(s + 1 < ns)
    def _():
        start(copies(s + 1, 1 - slot, x_hbm, xbuf, isem, True))

    wait(copies(s, slot, x_hbm, xbuf, isem, True))

    @pl.when(s >= 2)
    def _():
        wait(copies(s - 2, slot, o_hbm, obuf, osem, False))

    _major_compute(g_ref[...], xbuf, obuf, slot)
    start(copies(s, slot, o_hbm, obuf, osem, False))

    @pl.when(s == ns - 1)
    def _():
        wait(copies(s, slot, o_hbm, obuf, osem, False))

        @pl.when(s >= 1)
        def _():
            wait(copies(s - 1, 1 - slot, o_hbm, obuf, osem, False))


def _dft_major(g_bf, x4, pack_out):
    b, k, n2, c = x4.shape
    m = g_bf.shape[0] // 2 if pack_out else g_bf.shape[0]
    out_dtype = U32 if pack_out else F32
    sub = 8
    n_ct = c // LANES
    bsz = max(1, min(b, MAJOR_BLOCK_BYTES // ((k + m) * sub * c * 4)))
    while b % bsz:
        bsz -= 1
    nj = n2 // sub
    return pl.pallas_call(
        functools.partial(_major_kernel, nj=nj),
        grid=((b // bsz) * nj,),
        in_specs=[pl.BlockSpec(g_bf.shape, lambda s: (0, 0)), pl.BlockSpec(memory_space=pl.ANY)],
        out_specs=pl.BlockSpec(memory_space=pl.ANY),
        out_shape=jax.ShapeDtypeStruct((b, m, n2, c), out_dtype),
        scratch_shapes=[pltpu.VMEM((2, n_ct, bsz, k, sub, LANES), x4.dtype),
                        pltpu.VMEM((2, n_ct, bsz, m, sub, LANES), out_dtype),
                        pltpu.SemaphoreType.DMA((2,)), pltpu.SemaphoreType.DMA((2,))],
        compiler_params=_params(1, "arbitrary"),
        name="dft_major",
    )(g_bf, x4)


def _minor_kernel(*refs, mode, k1b):
    if mode == "conv":
        y_ref, twb_ref, twr_ref, wf_ref, wi_ref, kf_ref, o_ref, ks_ref = refs
    else:
        y_ref, twb_ref, twr_ref, wf_ref, o_ref = refs
    n2 = y_ref.shape[2]
    ct = y_ref.shape[3]
    rep = ct // LANES

    def forward(packed, cs, sn, w):
        re, im = _unpack_pair(packed)
        return _dot(w, jnp.concatenate([re * cs + im * sn, im * cs - re * sn], axis=0).astype(BF16))

    cb, sb = twb_ref[0, 0], twb_ref[1, 0]
    for j in range(k1b):
        cr, sr = twr_ref[0, j], twr_ref[1, j]
        cs = jnp.concatenate([cb * cr - sb * sr] * rep, axis=1)
        sn = jnp.concatenate([sb * cr + cb * sr] * rep, axis=1)
        if mode == "real":
            o_ref[0, :, j, :] = forward(y_ref[0, j], cs, sn, wf_ref[...])
            continue

        @pl.when(pl.program_id(1) == 0)
        def _():
            ks_ref[j] = forward(kf_ref[0, j], cs, sn, wf_ref[...])

        z = forward(y_ref[0, j], cs, sn, wf_ref[...])
        zr, zi = z[:n2], z[n2:]
        kr = ks_ref[j, :n2]
        ki = ks_ref[j, n2:]
        p = jnp.concatenate([zr * kr - zi * ki, zr * ki + zi * kr], axis=0).astype(BF16)
        q = _dot(wi_ref[...], p)
        qr, qi = q[:n2], q[n2:]
        o_ref[0, j] = _pack_pair(qr * cs - qi * sn, qi * cs + qr * sn)


def _dft_minor(y4, wf_bf, mode, wi_bf=None, kf4=None):
    b, n1, n2, c = y4.shape
    k1b = min(8, n1)
    ct = c
    tw_block, tw_rem = _twiddles(n1, n2, k1b)
    grid = (n1 // k1b, b)
    dspec = pl.BlockSpec((1, k1b, n2, ct), lambda j, i: (i, j, 0, 0))
    in_specs = [dspec,
                pl.BlockSpec((2, 1, n2, LANES), lambda j, i: (0, j, 0, 0)),
                pl.BlockSpec((2, k1b, n2, LANES), lambda j, i: (0, 0, 0, 0)),
                pl.BlockSpec(wf_bf.shape, lambda j, i: (0, 0))]
    args = [y4, tw_block, tw_rem, wf_bf]
    scratch = []
    if mode == "conv":
        in_specs += [pl.BlockSpec(wi_bf.shape, lambda j, i: (0, 0)),
                     pl.BlockSpec((1, k1b, n2, ct), lambda j, i: (0, j, 0, 0))]
        args += [wi_bf, kf4]
        scratch = [pltpu.VMEM((k1b, 2 * n2, ct), F32)]
    if mode == "real":
        out_spec = pl.BlockSpec((1, n2, k1b, ct), lambda j, i: (i, 0, j, 0))
        out_shape = jax.ShapeDtypeStruct((b, n2, n1, c), F32)
    else:
        out_spec = dspec
        out_shape = jax.ShapeDtypeStruct((b, n1, n2, c), U32)
    return pl.pallas_call(
        functools.partial(_minor_kernel, mode=mode, k1b=k1b),
        grid=grid, in_specs=in_specs, out_specs=out_spec, out_shape=out_shape,
        scratch_shapes=scratch,
        compiler_params=pltpu.CompilerParams(dimension_semantics=("parallel", "arbitrary"),
                                             vmem_limit_bytes=VMEM_LIMIT),
        name="dft_minor_" + mode,
    )(*args)


def _cs(num, den):
    ang = 2.0 * np.pi * (num % den) / den
    return np.cos(ang), np.sin(ang)


def _complex_block(c, s, sign):
    return np.block([[c, -sign * s], [sign * s, c]]).astype(np.float32)


def _major_matrix(n1, a_in, a_out, sign, real_in=False):
    c, s = _cs(np.outer(np.arange(a_out), np.arange(a_in)), n1)
    g = _complex_block(c, s, sign)
    return g[:, :a_in] if real_in else g


def _minor_matrix(n2, sign):
    c, s = _cs(np.outer(np.arange(n2), np.arange(n2)), n2)
    return _complex_block(c, s, sign)


def _twiddles(n1, n2, k1b):
    def table(k1):
        c, s = _cs(np.outer(k1, np.arange(n2)), n1 * n2)
        tw = jnp.asarray(np.stack([c, s]).astype(np.float32))
        return jnp.broadcast_to(tw[..., None], (2, len(k1), n2, LANES))
    return table(np.arange(0, n1, k1b)), table(np.arange(k1b))


def _out_kernel(yf_ref, yc_ref, x0_ref, w_ref, x_ref, lng_ref, lnb_ref, inv_ref, skip_ref, gf_ref, gh_ref,
                wo_ref, bo_ref, g1_ref, b1_ref, wr_ref, br_ref, x1_ref, idx_ref, gate_ref, cnt_ref,
                *, alpha, apply_ln, top_k):
    c = yc_ref.shape[1]
    w = w_ref[...]
    yh = x0_ref[...] * (yc_ref[...] * inv_ref[...] + skip_ref[...] * w)
    mf = _rms_norm(yf_ref[...], gf_ref[...]).astype(BF16)
    mh = _rms_norm(yh, gh_ref[...]).astype(BF16)
    m = _dot(mf, wo_ref[:c]) + _dot(mh, wo_ref[c:]) + bo_ref[...]
    xn = _layer_norm(x_ref[...], lng_ref[...], lnb_ref[...]) if apply_ln else x_ref[...]
    x1 = _layer_norm(alpha * xn + m, g1_ref[...], b1_ref[...])
    x1_ref[...] = x1
    _route_tile(x1, wr_ref[...], br_ref[...], idx_ref, gate_ref, cnt_ref, top_k)


def _out_proj(yf, yc, x0, w, x, lng, lnb, inv_norm, skip, gf, gh, wo_bf, bo, g1, b1, wr_t, br_col, alpha,
              apply_ln):
    t, d = x.shape
    c = yc.shape[1]
    e = wr_t.shape[0]
    tm = TOKEN_TILE
    nt = t // tm
    row = lambda i: (i, 0)
    const = lambda i: (0, 0)
    tile3 = lambda i: (i, 0, 0)
    return pl.pallas_call(
        functools.partial(_out_kernel, alpha=alpha, apply_ln=apply_ln, top_k=TOP_K),
        grid=(nt,),
        in_specs=[
            pl.BlockSpec((tm, c), row),
            pl.BlockSpec((tm, c), row), pl.BlockSpec((tm, c), row), pl.BlockSpec((tm, c), row),
            pl.BlockSpec((tm, d), row),
            pl.BlockSpec((1, d), const), pl.BlockSpec((1, d), const),
            pl.BlockSpec((1, c), const), pl.BlockSpec((1, c), const),
            pl.BlockSpec((1, c), const), pl.BlockSpec((1, c), const),
            pl.BlockSpec(wo_bf.shape, const), pl.BlockSpec((1, d), const),
            pl.BlockSpec((1, d), const), pl.BlockSpec((1, d), const),
            pl.BlockSpec((e, d), const), pl.BlockSpec((e, 1), const),
        ],
        out_specs=[pl.BlockSpec((tm, d), row),
                   pl.BlockSpec((1, 2 * TOP_K, tm), tile3),
                   pl.BlockSpec((1, 8, tm), tile3),
                   pl.BlockSpec((1, e, 1), tile3)],
        out_shape=[jax.ShapeDtypeStruct((t, d), F32),
                   jax.ShapeDtypeStruct((nt, 2 * TOP_K, tm), I32),
                   jax.ShapeDtypeStruct((nt, 8, tm), F32),
                   jax.ShapeDtypeStruct((nt, e, 1), I32)],
        compiler_params=_params(1),
        name="out_proj",
    )(yf, yc, x0, w, x, lng, lnb, inv_norm, skip, gf, gh, wo_bf, bo, g1, b1, wr_t, br_col)


def _route_tile(x1, wr, br, idx_ref, gate_ref, cnt_ref, top_k):
    tt = x1.shape[0]
    e = wr.shape[0]
    nt = (((1,), (1,)), ((), ()))
    xh, xl = _split(x1)
    wh, wl = _split(wr)
    logits = (lax.dot_general(wh, xh, nt, preferred_element_type=F32)
              + lax.dot_general(wh, xl, nt, preferred_element_type=F32)
              + lax.dot_general(wl, xh, nt, preferred_element_type=F32)) + br
    iota_e = lax.broadcasted_iota(I32, (e, tt), 0).astype(F32)
    l = logits
    tops, idxs, hots = [], [], []
    for _ in range(top_k):
        m = jnp.max(l, axis=0, keepdims=True)
        idx = jnp.min(jnp.where(l == m, iota_e, float(e)), axis=0, keepdims=True)
        hot = iota_e == idx
        l = jnp.where(hot, -jnp.inf, l)
        tops.append(m)
        idxs.append(idx)
        hots.append(hot)
    ex = [jnp.exp(m - tops[0]) for m in tops]
    den = ex[0]
    for v in ex[1:]:
        den = den + v
    gates = [v / den for v in ex]
    hot_all = jnp.zeros((e, tt), F32)
    for hot in hots:
        hot_all = jnp.where(hot, 1.0, hot_all)
    upper = (lax.broadcasted_iota(I32, (tt, tt), 0) < lax.broadcasted_iota(I32, (tt, tt), 1))
    before = _dot(hot_all.astype(BF16), jnp.where(upper, 1.0, 0.0).astype(BF16))
    ranks = [jnp.sum(jnp.where(hot, before, 0.0), axis=0, keepdims=True).astype(I32) for hot in hots]
    pad = 8 - top_k
    idx_ref[0] = jnp.concatenate([v.astype(I32) for v in idxs] + ranks, axis=0)
    gate_ref[0] = jnp.concatenate(gates + [jnp.zeros((pad, tt), F32)], axis=0)
    cnt_ref[0] = jnp.sum(hot_all, axis=1, keepdims=True).astype(I32)


def _dual_specs(shape, nt0):
    first = pl.BlockSpec(shape, lambda i, *_: (jnp.minimum(i, nt0 - 1),) + (0,) * (len(shape) - 1))
    second = pl.BlockSpec(shape, lambda i, *_: (jnp.maximum(i - nt0, 0),) + (0,) * (len(shape) - 1))
    return first, second


SEG_PIECES = 3
SEG_CHUNK = SEG_ALIGN << SEG_PIECES
PIECE_CLASSES = SEG_PIECES + 1
PIECE_CAP = 32


def _segment_copies(first, count, loff_s, len_s, goff_s, local, remote, sem, to_remote, wait):
    def copy(lstart, gstart, size):
        lref = local.at[pl.ds(pl.multiple_of(lstart, SEG_ALIGN), size)]
        rref = remote.at[pl.ds(pl.multiple_of(gstart, SEG_ALIGN), size)]
        cp = (pltpu.make_async_copy(lref, rref, sem) if to_remote
              else pltpu.make_async_copy(rref, lref, sem))
        if wait:
            cp.wait()
        else:
            cp.start()

    def body(e, carry):
        s = first + e
        n = len_s[s] // SEG_ALIGN
        lo = 0 if loff_s is None else loff_s[s]
        go = goff_s[s]
        chunks = n >> SEG_PIECES

        def chunk(i, c):
            copy(lo + i * SEG_CHUNK, go + i * SEG_CHUNK, SEG_CHUNK)
            return c

        lax.fori_loop(0, chunks, chunk, 0)
        rem = n & ((1 << SEG_PIECES) - 1)
        base = chunks * SEG_CHUNK
        for p in reversed(range(SEG_PIECES)):
            start = base + ((rem >> (p + 1)) << (p + 1)) * SEG_ALIGN

            @pl.when(((rem >> p) & 1) == 1)
            def _():
                copy(lo + start, go + start, SEG_ALIGN << p)
        return carry

    lax.fori_loop(0, count, body, 0)


def _piece_copies(step, lo_t, go_t, cnt_t, local, remote, sem, to_remote):
    for c in range(PIECE_CLASSES):
        size = SEG_ALIGN << c
        base = (step * PIECE_CLASSES + c) * PIECE_CAP

        def body(i, carry, size=size, base=base):
            lref = local.at[pl.ds(pl.multiple_of(lo_t[base + i], SEG_ALIGN), size)]
            rref = remote.at[pl.ds(pl.multiple_of(go_t[base + i], SEG_ALIGN), size)]
            cp = (pltpu.make_async_copy(lref, rref, sem) if to_remote
                  else pltpu.make_async_copy(rref, lref, sem))
            cp.start()
            return carry

        lax.fori_loop(0, cnt_t[step * PIECE_CLASSES + c], body, 0)


def _piece_tables(seg, loff, goff):
    nt, e = seg.shape
    n = seg // SEG_ALIGN
    chunks = n >> SEG_PIECES
    rem = n & ((1 << SEG_PIECES) - 1)
    base = chunks * SEG_CHUNK

    def compact(valid, off, lo, go):
        slots = valid.shape[1]
        key = jnp.where(valid, jnp.arange(slots, dtype=I32)[None, :], slots)
        order = jnp.argsort(key, axis=1)[:, :PIECE_CAP]
        pick = lambda a: jnp.take_along_axis(a + off, order, axis=1)
        return pick(lo), pick(go), jnp.sum(valid, axis=1)

    tables = []
    for c in range(SEG_PIECES):
        off = base + ((rem >> (c + 1)) << (c + 1)) * SEG_ALIGN
        tables.append(compact(((rem >> c) & 1) == 1, off, loff, goff))
    max_chunks = TOKEN_TILE // SEG_CHUNK
    i = jnp.arange(max_chunks, dtype=I32)[None, None, :]
    rep = lambda a: jnp.broadcast_to(a[:, :, None], (nt, e, max_chunks)).reshape(nt, e * max_chunks)
    tables.append(compact((i < chunks[:, :, None]).reshape(nt, e * max_chunks),
                          jnp.broadcast_to(i * SEG_CHUNK, (nt, e, max_chunks)).reshape(nt, e * max_chunks),
                          rep(loff), rep(goff)))
    flat = lambda k: jnp.stack([t[k] for t in tables], axis=1).reshape(-1).astype(I32)
    return flat(0), flat(1), flat(2)


def _wait_tile(step, n_experts, loff_s, len_s, local, remote, sem, to_remote):
    last = step * n_experts + n_experts - 1
    n = (loff_s[last] + len_s[last]) // SEG_ALIGN
    for p in range((local.shape[0] // SEG_ALIGN).bit_length()):
        size = SEG_ALIGN << p

        @pl.when(((n >> p) & 1) == 1)
        def _():
            lref = local.at[pl.ds(0, size)]
            rref = remote.at[pl.ds(0, size)]
            cp = (pltpu.make_async_copy(lref, rref, sem) if to_remote
                  else pltpu.make_async_copy(rref, lref, sem))
            cp.wait()


def _dispatch_kernel(loff_s, seg_s, lo_t, go_t, cnt_t, plen_s, poff_s, nb_s, xa_ref, xb_ref, idx_ref, xs_ref,
                     pos_ref,
                     buf, sems, *, n_experts, top_k, nt0):
    j = pl.program_id(0)
    nt = pl.num_programs(0)
    slot = j % 2
    tt = xa_ref.shape[0]
    lr = buf.shape[1]
    idx = idx_ref[0]
    tope = idx[:top_k]
    base = jnp.zeros_like(tope)
    for e in range(n_experts):
        base = jnp.where(tope == e, loff_s[j * n_experts + e], base)
    pos = base + idx[top_k:]
    pos_ref[0] = jnp.concatenate([pos, jnp.zeros((8 - top_k, tt), I32)], axis=0)
    rows = lax.broadcasted_iota(jnp.int16, (lr, tt), 0)
    pos16 = pos.astype(jnp.int16)
    perm = jnp.zeros((lr, tt), BF16)
    for k in range(top_k):
        perm = jnp.where(rows == pos16[k:k + 1], jnp.ones((lr, tt), BF16), perm)
    x = jnp.where(j < nt0, xa_ref[...], xb_ref[...])
    buf[slot] = _dot(perm, x.astype(BF16)).astype(BF16)

    def drained(step, s):
        _wait_tile(step, n_experts, loff_s, seg_s, buf.at[s], xs_ref, sems.at[s], True)

    _piece_copies(j, lo_t, go_t, cnt_t, buf.at[slot], xs_ref, sems.at[slot], True)

    @pl.when(j > 0)
    def _():
        drained(j - 1, 1 - slot)

    @pl.when(j == nt - 1)
    def _():
        drained(j, slot)
        zrows = EXPERT_BLOCK
        buf[0, :zrows] = jnp.zeros((zrows, buf.shape[2]), BF16)
        _segment_copies(0, n_experts, None, plen_s, poff_s, buf.at[0], xs_ref, sems.at[0], True, False)
        _segment_copies(0, n_experts, None, plen_s, poff_s, buf.at[0], xs_ref, sems.at[0], True, True)
        n_tail = xs_ref.shape[0] // zrows - nb_s[0]

        def tail_copy(i):
            dst = xs_ref.at[pl.ds(pl.multiple_of((nb_s[0] + i) * zrows, zrows), zrows)]
            return pltpu.make_async_copy(buf.at[0, pl.ds(0, zrows)], dst, sems.at[0])

        lax.fori_loop(0, n_tail, lambda i, c: (tail_copy(i).start(), c)[1], 0)
        lax.fori_loop(0, n_tail, lambda i, c: (tail_copy(i).wait(), c)[1], 0)


def _dispatch(x1s, idx, loff, seg, pieces, plen, poff, n_blocks, n_rows, local_rows):
    d = x1s[0].shape[1]
    tt = TOKEN_TILE
    nt0 = x1s[0].shape[0] // tt
    nt = nt0 + x1s[1].shape[0] // tt
    e = loff.shape[0] // nt
    xa, xb = _dual_specs((tt, d), nt0)
    grid_spec = pltpu.PrefetchScalarGridSpec(
        num_scalar_prefetch=8,
        grid=(nt,),
        in_specs=[xa, xb, pl.BlockSpec((1, 2 * TOP_K, tt), lambda i, *_: (i, 0, 0))],
        out_specs=[pl.BlockSpec(memory_space=pl.ANY),
                   pl.BlockSpec((1, 8, tt), lambda i, *_: (i, 0, 0))],
        scratch_shapes=[pltpu.VMEM((2, local_rows, d), BF16), pltpu.SemaphoreType.DMA((2,))],
    )
    return pl.pallas_call(
        functools.partial(_dispatch_kernel, n_experts=e, top_k=TOP_K, nt0=nt0),
        grid_spec=grid_spec,
        out_shape=[jax.ShapeDtypeStruct((n_rows, d), BF16), jax.ShapeDtypeStruct((nt, 8, tt), I32)],
        compiler_params=_params(1, "arbitrary"),
        name="moe_dispatch",
    )(loff, seg, *pieces, plen, poff, n_blocks, x1s[0], x1s[1], idx)


def _combine_kernel(loff_s, seg_s, lo_t, go_t, cnt_t, ys_ref, pos_ref, gate_ref, xa_ref, xb_ref, g2_ref, b2_ref,
                    oa_ref, ob_ref, buf, sems, *, n_experts, top_k, alpha, nt0):
    j = pl.program_id(0)
    nt = pl.num_programs(0)
    slot = j % 2
    tt = xa_ref.shape[0]
    lr = buf.shape[1]

    def prefetch(step, s):
        buf[s] = jnp.zeros((lr, buf.shape[2]), BF16)
        _piece_copies(step, lo_t, go_t, cnt_t, buf.at[s], ys_ref, sems.at[s], False)

    @pl.when(j == 0)
    def _():
        prefetch(0, 0)

    @pl.when(j + 1 < nt)
    def _():
        prefetch(j + 1, 1 - slot)

    _wait_tile(j, n_experts, loff_s, seg_s, buf.at[slot], ys_ref, sems.at[slot], False)
    pos = pos_ref[0].astype(jnp.int16)
    gate = gate_ref[0].astype(BF16)
    rows = lax.broadcasted_iota(jnp.int16, (lr, tt), 0)
    wsel = jnp.zeros((lr, tt), BF16)
    for k in range(top_k):
        wsel = jnp.where(rows == pos[k:k + 1], gate[k:k + 1], wsel)
    moe = lax.dot_general(wsel, buf[slot], (((0,), (0,)), ((), ())), preferred_element_type=F32)
    x1 = jnp.where(j < nt0, xa_ref[...], xb_ref[...])
    out = _layer_norm(alpha * x1 + moe, g2_ref[...], b2_ref[...])

    @pl.when(j < nt0)
    def _():
        oa_ref[...] = out

    @pl.when(j >= nt0)
    def _():
        ob_ref[...] = out


def _combine(ys, pos_tm, gate_tm, x1s, g2, b2, loff, seg, pieces, local_rows, alpha):
    d = x1s[0].shape[1]
    tt = TOKEN_TILE
    nt0 = x1s[0].shape[0] // tt
    nt = nt0 + x1s[1].shape[0] // tt
    e = loff.shape[0] // nt
    xa, xb = _dual_specs((tt, d), nt0)
    grid_spec = pltpu.PrefetchScalarGridSpec(
        num_scalar_prefetch=5,
        grid=(nt,),
        in_specs=[pl.BlockSpec(memory_space=pl.ANY),
                  pl.BlockSpec((1, 8, tt), lambda i, *_: (i, 0, 0)),
                  pl.BlockSpec((1, 8, tt), lambda i, *_: (i, 0, 0)),
                  xa, xb,
                  pl.BlockSpec((1, d), lambda i, *_: (0, 0)),
                  pl.BlockSpec((1, d), lambda i, *_: (0, 0))],
        out_specs=list(_dual_specs((tt, d), nt0)),
        scratch_shapes=[pltpu.VMEM((2, local_rows, d), BF16), pltpu.SemaphoreType.DMA((2,))],
    )
    return pl.pallas_call(
        functools.partial(_combine_kernel, n_experts=e, top_k=TOP_K, alpha=alpha, nt0=nt0),
        grid_spec=grid_spec,
        out_shape=[jax.ShapeDtypeStruct(x.shape, F32) for x in x1s],
        compiler_params=_params(1, "arbitrary"),
        name="moe_combine",
    )(loff, seg, *pieces, ys, pos_tm, gate_tm, x1s[0], x1s[1], g2, b2)


def _expert_kernel(be_s, nb_s, xs_ref, wgu_ref, bgu_ref, wd_ref, bd_ref, ys_ref, wgu_bf, wd_bf):
    b = pl.program_id(0)
    active = b < nb_s[0]
    fresh = jnp.logical_or(b == 0, be_s[b] != be_s[jnp.maximum(b - 1, 0)])

    @pl.when(jnp.logical_and(active, fresh))
    def _():
        wgu_bf[...] = wgu_ref[0].astype(BF16)
        wd_bf[...] = wd_ref[0].astype(BF16)

    @pl.when(active)
    def _():
        dff = wd_ref.shape[1]
        gu = _dot(xs_ref[...], wgu_bf[...]) + bgu_ref[0]
        gate = jnp.minimum(gu[:, :dff], SWIGLU_LIMIT)
        up = jnp.clip(gu[:, dff:], -SWIGLU_LIMIT, SWIGLU_LIMIT)
        h = (up + 1.0) * (gate * jax.nn.sigmoid(SWIGLU_ALPHA * gate))
        ys_ref[...] = (_dot(h.astype(BF16), wd_bf[...]) + bd_ref[0]).astype(BF16)

    @pl.when(jnp.logical_not(active))
    def _():
        ys_ref[...] = jnp.zeros_like(ys_ref)


def _experts(xs, block_expert, n_blocks, wgu, bgu, wd, bdn):
    n_rows, d = xs.shape
    bm = EXPERT_BLOCK
    f2 = wgu.shape[2]
    dff = wd.shape[1]
    blk = lambda b, be, nb: (jnp.minimum(b, nb[0] - 1), 0)
    wsel = lambda b, be, nb: (be[b], 0, 0)
    grid_spec = pltpu.PrefetchScalarGridSpec(
        num_scalar_prefetch=2,
        grid=(n_rows // bm,),
        in_specs=[pl.BlockSpec((bm, d), blk),
                  pl.BlockSpec((1, d, f2), wsel), pl.BlockSpec((1, 1, f2), wsel),
                  pl.BlockSpec((1, dff, d), wsel), pl.BlockSpec((1, 1, d), wsel)],
        out_specs=pl.BlockSpec((bm, d), lambda b, be, nb: (b, 0)),
        scratch_shapes=[pltpu.VMEM((d, f2), BF16), pltpu.VMEM((dff, d), BF16)],
    )
    return pl.pallas_call(
        _expert_kernel,
        grid_spec=grid_spec,
        out_shape=jax.ShapeDtypeStruct((n_rows, d), BF16),
        compiler_params=_params(1, "arbitrary"),
        name="moe_experts",
    )(block_expert, n_blocks, xs, wgu, bgu, wd, bdn)


def _routed_moe(routed, w_gate_up, b_gate_up, w_down, b_down, g2, b2, alpha):
    x1s = [r[0] for r in routed]
    idx, gate, cnt = (jnp.concatenate([r[i] for r in routed], axis=0) for i in (1, 2, 3))
    t = x1s[0].shape[0] + x1s[1].shape[0]
    e = cnt.shape[1]
    tt = TOKEN_TILE
    nt = t // tt
    bm = EXPERT_BLOCK

    cnt = cnt.reshape(nt, e)
    seg = (cnt + SEG_ALIGN - 1) // SEG_ALIGN * SEG_ALIGN
    loff = jnp.cumsum(seg, axis=1) - seg
    per_expert = jnp.sum(seg, axis=0)
    padded = (per_expert + bm - 1) // bm * bm
    blocks_end = jnp.cumsum(padded) // bm
    start = jnp.cumsum(padded) - padded
    goff = start[None, :] + jnp.cumsum(seg, axis=0) - seg
    local_rows = TOP_K * tt + e * SEG_ALIGN
    n_rows_max = -(-(TOP_K * t + (SEG_ALIGN - 1) * e * nt) // bm) * bm + e * bm
    n_blocks = blocks_end[-1:].astype(I32)
    bidx = jnp.minimum(jnp.arange(n_rows_max // bm, dtype=I32), n_blocks[0] - 1)
    block_expert = jnp.minimum(jnp.sum(bidx[:, None] >= blocks_end[None, :], axis=1), e - 1).astype(I32)
    flat = lambda a: a.reshape(-1).astype(I32)
    plen, poff = flat(padded - per_expert), flat(start + per_expert)

    assert e <= PIECE_CAP and local_rows // SEG_CHUNK <= PIECE_CAP
    pieces = _piece_tables(seg, loff, goff)
    xs, pos = _dispatch(x1s, idx, flat(loff), flat(seg), pieces, plen, poff, n_blocks, n_rows_max,
                        local_rows)
    ys = _experts(xs, block_expert, n_blocks, w_gate_up, b_gate_up.reshape(e, 1, -1),
                  w_down, b_down.reshape(e, 1, -1))
    return _combine(ys, pos, gate, x1s, g2, b2, flat(loff), flat(seg), pieces, local_rows, alpha)


def _mixer(x, apply_ln, lng, lnb, lay, consts, alpha):
    batch, seq, d = x.shape
    t = batch * seq
    fw = lay["g_fourier"].shape[1]
    n2 = MINOR
    x2d = x.reshape(t, d)
    z, x0, w = _in_proj(x2d, batch, seq, fw, lng, lnb, lay["w_in"], lay["b_in"], consts["bd"],
                        lay["w_short"], lay["b_short"], apply_ln)
    c = x0.shape[1]

    n1 = seq // n2
    g = jnp.asarray(_major_matrix(n1, n1, n1, -1)).astype(BF16)
    y = _dft_major(g, z.reshape(batch, n1, n2, fw), True)
    ortho = 1.0 / math.sqrt(seq * FOURIER_GROUP_DIM)
    wf_real = jnp.asarray(_minor_matrix(n2, -1)[:n2] * ortho).astype(BF16)
    yf = _dft_minor(y, wf_real, "real")

    n = 2 * seq
    m1 = n // n2
    kf, asum = _hyena_filter(seq, consts["bands"], lay["filt_w1"], lay["filt_b1"], lay["filt_w2"],
                             lay["filt_b2"], lay["filt_w3"], lay["filt_b3"], lay["filt_w_out"],
                             lay["filt_freq"], lay["filt_decay"])
    gk = jnp.asarray(_major_matrix(m1, m1, m1, -1, real_in=True)).astype(BF16)
    ky = _dft_major(gk, kf.reshape(1, m1, n2, c), True)
    gfw = jnp.asarray(_major_matrix(m1, m1 // 2, m1, -1)).astype(BF16)
    gin = jnp.asarray(_major_matrix(m1, m1, m1 // 2, +1)).astype(BF16)
    pairs = batch // 2
    u = _dft_major(gfw, w.reshape(pairs, m1, n2, c), True)
    wi = jnp.asarray(_minor_matrix(n2, +1) * (1.0 / n)).astype(BF16)
    v = _dft_minor(u, consts["wf"], "conv", wi_bf=wi, kf4=ky)
    yc = _dft_major(gin, v, False)

    inv_norm = 1.0 / asum
    return _out_proj(yf.reshape(t, fw), yc.reshape(t, c), x0, w, x2d, lng, lnb, inv_norm,
                     lay["hyena_skip"], lay["g_fourier"], lay["g_hyena"], lay["w_out"], lay["b_out"],
                     lay["ln1_g"], lay["ln1_b"], lay["w_router_t"], lay["b_router"], alpha, apply_ln)


def kernel(x_prompt, x_sample, ln_in_g, ln_in_b, w_in, b_in, w_short, b_short, filt_w1, filt_b1, filt_w2, filt_b2, filt_w3, filt_b3, filt_w_out, filt_freq, filt_decay, hyena_skip, g_fourier, g_hyena, w_out, b_out, ln1_g, ln1_b, w_router, b_router, w_gate_up, b_gate_up, w_down, b_down, ln2_g, ln2_b):
    depth, d, in_width = w_in.shape
    fw = g_fourier.shape[1]
    alpha = (2.0 * depth) ** 0.25
    row = lambda a: a.reshape(1, -1)

    gd = FOURIER_GROUP_DIM
    cg, sg = _cs(np.outer(np.arange(gd), np.arange(gd)), gd)
    eye = np.eye(LANES // gd)
    bd = np.concatenate([np.kron(eye, cg), -np.kron(eye, sg)], axis=1).astype(np.float32)
    bands = jnp.linspace(1e-4, FILTER_BANDS - 1, FILTER_BANDS, dtype=F32)
    bands_row = jnp.zeros((1, LANES), F32).at[0, 1:1 + FILTER_BANDS].set(bands)
    bands_row = bands_row.at[0, 1 + FILTER_BANDS:1 + 2 * FILTER_BANDS].set(bands)
    consts = dict(bd=jnp.asarray(bd).astype(BF16), bands=bands_row,
                  wf=jnp.asarray(_minor_matrix(MINOR, -1)).astype(BF16))

    xs = [x_prompt, x_sample]
    for l in range(depth):
        emb = filt_w1.shape[1]
        twice = lambda a: jnp.concatenate([row(a), row(a)], axis=1)
        blockdiag = lambda a: jnp.kron(jnp.eye(2, dtype=F32), a)
        w1_pad = jnp.zeros((LANES, filt_w1.shape[2]), F32).at[:emb].set(filt_w1[l])
        wl = filt_w_out[l]
        wl_halves = jnp.stack([jnp.concatenate([wl, jnp.zeros_like(wl)], axis=0),
                               jnp.concatenate([jnp.zeros_like(wl), wl], axis=0)])
        lay = dict(
            w_in=w_in[l].astype(BF16), b_in=row(b_in[l]), w_short=w_short[l], b_short=row(b_short[l]),
            filt_w1=blockdiag(w1_pad),
            filt_b1=twice(filt_b1[l]), filt_w2=blockdiag(filt_w2[l]), filt_b2=twice(filt_b2[l]),
            filt_w3=blockdiag(filt_w3[l]),
            filt_b3=twice(filt_b3[l]), filt_w_out=wl_halves, filt_freq=twice(filt_freq[l]),
            filt_decay=filt_decay[l], hyena_skip=row(hyena_skip[l]), g_fourier=row(g_fourier[l]),
            g_hyena=row(g_hyena[l]), w_out=w_out[l].astype(BF16), b_out=row(b_out[l]),
            ln1_g=row(ln1_g[l]), ln1_b=row(ln1_b[l]),
            w_router_t=w_router[l].T, b_router=b_router[l].reshape(-1, 1))
        routed = [_mixer(x, l == 0, row(ln_in_g), row(ln_in_b), lay, consts, alpha) for x in xs]
        outs = _routed_moe(routed, w_gate_up[l], b_gate_up[l], w_down[l], b_down[l],
                           row(ln2_g[l]), row(ln2_b[l]), alpha)
        xs = [o.reshape(x.shape) for o, x in zip(outs, xs)]
    return (xs[0], xs[1])
```

```python
import functools
import math

import numpy as np
import jax
import jax.numpy as jnp
from jax import lax
from jax.experimental import pallas as pl
from jax.experimental.pallas import tpu as pltpu

F32 = jnp.float32
BF16 = jnp.bfloat16
I32 = jnp.int32
U32 = jnp.uint32

LN_EPS = 1e-5
RMS_EPS = 1e-6
SWIGLU_LIMIT = 7.0
SWIGLU_ALPHA = 1.702
TOP_K = 4
FOURIER_GROUP_DIM = 64
FILTER_BANDS = 16

LANES = 128
MINOR = 128
ROW_TILE = 512
IN_TILE = 512
IN_SPLIT = 2
MAJOR_BLOCK_BYTES = 12 * 1024 * 1024
TOKEN_TILE = 512
SEG_ALIGN = 16
EXPERT_BLOCK = 512
VMEM_LIMIT = 56 * 1024 * 1024


def _params(n_grid, semantics="parallel"):
    return pltpu.CompilerParams(dimension_semantics=(semantics,) * n_grid,
                                vmem_limit_bytes=VMEM_LIMIT)


def _dot(a, b):
    return jnp.dot(a, b, preferred_element_type=F32)


def _split(a):
    hi = a.astype(BF16)
    lo = (a - hi.astype(F32)).astype(BF16)
    return hi, lo


def _dot3(a, b):
    ah, al = _split(a)
    bh, bl = _split(b)
    return _dot(ah, bh) + _dot(ah, bl) + _dot(al, bh)


def _layer_norm(x, g, b):
    mu = jnp.mean(x, axis=-1, keepdims=True)
    xc = x - mu
    var = jnp.mean(xc * xc, axis=-1, keepdims=True)
    return xc * lax.rsqrt(var + LN_EPS) * g + b


def _rms_norm(x, g):
    return x * lax.rsqrt(jnp.mean(x * x, axis=-1, keepdims=True) + RMS_EPS) * g


def _in_kernel(x_ref, xp_ref, xq_ref, lng_ref, lnb_ref, win_ref, bin_ref, bd_ref, wsh_ref, bsh_ref,
               z_ref, x0_ref, w_ref, *, tiles_per_seq, apply_ln, fw):
    it = pl.program_id(0) % tiles_per_seq
    tm = x_ref.shape[0]

    def norm(x):
        return _layer_norm(x, lng_ref[...], lnb_ref[...]) if apply_ln else x

    halo = norm(jnp.concatenate([xp_ref[...], xq_ref[...]], axis=0))
    projh = _dot(halo.astype(BF16), win_ref[:, fw:]) + bin_ref[:, fw:]
    rp = tm // IN_SPLIT
    projs = [_dot(norm(x_ref[r * rp:(r + 1) * rp, :]).astype(BF16), win_ref[...]) + bin_ref[...]
             for r in range(IN_SPLIT)]
    wsh = wsh_ref[...]
    for r, proj in enumerate(projs):
        sl = slice(r * rp, (r + 1) * rp)
        pf = proj[:, :fw].astype(BF16)
        zs = [_dot(pf[:, c0:c0 + LANES], bd_ref[...]) for c0 in range(0, fw, LANES)]
        z_ref[0, sl, :] = _pack_pair(jnp.concatenate([z[:, :LANES] for z in zs], axis=1),
                                     jnp.concatenate([z[:, LANES:] for z in zs], axis=1))

        ph = proj[:, fw:]
        prev_row = projs[r - 1][rp - 1:rp, fw:] if r > 0 else jnp.where(it > 0, projh[7:8], 0.0)
        next_row = (projs[r + 1][0:1, fw:] if r + 1 < IN_SPLIT
                    else jnp.where(it < tiles_per_seq - 1, projh[8:9], 0.0))
        rows = lax.broadcasted_iota(I32, ph.shape, 0)
        up = jnp.where(rows == 0, prev_row, pltpu.roll(ph, 1, 0))
        dn = jnp.where(rows == rp - 1, next_row, pltpu.roll(ph, rp - 1, 0))
        uc = up * wsh[0:1] + ph * wsh[1:2] + dn * wsh[2:3] + bsh_ref[...]
        hw = uc.shape[1] // 3
        x0_ref[sl, :] = uc[:, :hw]
        w_ref[sl, :] = uc[:, 2 * hw:] * uc[:, hw:2 * hw]


def _in_proj(x2d, batch, seq, fw, lng, lnb, win_bf, b_in, bd_bf, w_short, b_short, apply_ln):
    t, d = x2d.shape
    inw = win_bf.shape[1]
    hw = (inw - fw) // 3
    tm = IN_TILE
    tps = seq // tm
    r8 = tm // 8
    nblk8 = t // 8
    const = lambda i: (0, 0)
    kern = functools.partial(_in_kernel, tiles_per_seq=tps, apply_ln=apply_ln, fw=fw)
    return pl.pallas_call(
        kern,
        grid=(t // tm,),
        in_specs=[
            pl.BlockSpec((tm, d), lambda i: (i, 0)),
            pl.BlockSpec((8, d), lambda i: (jnp.maximum(i * r8 - 1, 0), 0)),
            pl.BlockSpec((8, d), lambda i: (jnp.minimum((i + 1) * r8, nblk8 - 1), 0)),
            pl.BlockSpec((1, d), const), pl.BlockSpec((1, d), const),
            pl.BlockSpec((d, inw), const), pl.BlockSpec((1, inw), const),
            pl.BlockSpec(bd_bf.shape, const),
            pl.BlockSpec((3, 3 * hw), const), pl.BlockSpec((1, 3 * hw), const),
        ],
        out_specs=[
            pl.BlockSpec((1, tm, fw), lambda i: (i // tps, i % tps, 0)),
            pl.BlockSpec((tm, hw), lambda i: (i, 0)),
            pl.BlockSpec((tm, hw), lambda i: (i, 0)),
        ],
        out_shape=[
            jax.ShapeDtypeStruct((batch, seq, fw), U32),
            jax.ShapeDtypeStruct((t, hw), F32),
            jax.ShapeDtypeStruct((t, hw), F32),
        ],
        compiler_params=_params(1),
        name="in_proj",
    )(x2d, x2d, x2d, lng, lnb, win_bf, b_in, bd_bf, w_short, b_short)


FILTER_HALO = 16


def _filter_kernel(bands_ref, w1_ref, b1_ref, w2_ref, b2_ref, w3_ref, b3_ref, wl_ref, fr_ref, dec_ref,
                   k_ref, asum_ref, tc_ref, ts_ref, kf_buf, kb_buf, sems, *, seq):
    i = pl.program_id(0)
    nsteps = pl.num_programs(0)
    tile, c = kf_buf.shape
    ext = tc_ref.shape[0]
    rep = c // LANES
    w_unit = 2.0 * math.pi / seq
    bands = bands_ref[...]

    @pl.when(i == 0)
    def _():
        arg = bands * (w_unit * lax.broadcasted_iota(I32, (ext, LANES), 0).astype(F32))
        tc_ref[...] = jnp.cos(arg)
        ts_ref[...] = jnp.sin(arg)
        asum_ref[...] = jnp.zeros_like(asum_ref)

    j0 = i * tile
    base = bands * (w_unit * j0.astype(F32))
    c0, s0 = jnp.cos(base), jnp.sin(base)
    tc, ts = tc_ref[...], ts_ref[...]
    cosf = c0 * tc - s0 * ts
    sinf = s0 * tc + c0 * ts
    j = j0 + lax.broadcasted_iota(I32, (ext, LANES), 0)
    t = j.astype(F32) * (1.0 / (seq - 1))
    lane = lax.broadcasted_iota(I32, (ext, LANES), 1)
    feat = jnp.where(lane == 0, t,
                     jnp.where(lane <= FILTER_BANDS, cosf,
                               jnp.where(lane <= 2 * FILTER_BANDS, -sinf, 0.0)))
    half = ext // 2
    fr = fr_ref[...]
    h = jnp.concatenate([feat[:half], feat[half:]], axis=1)
    h = jnp.sin(fr * (_dot3(h, w1_ref[...]) + b1_ref[...]))
    h = jnp.sin(fr * (_dot3(h, w2_ref[...]) + b2_ref[...]))
    h = jnp.sin(fr * (_dot3(h, w3_ref[...]) + b3_ref[...]))
    hl = jnp.concatenate([_dot3(h, wl_ref[0]), _dot3(h, wl_ref[1])], axis=0)
    t4 = jnp.concatenate([t] * rep, axis=1)
    j4 = jnp.concatenate([j] * rep, axis=1)
    dec = jnp.abs(dec_ref[...])
    hf = hl[:, :c] * jnp.exp(-t4 * dec[0:1])
    hb = hl[:, c:] * jnp.exp(-t4 * dec[1:2])
    hb = jnp.where(j4 < seq, hb, 0.0)
    asum_ref[...] += jnp.sum(jnp.abs(hf[:tile]) + jnp.abs(hb[:tile]), axis=0, keepdims=True)

    rev = (lax.broadcasted_iota(I32, (tile, ext), 1) == tile - lax.broadcasted_iota(I32, (tile, ext), 0))
    rev = jnp.where(rev, 1.0, 0.0).astype(BF16)
    b1 = hb.astype(BF16)
    r1 = hb - b1.astype(F32)
    b2 = r1.astype(BF16)
    b3 = (r1 - b2.astype(F32)).astype(BF16)
    kb = _dot(rev, b1) + _dot(rev, b2) + _dot(rev, b3)

    def copies():
        return (pltpu.make_async_copy(kf_buf, k_ref.at[pl.ds(pl.multiple_of(j0, tile), tile)], sems.at[0]),
                pltpu.make_async_copy(kb_buf, k_ref.at[pl.ds(pl.multiple_of(2 * seq - j0 - tile, tile), tile)],
                                      sems.at[1]))

    @pl.when(i > 0)
    def _():
        for cp in copies():
            cp.wait()

    kf_buf[...] = jnp.where(j4[:tile] == 0, hf[:tile] + hb[:tile], hf[:tile])
    kb_buf[...] = kb
    for cp in copies():
        cp.start()

    @pl.when(i == nsteps - 1)
    def _():
        for cp in copies():
            cp.wait()


def _hyena_filter(seq, bands_row, w1p, b1, w2, b2, w3, b3, wl, freq, decay):
    c = decay.shape[1]
    tile = ROW_TILE
    ext = tile + FILTER_HALO
    full = lambda a: pl.BlockSpec(a.shape, lambda i: (0,) * a.ndim)
    args = (bands_row, w1p, b1, w2, b2, w3, b3, wl, freq, decay)
    return pl.pallas_call(
        functools.partial(_filter_kernel, seq=seq),
        grid=(seq // tile,),
        in_specs=[full(a) for a in args],
        out_specs=[pl.BlockSpec(memory_space=pl.ANY), pl.BlockSpec((1, c), lambda i: (0, 0))],
        out_shape=[jax.ShapeDtypeStruct((2 * seq, c), F32), jax.ShapeDtypeStruct((1, c), F32)],
        scratch_shapes=[pltpu.VMEM((ext, LANES), F32), pltpu.VMEM((ext, LANES), F32),
                        pltpu.VMEM((tile, c), F32), pltpu.VMEM((tile, c), F32),
                        pltpu.SemaphoreType.DMA((2,))],
        compiler_params=_params(1, "arbitrary"),
        name="hyena_filter",
    )(*args)


def _pack_pair(re, im):
    rb = lax.bitcast_convert_type(re.astype(BF16).astype(F32), U32)
    ib = lax.bitcast_convert_type(im.astype(BF16).astype(F32), U32)
    return (rb >> 16) | ib


def _unpack_pair(word):
    re = lax.bitcast_convert_type(word << 16, F32)
    im = lax.bitcast_convert_type(word & jnp.uint32(0xFFFF0000), F32)
    return re, im


def _major_compute(g, xbuf, obuf, slot):
    _, n_ct, bsz, k, sub, lanes = xbuf.shape
    m = obuf.shape[3]
    x2 = xbuf.reshape(2 * n_ct * bsz * k * sub, lanes)
    o2 = obuf.reshape(2 * n_ct * bsz * m * sub, lanes)
    for t in range(n_ct * bsz):
        xbase = (slot * n_ct * bsz + t) * k * sub
        obase = (slot * n_ct * bsz + t) * m * sub
        for b in range(sub):
            xb = x2[pl.ds(xbase + b, k, stride=sub), :]
            if xb.dtype == U32:
                xb = jnp.concatenate(_unpack_pair(xb), axis=0)
            res = _dot(g, xb.astype(BF16))
            if obuf.dtype == U32:
                res = _pack_pair(res[:m], res[m:])
            o2[pl.ds(obase + b, m, stride=sub), :] = res


def _major_kernel(g_ref, x_hbm, o_hbm, xbuf, obuf, isem, osem, *, nj):
    s = pl.program_id(0)
    ns = pl.num_programs(0)
    slot = s % 2
    _, n_ct, bsz, k, sub, lanes = xbuf.shape
    m = obuf.shape[3]

    def copies(step, sl, hbm, buf, sem, inbound):
        i = step // nj
        j = step % nj
        out = []
        for ct in range(n_ct):
            tile = hbm.at[pl.ds(i * bsz, bsz), :, pl.ds(pl.multiple_of(j * sub, sub), sub),
                          pl.ds(ct * lanes, lanes)]
            out.append(pltpu.make_async_copy(tile, buf.at[sl, ct], sem.at[sl]) if inbound
                       else pltpu.make_async_copy(buf.at[sl, ct], tile, sem.at[sl]))
        return out

    def start(cps):
        for cp in cps:
            cp.start()

    def wait(cps):
        for cp in cps:
            cp.wait()

    @pl.when(s == 0)
    def _():
        start(copies(0, 0, x_hbm, xbuf, isem, True))

    @pl.when(s + 1 < ns)
    def _():
        start(copies(s + 1, 1 - slot, x_hbm, xbuf, isem, True))

    wait(copies(s, slot, x_hbm, xbuf, isem, True))

    @pl.when(s >= 2)
    def _():
        wait(copies(s - 2, slot, o_hbm, obuf, osem, False))

    _major_compute(g_ref[...], xbuf, obuf, slot)
    start(copies(s, slot, o_hbm, obuf, osem, False))

    @pl.when(s == ns - 1)
    def _():
        wait(copies(s, slot, o_hbm, obuf, osem, False))

        @pl.when(s >= 1)
        def _():
            wait(copies(s - 1, 1 - slot, o_hbm, obuf, osem, False))


def _dft_major(g_bf, x4, pack_out):
    b, k, n2, c = x4.shape
    m = g_bf.shape[0] // 2 if pack_out else g_bf.shape[0]
    out_dtype = U32 if pack_out else F32
    sub = 8
    n_ct = c // LANES
    bsz = max(1, min(b, MAJOR_BLOCK_BYTES // ((k + m) * sub * c * 4)))
    while b % bsz:
        bsz -= 1
    nj = n2 // sub
    return pl.pallas_call(
        functools.partial(_major_kernel, nj=nj),
        grid=((b // bsz) * nj,),
        in_specs=[pl.BlockSpec(g_bf.shape, lambda s: (0, 0)), pl.BlockSpec(memory_space=pl.ANY)],
        out_specs=pl.BlockSpec(memory_space=pl.ANY),
        out_shape=jax.ShapeDtypeStruct((b, m, n2, c), out_dtype),
        scratch_shapes=[pltpu.VMEM((2, n_ct, bsz, k, sub, LANES), x4.dtype),
                        pltpu.VMEM((2, n_ct, bsz, m, sub, LANES), out_dtype),
                        pltpu.SemaphoreType.DMA((2,)), pltpu.SemaphoreType.DMA((2,))],
        compiler_params=_params(1, "arbitrary"),
        name="dft_major",
    )(g_bf, x4)


def _minor_kernel(*refs, mode, k1b):
    if mode == "conv":
        y_ref, twb_ref, twr_ref, wf_ref, wi_ref, kf_ref, o_ref, ks_ref = refs
    else:
        y_ref, twb_ref, twr_ref, wf_ref, o_ref = refs
    n2 = y_ref.shape[2]
    ct = y_ref.shape[3]
    rep = ct // LANES

    def forward(packed, cs, sn, w):
        re, im = _unpack_pair(packed)
        return _dot(w, jnp.concatenate([re * cs + im * sn, im * cs - re * sn], axis=0).astype(BF16))

    cb, sb = twb_ref[0, 0], twb_ref[1, 0]
    for j in range(k1b):
        cr, sr = twr_ref[0, j], twr_ref[1, j]
        cs = jnp.concatenate([cb * cr - sb * sr] * rep, axis=1)
        sn = jnp.concatenate([sb * cr + cb * sr] * rep, axis=1)
        if mode == "real":
            o_ref[0, :, j, :] = forward(y_ref[0, j], cs, sn, wf_ref[...])
            continue

        @pl.when(pl.program_id(1) == 0)
        def _():
            ks_ref[j] = forward(kf_ref[0, j], cs, sn, wf_ref[...])

        z = forward(y_ref[0, j], cs, sn, wf_ref[...])
        zr, zi = z[:n2], z[n2:]
        kr = ks_ref[j, :n2]
        ki = ks_ref[j, n2:]
        p = jnp.concatenate([zr * kr - zi * ki, zr * ki + zi * kr], axis=0).astype(BF16)
        q = _dot(wi_ref[...], p)
        qr, qi = q[:n2], q[n2:]
        o_ref[0, j] = _pack_pair(qr * cs - qi * sn, qi * cs + qr * sn)


def _dft_minor(y4, wf_bf, mode, wi_bf=None, kf4=None):
    b, n1, n2, c = y4.shape
    k1b = min(8, n1)
    ct = c
    tw_block, tw_rem = _twiddles(n1, n2, k1b)
    grid = (n1 // k1b, b)
    dspec = pl.BlockSpec((1, k1b, n2, ct), lambda j, i: (i, j, 0, 0))
    in_specs = [dspec,
                pl.BlockSpec((2, 1, n2, LANES), lambda j, i: (0, j, 0, 0)),
                pl.BlockSpec((2, k1b, n2, LANES), lambda j, i: (0, 0, 0, 0)),
                pl.BlockSpec(wf_bf.shape, lambda j, i: (0, 0))]
    args = [y4, tw_block, tw_rem, wf_bf]
    scratch = []
    if mode == "conv":
        in_specs += [pl.BlockSpec(wi_bf.shape, lambda j, i: (0, 0)),
                     pl.BlockSpec((1, k1b, n2, ct), lambda j, i: (0, j, 0, 0))]
        args += [wi_bf, kf4]
        scratch = [pltpu.VMEM((k1b, 2 * n2, ct), F32)]
    if mode == "real":
        out_spec = pl.BlockSpec((1, n2, k1b, ct), lambda j, i: (i, 0, j, 0))
        out_shape = jax.ShapeDtypeStruct((b, n2, n1, c), F32)
    else:
        out_spec = dspec
        out_shape = jax.ShapeDtypeStruct((b, n1, n2, c), U32)
    return pl.pallas_call(
        functools.partial(_minor_kernel, mode=mode, k1b=k1b),
        grid=grid, in_specs=in_specs, out_specs=out_spec, out_shape=out_shape,
        scratch_shapes=scratch,
        compiler_params=pltpu.CompilerParams(dimension_semantics=("parallel", "arbitrary"),
                                             vmem_limit_bytes=VMEM_LIMIT),
        name="dft_minor_" + mode,
    )(*args)


def _cs(num, den):
    ang = 2.0 * np.pi * (num % den) / den
    return np.cos(ang), np.sin(ang)


def _complex_block(c, s, sign):
    return np.block([[c, -sign * s], [sign * s, c]]).astype(np.float32)


def _major_matrix(n1, a_in, a_out, sign, real_in=False):
    c, s = _cs(np.outer(np.arange(a_out), np.arange(a_in)), n1)
    g = _complex_block(c, s, sign)
    return g[:, :a_in] if real_in else g


def _minor_matrix(n2, sign):
    c, s = _cs(np.outer(np.arange(n2), np.arange(n2)), n2)
    return _complex_block(c, s, sign)


def _twiddles(n1, n2, k1b):
    def table(k1):
        c, s = _cs(np.outer(k1, np.arange(n2)), n1 * n2)
        tw = jnp.asarray(np.stack([c, s]).astype(np.float32))
        return jnp.broadcast_to(tw[..., None], (2, len(k1), n2, LANES))
    return table(np.arange(0, n1, k1b)), table(np.arange(k1b))


def _out_kernel(yf_ref, yc_ref, x0_ref, w_ref, x_ref, lng_ref, lnb_ref, inv_ref, skip_ref, gf_ref, gh_ref,
                wo_ref, bo_ref, g1_ref, b1_ref, wr_ref, br_ref, x1_ref, idx_ref, gate_ref, cnt_ref,
                *, alpha, apply_ln, top_k):
    c = yc_ref.shape[1]
    w = w_ref[...]
    yh = x0_ref[...] * (yc_ref[...] * inv_ref[...] + skip_ref[...] * w)
    mf = _rms_norm(yf_ref[...], gf_ref[...]).astype(BF16)
    mh = _rms_norm(yh, gh_ref[...]).astype(BF16)
    m = _dot(mf, wo_ref[:c]) + _dot(mh, wo_ref[c:]) + bo_ref[...]
    xn = _layer_norm(x_ref[...], lng_ref[...], lnb_ref[...]) if apply_ln else x_ref[...]
    x1 = _layer_norm(alpha * xn + m, g1_ref[...], b1_ref[...])
    x1_ref[...] = x1
    _route_tile(x1, wr_ref[...], br_ref[...], idx_ref, gate_ref, cnt_ref, top_k)


def _out_proj(yf, yc, x0, w, x, lng, lnb, inv_norm, skip, gf, gh, wo_bf, bo, g1, b1, wr_t, br_col, alpha,
              apply_ln):
    t, d = x.shape
    c = yc.shape[1]
    e = wr_t.shape[0]
    tm = TOKEN_TILE
    nt = t // tm
    row = lambda i: (i, 0)
    const = lambda i: (0, 0)
    tile3 = lambda i: (i, 0, 0)
    return pl.pallas_call(
        functools.partial(_out_kernel, alpha=alpha, apply_ln=apply_ln, top_k=TOP_K),
        grid=(nt,),
        in_specs=[
            pl.BlockSpec((tm, c), row),
            pl.BlockSpec((tm, c), row), pl.BlockSpec((tm, c), row), pl.BlockSpec((tm, c), row),
            pl.BlockSpec((tm, d), row),
            pl.BlockSpec((1, d), const), pl.BlockSpec((1, d), const),
            pl.BlockSpec((1, c), const), pl.BlockSpec((1, c), const),
            pl.BlockSpec((1, c), const), pl.BlockSpec((1, c), const),
            pl.BlockSpec(wo_bf.shape, const), pl.BlockSpec((1, d), const),
            pl.BlockSpec((1, d), const), pl.BlockSpec((1, d), const),
            pl.BlockSpec((e, d), const), pl.BlockSpec((e, 1), const),
        ],
        out_specs=[pl.BlockSpec((tm, d), row),
                   pl.BlockSpec((1, 2 * TOP_K, tm), tile3),
                   pl.BlockSpec((1, 8, tm), tile3),
                   pl.BlockSpec((1, e, 1), tile3)],
        out_shape=[jax.ShapeDtypeStruct((t, d), F32),
                   jax.ShapeDtypeStruct((nt, 2 * TOP_K, tm), I32),
                   jax.ShapeDtypeStruct((nt, 8, tm), F32),
                   jax.ShapeDtypeStruct((nt, e, 1), I32)],
        compiler_params=_params(1),
        name="out_proj",
    )(yf, yc, x0, w, x, lng, lnb, inv_norm, skip, gf, gh, wo_bf, bo, g1, b1, wr_t, br_col)


def _route_tile(x1, wr, br, idx_ref, gate_ref, cnt_ref, top_k):
    tt = x1.shape[0]
    e = wr.shape[0]
    nt = (((1,), (1,)), ((), ()))
    xh, xl = _split(x1)
    wh, wl = _split(wr)
    logits = (lax.dot_general(wh, xh, nt, preferred_element_type=F32)
              + lax.dot_general(wh, xl, nt, preferred_element_type=F32)
              + lax.dot_general(wl, xh, nt, preferred_element_type=F32)) + br
    iota_e = lax.broadcasted_iota(I32, (e, tt), 0).astype(F32)
    l = logits
    tops, idxs, hots = [], [], []
    for _ in range(top_k):
        m = jnp.max(l, axis=0, keepdims=True)
        idx = jnp.min(jnp.where(l == m, iota_e, float(e)), axis=0, keepdims=True)
        hot = iota_e == idx
        l = jnp.where(hot, -jnp.inf, l)
        tops.append(m)
        idxs.append(idx)
        hots.append(hot)
    ex = [jnp.exp(m - tops[0]) for m in tops]
    den = ex[0]
    for v in ex[1:]:
        den = den + v
    gates = [v / den for v in ex]
    hot_all = jnp.zeros((e, tt), F32)
    for hot in hots:
        hot_all = jnp.where(hot, 1.0, hot_all)
    upper = (lax.broadcasted_iota(I32, (tt, tt), 0) < lax.broadcasted_iota(I32, (tt, tt), 1))
    before = _dot(hot_all.astype(BF16), jnp.where(upper, 1.0, 0.0).astype(BF16))
    ranks = [jnp.sum(jnp.where(hot, before, 0.0), axis=0, keepdims=True).astype(I32) for hot in hots]
    pad = 8 - top_k
    idx_ref[0] = jnp.concatenate([v.astype(I32) for v in idxs] + ranks, axis=0)
    gate_ref[0] = jnp.concatenate(gates + [jnp.zeros((pad, tt), F32)], axis=0)
    cnt_ref[0] = jnp.sum(hot_all, axis=1, keepdims=True).astype(I32)


def _dual_specs(shape, nt0):
    first = pl.BlockSpec(shape, lambda i, *_: (jnp.minimum(i, nt0 - 1),) + (0,) * (len(shape) - 1))
    second = pl.BlockSpec(shape, lambda i, *_: (jnp.maximum(i - nt0, 0),) + (0,) * (len(shape) - 1))
    return first, second


SEG_PIECES = 3
SEG_CHUNK = SEG_ALIGN << SEG_PIECES
PIECE_CLASSES = SEG_PIECES + 1
PIECE_CAP = 32


def _segment_copies(first, count, loff_s, len_s, goff_s, local, remote, sem, to_remote, wait):
    def copy(lstart, gstart, size):
        lref = local.at[pl.ds(pl.multiple_of(lstart, SEG_ALIGN), size)]
        rref = remote.at[pl.ds(pl.multiple_of(gstart, SEG_ALIGN), size)]
        cp = (pltpu.make_async_copy(lref, rref, sem) if to_remote
              else pltpu.make_async_copy(rref, lref, sem))
        if wait:
            cp.wait()
        else:
            cp.start()

    def body(e, carry):
        s = first + e
        n = len_s[s] // SEG_ALIGN
        lo = 0 if loff_s is None else loff_s[s]
        go = goff_s[s]
        chunks = n >> SEG_PIECES

        def chunk(i, c):
            copy(lo + i * SEG_CHUNK, go + i * SEG_CHUNK, SEG_CHUNK)
            return c

        lax.fori_loop(0, chunks, chunk, 0)
        rem = n & ((1 << SEG_PIECES) - 1)
        base = chunks * SEG_CHUNK
        for p in reversed(range(SEG_PIECES)):
            start = base + ((rem >> (p + 1)) << (p + 1)) * SEG_ALIGN

            @pl.when(((rem >> p) & 1) == 1)
            def _():
                copy(lo + start, go + start, SEG_ALIGN << p)
        return carry

    lax.fori_loop(0, count, body, 0)


def _piece_copies(step, lo_t, go_t, cnt_t, local, remote, sem, to_remote):
    for c in range(PIECE_CLASSES):
        size = SEG_ALIGN << c
        base = (step * PIECE_CLASSES + c) * PIECE_CAP

        def body(i, carry, size=size, base=base):
            lref = local.at[pl.ds(pl.multiple_of(lo_t[base + i], SEG_ALIGN), size)]
            rref = remote.at[pl.ds(pl.multiple_of(go_t[base + i], SEG_ALIGN), size)]
            cp = (pltpu.make_async_copy(lref, rref, sem) if to_remote
                  else pltpu.make_async_copy(rref, lref, sem))
            cp.start()
            return carry

        lax.fori_loop(0, cnt_t[step * PIECE_CLASSES + c], body, 0)


def _piece_tables(seg, loff, goff):
    nt, e = seg.shape
    n = seg // SEG_ALIGN
    chunks = n >> SEG_PIECES
    rem = n & ((1 << SEG_PIECES) - 1)
    base = chunks * SEG_CHUNK

    def compact(valid, off, lo, go):
        slots = valid.shape[1]
        key = jnp.where(valid, jnp.arange(slots, dtype=I32)[None, :], slots)
        _, lo_sorted, go_sorted = lax.sort((key, lo + off, go + off), dimension=1, num_keys=1)
        return lo_sorted[:, :PIECE_CAP], go_sorted[:, :PIECE_CAP], jnp.sum(valid, axis=1)

    tables = []
    for c in range(SEG_PIECES):
        off = base + ((rem >> (c + 1)) << (c + 1)) * SEG_ALIGN
        tables.append(compact(((rem >> c) & 1) == 1, off, loff, goff))
    max_chunks = TOKEN_TILE // SEG_CHUNK
    i = jnp.arange(max_chunks, dtype=I32)[None, None, :]
    rep = lambda a: jnp.broadcast_to(a[:, :, None], (nt, e, max_chunks)).reshape(nt, e * max_chunks)
    tables.append(compact((i < chunks[:, :, None]).reshape(nt, e * max_chunks),
                          jnp.broadcast_to(i * SEG_CHUNK, (nt, e, max_chunks)).reshape(nt, e * max_chunks),
                          rep(loff), rep(goff)))
    flat = lambda k: jnp.stack([t[k] for t in tables], axis=1).reshape(-1).astype(I32)
    return flat(0), flat(1), flat(2)


def _wait_tile(step, n_experts, loff_s, len_s, local, remote, sem, to_remote):
    last = step * n_experts + n_experts - 1
    n = (loff_s[last] + len_s[last]) // SEG_ALIGN
    for p in range((local.shape[0] // SEG_ALIGN).bit_length()):
        size = SEG_ALIGN << p

        @pl.when(((n >> p) & 1) == 1)
        def _():
            lref = local.at[pl.ds(0, size)]
            rref = remote.at[pl.ds(0, size)]
            cp = (pltpu.make_async_copy(lref, rref, sem) if to_remote
                  else pltpu.make_async_copy(rref, lref, sem))
            cp.wait()


def _dispatch_kernel(loff_s, seg_s, lo_t, go_t, cnt_t, plen_s, poff_s, nb_s, xa_ref, xb_ref, idx_ref, xs_ref,
                     pos_ref,
                     buf, sems, *, n_experts, top_k, nt0):
    j = pl.program_id(0)
    nt = pl.num_programs(0)
    slot = j % 2
    tt = xa_ref.shape[0]
    lr = buf.shape[1]
    idx = idx_ref[0]
    tope = idx[:top_k]
    base = jnp.zeros_like(tope)
    for e in range(n_experts):
        base = jnp.where(tope == e, loff_s[j * n_experts + e], base)
    pos = base + idx[top_k:]
    pos_ref[0] = jnp.concatenate([pos, jnp.zeros((8 - top_k, tt), I32)], axis=0)
    rows = lax.broadcasted_iota(jnp.int16, (lr, tt), 0)
    pos16 = pos.astype(jnp.int16)
    perm = jnp.zeros((lr, tt), BF16)
    for k in range(top_k):
        perm = jnp.where(rows == pos16[k:k + 1], jnp.ones((lr, tt), BF16), perm)
    x = jnp.where(j < nt0, xa_ref[...], xb_ref[...])
    buf[slot] = _dot(perm, x.astype(BF16)).astype(BF16)

    def drained(step, s):
        _wait_tile(step, n_experts, loff_s, seg_s, buf.at[s], xs_ref, sems.at[s], True)

    _piece_copies(j, lo_t, go_t, cnt_t, buf.at[slot], xs_ref, sems.at[slot], True)

    @pl.when(j > 0)
    def _():
        drained(j - 1, 1 - slot)

    @pl.when(j == nt - 1)
    def _():
        drained(j, slot)
        zrows = EXPERT_BLOCK
        buf[0, :zrows] = jnp.zeros((zrows, buf.shape[2]), BF16)
        _segment_copies(0, n_experts, None, plen_s, poff_s, buf.at[0], xs_ref, sems.at[0], True, False)
        _segment_copies(0, n_experts, None, plen_s, poff_s, buf.at[0], xs_ref, sems.at[0], True, True)
        n_tail = xs_ref.shape[0] // zrows - nb_s[0]

        def tail_copy(i):
            dst = xs_ref.at[pl.ds(pl.multiple_of((nb_s[0] + i) * zrows, zrows), zrows)]
            return pltpu.make_async_copy(buf.at[0, pl.ds(0, zrows)], dst, sems.at[0])

        lax.fori_loop(0, n_tail, lambda i, c: (tail_copy(i).start(), c)[1], 0)
        lax.fori_loop(0, n_tail, lambda i, c: (tail_copy(i).wait(), c)[1], 0)


def _dispatch(x1s, idx, loff, seg, pieces, plen, poff, n_blocks, n_rows, local_rows):
    d = x1s[0].shape[1]
    tt = TOKEN_TILE
    nt0 = x1s[0].shape[0] // tt
    nt = nt0 + x1s[1].shape[0] // tt
    e = loff.shape[0] // nt
    xa, xb = _dual_specs((tt, d), nt0)
    grid_spec = pltpu.PrefetchScalarGridSpec(
        num_scalar_prefetch=8,
        grid=(nt,),
        in_specs=[xa, xb, pl.BlockSpec((1, 2 * TOP_K, tt), lambda i, *_: (i, 0, 0))],
        out_specs=[pl.BlockSpec(memory_space=pl.ANY),
                   pl.BlockSpec((1, 8, tt), lambda i, *_: (i, 0, 0))],
        scratch_shapes=[pltpu.VMEM((2, local_rows, d), BF16), pltpu.SemaphoreType.DMA((2,))],
    )
    return pl.pallas_call(
        functools.partial(_dispatch_kernel, n_experts=e, top_k=TOP_K, nt0=nt0),
        grid_spec=grid_spec,
        out_shape=[jax.ShapeDtypeStruct((n_rows, d), BF16), jax.ShapeDtypeStruct((nt, 8, tt), I32)],
        compiler_params=_params(1, "arbitrary"),
        name="moe_dispatch",
    )(loff, seg, *pieces, plen, poff, n_blocks, x1s[0], x1s[1], idx)


def _combine_kernel(loff_s, seg_s, lo_t, go_t, cnt_t, ys_ref, pos_ref, gate_ref, xa_ref, xb_ref, g2_ref, b2_ref,
                    oa_ref, ob_ref, buf, sems, *, n_experts, top_k, alpha, nt0):
    j = pl.program_id(0)
    nt = pl.num_programs(0)
    slot = j % 2
    tt = xa_ref.shape[0]
    lr = buf.shape[1]

    def prefetch(step, s):
        buf[s] = jnp.zeros((lr, buf.shape[2]), BF16)
        _piece_copies(step, lo_t, go_t, cnt_t, buf.at[s], ys_ref, sems.at[s], False)

    @pl.when(j == 0)
    def _():
        prefetch(0, 0)

    @pl.when(j + 1 < nt)
    def _():
        prefetch(j + 1, 1 - slot)

    _wait_tile(j, n_experts, loff_s, seg_s, buf.at[slot], ys_ref, sems.at[slot], False)
    pos = pos_ref[0].astype(jnp.int16)
    gate = gate_ref[0].astype(BF16)
    rows = lax.broadcasted_iota(jnp.int16, (lr, tt), 0)
    wsel = jnp.zeros((lr, tt), BF16)
    for k in range(top_k):
        wsel = jnp.where(rows == pos[k:k + 1], gate[k:k + 1], wsel)
    moe = lax.dot_general(wsel, buf[slot], (((0,), (0,)), ((), ())), preferred_element_type=F32)
    x1 = jnp.where(j < nt0, xa_ref[...], xb_ref[...])
    out = _layer_norm(alpha * x1 + moe, g2_ref[...], b2_ref[...])

    @pl.when(j < nt0)
    def _():
        oa_ref[...] = out

    @pl.when(j >= nt0)
    def _():
        ob_ref[...] = out


def _combine(ys, pos_tm, gate_tm, x1s, g2, b2, loff, seg, pieces, local_rows, alpha):
    d = x1s[0].shape[1]
    tt = TOKEN_TILE
    nt0 = x1s[0].shape[0] // tt
    nt = nt0 + x1s[1].shape[0] // tt
    e = loff.shape[0] // nt
    xa, xb = _dual_specs((tt, d), nt0)
    grid_spec = pltpu.PrefetchScalarGridSpec(
        num_scalar_prefetch=5,
        grid=(nt,),
        in_specs=[pl.BlockSpec(memory_space=pl.ANY),
                  pl.BlockSpec((1, 8, tt), lambda i, *_: (i, 0, 0)),
                  pl.BlockSpec((1, 8, tt), lambda i, *_: (i, 0, 0)),
                  xa, xb,
                  pl.BlockSpec((1, d), lambda i, *_: (0, 0)),
                  pl.BlockSpec((1, d), lambda i, *_: (0, 0))],
        out_specs=list(_dual_specs((tt, d), nt0)),
        scratch_shapes=[pltpu.VMEM((2, local_rows, d), BF16), pltpu.SemaphoreType.DMA((2,))],
    )
    return pl.pallas_call(
        functools.partial(_combine_kernel, n_experts=e, top_k=TOP_K, alpha=alpha, nt0=nt0),
        grid_spec=grid_spec,
        out_shape=[jax.ShapeDtypeStruct(x.shape, F32) for x in x1s],
        compiler_params=_params(1, "arbitrary"),
        name="moe_combine",
    )(loff, seg, *pieces, ys, pos_tm, gate_tm, x1s[0], x1s[1], g2, b2)


def _expert_kernel(be_s, nb_s, xs_ref, wgu_ref, bgu_ref, wd_ref, bd_ref, ys_ref, wgu_bf, wd_bf):
    b = pl.program_id(0)
    active = b < nb_s[0]
    fresh = jnp.logical_or(b == 0, be_s[b] != be_s[jnp.maximum(b - 1, 0)])

    @pl.when(jnp.logical_and(active, fresh))
    def _():
        wgu_bf[...] = wgu_ref[0].astype(BF16)
        wd_bf[...] = wd_ref[0].astype(BF16)

    @pl.when(active)
    def _():
        dff = wd_ref.shape[1]
        gu = _dot(xs_ref[...], wgu_bf[...]) + bgu_ref[0]
        gate = jnp.minimum(gu[:, :dff], SWIGLU_LIMIT)
        up = jnp.clip(gu[:, dff:], -SWIGLU_LIMIT, SWIGLU_LIMIT)
        h = (up + 1.0) * (gate * jax.nn.sigmoid(SWIGLU_ALPHA * gate))
        ys_ref[...] = (_dot(h.astype(BF16), wd_bf[...]) + bd_ref[0]).astype(BF16)

    @pl.when(jnp.logical_not(active))
    def _():
        ys_ref[...] = jnp.zeros_like(ys_ref)


def _experts(xs, block_expert, n_blocks, wgu, bgu, wd, bdn):
    n_rows, d = xs.shape
    bm = EXPERT_BLOCK
    f2 = wgu.shape[2]
    dff = wd.shape[1]
    blk = lambda b, be, nb: (jnp.minimum(b, nb[0] - 1), 0)
    wsel = lambda b, be, nb: (be[b], 0, 0)
    grid_spec = pltpu.PrefetchScalarGridSpec(
        num_scalar_prefetch=2,
        grid=(n_rows // bm,),
        in_specs=[pl.BlockSpec((bm, d), blk),
                  pl.BlockSpec((1, d, f2), wsel), pl.BlockSpec((1, 1, f2), wsel),
                  pl.BlockSpec((1, dff, d), wsel), pl.BlockSpec((1, 1, d), wsel)],
        out_specs=pl.BlockSpec((bm, d), lambda b, be, nb: (b, 0)),
        scratch_shapes=[pltpu.VMEM((d, f2), BF16), pltpu.VMEM((dff, d), BF16)],
    )
    return pl.pallas_call(
        _expert_kernel,
        grid_spec=grid_spec,
        out_shape=jax.ShapeDtypeStruct((n_rows, d), BF16),
        compiler_params=_params(1, "arbitrary"),
        name="moe_experts",
    )(block_expert, n_blocks, xs, wgu, bgu, wd, bdn)


def _routed_moe(routed, w_gate_up, b_gate_up, w_down, b_down, g2, b2, alpha):
    x1s = [r[0] for r in routed]
    idx, gate, cnt = (jnp.concatenate([r[i] for r in routed], axis=0) for i in (1, 2, 3))
    t = x1s[0].shape[0] + x1s[1].shape[0]
    e = cnt.shape[1]
    tt = TOKEN_TILE
    nt = t // tt
    bm = EXPERT_BLOCK

    cnt = cnt.reshape(nt, e)
    seg = (cnt + SEG_ALIGN - 1) // SEG_ALIGN * SEG_ALIGN
    loff = jnp.cumsum(seg, axis=1) - seg
    per_expert = jnp.sum(seg, axis=0)
    padded = (per_expert + bm - 1) // bm * bm
    blocks_end = jnp.cumsum(padded) // bm
    start = jnp.cumsum(padded) - padded
    goff = start[None, :] + jnp.cumsum(seg, axis=0) - seg
    local_rows = TOP_K * tt + e * SEG_ALIGN
    n_rows_max = -(-(TOP_K * t + (SEG_ALIGN - 1) * e * nt) // bm) * bm + e * bm
    n_blocks = blocks_end[-1:].astype(I32)
    bidx = jnp.minimum(jnp.arange(n_rows_max // bm, dtype=I32), n_blocks[0] - 1)
    block_expert = jnp.minimum(jnp.sum(bidx[:, None] >= blocks_end[None, :], axis=1), e - 1).astype(I32)
    flat = lambda a: a.reshape(-1).astype(I32)
    plen, poff = flat(padded - per_expert), flat(start + per_expert)

    assert e <= PIECE_CAP and local_rows // SEG_CHUNK <= PIECE_CAP
    pieces = _piece_tables(seg, loff, goff)
    xs, pos = _dispatch(x1s, idx, flat(loff), flat(seg), pieces, plen, poff, n_blocks, n_rows_max,
                        local_rows)
    ys = _experts(xs, block_expert, n_blocks, w_gate_up, b_gate_up.reshape(e, 1, -1),
                  w_down, b_down.reshape(e, 1, -1))
    return _combine(ys, pos, gate, x1s, g2, b2, flat(loff), flat(seg), pieces, local_rows, alpha)


def _mixer(x, apply_ln, lng, lnb, lay, consts, alpha):
    batch, seq, d = x.shape
    t = batch * seq
    fw = lay["g_fourier"].shape[1]
    n2 = MINOR
    x2d = x.reshape(t, d)
    z, x0, w = _in_proj(x2d, batch, seq, fw, lng, lnb, lay["w_in"], lay["b_in"], consts["bd"],
                        lay["w_short"], lay["b_short"], apply_ln)
    c = x0.shape[1]

    n1 = seq // n2
    g = jnp.asarray(_major_matrix(n1, n1, n1, -1)).astype(BF16)
    y = _dft_major(g, z.reshape(batch, n1, n2, fw), True)
    ortho = 1.0 / math.sqrt(seq * FOURIER_GROUP_DIM)
    wf_real = jnp.asarray(_minor_matrix(n2, -1)[:n2] * ortho).astype(BF16)
    yf = _dft_minor(y, wf_real, "real")

    n = 2 * seq
    m1 = n // n2
    kf, asum = _hyena_filter(seq, consts["bands"], lay["filt_w1"], lay["filt_b1"], lay["filt_w2"],
                             lay["filt_b2"], lay["filt_w3"], lay["filt_b3"], lay["filt_w_out"],
                             lay["filt_freq"], lay["filt_decay"])
    gk = jnp.asarray(_major_matrix(m1, m1, m1, -1, real_in=True)).astype(BF16)
    ky = _dft_major(gk, kf.reshape(1, m1, n2, c), True)
    gfw = jnp.asarray(_major_matrix(m1, m1 // 2, m1, -1)).astype(BF16)
    gin = jnp.asarray(_major_matrix(m1, m1, m1 // 2, +1)).astype(BF16)
    pairs = batch // 2
    u = _dft_major(gfw, w.reshape(pairs, m1, n2, c), True)
    wi = jnp.asarray(_minor_matrix(n2, +1) * (1.0 / n)).astype(BF16)
    v = _dft_minor(u, consts["wf"], "conv", wi_bf=wi, kf4=ky)
    yc = _dft_major(gin, v, False)

    inv_norm = 1.0 / asum
    return _out_proj(yf.reshape(t, fw), yc.reshape(t, c), x0, w, x2d, lng, lnb, inv_norm,
                     lay["hyena_skip"], lay["g_fourier"], lay["g_hyena"], lay["w_out"], lay["b_out"],
                     lay["ln1_g"], lay["ln1_b"], lay["w_router_t"], lay["b_router"], alpha, apply_ln)


def kernel(x_prompt, x_sample, ln_in_g, ln_in_b, w_in, b_in, w_short, b_short, filt_w1, filt_b1, filt_w2, filt_b2, filt_w3, filt_b3, filt_w_out, filt_freq, filt_decay, hyena_skip, g_fourier, g_hyena, w_out, b_out, ln1_g, ln1_b, w_router, b_router, w_gate_up, b_gate_up, w_down, b_down, ln2_g, ln2_b):
    depth, d, in_width = w_in.shape
    fw = g_fourier.shape[1]
    alpha = (2.0 * depth) ** 0.25
    row = lambda a: a.reshape(1, -1)

    gd = FOURIER_GROUP_DIM
    cg, sg = _cs(np.outer(np.arange(gd), np.arange(gd)), gd)
    eye = np.eye(LANES // gd)
    bd = np.concatenate([np.kron(eye, cg), -np.kron(eye, sg)], axis=1).astype(np.float32)
    bands = jnp.linspace(1e-4, FILTER_BANDS - 1, FILTER_BANDS, dtype=F32)
    bands_row = jnp.zeros((1, LANES), F32).at[0, 1:1 + FILTER_BANDS].set(bands)
    bands_row = bands_row.at[0, 1 + FILTER_BANDS:1 + 2 * FILTER_BANDS].set(bands)
    consts = dict(bd=jnp.asarray(bd).astype(BF16), bands=bands_row,
                  wf=jnp.asarray(_minor_matrix(MINOR, -1)).astype(BF16))

    xs = [x_prompt, x_sample]
    for l in range(depth):
        emb = filt_w1.shape[1]
        twice = lambda a: jnp.concatenate([row(a), row(a)], axis=1)
        blockdiag = lambda a: jnp.kron(jnp.eye(2, dtype=F32), a)
        w1_pad = jnp.zeros((LANES, filt_w1.shape[2]), F32).at[:emb].set(filt_w1[l])
        wl = filt_w_out[l]
        wl_halves = jnp.stack([jnp.concatenate([wl, jnp.zeros_like(wl)], axis=0),
                               jnp.concatenate([jnp.zeros_like(wl), wl], axis=0)])
        lay = dict(
            w_in=w_in[l].astype(BF16), b_in=row(b_in[l]), w_short=w_short[l], b_short=row(b_short[l]),
            filt_w1=blockdiag(w1_pad),
            filt_b1=twice(filt_b1[l]), filt_w2=blockdiag(filt_w2[l]), filt_b2=twice(filt_b2[l]),
            filt_w3=blockdiag(filt_w3[l]),
            filt_b3=twice(filt_b3[l]), filt_w_out=wl_halves, filt_freq=twice(filt_freq[l]),
            filt_decay=filt_decay[l], hyena_skip=row(hyena_skip[l]), g_fourier=row(g_fourier[l]),
            g_hyena=row(g_hyena[l]), w_out=w_out[l].astype(BF16), b_out=row(b_out[l]),
            ln1_g=row(ln1_g[l]), ln1_b=row(ln1_b[l]),
            w_router_t=w_router[l].T, b_router=b_router[l].reshape(-1, 1))
        routed = [_mixer(x, l == 0, row(ln_in_g), row(ln_in_b), lay, consts, alpha) for x in xs]
        outs = _routed_moe(routed, w_gate_up[l], b_gate_up[l], w_down[l], b_down[l],
                           row(ln2_g[l]), row(ln2_b[l]), alpha)
        xs = [o.reshape(x.shape) for o, x in zip(outs, xs)]
    return (xs[0], xs[1])
```

```python
import functools
import math

import numpy as np
import jax
import jax.numpy as jnp
from jax import lax
from jax.experimental import pallas as pl
from jax.experimental.pallas import tpu as pltpu

F32 = jnp.float32
BF16 = jnp.bfloat16
I32 = jnp.int32
U32 = jnp.uint32

LN_EPS = 1e-5
RMS_EPS = 1e-6
SWIGLU_LIMIT = 7.0
SWIGLU_ALPHA = 1.702
TOP_K = 4
FOURIER_GROUP_DIM = 64
FILTER_BANDS = 16

LANES = 128
MINOR = 128
ROW_TILE = 512
IN_TILE = 512
IN_SPLIT = 2
MAJOR_BLOCK_BYTES = 12 * 1024 * 1024
TOKEN_TILE = 512
SEG_ALIGN = 16
EXPERT_BLOCK = 512
VMEM_LIMIT = 56 * 1024 * 1024


def _params(n_grid, semantics="parallel"):
    return pltpu.CompilerParams(dimension_semantics=(semantics,) * n_grid,
                                vmem_limit_bytes=VMEM_LIMIT)


def _dot(a, b):
    return jnp.dot(a, b, preferred_element_type=F32)


def _split(a):
    hi = a.astype(BF16)
    lo = (a - hi.astype(F32)).astype(BF16)
    return hi, lo


def _dot3(a, b):
    ah, al = _split(a)
    bh, bl = _split(b)
    return _dot(ah, bh) + _dot(ah, bl) + _dot(al, bh)


def _layer_norm(x, g, b):
    mu = jnp.mean(x, axis=-1, keepdims=True)
    xc = x - mu
    var = jnp.mean(xc * xc, axis=-1, keepdims=True)
    return xc * lax.rsqrt(var + LN_EPS) * g + b


def _rms_norm(x, g):
    return x * lax.rsqrt(jnp.mean(x * x, axis=-1, keepdims=True) + RMS_EPS) * g


def _in_kernel(x_ref, xp_ref, xq_ref, lng_ref, lnb_ref, win_ref, bin_ref, bd_ref, wsh_ref, bsh_ref,
               z_ref, x0_ref, w_ref, *, tiles_per_seq, apply_ln, fw):
    it = pl.program_id(0) % tiles_per_seq
    tm = x_ref.shape[0]

    def norm(x):
        return _layer_norm(x, lng_ref[...], lnb_ref[...]) if apply_ln else x

    halo = norm(jnp.concatenate([xp_ref[...], xq_ref[...]], axis=0))
    projh = _dot(halo.astype(BF16), win_ref[:, fw:]) + bin_ref[:, fw:]
    rp = tm // IN_SPLIT
    projs = [_dot(norm(x_ref[r * rp:(r + 1) * rp, :]).astype(BF16), win_ref[...]) + bin_ref[...]
             for r in range(IN_SPLIT)]
    wsh = wsh_ref[...]
    for r, proj in enumerate(projs):
        sl = slice(r * rp, (r + 1) * rp)
        pf = proj[:, :fw].astype(BF16)
        zs = [_dot(pf[:, c0:c0 + LANES], bd_ref[...]) for c0 in range(0, fw, LANES)]
        z_ref[0, sl, :] = _pack_pair(jnp.concatenate([z[:, :LANES] for z in zs], axis=1),
                                     jnp.concatenate([z[:, LANES:] for z in zs], axis=1))

        ph = proj[:, fw:]
        prev_row = projs[r - 1][rp - 1:rp, fw:] if r > 0 else jnp.where(it > 0, projh[7:8], 0.0)
        next_row = (projs[r + 1][0:1, fw:] if r + 1 < IN_SPLIT
                    else jnp.where(it < tiles_per_seq - 1, projh[8:9], 0.0))
        rows = lax.broadcasted_iota(I32, ph.shape, 0)
        up = jnp.where(rows == 0, prev_row, pltpu.roll(ph, 1, 0))
        dn = jnp.where(rows == rp - 1, next_row, pltpu.roll(ph, rp - 1, 0))
        uc = up * wsh[0:1] + ph * wsh[1:2] + dn * wsh[2:3] + bsh_ref[...]
        hw = uc.shape[1] // 3
        x0_ref[sl, :] = uc[:, :hw]
        w_ref[sl, :] = uc[:, 2 * hw:] * uc[:, hw:2 * hw]


def _in_proj(x2d, batch, seq, fw, lng, lnb, win_bf, b_in, bd_bf, w_short, b_short, apply_ln):
    t, d = x2d.shape
    inw = win_bf.shape[1]
    hw = (inw - fw) // 3
    tm = IN_TILE
    tps = seq // tm
    r8 = tm // 8
    nblk8 = t // 8
    const = lambda i: (0, 0)
    kern = functools.partial(_in_kernel, tiles_per_seq=tps, apply_ln=apply_ln, fw=fw)
    return pl.pallas_call(
        kern,
        grid=(t // tm,),
        in_specs=[
            pl.BlockSpec((tm, d), lambda i: (i, 0)),
            pl.BlockSpec((8, d), lambda i: (jnp.maximum(i * r8 - 1, 0), 0)),
            pl.BlockSpec((8, d), lambda i: (jnp.minimum((i + 1) * r8, nblk8 - 1), 0)),
            pl.BlockSpec((1, d), const), pl.BlockSpec((1, d), const),
            pl.BlockSpec((d, inw), const), pl.BlockSpec((1, inw), const),
            pl.BlockSpec(bd_bf.shape, const),
            pl.BlockSpec((3, 3 * hw), const), pl.BlockSpec((1, 3 * hw), const),
        ],
        out_specs=[
            pl.BlockSpec((1, tm, fw), lambda i: (i // tps, i % tps, 0)),
            pl.BlockSpec((tm, hw), lambda i: (i, 0)),
            pl.BlockSpec((tm, hw), lambda i: (i, 0)),
        ],
        out_shape=[
            jax.ShapeDtypeStruct((batch, seq, fw), U32),
            jax.ShapeDtypeStruct((t, hw), F32),
            jax.ShapeDtypeStruct((t, hw), F32),
        ],
        compiler_params=_params(1),
        name="in_proj",
    )(x2d, x2d, x2d, lng, lnb, win_bf, b_in, bd_bf, w_short, b_short)


FILTER_HALO = 16


def _filter_kernel(bands_ref, w1_ref, b1_ref, w2_ref, b2_ref, w3_ref, b3_ref, wl_ref, fr_ref, dec_ref,
                   k_ref, asum_ref, tc_ref, ts_ref, kf_buf, kb_buf, sems, *, seq):
    i = pl.program_id(0)
    nsteps = pl.num_programs(0)
    tile, c = kf_buf.shape
    ext = tc_ref.shape[0]
    rep = c // LANES
    w_unit = 2.0 * math.pi / seq
    bands = bands_ref[...]

    @pl.when(i == 0)
    def _():
        arg = bands * (w_unit * lax.broadcasted_iota(I32, (ext, LANES), 0).astype(F32))
        tc_ref[...] = jnp.cos(arg)
        ts_ref[...] = jnp.sin(arg)
        asum_ref[...] = jnp.zeros_like(asum_ref)

    j0 = i * tile
    base = bands * (w_unit * j0.astype(F32))
    c0, s0 = jnp.cos(base), jnp.sin(base)
    tc, ts = tc_ref[...], ts_ref[...]
    cosf = c0 * tc - s0 * ts
    sinf = s0 * tc + c0 * ts
    j = j0 + lax.broadcasted_iota(I32, (ext, LANES), 0)
    t = j.astype(F32) * (1.0 / (seq - 1))
    lane = lax.broadcasted_iota(I32, (ext, LANES), 1)
    feat = jnp.where(lane == 0, t,
                     jnp.where(lane <= FILTER_BANDS, cosf,
                               jnp.where(lane <= 2 * FILTER_BANDS, -sinf, 0.0)))
    half = ext // 2
    fr = fr_ref[...]
    h = jnp.concatenate([feat[:half], feat[half:]], axis=1)
    h = jnp.sin(fr * (_dot3(h, w1_ref[...]) + b1_ref[...]))
    h = jnp.sin(fr * (_dot3(h, w2_ref[...]) + b2_ref[...]))
    h = jnp.sin(fr * (_dot3(h, w3_ref[...]) + b3_ref[...]))
    hl = jnp.concatenate([_dot3(h, wl_ref[0]), _dot3(h, wl_ref[1])], axis=0)
    t4 = jnp.concatenate([t] * rep, axis=1)
    j4 = jnp.concatenate([j] * rep, axis=1)
    dec = jnp.abs(dec_ref[...])
    hf = hl[:, :c] * jnp.exp(-t4 * dec[0:1])
    hb = hl[:, c:] * jnp.exp(-t4 * dec[1:2])
    hb = jnp.where(j4 < seq, hb, 0.0)
    asum_ref[...] += jnp.sum(jnp.abs(hf[:tile]) + jnp.abs(hb[:tile]), axis=0, keepdims=True)

    rev = (lax.broadcasted_iota(I32, (tile, ext), 1) == tile - lax.broadcasted_iota(I32, (tile, ext), 0))
    rev = jnp.where(rev, 1.0, 0.0).astype(BF16)
    b1 = hb.astype(BF16)
    r1 = hb - b1.astype(F32)
    b2 = r1.astype(BF16)
    b3 = (r1 - b2.astype(F32)).astype(BF16)
    kb = _dot(rev, b1) + _dot(rev, b2) + _dot(rev, b3)

    def copies():
        return (pltpu.make_async_copy(kf_buf, k_ref.at[pl.ds(pl.multiple_of(j0, tile), tile)], sems.at[0]),
                pltpu.make_async_copy(kb_buf, k_ref.at[pl.ds(pl.multiple_of(2 * seq - j0 - tile, tile), tile)],
                                      sems.at[1]))

    @pl.when(i > 0)
    def _():
        for cp in copies():
            cp.wait()

    kf_buf[...] = jnp.where(j4[:tile] == 0, hf[:tile] + hb[:tile], hf[:tile])
    kb_buf[...] = kb
    for cp in copies():
        cp.start()

    @pl.when(i == nsteps - 1)
    def _():
        for cp in copies():
            cp.wait()


def _hyena_filter(seq, bands_row, w1p, b1, w2, b2, w3, b3, wl, freq, decay):
    c = decay.shape[1]
    tile = ROW_TILE
    ext = tile + FILTER_HALO
    full = lambda a: pl.BlockSpec(a.shape, lambda i: (0,) * a.ndim)
    args = (bands_row, w1p, b1, w2, b2, w3, b3, wl, freq, decay)
    return pl.pallas_call(
        functools.partial(_filter_kernel, seq=seq),
        grid=(seq // tile,),
        in_specs=[full(a) for a in args],
        out_specs=[pl.BlockSpec(memory_space=pl.ANY), pl.BlockSpec((1, c), lambda i: (0, 0))],
        out_shape=[jax.ShapeDtypeStruct((2 * seq, c), F32), jax.ShapeDtypeStruct((1, c), F32)],
        scratch_shapes=[pltpu.VMEM((ext, LANES), F32), pltpu.VMEM((ext, LANES), F32),
                        pltpu.VMEM((tile, c), F32), pltpu.VMEM((tile, c), F32),
                        pltpu.SemaphoreType.DMA((2,))],
        compiler_params=_params(1, "arbitrary"),
        name="hyena_filter",
    )(*args)


def _pack_pair(re, im):
    rb = lax.bitcast_convert_type(re.astype(BF16).astype(F32), U32)
    ib = lax.bitcast_convert_type(im.astype(BF16).astype(F32), U32)
    return (rb >> 16) | ib


def _unpack_pair(word):
    re = lax.bitcast_convert_type(word << 16, F32)
    im = lax.bitcast_convert_type(word & jnp.uint32(0xFFFF0000), F32)
    return re, im


def _major_compute(g, xbuf, obuf, slot):
    _, n_ct, bsz, k, sub, lanes = xbuf.shape
    m = obuf.shape[3]
    x2 = xbuf.reshape(2 * n_ct * bsz * k * sub, lanes)
    o2 = obuf.reshape(2 * n_ct * bsz * m * sub, lanes)
    for t in range(n_ct * bsz):
        xbase = (slot * n_ct * bsz + t) * k * sub
        obase = (slot * n_ct * bsz + t) * m * sub
        for b in range(sub):
            xb = x2[pl.ds(xbase + b, k, stride=sub), :]
            if xb.dtype == U32:
                xb = jnp.concatenate(_unpack_pair(xb), axis=0)
            res = _dot(g, xb.astype(BF16))
            if obuf.dtype == U32:
                res = _pack_pair(res[:m], res[m:])
            o2[pl.ds(obase + b, m, stride=sub), :] = res


def _major_kernel(g_ref, x_hbm, o_hbm, xbuf, obuf, isem, osem, *, nj):
    s = pl.program_id(0)
    ns = pl.num_programs(0)
    slot = s % 2
    _, n_ct, bsz, k, sub, lanes = xbuf.shape
    m = obuf.shape[3]

    def copies(step, sl, hbm, buf, sem, inbound):
        i = step // nj
        j = step % nj
        out = []
        for ct in range(n_ct):
            tile = hbm.at[pl.ds(i * bsz, bsz), :, pl.ds(pl.multiple_of(j * sub, sub), sub),
                          pl.ds(ct * lanes, lanes)]
            out.append(pltpu.make_async_copy(tile, buf.at[sl, ct], sem.at[sl]) if inbound
                       else pltpu.make_async_copy(buf.at[sl, ct], tile, sem.at[sl]))
        return out

    def start(cps):
        for cp in cps:
            cp.start()

    def wait(cps):
        for cp in cps:
            cp.wait()

    @pl.when(s == 0)
    def _():
        start(copies(0, 0, x_hbm, xbuf, isem, True))

    @pl.when(s + 1 < ns)
    def _():
        start(copies(s + 1, 1 - slot, x_hbm, xbuf, isem, True))

    wait(copies(s, slot, x_hbm, xbuf, isem, True))

    @pl.when(s >= 2)
    def _():
        wait(copies(s - 2, slot, o_hbm, obuf, osem, False))

    _major_compute(g_ref[...], xbuf, obuf, slot)
    start(copies(s, slot, o_hbm, obuf, osem, False))

    @pl.when(s == ns - 1)
    def _():
        wait(copies(s, slot, o_hbm, obuf, osem, False))

        @pl.when(s >= 1)
        def _():
            wait(copies(s - 1, 1 - slot, o_hbm, obuf, osem, False))


def _dft_major(g_bf, x4, pack_out):
    b, k, n2, c = x4.shape
    m = g_bf.shape[0] // 2 if pack_out else g_bf.shape[0]
    out_dtype = U32 if pack_out else F32
    sub = 8
    n_ct = c // LANES
    bsz = max(1, min(b, MAJOR_BLOCK_BYTES // ((k + m) * sub * c * 4)))
    while b % bsz:
        bsz -= 1
    nj = n2 // sub
    return pl.pallas_call(
        functools.partial(_major_kernel, nj=nj),
        grid=((b // bsz) * nj,),
        in_specs=[pl.BlockSpec(g_bf.shape, lambda s: (0, 0)), pl.BlockSpec(memory_space=pl.ANY)],
        out_specs=pl.BlockSpec(memory_space=pl.ANY),
        out_shape=jax.ShapeDtypeStruct((b, m, n2, c), out_dtype),
        scratch_shapes=[pltpu.VMEM((2, n_ct, bsz, k, sub, LANES), x4.dtype),
                        pltpu.VMEM((2, n_ct, bsz, m, sub, LANES), out_dtype),
                        pltpu.SemaphoreType.DMA((2,)), pltpu.SemaphoreType.DMA((2,))],
        compiler_params=_params(1, "arbitrary"),
        name="dft_major",
    )(g_bf, x4)


def _minor_kernel(*refs, mode, k1b):
    if mode == "conv":
        y_ref, twb_ref, twr_ref, wf_ref, wi_ref, kf_ref, o_ref, ks_ref = refs
    else:
        y_ref, twb_ref, twr_ref, wf_ref, o_ref = refs
    n2 = y_ref.shape[2]
    ct = y_ref.shape[3]
    rep = ct // LANES

    def forward(packed, cs, sn, w):
        re, im = _unpack_pair(packed)
        return _dot(w, jnp.concatenate([re * cs + im * sn, im * cs - re * sn], axis=0).astype(BF16))

    cb, sb = twb_ref[0, 0], twb_ref[1, 0]
    for j in range(k1b):
        cr, sr = twr_ref[0, j], twr_ref[1, j]
        cs = jnp.concatenate([cb * cr - sb * sr] * rep, axis=1)
        sn = jnp.concatenate([sb * cr + cb * sr] * rep, axis=1)
        if mode == "real":
            o_ref[0, :, j, :] = forward(y_ref[0, j], cs, sn, wf_ref[...])
            continue

        @pl.when(pl.program_id(1) == 0)
        def _():
            ks_ref[j] = forward(kf_ref[0, j], cs, sn, wf_ref[...])

        z = forward(y_ref[0, j], cs, sn, wf_ref[...])
        zr, zi = z[:n2], z[n2:]
        kr = ks_ref[j, :n2]
        ki = ks_ref[j, n2:]
        p = jnp.concatenate([zr * kr - zi * ki, zr * ki + zi * kr], axis=0).astype(BF16)
        q = _dot(wi_ref[...], p)
        qr, qi = q[:n2], q[n2:]
        o_ref[0, j] = _pack_pair(qr * cs - qi * sn, qi * cs + qr * sn)


def _dft_minor(y4, wf_bf, mode, wi_bf=None, kf4=None):
    b, n1, n2, c = y4.shape
    k1b = min(8, n1)
    ct = c
    tw_block, tw_rem = _twiddles(n1, n2, k1b)
    grid = (n1 // k1b, b)
    dspec = pl.BlockSpec((1, k1b, n2, ct), lambda j, i: (i, j, 0, 0))
    in_specs = [dspec,
                pl.BlockSpec((2, 1, n2, LANES), lambda j, i: (0, j, 0, 0)),
                pl.BlockSpec((2, k1b, n2, LANES), lambda j, i: (0, 0, 0, 0)),
                pl.BlockSpec(wf_bf.shape, lambda j, i: (0, 0))]
    args = [y4, tw_block, tw_rem, wf_bf]
    scratch = []
    if mode == "conv":
        in_specs += [pl.BlockSpec(wi_bf.shape, lambda j, i: (0, 0)),
                     pl.BlockSpec((1, k1b, n2, ct), lambda j, i: (0, j, 0, 0))]
        args += [wi_bf, kf4]
        scratch = [pltpu.VMEM((k1b, 2 * n2, ct), F32)]
    if mode == "real":
        out_spec = pl.BlockSpec((1, n2, k1b, ct), lambda j, i: (i, 0, j, 0))
        out_shape = jax.ShapeDtypeStruct((b, n2, n1, c), F32)
    else:
        out_spec = dspec
        out_shape = jax.ShapeDtypeStruct((b, n1, n2, c), U32)
    return pl.pallas_call(
        functools.partial(_minor_kernel, mode=mode, k1b=k1b),
        grid=grid, in_specs=in_specs, out_specs=out_spec, out_shape=out_shape,
        scratch_shapes=scratch,
        compiler_params=pltpu.CompilerParams(dimension_semantics=("parallel", "arbitrary"),
                                             vmem_limit_bytes=VMEM_LIMIT),
        name="dft_minor_" + mode,
    )(*args)


def _cs(num, den):
    ang = 2.0 * np.pi * (num % den) / den
    return np.cos(ang), np.sin(ang)


def _complex_block(c, s, sign):
    return np.block([[c, -sign * s], [sign * s, c]]).astype(np.float32)


def _major_matrix(n1, a_in, a_out, sign, real_in=False):
    c, s = _cs(np.outer(np.arange(a_out), np.arange(a_in)), n1)
    g = _complex_block(c, s, sign)
    return g[:, :a_in] if real_in else g


def _minor_matrix(n2, sign):
    c, s = _cs(np.outer(np.arange(n2), np.arange(n2)), n2)
    return _complex_block(c, s, sign)


def _twiddles(n1, n2, k1b):
    def table(k1):
        c, s = _cs(np.outer(k1, np.arange(n2)), n1 * n2)
        tw = jnp.asarray(np.stack([c, s]).astype(np.float32))
        return jnp.broadcast_to(tw[..., None], (2, len(k1), n2, LANES))
    return table(np.arange(0, n1, k1b)), table(np.arange(k1b))


def _out_kernel(yf_ref, yc_ref, x0_ref, w_ref, x_ref, lng_ref, lnb_ref, inv_ref, skip_ref, gf_ref, gh_ref,
                wo_ref, bo_ref, g1_ref, b1_ref, wr_ref, br_ref, x1_ref, idx_ref, gate_ref, cnt_ref,
                *, alpha, apply_ln, top_k):
    c = yc_ref.shape[1]
    w = w_ref[...]
    yh = x0_ref[...] * (yc_ref[...] * inv_ref[...] + skip_ref[...] * w)
    mf = _rms_norm(yf_ref[...], gf_ref[...]).astype(BF16)
    mh = _rms_norm(yh, gh_ref[...]).astype(BF16)
    m = _dot(mf, wo_ref[:c]) + _dot(mh, wo_ref[c:]) + bo_ref[...]
    xn = _layer_norm(x_ref[...], lng_ref[...], lnb_ref[...]) if apply_ln else x_ref[...]
    x1 = _layer_norm(alpha * xn + m, g1_ref[...], b1_ref[...])
    x1_ref[...] = x1
    _route_tile(x1, wr_ref[...], br_ref[...], idx_ref, gate_ref, cnt_ref, top_k)


def _out_proj(yf, yc, x0, w, x, lng, lnb, inv_norm, skip, gf, gh, wo_bf, bo, g1, b1, wr_t, br_col, alpha,
              apply_ln):
    t, d = x.shape
    c = yc.shape[1]
    e = wr_t.shape[0]
    tm = TOKEN_TILE
    nt = t // tm
    row = lambda i: (i, 0)
    const = lambda i: (0, 0)
    tile3 = lambda i: (i, 0, 0)
    return pl.pallas_call(
        functools.partial(_out_kernel, alpha=alpha, apply_ln=apply_ln, top_k=TOP_K),
        grid=(nt,),
        in_specs=[
            pl.BlockSpec((tm, c), row),
            pl.BlockSpec((tm, c), row), pl.BlockSpec((tm, c), row), pl.BlockSpec((tm, c), row),
            pl.BlockSpec((tm, d), row),
            pl.BlockSpec((1, d), const), pl.BlockSpec((1, d), const),
            pl.BlockSpec((1, c), const), pl.BlockSpec((1, c), const),
            pl.BlockSpec((1, c), const), pl.BlockSpec((1, c), const),
            pl.BlockSpec(wo_bf.shape, const), pl.BlockSpec((1, d), const),
            pl.BlockSpec((1, d), const), pl.BlockSpec((1, d), const),
            pl.BlockSpec((e, d), const), pl.BlockSpec((e, 1), const),
        ],
        out_specs=[pl.BlockSpec((tm, d), row),
                   pl.BlockSpec((1, 2 * TOP_K, tm), tile3),
                   pl.BlockSpec((1, 8, tm), tile3),
                   pl.BlockSpec((1, e, 1), tile3)],
        out_shape=[jax.ShapeDtypeStruct((t, d), F32),
                   jax.ShapeDtypeStruct((nt, 2 * TOP_K, tm), I32),
                   jax.ShapeDtypeStruct((nt, 8, tm), F32),
                   jax.ShapeDtypeStruct((nt, e, 1), I32)],
        compiler_params=_params(1),
        name="out_proj",
    )(yf, yc, x0, w, x, lng, lnb, inv_norm, skip, gf, gh, wo_bf, bo, g1, b1, wr_t, br_col)


def _route_tile(x1, wr, br, idx_ref, gate_ref, cnt_ref, top_k):
    tt = x1.shape[0]
    e = wr.shape[0]
    nt = (((1,), (1,)), ((), ()))
    xh, xl = _split(x1)
    wh, wl = _split(wr)
    logits = (lax.dot_general(wh, xh, nt, preferred_element_type=F32)
              + lax.dot_general(wh, xl, nt, preferred_element_type=F32)
              + lax.dot_general(wl, xh, nt, preferred_element_type=F32)) + br
    iota_e = lax.broadcasted_iota(I32, (e, tt), 0).astype(F32)
    l = logits
    tops, idxs, hots = [], [], []
    for _ in range(top_k):
        m = jnp.max(l, axis=0, keepdims=True)
        idx = jnp.min(jnp.where(l == m, iota_e, float(e)), axis=0, keepdims=True)
        hot = iota_e == idx
        l = jnp.where(hot, -jnp.inf, l)
        tops.append(m)
        idxs.append(idx)
        hots.append(hot)
    ex = [jnp.exp(m - tops[0]) for m in tops]
    den = ex[0]
    for v in ex[1:]:
        den = den + v
    gates = [v / den for v in ex]
    hot_all = jnp.zeros((e, tt), F32)
    for hot in hots:
        hot_all = jnp.where(hot, 1.0, hot_all)
    upper = (lax.broadcasted_iota(I32, (tt, tt), 0) < lax.broadcasted_iota(I32, (tt, tt), 1))
    before = _dot(hot_all.astype(BF16), jnp.where(upper, 1.0, 0.0).astype(BF16))
    ranks = [jnp.sum(jnp.where(hot, before, 0.0), axis=0, keepdims=True).astype(I32) for hot in hots]
    pad = 8 - top_k
    idx_ref[0] = jnp.concatenate([v.astype(I32) for v in idxs] + ranks, axis=0)
    gate_ref[0] = jnp.concatenate(gates + [jnp.zeros((pad, tt), F32)], axis=0)
    cnt_ref[0] = jnp.sum(hot_all, axis=1, keepdims=True).astype(I32)


def _dual_specs(shape, nt0):
    first = pl.BlockSpec(shape, lambda i, *_: (jnp.minimum(i, nt0 - 1),) + (0,) * (len(shape) - 1))
    second = pl.BlockSpec(shape, lambda i, *_: (jnp.maximum(i - nt0, 0),) + (0,) * (len(shape) - 1))
    return first, second


SEG_PIECES = 3
SEG_CHUNK = SEG_ALIGN << SEG_PIECES
PIECE_CLASSES = SEG_PIECES + 1
PIECE_CAP = 32
ZERO_CHUNK = 256


def _segment_copies(first, count, loff_s, len_s, goff_s, local, remote, sem, to_remote, wait):
    def copy(lstart, gstart, size):
        lref = local.at[pl.ds(pl.multiple_of(lstart, SEG_ALIGN), size)]
        rref = remote.at[pl.ds(pl.multiple_of(gstart, SEG_ALIGN), size)]
        cp = (pltpu.make_async_copy(lref, rref, sem) if to_remote
              else pltpu.make_async_copy(rref, lref, sem))
        if wait:
            cp.wait()
        else:
            cp.start()

    def body(e, carry):
        s = first + e
        n = len_s[s] // SEG_ALIGN
        lo = 0 if loff_s is None else loff_s[s]
        go = goff_s[s]
        chunks = n >> SEG_PIECES

        def chunk(i, c):
            copy(lo + i * SEG_CHUNK, go + i * SEG_CHUNK, SEG_CHUNK)
            return c

        lax.fori_loop(0, chunks, chunk, 0)
        rem = n & ((1 << SEG_PIECES) - 1)
        base = chunks * SEG_CHUNK
        for p in reversed(range(SEG_PIECES)):
            start = base + ((rem >> (p + 1)) << (p + 1)) * SEG_ALIGN

            @pl.when(((rem >> p) & 1) == 1)
            def _():
                copy(lo + start, go + start, SEG_ALIGN << p)
        return carry

    lax.fori_loop(0, count, body, 0)


def _piece_copies(step, lo_t, go_t, cnt_t, local, remote, sem, to_remote):
    for c in range(PIECE_CLASSES):
        size = SEG_ALIGN << c
        base = (step * PIECE_CLASSES + c) * PIECE_CAP

        def body(i, carry, size=size, base=base):
            lref = local.at[pl.ds(pl.multiple_of(lo_t[base + i], SEG_ALIGN), size)]
            rref = remote.at[pl.ds(pl.multiple_of(go_t[base + i], SEG_ALIGN), size)]
            cp = (pltpu.make_async_copy(lref, rref, sem) if to_remote
                  else pltpu.make_async_copy(rref, lref, sem))
            cp.start()
            return carry

        lax.fori_loop(0, cnt_t[step * PIECE_CLASSES + c], body, 0)


def _piece_tables(seg, loff, goff):
    nt, e = seg.shape
    n = seg // SEG_ALIGN
    chunks = n >> SEG_PIECES
    rem = n & ((1 << SEG_PIECES) - 1)
    base = chunks * SEG_CHUNK

    def compact(valid, off, lo, go):
        slots = valid.shape[1]
        key = jnp.where(valid, jnp.arange(slots, dtype=I32)[None, :], slots)
        _, lo_sorted, go_sorted = lax.sort((key, lo + off, go + off), dimension=1, num_keys=1)
        return lo_sorted[:, :PIECE_CAP], go_sorted[:, :PIECE_CAP], jnp.sum(valid, axis=1)

    tables = []
    for c in range(SEG_PIECES):
        off = base + ((rem >> (c + 1)) << (c + 1)) * SEG_ALIGN
        tables.append(compact(((rem >> c) & 1) == 1, off, loff, goff))
    max_chunks = TOKEN_TILE // SEG_CHUNK
    i = jnp.arange(max_chunks, dtype=I32)[None, None, :]
    rep = lambda a: jnp.broadcast_to(a[:, :, None], (nt, e, max_chunks)).reshape(nt, e * max_chunks)
    tables.append(compact((i < chunks[:, :, None]).reshape(nt, e * max_chunks),
                          jnp.broadcast_to(i * SEG_CHUNK, (nt, e, max_chunks)).reshape(nt, e * max_chunks),
                          rep(loff), rep(goff)))
    flat = lambda k: jnp.stack([t[k] for t in tables], axis=1).reshape(-1).astype(I32)
    return flat(0), flat(1), flat(2)


def _wait_tile(step, n_experts, loff_s, len_s, local, remote, sem, to_remote):
    last = step * n_experts + n_experts - 1
    n = (loff_s[last] + len_s[last]) // SEG_ALIGN
    for p in range((local.shape[0] // SEG_ALIGN).bit_length()):
        size = SEG_ALIGN << p

        @pl.when(((n >> p) & 1) == 1)
        def _():
            lref = local.at[pl.ds(0, size)]
            rref = remote.at[pl.ds(0, size)]
            cp = (pltpu.make_async_copy(lref, rref, sem) if to_remote
                  else pltpu.make_async_copy(rref, lref, sem))
            cp.wait()


def _dispatch_kernel(loff_s, seg_s, lo_t, go_t, cnt_t, plen_s, poff_s, nb_s, xa_ref, xb_ref, idx_ref, xs_ref,
                     pos_ref,
                     buf, sems, *, n_experts, top_k, nt0):
    j = pl.program_id(0)
    nt = pl.num_programs(0)
    slot = j % 2
    tt = xa_ref.shape[0]
    lr = buf.shape[1]
    idx = idx_ref[0]
    tope = idx[:top_k]
    base = jnp.zeros_like(tope)
    for e in range(n_experts):
        base = jnp.where(tope == e, loff_s[j * n_experts + e], base)
    pos = base + idx[top_k:]
    pos_ref[0] = jnp.concatenate([pos, jnp.zeros((8 - top_k, tt), I32)], axis=0)
    rows = lax.broadcasted_iota(jnp.int16, (lr, tt), 0)
    pos16 = pos.astype(jnp.int16)
    perm = jnp.zeros((lr, tt), BF16)
    for k in range(top_k):
        perm = jnp.where(rows == pos16[k:k + 1], jnp.ones((lr, tt), BF16), perm)
    x = jnp.where(j < nt0, xa_ref[...], xb_ref[...])
    buf[slot] = _dot(perm, x.astype(BF16)).astype(BF16)

    def drained(step, s):
        _wait_tile(step, n_experts, loff_s, seg_s, buf.at[s], xs_ref, sems.at[s], True)

    _piece_copies(j, lo_t, go_t, cnt_t, buf.at[slot], xs_ref, sems.at[slot], True)

    @pl.when(j > 0)
    def _():
        drained(j - 1, 1 - slot)

    @pl.when(j == nt - 1)
    def _():
        drained(j, slot)
        zrows = EXPERT_BLOCK
        buf[0, :zrows] = jnp.zeros((zrows, buf.shape[2]), BF16)
        _segment_copies(0, n_experts, None, plen_s, poff_s, buf.at[0], xs_ref, sems.at[0], True, False)
        _segment_copies(0, n_experts, None, plen_s, poff_s, buf.at[0], xs_ref, sems.at[0], True, True)
        n_tail = xs_ref.shape[0] // zrows - nb_s[0]

        def tail_copy(i):
            dst = xs_ref.at[pl.ds(pl.multiple_of((nb_s[0] + i) * zrows, zrows), zrows)]
            return pltpu.make_async_copy(buf.at[0, pl.ds(0, zrows)], dst, sems.at[0])

        lax.fori_loop(0, n_tail, lambda i, c: (tail_copy(i).start(), c)[1], 0)
        lax.fori_loop(0, n_tail, lambda i, c: (tail_copy(i).wait(), c)[1], 0)


def _dispatch(x1s, idx, loff, seg, pieces, plen, poff, n_blocks, n_rows, local_rows):
    d = x1s[0].shape[1]
    tt = TOKEN_TILE
    nt0 = x1s[0].shape[0] // tt
    nt = nt0 + x1s[1].shape[0] // tt
    e = loff.shape[0] // nt
    xa, xb = _dual_specs((tt, d), nt0)
    grid_spec = pltpu.PrefetchScalarGridSpec(
        num_scalar_prefetch=8,
        grid=(nt,),
        in_specs=[xa, xb, pl.BlockSpec((1, 2 * TOP_K, tt), lambda i, *_: (i, 0, 0))],
        out_specs=[pl.BlockSpec(memory_space=pl.ANY),
                   pl.BlockSpec((1, 8, tt), lambda i, *_: (i, 0, 0))],
        scratch_shapes=[pltpu.VMEM((2, local_rows, d), BF16), pltpu.SemaphoreType.DMA((2,))],
    )
    return pl.pallas_call(
        functools.partial(_dispatch_kernel, n_experts=e, top_k=TOP_K, nt0=nt0),
        grid_spec=grid_spec,
        out_shape=[jax.ShapeDtypeStruct((n_rows, d), BF16), jax.ShapeDtypeStruct((nt, 8, tt), I32)],
        compiler_params=_params(1, "arbitrary"),
        name="moe_dispatch",
    )(loff, seg, *pieces, plen, poff, n_blocks, x1s[0], x1s[1], idx)


def _combine_kernel(loff_s, seg_s, lo_t, go_t, cnt_t, ys_ref, pos_ref, gate_ref, xa_ref, xb_ref, g2_ref, b2_ref,
                    oa_ref, ob_ref, buf, sems, *, n_experts, top_k, alpha, nt0):
    j = pl.program_id(0)
    nt = pl.num_programs(0)
    slot = j % 2
    tt = xa_ref.shape[0]
    lr = buf.shape[1]

    def prefetch(step, s):
        last = step * n_experts + n_experts - 1
        covered = loff_s[last] + seg_s[last]
        for r0 in range(0, lr, ZERO_CHUNK):
            @pl.when(r0 + ZERO_CHUNK > covered)
            def _(r0=r0):
                buf[s, r0:r0 + ZERO_CHUNK] = jnp.zeros((ZERO_CHUNK, buf.shape[2]), BF16)
        _piece_copies(step, lo_t, go_t, cnt_t, buf.at[s], ys_ref, sems.at[s], False)

    @pl.when(j == 0)
    def _():
        prefetch(0, 0)

    @pl.when(j + 1 < nt)
    def _():
        prefetch(j + 1, 1 - slot)

    _wait_tile(j, n_experts, loff_s, seg_s, buf.at[slot], ys_ref, sems.at[slot], False)
    pos = pos_ref[0].astype(jnp.int16)
    gate = gate_ref[0].astype(BF16)
    rows = lax.broadcasted_iota(jnp.int16, (lr, tt), 0)
    wsel = jnp.zeros((lr, tt), BF16)
    for k in range(top_k):
        wsel = jnp.where(rows == pos[k:k + 1], gate[k:k + 1], wsel)
    moe = lax.dot_general(wsel, buf[slot], (((0,), (0,)), ((), ())), preferred_element_type=F32)
    x1 = jnp.where(j < nt0, xa_ref[...], xb_ref[...])
    out = _layer_norm(alpha * x1 + moe, g2_ref[...], b2_ref[...])

    @pl.when(j < nt0)
    def _():
        oa_ref[...] = out

    @pl.when(j >= nt0)
    def _():
        ob_ref[...] = out


def _combine(ys, pos_tm, gate_tm, x1s, g2, b2, loff, seg, pieces, local_rows, alpha):
    d = x1s[0].shape[1]
    tt = TOKEN_TILE
    nt0 = x1s[0].shape[0] // tt
    nt = nt0 + x1s[1].shape[0] // tt
    e = loff.shape[0] // nt
    xa, xb = _dual_specs((tt, d), nt0)
    grid_spec = pltpu.PrefetchScalarGridSpec(
        num_scalar_prefetch=5,
        grid=(nt,),
        in_specs=[pl.BlockSpec(memory_space=pl.ANY),
                  pl.BlockSpec((1, 8, tt), lambda i, *_: (i, 0, 0)),
                  pl.BlockSpec((1, 8, tt), lambda i, *_: (i, 0, 0)),
                  xa, xb,
                  pl.BlockSpec((1, d), lambda i, *_: (0, 0)),
                  pl.BlockSpec((1, d), lambda i, *_: (0, 0))],
        out_specs=list(_dual_specs((tt, d), nt0)),
        scratch_shapes=[pltpu.VMEM((2, local_rows, d), BF16), pltpu.SemaphoreType.DMA((2,))],
    )
    return pl.pallas_call(
        functools.partial(_combine_kernel, n_experts=e, top_k=TOP_K, alpha=alpha, nt0=nt0),
        grid_spec=grid_spec,
        out_shape=[jax.ShapeDtypeStruct(x.shape, F32) for x in x1s],
        compiler_params=_params(1, "arbitrary"),
        name="moe_combine",
    )(loff, seg, *pieces, ys, pos_tm, gate_tm, x1s[0], x1s[1], g2, b2)


def _expert_kernel(be_s, nb_s, xs_ref, wgu_ref, bgu_ref, wd_ref, bd_ref, ys_ref, wgu_bf, wd_bf):
    b = pl.program_id(0)
    active = b < nb_s[0]
    fresh = jnp.logical_or(b == 0, be_s[b] != be_s[jnp.maximum(b - 1, 0)])

    @pl.when(jnp.logical_and(active, fresh))
    def _():
        wgu_bf[...] = wgu_ref[0].astype(BF16)
        wd_bf[...] = wd_ref[0].astype(BF16)

    @pl.when(active)
    def _():
        dff = wd_ref.shape[1]
        gu = _dot(xs_ref[...], wgu_bf[...]) + bgu_ref[0]
        gate = jnp.minimum(gu[:, :dff], SWIGLU_LIMIT)
        up = jnp.clip(gu[:, dff:], -SWIGLU_LIMIT, SWIGLU_LIMIT)
        h = (up + 1.0) * (gate * jax.nn.sigmoid(SWIGLU_ALPHA * gate))
        ys_ref[...] = (_dot(h.astype(BF16), wd_bf[...]) + bd_ref[0]).astype(BF16)

    @pl.when(jnp.logical_not(active))
    def _():
        ys_ref[...] = jnp.zeros_like(ys_ref)


def _experts(xs, block_expert, n_blocks, wgu, bgu, wd, bdn):
    n_rows, d = xs.shape
    bm = EXPERT_BLOCK
    f2 = wgu.shape[2]
    dff = wd.shape[1]
    blk = lambda b, be, nb: (jnp.minimum(b, nb[0] - 1), 0)
    wsel = lambda b, be, nb: (be[b], 0, 0)
    grid_spec = pltpu.PrefetchScalarGridSpec(
        num_scalar_prefetch=2,
        grid=(n_rows // bm,),
        in_specs=[pl.BlockSpec((bm, d), blk),
                  pl.BlockSpec((1, d, f2), wsel), pl.BlockSpec((1, 1, f2), wsel),
                  pl.BlockSpec((1, dff, d), wsel), pl.BlockSpec((1, 1, d), wsel)],
        out_specs=pl.BlockSpec((bm, d), lambda b, be, nb: (b, 0)),
        scratch_shapes=[pltpu.VMEM((d, f2), BF16), pltpu.VMEM((dff, d), BF16)],
    )
    return pl.pallas_call(
        _expert_kernel,
        grid_spec=grid_spec,
        out_shape=jax.ShapeDtypeStruct((n_rows, d), BF16),
        compiler_params=_params(1, "arbitrary"),
        name="moe_experts",
    )(block_expert, n_blocks, xs, wgu, bgu, wd, bdn)


def _routed_moe(routed, w_gate_up, b_gate_up, w_down, b_down, g2, b2, alpha):
    x1s = [r[0] for r in routed]
    idx, gate, cnt = (jnp.concatenate([r[i] for r in routed], axis=0) for i in (1, 2, 3))
    t = x1s[0].shape[0] + x1s[1].shape[0]
    e = cnt.shape[1]
    tt = TOKEN_TILE
    nt = t // tt
    bm = EXPERT_BLOCK

    cnt = cnt.reshape(nt, e)
    seg = (cnt + SEG_ALIGN - 1) // SEG_ALIGN * SEG_ALIGN
    loff = jnp.cumsum(seg, axis=1) - seg
    per_expert = jnp.sum(seg, axis=0)
    padded = (per_expert + bm - 1) // bm * bm
    blocks_end = jnp.cumsum(padded) // bm
    start = jnp.cumsum(padded) - padded
    goff = start[None, :] + jnp.cumsum(seg, axis=0) - seg
    local_rows = TOP_K * tt + e * SEG_ALIGN
    n_rows_max = -(-(TOP_K * t + (SEG_ALIGN - 1) * e * nt) // bm) * bm + e * bm
    n_blocks = blocks_end[-1:].astype(I32)
    bidx = jnp.minimum(jnp.arange(n_rows_max // bm, dtype=I32), n_blocks[0] - 1)
    block_expert = jnp.minimum(jnp.sum(bidx[:, None] >= blocks_end[None, :], axis=1), e - 1).astype(I32)
    flat = lambda a: a.reshape(-1).astype(I32)
    plen, poff = flat(padded - per_expert), flat(start + per_expert)

    assert e <= PIECE_CAP and local_rows // SEG_CHUNK <= PIECE_CAP
    pieces = _piece_tables(seg, loff, goff)
    xs, pos = _dispatch(x1s, idx, flat(loff), flat(seg), pieces, plen, poff, n_blocks, n_rows_max,
                        local_rows)
    ys = _experts(xs, block_expert, n_blocks, w_gate_up, b_gate_up.reshape(e, 1, -1),
                  w_down, b_down.reshape(e, 1, -1))
    return _combine(ys, pos, gate, x1s, g2, b2, flat(loff), flat(seg), pieces, local_rows, alpha)


def _mixer(x, apply_ln, lng, lnb, lay, consts, alpha):
    batch, seq, d = x.shape
    t = batch * seq
    fw = lay["g_fourier"].shape[1]
    n2 = MINOR
    x2d = x.reshape(t, d)
    z, x0, w = _in_proj(x2d, batch, seq, fw, lng, lnb, lay["w_in"], lay["b_in"], consts["bd"],
                        lay["w_short"], lay["b_short"], apply_ln)
    c = x0.shape[1]

    n1 = seq // n2
    g = jnp.asarray(_major_matrix(n1, n1, n1, -1)).astype(BF16)
    y = _dft_major(g, z.reshape(batch, n1, n2, fw), True)
    ortho = 1.0 / math.sqrt(seq * FOURIER_GROUP_DIM)
    wf_real = jnp.asarray(_minor_matrix(n2, -1)[:n2] * ortho).astype(BF16)
    yf = _dft_minor(y, wf_real, "real")

    n = 2 * seq
    m1 = n // n2
    kf, asum = _hyena_filter(seq, consts["bands"], lay["filt_w1"], lay["filt_b1"], lay["filt_w2"],
                             lay["filt_b2"], lay["filt_w3"], lay["filt_b3"], lay["filt_w_out"],
                             lay["filt_freq"], lay["filt_decay"])
    gk = jnp.asarray(_major_matrix(m1, m1, m1, -1, real_in=True)).astype(BF16)
    ky = _dft_major(gk, kf.reshape(1, m1, n2, c), True)
    gfw = jnp.asarray(_major_matrix(m1, m1 // 2, m1, -1)).astype(BF16)
    gin = jnp.asarray(_major_matrix(m1, m1, m1 // 2, +1)).astype(BF16)
    pairs = batch // 2
    u = _dft_major(gfw, w.reshape(pairs, m1, n2, c), True)
    wi = jnp.asarray(_minor_matrix(n2, +1) * (1.0 / n)).astype(BF16)
    v = _dft_minor(u, consts["wf"], "conv", wi_bf=wi, kf4=ky)
    yc = _dft_major(gin, v, False)

    inv_norm = 1.0 / asum
    return _out_proj(yf.reshape(t, fw), yc.reshape(t, c), x0, w, x2d, lng, lnb, inv_norm,
                     lay["hyena_skip"], lay["g_fourier"], lay["g_hyena"], lay["w_out"], lay["b_out"],
                     lay["ln1_g"], lay["ln1_b"], lay["w_router_t"], lay["b_router"], alpha, apply_ln)


def kernel(x_prompt, x_sample, ln_in_g, ln_in_b, w_in, b_in, w_short, b_short, filt_w1, filt_b1, filt_w2, filt_b2, filt_w3, filt_b3, filt_w_out, filt_freq, filt_decay, hyena_skip, g_fourier, g_hyena, w_out, b_out, ln1_g, ln1_b, w_router, b_router, w_gate_up, b_gate_up, w_down, b_down, ln2_g, ln2_b):
    depth, d, in_width = w_in.shape
    fw = g_fourier.shape[1]
    alpha = (2.0 * depth) ** 0.25
    row = lambda a: a.reshape(1, -1)

    gd = FOURIER_GROUP_DIM
    cg, sg = _cs(np.outer(np.arange(gd), np.arange(gd)), gd)
    eye = np.eye(LANES // gd)
    bd = np.concatenate([np.kron(eye, cg), -np.kron(eye, sg)], axis=1).astype(np.float32)
    bands = jnp.linspace(1e-4, FILTER_BANDS - 1, FILTER_BANDS, dtype=F32)
    bands_row = jnp.zeros((1, LANES), F32).at[0, 1:1 + FILTER_BANDS].set(bands)
    bands_row = bands_row.at[0, 1 + FILTER_BANDS:1 + 2 * FILTER_BANDS].set(bands)
    consts = dict(bd=jnp.asarray(bd).astype(BF16), bands=bands_row,
                  wf=jnp.asarray(_minor_matrix(MINOR, -1)).astype(BF16))

    xs = [x_prompt, x_sample]
    for l in range(depth):
        emb = filt_w1.shape[1]
        twice = lambda a: jnp.concatenate([row(a), row(a)], axis=1)
        blockdiag = lambda a: jnp.kron(jnp.eye(2, dtype=F32), a)
        w1_pad = jnp.zeros((LANES, filt_w1.shape[2]), F32).at[:emb].set(filt_w1[l])
        wl = filt_w_out[l]
        wl_halves = jnp.stack([jnp.concatenate([wl, jnp.zeros_like(wl)], axis=0),
                               jnp.concatenate([jnp.zeros_like(wl), wl], axis=0)])
        lay = dict(
            w_in=w_in[l].astype(BF16), b_in=row(b_in[l]), w_short=w_short[l], b_short=row(b_short[l]),
            filt_w1=blockdiag(w1_pad),
            filt_b1=twice(filt_b1[l]), filt_w2=blockdiag(filt_w2[l]), filt_b2=twice(filt_b2[l]),
            filt_w3=blockdiag(filt_w3[l]),
            filt_b3=twice(filt_b3[l]), filt_w_out=wl_halves, filt_freq=twice(filt_freq[l]),
            filt_decay=filt_decay[l], hyena_skip=row(hyena_skip[l]), g_fourier=row(g_fourier[l]),
            g_hyena=row(g_hyena[l]), w_out=w_out[l].astype(BF16), b_out=row(b_out[l]),
            ln1_g=row(ln1_g[l]), ln1_b=row(ln1_b[l]),
            w_router_t=w_router[l].T, b_router=b_router[l].reshape(-1, 1))
        routed = [_mixer(x, l == 0, row(ln_in_g), row(ln_in_b), lay, consts, alpha) for x in xs]
        outs = _routed_moe(routed, w_gate_up[l], b_gate_up[l], w_down[l], b_down[l],
                           row(ln2_g[l]), row(ln2_b[l]), alpha)
        xs = [o.reshape(x.shape) for o, x in zip(outs, xs)]
    return (xs[0], xs[1])
```

```python
import functools
import math

import numpy as np
import jax
import jax.numpy as jnp
from jax import lax
from jax.experimental import pallas as pl
from jax.experimental.pallas import tpu as pltpu

F32 = jnp.float32
BF16 = jnp.bfloat16
I32 = jnp.int32
U32 = jnp.uint32

LN_EPS = 1e-5
RMS_EPS = 1e-6
SWIGLU_LIMIT = 7.0
SWIGLU_ALPHA = 1.702
TOP_K = 4
FOURIER_GROUP_DIM = 64
FILTER_BANDS = 16

LANES = 128
MINOR = 128
ROW_TILE = 512
IN_TILE = 1024
IN_SPLIT = 4
MAJOR_BLOCK_BYTES = 12 * 1024 * 1024
TOKEN_TILE = 512
SEG_ALIGN = 16
EXPERT_BLOCK = 512
VMEM_LIMIT = 56 * 1024 * 1024


def _params(n_grid, semantics="parallel"):
    return pltpu.CompilerParams(dimension_semantics=(semantics,) * n_grid,
                                vmem_limit_bytes=VMEM_LIMIT)


def _dot(a, b):
    return jnp.dot(a, b, preferred_element_type=F32)


def _split(a):
    hi = a.astype(BF16)
    lo = (a - hi.astype(F32)).astype(BF16)
    return hi, lo


def _dot3(a, b):
    ah, al = _split(a)
    bh, bl = _split(b)
    return _dot(ah, bh) + _dot(ah, bl) + _dot(al, bh)


def _layer_norm(x, g, b):
    mu = jnp.mean(x, axis=-1, keepdims=True)
    xc = x - mu
    var = jnp.mean(xc * xc, axis=-1, keepdims=True)
    return xc * lax.rsqrt(var + LN_EPS) * g + b


def _rms_norm(x, g):
    return x * lax.rsqrt(jnp.mean(x * x, axis=-1, keepdims=True) + RMS_EPS) * g


def _in_kernel(x_ref, xp_ref, xq_ref, lng_ref, lnb_ref, win_ref, bin_ref, bd_ref, wsh_ref, bsh_ref,
               z_ref, x0_ref, w_ref, *, tiles_per_seq, apply_ln, fw):
    it = pl.program_id(0) % tiles_per_seq
    tm = x_ref.shape[0]

    def norm(x):
        return _layer_norm(x, lng_ref[...], lnb_ref[...]) if apply_ln else x

    halo = norm(jnp.concatenate([xp_ref[...], xq_ref[...]], axis=0))
    projh = _dot(halo.astype(BF16), win_ref[:, fw:]) + bin_ref[:, fw:]
    rp = tm // IN_SPLIT
    projs = [_dot(norm(x_ref[r * rp:(r + 1) * rp, :]).astype(BF16), win_ref[...]) + bin_ref[...]
             for r in range(IN_SPLIT)]
    wsh = wsh_ref[...]
    for r, proj in enumerate(projs):
        sl = slice(r * rp, (r + 1) * rp)
        pf = proj[:, :fw].astype(BF16)
        zs = [_dot(pf[:, c0:c0 + LANES], bd_ref[...]) for c0 in range(0, fw, LANES)]
        z_ref[0, sl, :] = _pack_pair(jnp.concatenate([z[:, :LANES] for z in zs], axis=1),
                                     jnp.concatenate([z[:, LANES:] for z in zs], axis=1))

        ph = proj[:, fw:]
        prev_row = projs[r - 1][rp - 1:rp, fw:] if r > 0 else jnp.where(it > 0, projh[7:8], 0.0)
        next_row = (projs[r + 1][0:1, fw:] if r + 1 < IN_SPLIT
                    else jnp.where(it < tiles_per_seq - 1, projh[8:9], 0.0))
        rows = lax.broadcasted_iota(I32, ph.shape, 0)
        up = jnp.where(rows == 0, prev_row, pltpu.roll(ph, 1, 0))
        dn = jnp.where(rows == rp - 1, next_row, pltpu.roll(ph, rp - 1, 0))
        uc = up * wsh[0:1] + ph * wsh[1:2] + dn * wsh[2:3] + bsh_ref[...]
        hw = uc.shape[1] // 3
        x0_ref[sl, :] = uc[:, :hw]
        w_ref[sl, :] = uc[:, 2 * hw:] * uc[:, hw:2 * hw]


def _in_proj(x2d, batch, seq, fw, lng, lnb, win_bf, b_in, bd_bf, w_short, b_short, apply_ln):
    t, d = x2d.shape
    inw = win_bf.shape[1]
    hw = (inw - fw) // 3
    tm = IN_TILE
    tps = seq // tm
    r8 = tm // 8
    nblk8 = t // 8
    const = lambda i: (0, 0)
    kern = functools.partial(_in_kernel, tiles_per_seq=tps, apply_ln=apply_ln, fw=fw)
    return pl.pallas_call(
        kern,
        grid=(t // tm,),
        in_specs=[
            pl.BlockSpec((tm, d), lambda i: (i, 0)),
            pl.BlockSpec((8, d), lambda i: (jnp.maximum(i * r8 - 1, 0), 0)),
            pl.BlockSpec((8, d), lambda i: (jnp.minimum((i + 1) * r8, nblk8 - 1), 0)),
            pl.BlockSpec((1, d), const), pl.BlockSpec((1, d), const),
            pl.BlockSpec((d, inw), const), pl.BlockSpec((1, inw), const),
            pl.BlockSpec(bd_bf.shape, const),
            pl.BlockSpec((3, 3 * hw), const), pl.BlockSpec((1, 3 * hw), const),
        ],
        out_specs=[
            pl.BlockSpec((1, tm, fw), lambda i: (i // tps, i % tps, 0)),
            pl.BlockSpec((tm, hw), lambda i: (i, 0)),
            pl.BlockSpec((tm, hw), lambda i: (i, 0)),
        ],
        out_shape=[
            jax.ShapeDtypeStruct((batch, seq, fw), U32),
            jax.ShapeDtypeStruct((t, hw), F32),
            jax.ShapeDtypeStruct((t, hw), F32),
        ],
        compiler_params=_params(1),
        name="in_proj",
    )(x2d, x2d, x2d, lng, lnb, win_bf, b_in, bd_bf, w_short, b_short)


FILTER_HALO = 16


def _filter_kernel(bands_ref, w1_ref, b1_ref, w2_ref, b2_ref, w3_ref, b3_ref, wl_ref, fr_ref, dec_ref,
                   k_ref, asum_ref, tc_ref, ts_ref, kf_buf, kb_buf, sems, *, seq):
    i = pl.program_id(0)
    nsteps = pl.num_programs(0)
    tile, c = kf_buf.shape
    ext = tc_ref.shape[0]
    rep = c // LANES
    w_unit = 2.0 * math.pi / seq
    bands = bands_ref[...]

    @pl.when(i == 0)
    def _():
        arg = bands * (w_unit * lax.broadcasted_iota(I32, (ext, LANES), 0).astype(F32))
        tc_ref[...] = jnp.cos(arg)
        ts_ref[...] = jnp.sin(arg)
        asum_ref[...] = jnp.zeros_like(asum_ref)

    j0 = i * tile
    base = bands * (w_unit * j0.astype(F32))
    c0, s0 = jnp.cos(base), jnp.sin(base)
    tc, ts = tc_ref[...], ts_ref[...]
    cosf = c0 * tc - s0 * ts
    sinf = s0 * tc + c0 * ts
    j = j0 + lax.broadcasted_iota(I32, (ext, LANES), 0)
    t = j.astype(F32) * (1.0 / (seq - 1))
    lane = lax.broadcasted_iota(I32, (ext, LANES), 1)
    feat = jnp.where(lane == 0, t,
                     jnp.where(lane <= FILTER_BANDS, cosf,
                               jnp.where(lane <= 2 * FILTER_BANDS, -sinf, 0.0)))
    half = ext // 2
    fr = fr_ref[...]
    h = jnp.concatenate([feat[:half], feat[half:]], axis=1)
    h = jnp.sin(fr * (_dot3(h, w1_ref[...]) + b1_ref[...]))
    h = jnp.sin(fr * (_dot3(h, w2_ref[...]) + b2_ref[...]))
    h = jnp.sin(fr * (_dot3(h, w3_ref[...]) + b3_ref[...]))
    hl = jnp.concatenate([_dot3(h, wl_ref[0]), _dot3(h, wl_ref[1])], axis=0)
    t4 = jnp.concatenate([t] * rep, axis=1)
    j4 = jnp.concatenate([j] * rep, axis=1)
    dec = jnp.abs(dec_ref[...])
    hf = hl[:, :c] * jnp.exp(-t4 * dec[0:1])
    hb = hl[:, c:] * jnp.exp(-t4 * dec[1:2])
    hb = jnp.where(j4 < seq, hb, 0.0)
    asum_ref[...] += jnp.sum(jnp.abs(hf[:tile]) + jnp.abs(hb[:tile]), axis=0, keepdims=True)

    rev = (lax.broadcasted_iota(I32, (tile, ext), 1) == tile - lax.broadcasted_iota(I32, (tile, ext), 0))
    rev = jnp.where(rev, 1.0, 0.0).astype(BF16)
    b1 = hb.astype(BF16)
    r1 = hb - b1.astype(F32)
    b2 = r1.astype(BF16)
    b3 = (r1 - b2.astype(F32)).astype(BF16)
    kb = _dot(rev, b1) + _dot(rev, b2) + _dot(rev, b3)

    def copies():
        return (pltpu.make_async_copy(kf_buf, k_ref.at[pl.ds(pl.multiple_of(j0, tile), tile)], sems.at[0]),
                pltpu.make_async_copy(kb_buf, k_ref.at[pl.ds(pl.multiple_of(2 * seq - j0 - tile, tile), tile)],
                                      sems.at[1]))

    @pl.when(i > 0)
    def _():
        for cp in copies():
            cp.wait()

    kf_buf[...] = jnp.where(j4[:tile] == 0, hf[:tile] + hb[:tile], hf[:tile])
    kb_buf[...] = kb
    for cp in copies():
        cp.start()

    @pl.when(i == nsteps - 1)
    def _():
        for cp in copies():
            cp.wait()


def _hyena_filter(seq, bands_row, w1p, b1, w2, b2, w3, b3, wl, freq, decay):
    c = decay.shape[1]
    tile = ROW_TILE
    ext = tile + FILTER_HALO
    full = lambda a: pl.BlockSpec(a.shape, lambda i: (0,) * a.ndim)
    args = (bands_row, w1p, b1, w2, b2, w3, b3, wl, freq, decay)
    return pl.pallas_call(
        functools.partial(_filter_kernel, seq=seq),
        grid=(seq // tile,),
        in_specs=[full(a) for a in args],
        out_specs=[pl.BlockSpec(memory_space=pl.ANY), pl.BlockSpec((1, c), lambda i: (0, 0))],
        out_shape=[jax.ShapeDtypeStruct((2 * seq, c), F32), jax.ShapeDtypeStruct((1, c), F32)],
        scratch_shapes=[pltpu.VMEM((ext, LANES), F32), pltpu.VMEM((ext, LANES), F32),
                        pltpu.VMEM((tile, c), F32), pltpu.VMEM((tile, c), F32),
                        pltpu.SemaphoreType.DMA((2,))],
        compiler_params=_params(1, "arbitrary"),
        name="hyena_filter",
    )(*args)


def _pack_pair(re, im):
    rb = lax.bitcast_convert_type(re.astype(BF16).astype(F32), U32)
    ib = lax.bitcast_convert_type(im.astype(BF16).astype(F32), U32)
    return (rb >> 16) | ib


def _unpack_pair(word):
    re = lax.bitcast_convert_type(word << 16, F32)
    im = lax.bitcast_convert_type(word & jnp.uint32(0xFFFF0000), F32)
    return re, im


def _major_compute(g, xbuf, obuf, slot):
    _, n_ct, bsz, k, sub, lanes = xbuf.shape
    m = obuf.shape[3]
    x2 = xbuf.reshape(2 * n_ct * bsz * k * sub, lanes)
    o2 = obuf.reshape(2 * n_ct * bsz * m * sub, lanes)
    for t in range(n_ct * bsz):
        xbase = (slot * n_ct * bsz + t) * k * sub
        obase = (slot * n_ct * bsz + t) * m * sub
        for b in range(sub):
            xb = x2[pl.ds(xbase + b, k, stride=sub), :]
            if xb.dtype == U32:
                xb = jnp.concatenate(_unpack_pair(xb), axis=0)
            res = _dot(g, xb.astype(BF16))
            if obuf.dtype == U32:
                res = _pack_pair(res[:m], res[m:])
            o2[pl.ds(obase + b, m, stride=sub), :] = res


def _major_kernel(g_ref, x_hbm, o_hbm, xbuf, obuf, isem, osem, *, nj):
    s = pl.program_id(0)
    ns = pl.num_programs(0)
    slot = s % 2
    _, n_ct, bsz, k, sub, lanes = xbuf.shape
    m = obuf.shape[3]

    def copies(step, sl, hbm, buf, sem, inbound):
        i = step // nj
        j = step % nj
        out = []
        for ct in range(n_ct):
            tile = hbm.at[pl.ds(i * bsz, bsz), :, pl.ds(pl.multiple_of(j * sub, sub), sub),
                          pl.ds(ct * lanes, lanes)]
            out.append(pltpu.make_async_copy(tile, buf.at[sl, ct], sem.at[sl]) if inbound
                       else pltpu.make_async_copy(buf.at[sl, ct], tile, sem.at[sl]))
        return out

    def start(cps):
        for cp in cps:
            cp.start()

    def wait(cps):
        for cp in cps:
            cp.wait()

    @pl.when(s == 0)
    def _():
        start(copies(0, 0, x_hbm, xbuf, isem, True))

    @pl.when(s + 1 < ns)
    def _():
        start(copies(s + 1, 1 - slot, x_hbm, xbuf, isem, True))

    wait(copies(s, slot, x_hbm, xbuf, isem, True))

    @pl.when(s >= 2)
    def _():
        wait(copies(s - 2, slot, o_hbm, obuf, osem, False))

    _major_compute(g_ref[...], xbuf, obuf, slot)
    start(copies(s, slot, o_hbm, obuf, osem, False))

    @pl.when(s == ns - 1)
    def _():
        wait(copies(s, slot, o_hbm, obuf, osem, False))

        @pl.when(s >= 1)
        def _():
            wait(copies(s - 1, 1 - slot, o_hbm, obuf, osem, False))


def _dft_major(g_bf, x4, pack_out):
    b, k, n2, c = x4.shape
    m = g_bf.shape[0] // 2 if pack_out else g_bf.shape[0]
    out_dtype = U32 if pack_out else F32
    sub = 8
    n_ct = c // LANES
    bsz = max(1, min(b, MAJOR_BLOCK_BYTES // ((k + m) * sub * c * 4)))
    while b % bsz:
        bsz -= 1
    nj = n2 // sub
    return pl.pallas_call(
        functools.partial(_major_kernel, nj=nj),
        grid=((b // bsz) * nj,),
        in_specs=[pl.BlockSpec(g_bf.shape, lambda s: (0, 0)), pl.BlockSpec(memory_space=pl.ANY)],
        out_specs=pl.BlockSpec(memory_space=pl.ANY),
        out_shape=jax.ShapeDtypeStruct((b, m, n2, c), out_dtype),
        scratch_shapes=[pltpu.VMEM((2, n_ct, bsz, k, sub, LANES), x4.dtype),
                        pltpu.VMEM((2, n_ct, bsz, m, sub, LANES), out_dtype),
                        pltpu.SemaphoreType.DMA((2,)), pltpu.SemaphoreType.DMA((2,))],
        compiler_params=_params(1, "arbitrary"),
        name="dft_major",
    )(g_bf, x4)


def _minor_kernel(*refs, mode, k1b):
    if mode == "conv":
        y_ref, twb_ref, twr_ref, wf_ref, wi_ref, kf_ref, o_ref, ks_ref = refs
    else:
        y_ref, twb_ref, twr_ref, wf_ref, o_ref = refs
    n2 = y_ref.shape[2]
    ct = y_ref.shape[3]
    rep = ct // LANES

    def forward(packed, cs, sn, w):
        re, im = _unpack_pair(packed)
        return _dot(w, jnp.concatenate([re * cs + im * sn, im * cs - re * sn], axis=0).astype(BF16))

    cb, sb = twb_ref[0, 0], twb_ref[1, 0]
    for j in range(k1b):
        cr, sr = twr_ref[0, j], twr_ref[1, j]
        cs = jnp.concatenate([cb * cr - sb * sr] * rep, axis=1)
        sn = jnp.concatenate([sb * cr + cb * sr] * rep, axis=1)
        if mode == "real":
            o_ref[0, :, j, :] = forward(y_ref[0, j], cs, sn, wf_ref[...])
            continue

        @pl.when(pl.program_id(1) == 0)
        def _():
            ks_ref[j] = forward(kf_ref[0, j], cs, sn, wf_ref[...])

        z = forward(y_ref[0, j], cs, sn, wf_ref[...])
        zr, zi = z[:n2], z[n2:]
        kr = ks_ref[j, :n2]
        ki = ks_ref[j, n2:]
        p = jnp.concatenate([zr * kr - zi * ki, zr * ki + zi * kr], axis=0).astype(BF16)
        q = _dot(wi_ref[...], p)
        qr, qi = q[:n2], q[n2:]
        o_ref[0, j] = _pack_pair(qr * cs - qi * sn, qi * cs + qr * sn)


def _dft_minor(y4, wf_bf, mode, wi_bf=None, kf4=None):
    b, n1, n2, c = y4.shape
    k1b = min(8, n1)
    ct = c
    tw_block, tw_rem = _twiddles(n1, n2, k1b)
    grid = (n1 // k1b, b)
    dspec = pl.BlockSpec((1, k1b, n2, ct), lambda j, i: (i, j, 0, 0))
    in_specs = [dspec,
                pl.BlockSpec((2, 1, n2, LANES), lambda j, i: (0, j, 0, 0)),
                pl.BlockSpec((2, k1b, n2, LANES), lambda j, i: (0, 0, 0, 0)),
                pl.BlockSpec(wf_bf.shape, lambda j, i: (0, 0))]
    args = [y4, tw_block, tw_rem, wf_bf]
    scratch = []
    if mode == "conv":
        in_specs += [pl.BlockSpec(wi_bf.shape, lambda j, i: (0, 0)),
                     pl.BlockSpec((1, k1b, n2, ct), lambda j, i: (0, j, 0, 0))]
        args += [wi_bf, kf4]
        scratch = [pltpu.VMEM((k1b, 2 * n2, ct), F32)]
    if mode == "real":
        out_spec = pl.BlockSpec((1, n2, k1b, ct), lambda j, i: (i, 0, j, 0))
        out_shape = jax.ShapeDtypeStruct((b, n2, n1, c), F32)
    else:
        out_spec = dspec
        out_shape = jax.ShapeDtypeStruct((b, n1, n2, c), U32)
    return pl.pallas_call(
        functools.partial(_minor_kernel, mode=mode, k1b=k1b),
        grid=grid, in_specs=in_specs, out_specs=out_spec, out_shape=out_shape,
        scratch_shapes=scratch,
        compiler_params=pltpu.CompilerParams(dimension_semantics=("parallel", "arbitrary"),
                                             vmem_limit_bytes=VMEM_LIMIT),
        name="dft_minor_" + mode,
    )(*args)


def _cs(num, den):
    ang = 2.0 * np.pi * (num % den) / den
    return np.cos(ang), np.sin(ang)


def _complex_block(c, s, sign):
    return np.block([[c, -sign * s], [sign * s, c]]).astype(np.float32)


def _major_matrix(n1, a_in, a_out, sign, real_in=False):
    c, s = _cs(np.outer(np.arange(a_out), np.arange(a_in)), n1)
    g = _complex_block(c, s, sign)
    return g[:, :a_in] if real_in else g


def _minor_matrix(n2, sign):
    c, s = _cs(np.outer(np.arange(n2), np.arange(n2)), n2)
    return _complex_block(c, s, sign)


def _twiddles(n1, n2, k1b):
    def table(k1):
        c, s = _cs(np.outer(k1, np.arange(n2)), n1 * n2)
        tw = jnp.asarray(np.stack([c, s]).astype(np.float32))
        return jnp.broadcast_to(tw[..., None], (2, len(k1), n2, LANES))
    return table(np.arange(0, n1, k1b)), table(np.arange(k1b))


def _out_kernel(yf_ref, yc_ref, x0_ref, w_ref, x_ref, lng_ref, lnb_ref, inv_ref, skip_ref, gf_ref, gh_ref,
                wo_ref, bo_ref, g1_ref, b1_ref, wr_ref, br_ref, x1_ref, idx_ref, gate_ref, cnt_ref,
                *, alpha, apply_ln, top_k):
    c = yc_ref.shape[1]
    w = w_ref[...]
    yh = x0_ref[...] * (yc_ref[...] * inv_ref[...] + skip_ref[...] * w)
    mf = _rms_norm(yf_ref[...], gf_ref[...]).astype(BF16)
    mh = _rms_norm(yh, gh_ref[...]).astype(BF16)
    m = _dot(mf, wo_ref[:c]) + _dot(mh, wo_ref[c:]) + bo_ref[...]
    xn = _layer_norm(x_ref[...], lng_ref[...], lnb_ref[...]) if apply_ln else x_ref[...]
    x1 = _layer_norm(alpha * xn + m, g1_ref[...], b1_ref[...])
    x1_ref[...] = x1
    _route_tile(x1, wr_ref[...], br_ref[...], idx_ref, gate_ref, cnt_ref, top_k)


def _out_proj(yf, yc, x0, w, x, lng, lnb, inv_norm, skip, gf, gh, wo_bf, bo, g1, b1, wr_t, br_col, alpha,
              apply_ln):
    t, d = x.shape
    c = yc.shape[1]
    e = wr_t.shape[0]
    tm = TOKEN_TILE
    nt = t // tm
    row = lambda i: (i, 0)
    const = lambda i: (0, 0)
    tile3 = lambda i: (i, 0, 0)
    return pl.pallas_call(
        functools.partial(_out_kernel, alpha=alpha, apply_ln=apply_ln, top_k=TOP_K),
        grid=(nt,),
        in_specs=[
            pl.BlockSpec((tm, c), row),
            pl.BlockSpec((tm, c), row), pl.BlockSpec((tm, c), row), pl.BlockSpec((tm, c), row),
            pl.BlockSpec((tm, d), row),
            pl.BlockSpec((1, d), const), pl.BlockSpec((1, d), const),
            pl.BlockSpec((1, c), const), pl.BlockSpec((1, c), const),
            pl.BlockSpec((1, c), const), pl.BlockSpec((1, c), const),
            pl.BlockSpec(wo_bf.shape, const), pl.BlockSpec((1, d), const),
            pl.BlockSpec((1, d), const), pl.BlockSpec((1, d), const),
            pl.BlockSpec((e, d), const), pl.BlockSpec((e, 1), const),
        ],
        out_specs=[pl.BlockSpec((tm, d), row),
                   pl.BlockSpec((1, 2 * TOP_K, tm), tile3),
                   pl.BlockSpec((1, 8, tm), tile3),
                   pl.BlockSpec((1, e, 1), tile3)],
        out_shape=[jax.ShapeDtypeStruct((t, d), F32),
                   jax.ShapeDtypeStruct((nt, 2 * TOP_K, tm), I32),
                   jax.ShapeDtypeStruct((nt, 8, tm), F32),
                   jax.ShapeDtypeStruct((nt, e, 1), I32)],
        compiler_params=_params(1),
        name="out_proj",
    )(yf, yc, x0, w, x, lng, lnb, inv_norm, skip, gf, gh, wo_bf, bo, g1, b1, wr_t, br_col)


def _route_tile(x1, wr, br, idx_ref, gate_ref, cnt_ref, top_k):
    tt = x1.shape[0]
    e = wr.shape[0]
    nt = (((1,), (1,)), ((), ()))
    xh, xl = _split(x1)
    wh, wl = _split(wr)
    logits = (lax.dot_general(wh, xh, nt, preferred_element_type=F32)
              + lax.dot_general(wh, xl, nt, preferred_element_type=F32)
              + lax.dot_general(wl, xh, nt, preferred_element_type=F32)) + br
    iota_e = lax.broadcasted_iota(I32, (e, tt), 0).astype(F32)
    l = logits
    tops, idxs, hots = [], [], []
    for _ in range(top_k):
        m = jnp.max(l, axis=0, keepdims=True)
        idx = jnp.min(jnp.where(l == m, iota_e, float(e)), axis=0, keepdims=True)
        hot = iota_e == idx
        l = jnp.where(hot, -jnp.inf, l)
        tops.append(m)
        idxs.append(idx)
        hots.append(hot)
    ex = [jnp.exp(m - tops[0]) for m in tops]
    den = ex[0]
    for v in ex[1:]:
        den = den + v
    gates = [v / den for v in ex]
    hot_all = jnp.zeros((e, tt), F32)
    for hot in hots:
        hot_all = jnp.where(hot, 1.0, hot_all)
    upper = (lax.broadcasted_iota(I32, (tt, tt), 0) < lax.broadcasted_iota(I32, (tt, tt), 1))
    before = _dot(hot_all.astype(BF16), jnp.where(upper, 1.0, 0.0).astype(BF16))
    ranks = [jnp.sum(jnp.where(hot, before, 0.0), axis=0, keepdims=True).astype(I32) for hot in hots]
    pad = 8 - top_k
    idx_ref[0] = jnp.concatenate([v.astype(I32) for v in idxs] + ranks, axis=0)
    gate_ref[0] = jnp.concatenate(gates + [jnp.zeros((pad, tt), F32)], axis=0)
    cnt_ref[0] = jnp.sum(hot_all, axis=1, keepdims=True).astype(I32)


def _dual_specs(shape, nt0):
    first = pl.BlockSpec(shape, lambda i, *_: (jnp.minimum(i, nt0 - 1),) + (0,) * (len(shape) - 1))
    second = pl.BlockSpec(shape, lambda i, *_: (jnp.maximum(i - nt0, 0),) + (0,) * (len(shape) - 1))
    return first, second


SEG_PIECES = 3
SEG_CHUNK = SEG_ALIGN << SEG_PIECES
PIECE_CLASSES = SEG_PIECES + 1
PIECE_CAP = 32
ZERO_CHUNK = 256


def _segment_copies(first, count, loff_s, len_s, goff_s, local, remote, sem, to_remote, wait):
    def copy(lstart, gstart, size):
        lref = local.at[pl.ds(pl.multiple_of(lstart, SEG_ALIGN), size)]
        rref = remote.at[pl.ds(pl.multiple_of(gstart, SEG_ALIGN), size)]
        cp = (pltpu.make_async_copy(lref, rref, sem) if to_remote
              else pltpu.make_async_copy(rref, lref, sem))
        if wait:
            cp.wait()
        else:
            cp.start()

    def body(e, carry):
        s = first + e
        n = len_s[s] // SEG_ALIGN
        lo = 0 if loff_s is None else loff_s[s]
        go = goff_s[s]
        chunks = n >> SEG_PIECES

        def chunk(i, c):
            copy(lo + i * SEG_CHUNK, go + i * SEG_CHUNK, SEG_CHUNK)
            return c

        lax.fori_loop(0, chunks, chunk, 0)
        rem = n & ((1 << SEG_PIECES) - 1)
        base = chunks * SEG_CHUNK
        for p in reversed(range(SEG_PIECES)):
            start = base + ((rem >> (p + 1)) << (p + 1)) * SEG_ALIGN

            @pl.when(((rem >> p) & 1) == 1)
            def _():
                copy(lo + start, go + start, SEG_ALIGN << p)
        return carry

    lax.fori_loop(0, count, body, 0)


def _piece_copies(step, lo_t, go_t, cnt_t, local, remote, sem, to_remote):
    for c in range(PIECE_CLASSES):
        size = SEG_ALIGN << c
        base = (step * PIECE_CLASSES + c) * PIECE_CAP

        def body(i, carry, size=size, base=base):
            lref = local.at[pl.ds(pl.multiple_of(lo_t[base + i], SEG_ALIGN), size)]
            rref = remote.at[pl.ds(pl.multiple_of(go_t[base + i], SEG_ALIGN), size)]
            cp = (pltpu.make_async_copy(lref, rref, sem) if to_remote
                  else pltpu.make_async_copy(rref, lref, sem))
            cp.start()
            return carry

        lax.fori_loop(0, cnt_t[step * PIECE_CLASSES + c], body, 0)


def _piece_tables(seg, loff, goff):
    nt, e = seg.shape
    n = seg // SEG_ALIGN
    chunks = n >> SEG_PIECES
    rem = n & ((1 << SEG_PIECES) - 1)
    base = chunks * SEG_CHUNK

    def compact(valid, off, lo, go):
        slots = valid.shape[1]
        key = jnp.where(valid, jnp.arange(slots, dtype=I32)[None, :], slots)
        _, lo_sorted, go_sorted = lax.sort((key, lo + off, go + off), dimension=1, num_keys=1)
        return lo_sorted[:, :PIECE_CAP], go_sorted[:, :PIECE_CAP], jnp.sum(valid, axis=1)

    tables = []
    for c in range(SEG_PIECES):
        off = base + ((rem >> (c + 1)) << (c + 1)) * SEG_ALIGN
        tables.append(compact(((rem >> c) & 1) == 1, off, loff, goff))
    max_chunks = TOKEN_TILE // SEG_CHUNK
    i = jnp.arange(max_chunks, dtype=I32)[None, None, :]
    rep = lambda a: jnp.broadcast_to(a[:, :, None], (nt, e, max_chunks)).reshape(nt, e * max_chunks)
    tables.append(compact((i < chunks[:, :, None]).reshape(nt, e * max_chunks),
                          jnp.broadcast_to(i * SEG_CHUNK, (nt, e, max_chunks)).reshape(nt, e * max_chunks),
                          rep(loff), rep(goff)))
    flat = lambda k: jnp.stack([t[k] for t in tables], axis=1).reshape(-1).astype(I32)
    return flat(0), flat(1), flat(2)


def _wait_tile(step, n_experts, loff_s, len_s, local, remote, sem, to_remote):
    last = step * n_experts + n_experts - 1
    n = (loff_s[last] + len_s[last]) // SEG_ALIGN
    for p in range((local.shape[0] // SEG_ALIGN).bit_length()):
        size = SEG_ALIGN << p

        @pl.when(((n >> p) & 1) == 1)
        def _():
            lref = local.at[pl.ds(0, size)]
            rref = remote.at[pl.ds(0, size)]
            cp = (pltpu.make_async_copy(lref, rref, sem) if to_remote
                  else pltpu.make_async_copy(rref, lref, sem))
            cp.wait()


def _dispatch_kernel(loff_s, seg_s, lo_t, go_t, cnt_t, plen_s, poff_s, nb_s, xa_ref, xb_ref, idx_ref, xs_ref,
                     pos_ref,
                     buf, sems, *, n_experts, top_k, nt0):
    j = pl.program_id(0)
    nt = pl.num_programs(0)
    slot = j % 2
    tt = xa_ref.shape[0]
    lr = buf.shape[1]
    idx = idx_ref[0]
    tope = idx[:top_k]
    base = jnp.zeros_like(tope)
    for e in range(n_experts):
        base = jnp.where(tope == e, loff_s[j * n_experts + e], base)
    pos = base + idx[top_k:]
    pos_ref[0] = jnp.concatenate([pos, jnp.zeros((8 - top_k, tt), I32)], axis=0)
    rows = lax.broadcasted_iota(jnp.int16, (lr, tt), 0)
    pos16 = pos.astype(jnp.int16)
    perm = jnp.zeros((lr, tt), BF16)
    for k in range(top_k):
        perm = jnp.where(rows == pos16[k:k + 1], jnp.ones((lr, tt), BF16), perm)
    x = jnp.where(j < nt0, xa_ref[...], xb_ref[...])
    buf[slot] = _dot(perm, x.astype(BF16)).astype(BF16)

    def drained(step, s):
        _wait_tile(step, n_experts, loff_s, seg_s, buf.at[s], xs_ref, sems.at[s], True)

    _piece_copies(j, lo_t, go_t, cnt_t, buf.at[slot], xs_ref, sems.at[slot], True)

    @pl.when(j > 0)
    def _():
        drained(j - 1, 1 - slot)

    @pl.when(j == nt - 1)
    def _():
        drained(j, slot)
        zrows = EXPERT_BLOCK
        buf[0, :zrows] = jnp.zeros((zrows, buf.shape[2]), BF16)
        _segment_copies(0, n_experts, None, plen_s, poff_s, buf.at[0], xs_ref, sems.at[0], True, False)
        _segment_copies(0, n_experts, None, plen_s, poff_s, buf.at[0], xs_ref, sems.at[0], True, True)
        n_tail = xs_ref.shape[0] // zrows - nb_s[0]

        def tail_copy(i):
            dst = xs_ref.at[pl.ds(pl.multiple_of((nb_s[0] + i) * zrows, zrows), zrows)]
            return pltpu.make_async_copy(buf.at[0, pl.ds(0, zrows)], dst, sems.at[0])

        lax.fori_loop(0, n_tail, lambda i, c: (tail_copy(i).start(), c)[1], 0)
        lax.fori_loop(0, n_tail, lambda i, c: (tail_copy(i).wait(), c)[1], 0)


def _dispatch(x1s, idx, loff, seg, pieces, plen, poff, n_blocks, n_rows, local_rows):
    d = x1s[0].shape[1]
    tt = TOKEN_TILE
    nt0 = x1s[0].shape[0] // tt
    nt = nt0 + x1s[1].shape[0] // tt
    e = loff.shape[0] // nt
    xa, xb = _dual_specs((tt, d), nt0)
    grid_spec = pltpu.PrefetchScalarGridSpec(
        num_scalar_prefetch=8,
        grid=(nt,),
        in_specs=[xa, xb, pl.BlockSpec((1, 2 * TOP_K, tt), lambda i, *_: (i, 0, 0))],
        out_specs=[pl.BlockSpec(memory_space=pl.ANY),
                   pl.BlockSpec((1, 8, tt), lambda i, *_: (i, 0, 0))],
        scratch_shapes=[pltpu.VMEM((2, local_rows, d), BF16), pltpu.SemaphoreType.DMA((2,))],
    )
    return pl.pallas_call(
        functools.partial(_dispatch_kernel, n_experts=e, top_k=TOP_K, nt0=nt0),
        grid_spec=grid_spec,
        out_shape=[jax.ShapeDtypeStruct((n_rows, d), BF16), jax.ShapeDtypeStruct((nt, 8, tt), I32)],
        compiler_params=_params(1, "arbitrary"),
        name="moe_dispatch",
    )(loff, seg, *pieces, plen, poff, n_blocks, x1s[0], x1s[1], idx)


def _combine_kernel(loff_s, seg_s, lo_t, go_t, cnt_t, ys_ref, pos_ref, gate_ref, xa_ref, xb_ref, g2_ref, b2_ref,
                    oa_ref, ob_ref, buf, sems, *, n_experts, top_k, alpha, nt0):
    j = pl.program_id(0)
    nt = pl.num_programs(0)
    slot = j % 2
    tt = xa_ref.shape[0]
    lr = buf.shape[1]

    def prefetch(step, s):
        last = step * n_experts + n_experts - 1
        covered = loff_s[last] + seg_s[last]
        for r0 in range(0, lr, ZERO_CHUNK):
            @pl.when(r0 + ZERO_CHUNK > covered)
            def _(r0=r0):
                buf[s, r0:r0 + ZERO_CHUNK] = jnp.zeros((ZERO_CHUNK, buf.shape[2]), BF16)
        _piece_copies(step, lo_t, go_t, cnt_t, buf.at[s], ys_ref, sems.at[s], False)

    @pl.when(j == 0)
    def _():
        prefetch(0, 0)

    @pl.when(j + 1 < nt)
    def _():
        prefetch(j + 1, 1 - slot)

    _wait_tile(j, n_experts, loff_s, seg_s, buf.at[slot], ys_ref, sems.at[slot], False)
    pos = pos_ref[0].astype(jnp.int16)
    gate = gate_ref[0].astype(BF16)
    rows = lax.broadcasted_iota(jnp.int16, (lr, tt), 0)
    wsel = jnp.zeros((lr, tt), BF16)
    for k in range(top_k):
        wsel = jnp.where(rows == pos[k:k + 1], gate[k:k + 1], wsel)
    moe = lax.dot_general(wsel, buf[slot], (((0,), (0,)), ((), ())), preferred_element_type=F32)
    x1 = jnp.where(j < nt0, xa_ref[...], xb_ref[...])
    out = _layer_norm(alpha * x1 + moe, g2_ref[...], b2_ref[...])

    @pl.when(j < nt0)
    def _():
        oa_ref[...] = out

    @pl.when(j >= nt0)
    def _():
        ob_ref[...] = out


def _combine(ys, pos_tm, gate_tm, x1s, g2, b2, loff, seg, pieces, local_rows, alpha):
    d = x1s[0].shape[1]
    tt = TOKEN_TILE
    nt0 = x1s[0].shape[0] // tt
    nt = nt0 + x1s[1].shape[0] // tt
    e = loff.shape[0] // nt
    xa, xb = _dual_specs((tt, d), nt0)
    grid_spec = pltpu.PrefetchScalarGridSpec(
        num_scalar_prefetch=5,
        grid=(nt,),
        in_specs=[pl.BlockSpec(memory_space=pl.ANY),
                  pl.BlockSpec((1, 8, tt), lambda i, *_: (i, 0, 0)),
                  pl.BlockSpec((1, 8, tt), lambda i, *_: (i, 0, 0)),
                  xa, xb,
                  pl.BlockSpec((1, d), lambda i, *_: (0, 0)),
                  pl.BlockSpec((1, d), lambda i, *_: (0, 0))],
        out_specs=list(_dual_specs((tt, d), nt0)),
        scratch_shapes=[pltpu.VMEM((2, local_rows, d), BF16), pltpu.SemaphoreType.DMA((2,))],
    )
    return pl.pallas_call(
        functools.partial(_combine_kernel, n_experts=e, top_k=TOP_K, alpha=alpha, nt0=nt0),
        grid_spec=grid_spec,
        out_shape=[jax.ShapeDtypeStruct(x.shape, F32) for x in x1s],
        compiler_params=_params(1, "arbitrary"),
        name="moe_combine",
    )(loff, seg, *pieces, ys, pos_tm, gate_tm, x1s[0], x1s[1], g2, b2)


def _expert_kernel(be_s, nb_s, xs_ref, wgu_ref, bgu_ref, wd_ref, bd_ref, ys_ref, wgu_bf, wd_bf):
    b = pl.program_id(0)
    active = b < nb_s[0]
    fresh = jnp.logical_or(b == 0, be_s[b] != be_s[jnp.maximum(b - 1, 0)])

    @pl.when(jnp.logical_and(active, fresh))
    def _():
        wgu_bf[...] = wgu_ref[0].astype(BF16)
        wd_bf[...] = wd_ref[0].astype(BF16)

    @pl.when(active)
    def _():
        dff = wd_ref.shape[1]
        gu = _dot(xs_ref[...], wgu_bf[...]) + bgu_ref[0]
        gate = jnp.minimum(gu[:, :dff], SWIGLU_LIMIT)
        up = jnp.clip(gu[:, dff:], -SWIGLU_LIMIT, SWIGLU_LIMIT)
        h = (up + 1.0) * (gate * jax.nn.sigmoid(SWIGLU_ALPHA * gate))
        ys_ref[...] = (_dot(h.astype(BF16), wd_bf[...]) + bd_ref[0]).astype(BF16)

    @pl.when(jnp.logical_not(active))
    def _():
        ys_ref[...] = jnp.zeros_like(ys_ref)


def _experts(xs, block_expert, n_blocks, wgu, bgu, wd, bdn):
    n_rows, d = xs.shape
    bm = EXPERT_BLOCK
    f2 = wgu.shape[2]
    dff = wd.shape[1]
    blk = lambda b, be, nb: (jnp.minimum(b, nb[0] - 1), 0)
    wsel = lambda b, be, nb: (be[b], 0, 0)
    grid_spec = pltpu.PrefetchScalarGridSpec(
        num_scalar_prefetch=2,
        grid=(n_rows // bm,),
        in_specs=[pl.BlockSpec((bm, d), blk),
                  pl.BlockSpec((1, d, f2), wsel), pl.BlockSpec((1, 1, f2), wsel),
                  pl.BlockSpec((1, dff, d), wsel), pl.BlockSpec((1, 1, d), wsel)],
        out_specs=pl.BlockSpec((bm, d), lambda b, be, nb: (b, 0)),
        scratch_shapes=[pltpu.VMEM((d, f2), BF16), pltpu.VMEM((dff, d), BF16)],
    )
    return pl.pallas_call(
        _expert_kernel,
        grid_spec=grid_spec,
        out_shape=jax.ShapeDtypeStruct((n_rows, d), BF16),
        compiler_params=_params(1, "arbitrary"),
        name="moe_experts",
    )(block_expert, n_blocks, xs, wgu, bgu, wd, bdn)


def _routed_moe(routed, w_gate_up, b_gate_up, w_down, b_down, g2, b2, alpha):
    x1s = [r[0] for r in routed]
    idx, gate, cnt = (jnp.concatenate([r[i] for r in routed], axis=0) for i in (1, 2, 3))
    t = x1s[0].shape[0] + x1s[1].shape[0]
    e = cnt.shape[1]
    tt = TOKEN_TILE
    nt = t // tt
    bm = EXPERT_BLOCK

    cnt = cnt.reshape(nt, e)
    seg = (cnt + SEG_ALIGN - 1) // SEG_ALIGN * SEG_ALIGN
    loff = jnp.cumsum(seg, axis=1) - seg
    per_expert = jnp.sum(seg, axis=0)
    padded = (per_expert + bm - 1) // bm * bm
    blocks_end = jnp.cumsum(padded) // bm
    start = jnp.cumsum(padded) - padded
    goff = start[None, :] + jnp.cumsum(seg, axis=0) - seg
    local_rows = TOP_K * tt + e * SEG_ALIGN
    n_rows_max = -(-(TOP_K * t + (SEG_ALIGN - 1) * e * nt) // bm) * bm + e * bm
    n_blocks = blocks_end[-1:].astype(I32)
    bidx = jnp.minimum(jnp.arange(n_rows_max // bm, dtype=I32), n_blocks[0] - 1)
    block_expert = jnp.minimum(jnp.sum(bidx[:, None] >= blocks_end[None, :], axis=1), e - 1).astype(I32)
    flat = lambda a: a.reshape(-1).astype(I32)
    plen, poff = flat(padded - per_expert), flat(start + per_expert)

    assert e <= PIECE_CAP and local_rows // SEG_CHUNK <= PIECE_CAP
    pieces = _piece_tables(seg, loff, goff)
    xs, pos = _dispatch(x1s, idx, flat(loff), flat(seg), pieces, plen, poff, n_blocks, n_rows_max,
                        local_rows)
    ys = _experts(xs, block_expert, n_blocks, w_gate_up, b_gate_up.reshape(e, 1, -1),
                  w_down, b_down.reshape(e, 1, -1))
    return _combine(ys, pos, gate, x1s, g2, b2, flat(loff), flat(seg), pieces, local_rows, alpha)


def _mixer(x, apply_ln, lng, lnb, lay, consts, alpha):
    batch, seq, d = x.shape
    t = batch * seq
    fw = lay["g_fourier"].shape[1]
    n2 = MINOR
    x2d = x.reshape(t, d)
    z, x0, w = _in_proj(x2d, batch, seq, fw, lng, lnb, lay["w_in"], lay["b_in"], consts["bd"],
                        lay["w_short"], lay["b_short"], apply_ln)
    c = x0.shape[1]

    n1 = seq // n2
    g = jnp.asarray(_major_matrix(n1, n1, n1, -1)).astype(BF16)
    y = _dft_major(g, z.reshape(batch, n1, n2, fw), True)
    ortho = 1.0 / math.sqrt(seq * FOURIER_GROUP_DIM)
    wf_real = jnp.asarray(_minor_matrix(n2, -1)[:n2] * ortho).astype(BF16)
    yf = _dft_minor(y, wf_real, "real")

    n = 2 * seq
    m1 = n // n2
    kf, asum = _hyena_filter(seq, consts["bands"], lay["filt_w1"], lay["filt_b1"], lay["filt_w2"],
                             lay["filt_b2"], lay["filt_w3"], lay["filt_b3"], lay["filt_w_out"],
                             lay["filt_freq"], lay["filt_decay"])
    gk = jnp.asarray(_major_matrix(m1, m1, m1, -1, real_in=True)).astype(BF16)
    ky = _dft_major(gk, kf.reshape(1, m1, n2, c), True)
    gfw = jnp.asarray(_major_matrix(m1, m1 // 2, m1, -1)).astype(BF16)
    gin = jnp.asarray(_major_matrix(m1, m1, m1 // 2, +1)).astype(BF16)
    pairs = batch // 2
    u = _dft_major(gfw, w.reshape(pairs, m1, n2, c), True)
    wi = jnp.asarray(_minor_matrix(n2, +1) * (1.0 / n)).astype(BF16)
    v = _dft_minor(u, consts["wf"], "conv", wi_bf=wi, kf4=ky)
    yc = _dft_major(gin, v, False)

    inv_norm = 1.0 / asum
    return _out_proj(yf.reshape(t, fw), yc.reshape(t, c), x0, w, x2d, lng, lnb, inv_norm,
                     lay["hyena_skip"], lay["g_fourier"], lay["g_hyena"], lay["w_out"], lay["b_out"],
                     lay["ln1_g"], lay["ln1_b"], lay["w_router_t"], lay["b_router"], alpha, apply_ln)


def kernel(x_prompt, x_sample, ln_in_g, ln_in_b, w_in, b_in, w_short, b_short, filt_w1, filt_b1, filt_w2, filt_b2, filt_w3, filt_b3, filt_w_out, filt_freq, filt_decay, hyena_skip, g_fourier, g_hyena, w_out, b_out, ln1_g, ln1_b, w_router, b_router, w_gate_up, b_gate_up, w_down, b_down, ln2_g, ln2_b):
    depth, d, in_width = w_in.shape
    fw = g_fourier.shape[1]
    alpha = (2.0 * depth) ** 0.25
    row = lambda a: a.reshape(1, -1)

    gd = FOURIER_GROUP_DIM
    cg, sg = _cs(np.outer(np.arange(gd), np.arange(gd)), gd)
    eye = np.eye(LANES // gd)
    bd = np.concatenate([np.kron(eye, cg), -np.kron(eye, sg)], axis=1).astype(np.float32)
    bands = jnp.linspace(1e-4, FILTER_BANDS - 1, FILTER_BANDS, dtype=F32)
    bands_row = jnp.zeros((1, LANES), F32).at[0, 1:1 + FILTER_BANDS].set(bands)
    bands_row = bands_row.at[0, 1 + FILTER_BANDS:1 + 2 * FILTER_BANDS].set(bands)
    consts = dict(bd=jnp.asarray(bd).astype(BF16), bands=bands_row,
                  wf=jnp.asarray(_minor_matrix(MINOR, -1)).astype(BF16))

    xs = [x_prompt, x_sample]
    for l in range(depth):
        emb = filt_w1.shape[1]
        twice = lambda a: jnp.concatenate([row(a), row(a)], axis=1)
        blockdiag = lambda a: jnp.kron(jnp.eye(2, dtype=F32), a)
        w1_pad = jnp.zeros((LANES, filt_w1.shape[2]), F32).at[:emb].set(filt_w1[l])
        wl = filt_w_out[l]
        wl_halves = jnp.stack([jnp.concatenate([wl, jnp.zeros_like(wl)], axis=0),
                               jnp.concatenate([jnp.zeros_like(wl), wl], axis=0)])
        lay = dict(
            w_in=w_in[l].astype(BF16), b_in=row(b_in[l]), w_short=w_short[l], b_short=row(b_short[l]),
            filt_w1=blockdiag(w1_pad),
            filt_b1=twice(filt_b1[l]), filt_w2=blockdiag(filt_w2[l]), filt_b2=twice(filt_b2[l]),
            filt_w3=blockdiag(filt_w3[l]),
            filt_b3=twice(filt_b3[l]), filt_w_out=wl_halves, filt_freq=twice(filt_freq[l]),
            filt_decay=filt_decay[l], hyena_skip=row(hyena_skip[l]), g_fourier=row(g_fourier[l]),
            g_hyena=row(g_hyena[l]), w_out=w_out[l].astype(BF16), b_out=row(b_out[l]),
            ln1_g=row(ln1_g[l]), ln1_b=row(ln1_b[l]),
            w_router_t=w_router[l].T, b_router=b_router[l].reshape(-1, 1))
        routed = [_mixer(x, l == 0, row(ln_in_g), row(ln_in_b), lay, consts, alpha) for x in xs]
        outs = _routed_moe(routed, w_gate_up[l], b_gate_up[l], w_down[l], b_down[l],
                           row(ln2_g[l]), row(ln2_b[l]), alpha)
        xs = [o.reshape(x.shape) for o, x in zip(outs, xs)]
    return (xs[0], xs[1])
```

```python
import functools
import math

import numpy as np
import jax
import jax.numpy as jnp
from jax import lax
from jax.experimental import pallas as pl
from jax.experimental.pallas import tpu as pltpu

F32 = jnp.float32
BF16 = jnp.bfloat16
I32 = jnp.int32
U32 = jnp.uint32

LN_EPS = 1e-5
RMS_EPS = 1e-6
SWIGLU_LIMIT = 7.0
SWIGLU_ALPHA = 1.702
TOP_K = 4
FOURIER_GROUP_DIM = 64
FILTER_BANDS = 16

LANES = 128
MINOR = 128
ROW_TILE = 512
IN_TILE = 1024
IN_SPLIT = 4
OUT_TILE = 1024
MAJOR_BLOCK_BYTES = 12 * 1024 * 1024
TOKEN_TILE = 512
SEG_ALIGN = 16
EXPERT_BLOCK = 512
VMEM_LIMIT = 56 * 1024 * 1024


def _params(n_grid, semantics="parallel"):
    return pltpu.CompilerParams(dimension_semantics=(semantics,) * n_grid,
                                vmem_limit_bytes=VMEM_LIMIT)


def _dot(a, b):
    return jnp.dot(a, b, preferred_element_type=F32)


def _split(a):
    hi = a.astype(BF16)
    lo = (a - hi.astype(F32)).astype(BF16)
    return hi, lo


def _dot3(a, b):
    ah, al = _split(a)
    bh, bl = _split(b)
    return _dot(ah, bh) + _dot(ah, bl) + _dot(al, bh)


def _layer_norm(x, g, b):
    mu = jnp.mean(x, axis=-1, keepdims=True)
    xc = x - mu
    var = jnp.mean(xc * xc, axis=-1, keepdims=True)
    return xc * lax.rsqrt(var + LN_EPS) * g + b


def _rms_norm(x, g):
    return x * lax.rsqrt(jnp.mean(x * x, axis=-1, keepdims=True) + RMS_EPS) * g


def _in_kernel(x_ref, xp_ref, xq_ref, lng_ref, lnb_ref, win_ref, bin_ref, bd_ref, wsh_ref, bsh_ref,
               z_ref, x0_ref, w_ref, *, tiles_per_seq, apply_ln, fw):
    it = pl.program_id(0) % tiles_per_seq
    tm = x_ref.shape[0]

    def norm(x):
        return _layer_norm(x, lng_ref[...], lnb_ref[...]) if apply_ln else x

    halo = norm(jnp.concatenate([xp_ref[...], xq_ref[...]], axis=0))
    projh = _dot(halo.astype(BF16), win_ref[:, fw:]) + bin_ref[:, fw:]
    rp = tm // IN_SPLIT
    projs = [_dot(norm(x_ref[r * rp:(r + 1) * rp, :]).astype(BF16), win_ref[...]) + bin_ref[...]
             for r in range(IN_SPLIT)]
    wsh = wsh_ref[...]
    for r, proj in enumerate(projs):
        sl = slice(r * rp, (r + 1) * rp)
        pf = proj[:, :fw].astype(BF16)
        zs = [_dot(pf[:, c0:c0 + LANES], bd_ref[...]) for c0 in range(0, fw, LANES)]
        z_ref[0, sl, :] = _pack_pair(jnp.concatenate([z[:, :LANES] for z in zs], axis=1),
                                     jnp.concatenate([z[:, LANES:] for z in zs], axis=1))

        ph = proj[:, fw:]
        prev_row = projs[r - 1][rp - 1:rp, fw:] if r > 0 else jnp.where(it > 0, projh[7:8], 0.0)
        next_row = (projs[r + 1][0:1, fw:] if r + 1 < IN_SPLIT
                    else jnp.where(it < tiles_per_seq - 1, projh[8:9], 0.0))
        rows = lax.broadcasted_iota(I32, ph.shape, 0)
        up = jnp.where(rows == 0, prev_row, pltpu.roll(ph, 1, 0))
        dn = jnp.where(rows == rp - 1, next_row, pltpu.roll(ph, rp - 1, 0))
        uc = up * wsh[0:1] + ph * wsh[1:2] + dn * wsh[2:3] + bsh_ref[...]
        hw = uc.shape[1] // 3
        x0_ref[sl, :] = uc[:, :hw]
        w_ref[sl, :] = uc[:, 2 * hw:] * uc[:, hw:2 * hw]


def _in_proj(x2d, batch, seq, fw, lng, lnb, win_bf, b_in, bd_bf, w_short, b_short, apply_ln):
    t, d = x2d.shape
    inw = win_bf.shape[1]
    hw = (inw - fw) // 3
    tm = IN_TILE
    tps = seq // tm
    r8 = tm // 8
    nblk8 = t // 8
    const = lambda i: (0, 0)
    kern = functools.partial(_in_kernel, tiles_per_seq=tps, apply_ln=apply_ln, fw=fw)
    return pl.pallas_call(
        kern,
        grid=(t // tm,),
        in_specs=[
            pl.BlockSpec((tm, d), lambda i: (i, 0)),
            pl.BlockSpec((8, d), lambda i: (jnp.maximum(i * r8 - 1, 0), 0)),
            pl.BlockSpec((8, d), lambda i: (jnp.minimum((i + 1) * r8, nblk8 - 1), 0)),
            pl.BlockSpec((1, d), const), pl.BlockSpec((1, d), const),
            pl.BlockSpec((d, inw), const), pl.BlockSpec((1, inw), const),
            pl.BlockSpec(bd_bf.shape, const),
            pl.BlockSpec((3, 3 * hw), const), pl.BlockSpec((1, 3 * hw), const),
        ],
        out_specs=[
            pl.BlockSpec((1, tm, fw), lambda i: (i // tps, i % tps, 0)),
            pl.BlockSpec((tm, hw), lambda i: (i, 0)),
            pl.BlockSpec((tm, hw), lambda i: (i, 0)),
        ],
        out_shape=[
            jax.ShapeDtypeStruct((batch, seq, fw), U32),
            jax.ShapeDtypeStruct((t, hw), F32),
            jax.ShapeDtypeStruct((t, hw), F32),
        ],
        compiler_params=_params(1),
        name="in_proj",
    )(x2d, x2d, x2d, lng, lnb, win_bf, b_in, bd_bf, w_short, b_short)


FILTER_HALO = 16


def _filter_kernel(bands_ref, w1_ref, b1_ref, w2_ref, b2_ref, w3_ref, b3_ref, wl_ref, fr_ref, dec_ref,
                   k_ref, asum_ref, tc_ref, ts_ref, kf_buf, kb_buf, sems, *, seq):
    i = pl.program_id(0)
    nsteps = pl.num_programs(0)
    tile, c = kf_buf.shape
    ext = tc_ref.shape[0]
    rep = c // LANES
    w_unit = 2.0 * math.pi / seq
    bands = bands_ref[...]

    @pl.when(i == 0)
    def _():
        arg = bands * (w_unit * lax.broadcasted_iota(I32, (ext, LANES), 0).astype(F32))
        tc_ref[...] = jnp.cos(arg)
        ts_ref[...] = jnp.sin(arg)
        asum_ref[...] = jnp.zeros_like(asum_ref)

    j0 = i * tile
    base = bands * (w_unit * j0.astype(F32))
    c0, s0 = jnp.cos(base), jnp.sin(base)
    tc, ts = tc_ref[...], ts_ref[...]
    cosf = c0 * tc - s0 * ts
    sinf = s0 * tc + c0 * ts
    j = j0 + lax.broadcasted_iota(I32, (ext, LANES), 0)
    t = j.astype(F32) * (1.0 / (seq - 1))
    lane = lax.broadcasted_iota(I32, (ext, LANES), 1)
    feat = jnp.where(lane == 0, t,
                     jnp.where(lane <= FILTER_BANDS, cosf,
                               jnp.where(lane <= 2 * FILTER_BANDS, -sinf, 0.0)))
    half = ext // 2
    fr = fr_ref[...]
    h = jnp.concatenate([feat[:half], feat[half:]], axis=1)
    h = jnp.sin(fr * (_dot3(h, w1_ref[...]) + b1_ref[...]))
    h = jnp.sin(fr * (_dot3(h, w2_ref[...]) + b2_ref[...]))
    h = jnp.sin(fr * (_dot3(h, w3_ref[...]) + b3_ref[...]))
    hl = jnp.concatenate([_dot3(h, wl_ref[0]), _dot3(h, wl_ref[1])], axis=0)
    t4 = jnp.concatenate([t] * rep, axis=1)
    j4 = jnp.concatenate([j] * rep, axis=1)
    dec = jnp.abs(dec_ref[...])
    hf = hl[:, :c] * jnp.exp(-t4 * dec[0:1])
    hb = hl[:, c:] * jnp.exp(-t4 * dec[1:2])
    hb = jnp.where(j4 < seq, hb, 0.0)
    asum_ref[...] += jnp.sum(jnp.abs(hf[:tile]) + jnp.abs(hb[:tile]), axis=0, keepdims=True)

    rev = (lax.broadcasted_iota(I32, (tile, ext), 1) == tile - lax.broadcasted_iota(I32, (tile, ext), 0))
    rev = jnp.where(rev, 1.0, 0.0).astype(BF16)
    b1 = hb.astype(BF16)
    r1 = hb - b1.astype(F32)
    b2 = r1.astype(BF16)
    b3 = (r1 - b2.astype(F32)).astype(BF16)
    kb = _dot(rev, b1) + _dot(rev, b2) + _dot(rev, b3)

    def copies():
        return (pltpu.make_async_copy(kf_buf, k_ref.at[pl.ds(pl.multiple_of(j0, tile), tile)], sems.at[0]),
                pltpu.make_async_copy(kb_buf, k_ref.at[pl.ds(pl.multiple_of(2 * seq - j0 - tile, tile), tile)],
                                      sems.at[1]))

    @pl.when(i > 0)
    def _():
        for cp in copies():
            cp.wait()

    kf_buf[...] = jnp.where(j4[:tile] == 0, hf[:tile] + hb[:tile], hf[:tile])
    kb_buf[...] = kb
    for cp in copies():
        cp.start()

    @pl.when(i == nsteps - 1)
    def _():
        for cp in copies():
            cp.wait()


def _hyena_filter(seq, bands_row, w1p, b1, w2, b2, w3, b3, wl, freq, decay):
    c = decay.shape[1]
    tile = ROW_TILE
    ext = tile + FILTER_HALO
    full = lambda a: pl.BlockSpec(a.shape, lambda i: (0,) * a.ndim)
    args = (bands_row, w1p, b1, w2, b2, w3, b3, wl, freq, decay)
    return pl.pallas_call(
        functools.partial(_filter_kernel, seq=seq),
        grid=(seq // tile,),
        in_specs=[full(a) for a in args],
        out_specs=[pl.BlockSpec(memory_space=pl.ANY), pl.BlockSpec((1, c), lambda i: (0, 0))],
        out_shape=[jax.ShapeDtypeStruct((2 * seq, c), F32), jax.ShapeDtypeStruct((1, c), F32)],
        scratch_shapes=[pltpu.VMEM((ext, LANES), F32), pltpu.VMEM((ext, LANES), F32),
                        pltpu.VMEM((tile, c), F32), pltpu.VMEM((tile, c), F32),
                        pltpu.SemaphoreType.DMA((2,))],
        compiler_params=_params(1, "arbitrary"),
        name="hyena_filter",
    )(*args)


def _pack_pair(re, im):
    rb = lax.bitcast_convert_type(re.astype(BF16).astype(F32), U32)
    ib = lax.bitcast_convert_type(im.astype(BF16).astype(F32), U32)
    return (rb >> 16) | ib


def _unpack_pair(word):
    re = lax.bitcast_convert_type(word << 16, F32)
    im = lax.bitcast_convert_type(word & jnp.uint32(0xFFFF0000), F32)
    return re, im


def _major_compute(g, xbuf, obuf, slot):
    _, n_ct, bsz, k, sub, lanes = xbuf.shape
    m = obuf.shape[3]
    x2 = xbuf.reshape(2 * n_ct * bsz * k * sub, lanes)
    o2 = obuf.reshape(2 * n_ct * bsz * m * sub, lanes)
    for t in range(n_ct * bsz):
        xbase = (slot * n_ct * bsz + t) * k * sub
        obase = (slot * n_ct * bsz + t) * m * sub
        for b in range(sub):
            xb = x2[pl.ds(xbase + b, k, stride=sub), :]
            if xb.dtype == U32:
                xb = jnp.concatenate(_unpack_pair(xb), axis=0)
            res = _dot(g, xb.astype(BF16))
            if obuf.dtype == U32:
                res = _pack_pair(res[:m], res[m:])
            o2[pl.ds(obase + b, m, stride=sub), :] = res


def _major_kernel(g_ref, x_hbm, o_hbm, xbuf, obuf, isem, osem, *, nj):
    s = pl.program_id(0)
    ns = pl.num_programs(0)
    slot = s % 2
    _, n_ct, bsz, k, sub, lanes = xbuf.shape
    m = obuf.shape[3]

    def copies(step, sl, hbm, buf, sem, inbound):
        i = step // nj
        j = step % nj
        out = []
        for ct in range(n_ct):
            tile = hbm.at[pl.ds(i * bsz, bsz), :, pl.ds(pl.multiple_of(j * sub, sub), sub),
                          pl.ds(ct * lanes, lanes)]
            out.append(pltpu.make_async_copy(tile, buf.at[sl, ct], sem.at[sl]) if inbound
                       else pltpu.make_async_copy(buf.at[sl, ct], tile, sem.at[sl]))
        return out

    def start(cps):
        for cp in cps:
            cp.start()

    def wait(cps):
        for cp in cps:
            cp.wait()

    @pl.when(s == 0)
    def _():
        start(copies(0, 0, x_hbm, xbuf, isem, True))

    @pl.when(s + 1 < ns)
    def _():
        start(copies(s + 1, 1 - slot, x_hbm, xbuf, isem, True))

    wait(copies(s, slot, x_hbm, xbuf, isem, True))

    @pl.when(s >= 2)
    def _():
        wait(copies(s - 2, slot, o_hbm, obuf, osem, False))

    _major_compute(g_ref[...], xbuf, obuf, slot)
    start(copies(s, slot, o_hbm, obuf, osem, False))

    @pl.when(s == ns - 1)
    def _():
        wait(copies(s, slot, o_hbm, obuf, osem, False))

        @pl.when(s >= 1)
        def _():
            wait(copies(s - 1, 1 - slot, o_hbm, obuf, osem, False))


def _dft_major(g_bf, x4, pack_out):
    b, k, n2, c = x4.shape
    m = g_bf.shape[0] // 2 if pack_out else g_bf.shape[0]
    out_dtype = U32 if pack_out else F32
    sub = 8
    n_ct = c // LANES
    bsz = max(1, min(b, MAJOR_BLOCK_BYTES // ((k + m) * sub * c * 4)))
    while b % bsz:
        bsz -= 1
    nj = n2 // sub
    return pl.pallas_call(
        functools.partial(_major_kernel, nj=nj),
        grid=((b // bsz) * nj,),
        in_specs=[pl.BlockSpec(g_bf.shape, lambda s: (0, 0)), pl.BlockSpec(memory_space=pl.ANY)],
        out_specs=pl.BlockSpec(memory_space=pl.ANY),
        out_shape=jax.ShapeDtypeStruct((b, m, n2, c), out_dtype),
        scratch_shapes=[pltpu.VMEM((2, n_ct, bsz, k, sub, LANES), x4.dtype),
                        pltpu.VMEM((2, n_ct, bsz, m, sub, LANES), out_dtype),
                        pltpu.SemaphoreType.DMA((2,)), pltpu.SemaphoreType.DMA((2,))],
        compiler_params=_params(1, "arbitrary"),
        name="dft_major",
    )(g_bf, x4)


def _minor_kernel(*refs, mode, k1b):
    if mode == "conv":
        y_ref, twb_ref, twr_ref, wf_ref, wi_ref, kf_ref, o_ref, ks_ref = refs
    else:
        y_ref, twb_ref, twr_ref, wf_ref, o_ref = refs
    n2 = y_ref.shape[2]
    ct = y_ref.shape[3]
    rep = ct // LANES

    def forward(packed, cs, sn, w):
        re, im = _unpack_pair(packed)
        return _dot(w, jnp.concatenate([re * cs + im * sn, im * cs - re * sn], axis=0).astype(BF16))

    cb, sb = twb_ref[0, 0], twb_ref[1, 0]
    for j in range(k1b):
        cr, sr = twr_ref[0, j], twr_ref[1, j]
        cs = jnp.concatenate([cb * cr - sb * sr] * rep, axis=1)
        sn = jnp.concatenate([sb * cr + cb * sr] * rep, axis=1)
        if mode == "real":
            o_ref[0, :, j, :] = forward(y_ref[0, j], cs, sn, wf_ref[...])
            continue

        @pl.when(pl.program_id(1) == 0)
        def _():
            ks_ref[j] = forward(kf_ref[0, j], cs, sn, wf_ref[...])

        z = forward(y_ref[0, j], cs, sn, wf_ref[...])
        zr, zi = z[:n2], z[n2:]
        kr = ks_ref[j, :n2]
        ki = ks_ref[j, n2:]
        p = jnp.concatenate([zr * kr - zi * ki, zr * ki + zi * kr], axis=0).astype(BF16)
        q = _dot(wi_ref[...], p)
        qr, qi = q[:n2], q[n2:]
        o_ref[0, j] = _pack_pair(qr * cs - qi * sn, qi * cs + qr * sn)


def _dft_minor(y4, wf_bf, mode, wi_bf=None, kf4=None):
    b, n1, n2, c = y4.shape
    k1b = min(8, n1)
    ct = c
    tw_block, tw_rem = _twiddles(n1, n2, k1b)
    grid = (n1 // k1b, b)
    dspec = pl.BlockSpec((1, k1b, n2, ct), lambda j, i: (i, j, 0, 0))
    in_specs = [dspec,
                pl.BlockSpec((2, 1, n2, LANES), lambda j, i: (0, j, 0, 0)),
                pl.BlockSpec((2, k1b, n2, LANES), lambda j, i: (0, 0, 0, 0)),
                pl.BlockSpec(wf_bf.shape, lambda j, i: (0, 0))]
    args = [y4, tw_block, tw_rem, wf_bf]
    scratch = []
    if mode == "conv":
        in_specs += [pl.BlockSpec(wi_bf.shape, lambda j, i: (0, 0)),
                     pl.BlockSpec((1, k1b, n2, ct), lambda j, i: (0, j, 0, 0))]
        args += [wi_bf, kf4]
        scratch = [pltpu.VMEM((k1b, 2 * n2, ct), F32)]
    if mode == "real":
        out_spec = pl.BlockSpec((1, n2, k1b, ct), lambda j, i: (i, 0, j, 0))
        out_shape = jax.ShapeDtypeStruct((b, n2, n1, c), F32)
    else:
        out_spec = dspec
        out_shape = jax.ShapeDtypeStruct((b, n1, n2, c), U32)
    return pl.pallas_call(
        functools.partial(_minor_kernel, mode=mode, k1b=k1b),
        grid=grid, in_specs=in_specs, out_specs=out_spec, out_shape=out_shape,
        scratch_shapes=scratch,
        compiler_params=pltpu.CompilerParams(dimension_semantics=("parallel", "arbitrary"),
                                             vmem_limit_bytes=VMEM_LIMIT),
        name="dft_minor_" + mode,
    )(*args)


def _cs(num, den):
    ang = 2.0 * np.pi * (num % den) / den
    return np.cos(ang), np.sin(ang)


def _complex_block(c, s, sign):
    return np.block([[c, -sign * s], [sign * s, c]]).astype(np.float32)


def _major_matrix(n1, a_in, a_out, sign, real_in=False):
    c, s = _cs(np.outer(np.arange(a_out), np.arange(a_in)), n1)
    g = _complex_block(c, s, sign)
    return g[:, :a_in] if real_in else g


def _minor_matrix(n2, sign):
    c, s = _cs(np.outer(np.arange(n2), np.arange(n2)), n2)
    return _complex_block(c, s, sign)


def _twiddles(n1, n2, k1b):
    def table(k1):
        c, s = _cs(np.outer(k1, np.arange(n2)), n1 * n2)
        tw = jnp.asarray(np.stack([c, s]).astype(np.float32))
        return jnp.broadcast_to(tw[..., None], (2, len(k1), n2, LANES))
    return table(np.arange(0, n1, k1b)), table(np.arange(k1b))


def _out_kernel(yf_ref, yc_ref, x0_ref, w_ref, x_ref, lng_ref, lnb_ref, inv_ref, skip_ref, gf_ref, gh_ref,
                wo_ref, bo_ref, g1_ref, b1_ref, wr_ref, br_ref, x1_ref, idx_ref, gate_ref, cnt_ref,
                *, alpha, apply_ln, top_k):
    c = yc_ref.shape[1]
    tt = idx_ref.shape[2]
    for h in range(idx_ref.shape[0]):
        sl = slice(h * tt, (h + 1) * tt)
        yh = x0_ref[sl, :] * (yc_ref[sl, :] * inv_ref[...] + skip_ref[...] * w_ref[sl, :])
        mf = _rms_norm(yf_ref[sl, :], gf_ref[...]).astype(BF16)
        mh = _rms_norm(yh, gh_ref[...]).astype(BF16)
        m = _dot(mf, wo_ref[:c]) + _dot(mh, wo_ref[c:]) + bo_ref[...]
        xn = _layer_norm(x_ref[sl, :], lng_ref[...], lnb_ref[...]) if apply_ln else x_ref[sl, :]
        x1 = _layer_norm(alpha * xn + m, g1_ref[...], b1_ref[...])
        x1_ref[sl, :] = x1
        _route_tile(x1, wr_ref[...], br_ref[...], idx_ref.at[h], gate_ref.at[h], cnt_ref.at[h], top_k)


def _out_proj(yf, yc, x0, w, x, lng, lnb, inv_norm, skip, gf, gh, wo_bf, bo, g1, b1, wr_t, br_col, alpha,
              apply_ln):
    t, d = x.shape
    c = yc.shape[1]
    e = wr_t.shape[0]
    tt = TOKEN_TILE
    per = OUT_TILE // tt
    tm = OUT_TILE
    nt = t // tt
    row = lambda i: (i, 0)
    const = lambda i: (0, 0)
    tile3 = lambda i: (i, 0, 0)
    return pl.pallas_call(
        functools.partial(_out_kernel, alpha=alpha, apply_ln=apply_ln, top_k=TOP_K),
        grid=(t // tm,),
        in_specs=[
            pl.BlockSpec((tm, c), row),
            pl.BlockSpec((tm, c), row), pl.BlockSpec((tm, c), row), pl.BlockSpec((tm, c), row),
            pl.BlockSpec((tm, d), row),
            pl.BlockSpec((1, d), const), pl.BlockSpec((1, d), const),
            pl.BlockSpec((1, c), const), pl.BlockSpec((1, c), const),
            pl.BlockSpec((1, c), const), pl.BlockSpec((1, c), const),
            pl.BlockSpec(wo_bf.shape, const), pl.BlockSpec((1, d), const),
            pl.BlockSpec((1, d), const), pl.BlockSpec((1, d), const),
            pl.BlockSpec((e, d), const), pl.BlockSpec((e, 1), const),
        ],
        out_specs=[pl.BlockSpec((tm, d), row),
                   pl.BlockSpec((per, 2 * TOP_K, tt), tile3),
                   pl.BlockSpec((per, 8, tt), tile3),
                   pl.BlockSpec((per, e, 1), tile3)],
        out_shape=[jax.ShapeDtypeStruct((t, d), F32),
                   jax.ShapeDtypeStruct((nt, 2 * TOP_K, tt), I32),
                   jax.ShapeDtypeStruct((nt, 8, tt), F32),
                   jax.ShapeDtypeStruct((nt, e, 1), I32)],
        compiler_params=_params(1),
        name="out_proj",
    )(yf, yc, x0, w, x, lng, lnb, inv_norm, skip, gf, gh, wo_bf, bo, g1, b1, wr_t, br_col)


def _route_tile(x1, wr, br, idx_ref, gate_ref, cnt_ref, top_k):
    tt = x1.shape[0]
    e = wr.shape[0]
    nt = (((1,), (1,)), ((), ()))
    xh, xl = _split(x1)
    wh, wl = _split(wr)
    logits = (lax.dot_general(wh, xh, nt, preferred_element_type=F32)
              + lax.dot_general(wh, xl, nt, preferred_element_type=F32)
              + lax.dot_general(wl, xh, nt, preferred_element_type=F32)) + br
    iota_e = lax.broadcasted_iota(I32, (e, tt), 0).astype(F32)
    l = logits
    tops, idxs, hots = [], [], []
    for _ in range(top_k):
        m = jnp.max(l, axis=0, keepdims=True)
        idx = jnp.min(jnp.where(l == m, iota_e, float(e)), axis=0, keepdims=True)
        hot = iota_e == idx
        l = jnp.where(hot, -jnp.inf, l)
        tops.append(m)
        idxs.append(idx)
        hots.append(hot)
    ex = [jnp.exp(m - tops[0]) for m in tops]
    den = ex[0]
    for v in ex[1:]:
        den = den + v
    gates = [v / den for v in ex]
    hot_all = jnp.zeros((e, tt), F32)
    for hot in hots:
        hot_all = jnp.where(hot, 1.0, hot_all)
    upper = (lax.broadcasted_iota(I32, (tt, tt), 0) < lax.broadcasted_iota(I32, (tt, tt), 1))
    before = _dot(hot_all.astype(BF16), jnp.where(upper, 1.0, 0.0).astype(BF16))
    ranks = [jnp.sum(jnp.where(hot, before, 0.0), axis=0, keepdims=True).astype(I32) for hot in hots]
    pad = 8 - top_k
    idx_ref[...] = jnp.concatenate([v.astype(I32) for v in idxs] + ranks, axis=0)
    gate_ref[...] = jnp.concatenate(gates + [jnp.zeros((pad, tt), F32)], axis=0)
    cnt_ref[...] = jnp.sum(hot_all, axis=1, keepdims=True).astype(I32)


def _dual_specs(shape, nt0):
    first = pl.BlockSpec(shape, lambda i, *_: (jnp.minimum(i, nt0 - 1),) + (0,) * (len(shape) - 1))
    second = pl.BlockSpec(shape, lambda i, *_: (jnp.maximum(i - nt0, 0),) + (0,) * (len(shape) - 1))
    return first, second


SEG_PIECES = 3
SEG_CHUNK = SEG_ALIGN << SEG_PIECES
PIECE_CLASSES = SEG_PIECES + 1
PIECE_CAP = 32
ZERO_CHUNK = 256


def _segment_copies(first, count, loff_s, len_s, goff_s, local, remote, sem, to_remote, wait):
    def copy(lstart, gstart, size):
        lref = local.at[pl.ds(pl.multiple_of(lstart, SEG_ALIGN), size)]
        rref = remote.at[pl.ds(pl.multiple_of(gstart, SEG_ALIGN), size)]
        cp = (pltpu.make_async_copy(lref, rref, sem) if to_remote
              else pltpu.make_async_copy(rref, lref, sem))
        if wait:
            cp.wait()
        else:
            cp.start()

    def body(e, carry):
        s = first + e
        n = len_s[s] // SEG_ALIGN
        lo = 0 if loff_s is None else loff_s[s]
        go = goff_s[s]
        chunks = n >> SEG_PIECES

        def chunk(i, c):
            copy(lo + i * SEG_CHUNK, go + i * SEG_CHUNK, SEG_CHUNK)
            return c

        lax.fori_loop(0, chunks, chunk, 0)
        rem = n & ((1 << SEG_PIECES) - 1)
        base = chunks * SEG_CHUNK
        for p in reversed(range(SEG_PIECES)):
            start = base + ((rem >> (p + 1)) << (p + 1)) * SEG_ALIGN

            @pl.when(((rem >> p) & 1) == 1)
            def _():
                copy(lo + start, go + start, SEG_ALIGN << p)
        return carry

    lax.fori_loop(0, count, body, 0)


def _piece_copies(step, lo_t, go_t, cnt_t, local, remote, sem, to_remote):
    for c in range(PIECE_CLASSES):
        size = SEG_ALIGN << c
        base = (step * PIECE_CLASSES + c) * PIECE_CAP

        def body(i, carry, size=size, base=base):
            lref = local.at[pl.ds(pl.multiple_of(lo_t[base + i], SEG_ALIGN), size)]
            rref = remote.at[pl.ds(pl.multiple_of(go_t[base + i], SEG_ALIGN), size)]
            cp = (pltpu.make_async_copy(lref, rref, sem) if to_remote
                  else pltpu.make_async_copy(rref, lref, sem))
            cp.start()
            return carry

        lax.fori_loop(0, cnt_t[step * PIECE_CLASSES + c], body, 0)


def _piece_tables(seg, loff, goff):
    nt, e = seg.shape
    n = seg // SEG_ALIGN
    chunks = n >> SEG_PIECES
    rem = n & ((1 << SEG_PIECES) - 1)
    base = chunks * SEG_CHUNK

    def compact(valid, off, lo, go):
        slots = valid.shape[1]
        key = jnp.where(valid, jnp.arange(slots, dtype=I32)[None, :], slots)
        _, lo_sorted, go_sorted = lax.sort((key, lo + off, go + off), dimension=1, num_keys=1)
        return lo_sorted[:, :PIECE_CAP], go_sorted[:, :PIECE_CAP], jnp.sum(valid, axis=1)

    tables = []
    for c in range(SEG_PIECES):
        off = base + ((rem >> (c + 1)) << (c + 1)) * SEG_ALIGN
        tables.append(compact(((rem >> c) & 1) == 1, off, loff, goff))
    max_chunks = TOKEN_TILE // SEG_CHUNK
    i = jnp.arange(max_chunks, dtype=I32)[None, None, :]
    rep = lambda a: jnp.broadcast_to(a[:, :, None], (nt, e, max_chunks)).reshape(nt, e * max_chunks)
    tables.append(compact((i < chunks[:, :, None]).reshape(nt, e * max_chunks),
                          jnp.broadcast_to(i * SEG_CHUNK, (nt, e, max_chunks)).reshape(nt, e * max_chunks),
                          rep(loff), rep(goff)))
    flat = lambda k: jnp.stack([t[k] for t in tables], axis=1).reshape(-1).astype(I32)
    return flat(0), flat(1), flat(2)


def _wait_tile(step, n_experts, loff_s, len_s, local, remote, sem, to_remote):
    last = step * n_experts + n_experts - 1
    n = (loff_s[last] + len_s[last]) // SEG_ALIGN
    for p in range((local.shape[0] // SEG_ALIGN).bit_length()):
        size = SEG_ALIGN << p

        @pl.when(((n >> p) & 1) == 1)
        def _():
            lref = local.at[pl.ds(0, size)]
            rref = remote.at[pl.ds(0, size)]
            cp = (pltpu.make_async_copy(lref, rref, sem) if to_remote
                  else pltpu.make_async_copy(rref, lref, sem))
            cp.wait()


def _dispatch_kernel(loff_s, seg_s, lo_t, go_t, cnt_t, plen_s, poff_s, nb_s, xa_ref, xb_ref, idx_ref, xs_ref,
                     pos_ref,
                     buf, sems, *, n_experts, top_k, nt0):
    j = pl.program_id(0)
    nt = pl.num_programs(0)
    slot = j % 2
    tt = xa_ref.shape[0]
    lr = buf.shape[1]
    idx = idx_ref[0]
    tope = idx[:top_k]
    base = jnp.zeros_like(tope)
    for e in range(n_experts):
        base = jnp.where(tope == e, loff_s[j * n_experts + e], base)
    pos = base + idx[top_k:]
    pos_ref[0] = jnp.concatenate([pos, jnp.zeros((8 - top_k, tt), I32)], axis=0)
    rows = lax.broadcasted_iota(jnp.int16, (lr, tt), 0)
    pos16 = pos.astype(jnp.int16)
    perm = jnp.zeros((lr, tt), BF16)
    for k in range(top_k):
        perm = jnp.where(rows == pos16[k:k + 1], jnp.ones((lr, tt), BF16), perm)
    x = jnp.where(j < nt0, xa_ref[...], xb_ref[...])
    buf[slot] = _dot(perm, x.astype(BF16)).astype(BF16)

    def drained(step, s):
        _wait_tile(step, n_experts, loff_s, seg_s, buf.at[s], xs_ref, sems.at[s], True)

    _piece_copies(j, lo_t, go_t, cnt_t, buf.at[slot], xs_ref, sems.at[slot], True)

    @pl.when(j > 0)
    def _():
        drained(j - 1, 1 - slot)

    @pl.when(j == nt - 1)
    def _():
        drained(j, slot)
        zrows = EXPERT_BLOCK
        buf[0, :zrows] = jnp.zeros((zrows, buf.shape[2]), BF16)
        _segment_copies(0, n_experts, None, plen_s, poff_s, buf.at[0], xs_ref, sems.at[0], True, False)
        _segment_copies(0, n_experts, None, plen_s, poff_s, buf.at[0], xs_ref, sems.at[0], True, True)
        n_tail = xs_ref.shape[0] // zrows - nb_s[0]

        def tail_copy(i):
            dst = xs_ref.at[pl.ds(pl.multiple_of((nb_s[0] + i) * zrows, zrows), zrows)]
            return pltpu.make_async_copy(buf.at[0, pl.ds(0, zrows)], dst, sems.at[0])

        lax.fori_loop(0, n_tail, lambda i, c: (tail_copy(i).start(), c)[1], 0)
        lax.fori_loop(0, n_tail, lambda i, c: (tail_copy(i).wait(), c)[1], 0)


def _dispatch(x1s, idx, loff, seg, pieces, plen, poff, n_blocks, n_rows, local_rows):
    d = x1s[0].shape[1]
    tt = TOKEN_TILE
    nt0 = x1s[0].shape[0] // tt
    nt = nt0 + x1s[1].shape[0] // tt
    e = loff.shape[0] // nt
    xa, xb = _dual_specs((tt, d), nt0)
    grid_spec = pltpu.PrefetchScalarGridSpec(
        num_scalar_prefetch=8,
        grid=(nt,),
        in_specs=[xa, xb, pl.BlockSpec((1, 2 * TOP_K, tt), lambda i, *_: (i, 0, 0))],
        out_specs=[pl.BlockSpec(memory_space=pl.ANY),
                   pl.BlockSpec((1, 8, tt), lambda i, *_: (i, 0, 0))],
        scratch_shapes=[pltpu.VMEM((2, local_rows, d), BF16), pltpu.SemaphoreType.DMA((2,))],
    )
    return pl.pallas_call(
        functools.partial(_dispatch_kernel, n_experts=e, top_k=TOP_K, nt0=nt0),
        grid_spec=grid_spec,
        out_shape=[jax.ShapeDtypeStruct((n_rows, d), BF16), jax.ShapeDtypeStruct((nt, 8, tt), I32)],
        compiler_params=_params(1, "arbitrary"),
        name="moe_dispatch",
    )(loff, seg, *pieces, plen, poff, n_blocks, x1s[0], x1s[1], idx)


def _combine_kernel(loff_s, seg_s, lo_t, go_t, cnt_t, ys_ref, pos_ref, gate_ref, xa_ref, xb_ref, g2_ref, b2_ref,
                    oa_ref, ob_ref, buf, sems, *, n_experts, top_k, alpha, nt0):
    j = pl.program_id(0)
    nt = pl.num_programs(0)
    slot = j % 2
    tt = xa_ref.shape[0]
    lr = buf.shape[1]

    def prefetch(step, s):
        last = step * n_experts + n_experts - 1
        covered = loff_s[last] + seg_s[last]
        for r0 in range(0, lr, ZERO_CHUNK):
            @pl.when(r0 + ZERO_CHUNK > covered)
            def _(r0=r0):
                buf[s, r0:r0 + ZERO_CHUNK] = jnp.zeros((ZERO_CHUNK, buf.shape[2]), BF16)
        _piece_copies(step, lo_t, go_t, cnt_t, buf.at[s], ys_ref, sems.at[s], False)

    @pl.when(j == 0)
    def _():
        prefetch(0, 0)

    @pl.when(j + 1 < nt)
    def _():
        prefetch(j + 1, 1 - slot)

    _wait_tile(j, n_experts, loff_s, seg_s, buf.at[slot], ys_ref, sems.at[slot], False)
    pos = pos_ref[0].astype(jnp.int16)
    gate = gate_ref[0].astype(BF16)
    rows = lax.broadcasted_iota(jnp.int16, (lr, tt), 0)
    wsel = jnp.zeros((lr, tt), BF16)
    for k in range(top_k):
        wsel = jnp.where(rows == pos[k:k + 1], gate[k:k + 1], wsel)
    moe = lax.dot_general(wsel, buf[slot], (((0,), (0,)), ((), ())), preferred_element_type=F32)
    x1 = jnp.where(j < nt0, xa_ref[...], xb_ref[...])
    out = _layer_norm(alpha * x1 + moe, g2_ref[...], b2_ref[...])

    @pl.when(j < nt0)
    def _():
        oa_ref[...] = out

    @pl.when(j >= nt0)
    def _():
        ob_ref[...] = out


def _combine(ys, pos_tm, gate_tm, x1s, g2, b2, loff, seg, pieces, local_rows, alpha):
    d = x1s[0].shape[1]
    tt = TOKEN_TILE
    nt0 = x1s[0].shape[0] // tt
    nt = nt0 + x1s[1].shape[0] // tt
    e = loff.shape[0] // nt
    xa, xb = _dual_specs((tt, d), nt0)
    grid_spec = pltpu.PrefetchScalarGridSpec(
        num_scalar_prefetch=5,
        grid=(nt,),
        in_specs=[pl.BlockSpec(memory_space=pl.ANY),
                  pl.BlockSpec((1, 8, tt), lambda i, *_: (i, 0, 0)),
                  pl.BlockSpec((1, 8, tt), lambda i, *_: (i, 0, 0)),
                  xa, xb,
                  pl.BlockSpec((1, d), lambda i, *_: (0, 0)),
                  pl.BlockSpec((1, d), lambda i, *_: (0, 0))],
        out_specs=list(_dual_specs((tt, d), nt0)),
        scratch_shapes=[pltpu.VMEM((2, local_rows, d), BF16), pltpu.SemaphoreType.DMA((2,))],
    )
    return pl.pallas_call(
        functools.partial(_combine_kernel, n_experts=e, top_k=TOP_K, alpha=alpha, nt0=nt0),
        grid_spec=grid_spec,
        out_shape=[jax.ShapeDtypeStruct(x.shape, F32) for x in x1s],
        compiler_params=_params(1, "arbitrary"),
        name="moe_combine",
    )(loff, seg, *pieces, ys, pos_tm, gate_tm, x1s[0], x1s[1], g2, b2)


def _expert_kernel(be_s, nb_s, xs_ref, wgu_ref, bgu_ref, wd_ref, bd_ref, ys_ref, wgu_bf, wd_bf):
    b = pl.program_id(0)
    active = b < nb_s[0]
    fresh = jnp.logical_or(b == 0, be_s[b] != be_s[jnp.maximum(b - 1, 0)])

    @pl.when(jnp.logical_and(active, fresh))
    def _():
        wgu_bf[...] = wgu_ref[0].astype(BF16)
        wd_bf[...] = wd_ref[0].astype(BF16)

    @pl.when(active)
    def _():
        dff = wd_ref.shape[1]
        gu = _dot(xs_ref[...], wgu_bf[...]) + bgu_ref[0]
        gate = jnp.minimum(gu[:, :dff], SWIGLU_LIMIT)
        up = jnp.clip(gu[:, dff:], -SWIGLU_LIMIT, SWIGLU_LIMIT)
        h = (up + 1.0) * (gate * jax.nn.sigmoid(SWIGLU_ALPHA * gate))
        ys_ref[...] = (_dot(h.astype(BF16), wd_bf[...]) + bd_ref[0]).astype(BF16)

    @pl.when(jnp.logical_not(active))
    def _():
        ys_ref[...] = jnp.zeros_like(ys_ref)


def _experts(xs, block_expert, n_blocks, wgu, bgu, wd, bdn):
    n_rows, d = xs.shape
    bm = EXPERT_BLOCK
    f2 = wgu.shape[2]
    dff = wd.shape[1]
    blk = lambda b, be, nb: (jnp.minimum(b, nb[0] - 1), 0)
    wsel = lambda b, be, nb: (be[b], 0, 0)
    grid_spec = pltpu.PrefetchScalarGridSpec(
        num_scalar_prefetch=2,
        grid=(n_rows // bm,),
        in_specs=[pl.BlockSpec((bm, d), blk),
                  pl.BlockSpec((1, d, f2), wsel), pl.BlockSpec((1, 1, f2), wsel),
                  pl.BlockSpec((1, dff, d), wsel), pl.BlockSpec((1, 1, d), wsel)],
        out_specs=pl.BlockSpec((bm, d), lambda b, be, nb: (b, 0)),
        scratch_shapes=[pltpu.VMEM((d, f2), BF16), pltpu.VMEM((dff, d), BF16)],
    )
    return pl.pallas_call(
        _expert_kernel,
        grid_spec=grid_spec,
        out_shape=jax.ShapeDtypeStruct((n_rows, d), BF16),
        compiler_params=_params(1, "arbitrary"),
        name="moe_experts",
    )(block_expert, n_blocks, xs, wgu, bgu, wd, bdn)


def _routed_moe(routed, w_gate_up, b_gate_up, w_down, b_down, g2, b2, alpha):
    x1s = [r[0] for r in routed]
    idx, gate, cnt = (jnp.concatenate([r[i] for r in routed], axis=0) for i in (1, 2, 3))
    t = x1s[0].shape[0] + x1s[1].shape[0]
    e = cnt.shape[1]
    tt = TOKEN_TILE
    nt = t // tt
    bm = EXPERT_BLOCK

    cnt = cnt.reshape(nt, e)
    seg = (cnt + SEG_ALIGN - 1) // SEG_ALIGN * SEG_ALIGN
    loff = jnp.cumsum(seg, axis=1) - seg
    per_expert = jnp.sum(seg, axis=0)
    padded = (per_expert + bm - 1) // bm * bm
    blocks_end = jnp.cumsum(padded) // bm
    start = jnp.cumsum(padded) - padded
    goff = start[None, :] + jnp.cumsum(seg, axis=0) - seg
    local_rows = TOP_K * tt + e * SEG_ALIGN
    n_rows_max = -(-(TOP_K * t + (SEG_ALIGN - 1) * e * nt) // bm) * bm + e * bm
    n_blocks = blocks_end[-1:].astype(I32)
    bidx = jnp.minimum(jnp.arange(n_rows_max // bm, dtype=I32), n_blocks[0] - 1)
    block_expert = jnp.minimum(jnp.sum(bidx[:, None] >= blocks_end[None, :], axis=1), e - 1).astype(I32)
    flat = lambda a: a.reshape(-1).astype(I32)
    plen, poff = flat(padded - per_expert), flat(start + per_expert)

    assert e <= PIECE_CAP and local_rows // SEG_CHUNK <= PIECE_CAP
    pieces = _piece_tables(seg, loff, goff)
    xs, pos = _dispatch(x1s, idx, flat(loff), flat(seg), pieces, plen, poff, n_blocks, n_rows_max,
                        local_rows)
    ys = _experts(xs, block_expert, n_blocks, w_gate_up, b_gate_up.reshape(e, 1, -1),
                  w_down, b_down.reshape(e, 1, -1))
    return _combine(ys, pos, gate, x1s, g2, b2, flat(loff), flat(seg), pieces, local_rows, alpha)


def _mixer(x, apply_ln, lng, lnb, lay, consts, alpha):
    batch, seq, d = x.shape
    t = batch * seq
    fw = lay["g_fourier"].shape[1]
    n2 = MINOR
    x2d = x.reshape(t, d)
    z, x0, w = _in_proj(x2d, batch, seq, fw, lng, lnb, lay["w_in"], lay["b_in"], consts["bd"],
                        lay["w_short"], lay["b_short"], apply_ln)
    c = x0.shape[1]

    n1 = seq // n2
    g = jnp.asarray(_major_matrix(n1, n1, n1, -1)).astype(BF16)
    y = _dft_major(g, z.reshape(batch, n1, n2, fw), True)
    ortho = 1.0 / math.sqrt(seq * FOURIER_GROUP_DIM)
    wf_real = jnp.asarray(_minor_matrix(n2, -1)[:n2] * ortho).astype(BF16)
    yf = _dft_minor(y, wf_real, "real")

    n = 2 * seq
    m1 = n // n2
    kf, asum = _hyena_filter(seq, consts["bands"], lay["filt_w1"], lay["filt_b1"], lay["filt_w2"],
                             lay["filt_b2"], lay["filt_w3"], lay["filt_b3"], lay["filt_w_out"],
                             lay["filt_freq"], lay["filt_decay"])
    gk = jnp.asarray(_major_matrix(m1, m1, m1, -1, real_in=True)).astype(BF16)
    ky = _dft_major(gk, kf.reshape(1, m1, n2, c), True)
    gfw = jnp.asarray(_major_matrix(m1, m1 // 2, m1, -1)).astype(BF16)
    gin = jnp.asarray(_major_matrix(m1, m1, m1 // 2, +1)).astype(BF16)
    pairs = batch // 2
    u = _dft_major(gfw, w.reshape(pairs, m1, n2, c), True)
    wi = jnp.asarray(_minor_matrix(n2, +1) * (1.0 / n)).astype(BF16)
    v = _dft_minor(u, consts["wf"], "conv", wi_bf=wi, kf4=ky)
    yc = _dft_major(gin, v, False)

    inv_norm = 1.0 / asum
    return _out_proj(yf.reshape(t, fw), yc.reshape(t, c), x0, w, x2d, lng, lnb, inv_norm,
                     lay["hyena_skip"], lay["g_fourier"], lay["g_hyena"], lay["w_out"], lay["b_out"],
                     lay["ln1_g"], lay["ln1_b"], lay["w_router_t"], lay["b_router"], alpha, apply_ln)


def kernel(x_prompt, x_sample, ln_in_g, ln_in_b, w_in, b_in, w_short, b_short, filt_w1, filt_b1, filt_w2, filt_b2, filt_w3, filt_b3, filt_w_out, filt_freq, filt_decay, hyena_skip, g_fourier, g_hyena, w_out, b_out, ln1_g, ln1_b, w_router, b_router, w_gate_up, b_gate_up, w_down, b_down, ln2_g, ln2_b):
    depth, d, in_width = w_in.shape
    fw = g_fourier.shape[1]
    alpha = (2.0 * depth) ** 0.25
    row = lambda a: a.reshape(1, -1)

    gd = FOURIER_GROUP_DIM
    cg, sg = _cs(np.outer(np.arange(gd), np.arange(gd)), gd)
    eye = np.eye(LANES // gd)
    bd = np.concatenate([np.kron(eye, cg), -np.kron(eye, sg)], axis=1).astype(np.float32)
    bands = jnp.linspace(1e-4, FILTER_BANDS - 1, FILTER_BANDS, dtype=F32)
    bands_row = jnp.zeros((1, LANES), F32).at[0, 1:1 + FILTER_BANDS].set(bands)
    bands_row = bands_row.at[0, 1 + FILTER_BANDS:1 + 2 * FILTER_BANDS].set(bands)
    consts = dict(bd=jnp.asarray(bd).astype(BF16), bands=bands_row,
                  wf=jnp.asarray(_minor_matrix(MINOR, -1)).astype(BF16))

    xs = [x_prompt, x_sample]
    for l in range(depth):
        emb = filt_w1.shape[1]
        twice = lambda a: jnp.concatenate([row(a), row(a)], axis=1)
        blockdiag = lambda a: jnp.kron(jnp.eye(2, dtype=F32), a)
        w1_pad = jnp.zeros((LANES, filt_w1.shape[2]), F32).at[:emb].set(filt_w1[l])
        wl = filt_w_out[l]
        wl_halves = jnp.stack([jnp.concatenate([wl, jnp.zeros_like(wl)], axis=0),
                               jnp.concatenate([jnp.zeros_like(wl), wl], axis=0)])
        lay = dict(
            w_in=w_in[l].astype(BF16), b_in=row(b_in[l]), w_short=w_short[l], b_short=row(b_short[l]),
            filt_w1=blockdiag(w1_pad),
            filt_b1=twice(filt_b1[l]), filt_w2=blockdiag(filt_w2[l]), filt_b2=twice(filt_b2[l]),
            filt_w3=blockdiag(filt_w3[l]),
            filt_b3=twice(filt_b3[l]), filt_w_out=wl_halves, filt_freq=twice(filt_freq[l]),
            filt_decay=filt_decay[l], hyena_skip=row(hyena_skip[l]), g_fourier=row(g_fourier[l]),
            g_hyena=row(g_hyena[l]), w_out=w_out[l].astype(BF16), b_out=row(b_out[l]),
            ln1_g=row(ln1_g[l]), ln1_b=row(ln1_b[l]),
            w_router_t=w_router[l].T, b_router=b_router[l].reshape(-1, 1))
        routed = [_mixer(x, l == 0, row(ln_in_g), row(ln_in_b), lay, consts, alpha) for x in xs]
        outs = _routed_moe(routed, w_gate_up[l], b_gate_up[l], w_down[l], b_down[l],
                           row(ln2_g[l]), row(ln2_b[l]), alpha)
        xs = [o.reshape(x.shape) for o, x in zip(outs, xs)]
    return (xs[0], xs[1])
```

```python
import functools
import math

import numpy as np
import jax
import jax.numpy as jnp
from jax import lax
from jax.experimental import pallas as pl
from jax.experimental.pallas import tpu as pltpu

F32 = jnp.float32
BF16 = jnp.bfloat16
I32 = jnp.int32
U32 = jnp.uint32

LN_EPS = 1e-5
RMS_EPS = 1e-6
SWIGLU_LIMIT = 7.0
SWIGLU_ALPHA = 1.702
TOP_K = 4
FOURIER_GROUP_DIM = 64
FILTER_BANDS = 16

LANES = 128
MINOR = 128
MINOR_K1_BLOCK = 16
ROW_TILE = 512
IN_TILE = 1024
IN_SPLIT = 4
MAJOR_BLOCK_BYTES = 12 * 1024 * 1024
TOKEN_TILE = 512
SEG_ALIGN = 16
EXPERT_BLOCK = 512
VMEM_LIMIT = 56 * 1024 * 1024


def _params(n_grid, semantics="parallel"):
    return pltpu.CompilerParams(dimension_semantics=(semantics,) * n_grid,
                                vmem_limit_bytes=VMEM_LIMIT)


def _dot(a, b):
    return jnp.dot(a, b, preferred_element_type=F32)


def _split(a):
    hi = a.astype(BF16)
    lo = (a - hi.astype(F32)).astype(BF16)
    return hi, lo


def _dot3(a, b):
    ah, al = _split(a)
    bh, bl = _split(b)
    return _dot(ah, bh) + _dot(ah, bl) + _dot(al, bh)


def _layer_norm(x, g, b):
    mu = jnp.mean(x, axis=-1, keepdims=True)
    xc = x - mu
    var = jnp.mean(xc * xc, axis=-1, keepdims=True)
    return xc * lax.rsqrt(var + LN_EPS) * g + b


def _rms_norm(x, g):
    return x * lax.rsqrt(jnp.mean(x * x, axis=-1, keepdims=True) + RMS_EPS) * g


def _in_kernel(x_ref, xp_ref, xq_ref, lng_ref, lnb_ref, win_ref, bin_ref, bd_ref, wsh_ref, bsh_ref,
               z_ref, x0_ref, w_ref, *, tiles_per_seq, apply_ln, fw):
    it = pl.program_id(0) % tiles_per_seq
    tm = x_ref.shape[0]

    def norm(x):
        return _layer_norm(x, lng_ref[...], lnb_ref[...]) if apply_ln else x

    halo = norm(jnp.concatenate([xp_ref[...], xq_ref[...]], axis=0))
    projh = _dot(halo.astype(BF16), win_ref[:, fw:]) + bin_ref[:, fw:]
    rp = tm // IN_SPLIT
    projs = [_dot(norm(x_ref[r * rp:(r + 1) * rp, :]).astype(BF16), win_ref[...]) + bin_ref[...]
             for r in range(IN_SPLIT)]
    wsh = wsh_ref[...]
    for r, proj in enumerate(projs):
        sl = slice(r * rp, (r + 1) * rp)
        pf = proj[:, :fw].astype(BF16)
        zs = [_dot(pf[:, c0:c0 + LANES], bd_ref[...]) for c0 in range(0, fw, LANES)]
        z_ref[0, sl, :] = _pack_pair(jnp.concatenate([z[:, :LANES] for z in zs], axis=1),
                                     jnp.concatenate([z[:, LANES:] for z in zs], axis=1))

        ph = proj[:, fw:]
        prev_row = projs[r - 1][rp - 1:rp, fw:] if r > 0 else jnp.where(it > 0, projh[7:8], 0.0)
        next_row = (projs[r + 1][0:1, fw:] if r + 1 < IN_SPLIT
                    else jnp.where(it < tiles_per_seq - 1, projh[8:9], 0.0))
        rows = lax.broadcasted_iota(I32, ph.shape, 0)
        up = jnp.where(rows == 0, prev_row, pltpu.roll(ph, 1, 0))
        dn = jnp.where(rows == rp - 1, next_row, pltpu.roll(ph, rp - 1, 0))
        uc = up * wsh[0:1] + ph * wsh[1:2] + dn * wsh[2:3] + bsh_ref[...]
        hw = uc.shape[1] // 3
        x0_ref[sl, :] = uc[:, :hw]
        w_ref[sl, :] = uc[:, 2 * hw:] * uc[:, hw:2 * hw]


def _in_proj(x2d, batch, seq, fw, lng, lnb, win_bf, b_in, bd_bf, w_short, b_short, apply_ln):
    t, d = x2d.shape
    inw = win_bf.shape[1]
    hw = (inw - fw) // 3
    tm = IN_TILE
    tps = seq // tm
    r8 = tm // 8
    nblk8 = t // 8
    const = lambda i: (0, 0)
    kern = functools.partial(_in_kernel, tiles_per_seq=tps, apply_ln=apply_ln, fw=fw)
    return pl.pallas_call(
        kern,
        grid=(t // tm,),
        in_specs=[
            pl.BlockSpec((tm, d), lambda i: (i, 0)),
            pl.BlockSpec((8, d), lambda i: (jnp.maximum(i * r8 - 1, 0), 0)),
            pl.BlockSpec((8, d), lambda i: (jnp.minimum((i + 1) * r8, nblk8 - 1), 0)),
            pl.BlockSpec((1, d), const), pl.BlockSpec((1, d), const),
            pl.BlockSpec((d, inw), const), pl.BlockSpec((1, inw), const),
            pl.BlockSpec(bd_bf.shape, const),
            pl.BlockSpec((3, 3 * hw), const), pl.BlockSpec((1, 3 * hw), const),
        ],
        out_specs=[
            pl.BlockSpec((1, tm, fw), lambda i: (i // tps, i % tps, 0)),
            pl.BlockSpec((tm, hw), lambda i: (i, 0)),
            pl.BlockSpec((tm, hw), lambda i: (i, 0)),
        ],
        out_shape=[
            jax.ShapeDtypeStruct((batch, seq, fw), U32),
            jax.ShapeDtypeStruct((t, hw), F32),
            jax.ShapeDtypeStruct((t, hw), F32),
        ],
        compiler_params=_params(1),
        name="in_proj",
    )(x2d, x2d, x2d, lng, lnb, win_bf, b_in, bd_bf, w_short, b_short)


FILTER_HALO = 16


def _filter_kernel(bands_ref, w1_ref, b1_ref, w2_ref, b2_ref, w3_ref, b3_ref, wl_ref, fr_ref, dec_ref,
                   k_ref, asum_ref, tc_ref, ts_ref, kf_buf, kb_buf, sems, *, seq):
    i = pl.program_id(0)
    nsteps = pl.num_programs(0)
    tile, c = kf_buf.shape
    ext = tc_ref.shape[0]
    rep = c // LANES
    w_unit = 2.0 * math.pi / seq
    bands = bands_ref[...]

    @pl.when(i == 0)
    def _():
        arg = bands * (w_unit * lax.broadcasted_iota(I32, (ext, LANES), 0).astype(F32))
        tc_ref[...] = jnp.cos(arg)
        ts_ref[...] = jnp.sin(arg)
        asum_ref[...] = jnp.zeros_like(asum_ref)

    j0 = i * tile
    base = bands * (w_unit * j0.astype(F32))
    c0, s0 = jnp.cos(base), jnp.sin(base)
    tc, ts = tc_ref[...], ts_ref[...]
    cosf = c0 * tc - s0 * ts
    sinf = s0 * tc + c0 * ts
    j = j0 + lax.broadcasted_iota(I32, (ext, LANES), 0)
    t = j.astype(F32) * (1.0 / (seq - 1))
    lane = lax.broadcasted_iota(I32, (ext, LANES), 1)
    feat = jnp.where(lane == 0, t,
                     jnp.where(lane <= FILTER_BANDS, cosf,
                               jnp.where(lane <= 2 * FILTER_BANDS, -sinf, 0.0)))
    half = ext // 2
    fr = fr_ref[...]
    h = jnp.concatenate([feat[:half], feat[half:]], axis=1)
    h = jnp.sin(fr * (_dot3(h, w1_ref[...]) + b1_ref[...]))
    h = jnp.sin(fr * (_dot3(h, w2_ref[...]) + b2_ref[...]))
    h = jnp.sin(fr * (_dot3(h, w3_ref[...]) + b3_ref[...]))
    hl = jnp.concatenate([_dot3(h, wl_ref[0]), _dot3(h, wl_ref[1])], axis=0)
    t4 = jnp.concatenate([t] * rep, axis=1)
    j4 = jnp.concatenate([j] * rep, axis=1)
    dec = jnp.abs(dec_ref[...])
    hf = hl[:, :c] * jnp.exp(-t4 * dec[0:1])
    hb = hl[:, c:] * jnp.exp(-t4 * dec[1:2])
    hb = jnp.where(j4 < seq, hb, 0.0)
    asum_ref[...] += jnp.sum(jnp.abs(hf[:tile]) + jnp.abs(hb[:tile]), axis=0, keepdims=True)

    rev = (lax.broadcasted_iota(I32, (tile, ext), 1) == tile - lax.broadcasted_iota(I32, (tile, ext), 0))
    rev = jnp.where(rev, 1.0, 0.0).astype(BF16)
    b1 = hb.astype(BF16)
    r1 = hb - b1.astype(F32)
    b2 = r1.astype(BF16)
    b3 = (r1 - b2.astype(F32)).astype(BF16)
    kb = _dot(rev, b1) + _dot(rev, b2) + _dot(rev, b3)

    def copies():
        return (pltpu.make_async_copy(kf_buf, k_ref.at[pl.ds(pl.multiple_of(j0, tile), tile)], sems.at[0]),
                pltpu.make_async_copy(kb_buf, k_ref.at[pl.ds(pl.multiple_of(2 * seq - j0 - tile, tile), tile)],
                                      sems.at[1]))

    @pl.when(i > 0)
    def _():
        for cp in copies():
            cp.wait()

    kf_buf[...] = jnp.where(j4[:tile] == 0, hf[:tile] + hb[:tile], hf[:tile])
    kb_buf[...] = kb
    for cp in copies():
        cp.start()

    @pl.when(i == nsteps - 1)
    def _():
        for cp in copies():
            cp.wait()


def _hyena_filter(seq, bands_row, w1p, b1, w2, b2, w3, b3, wl, freq, decay):
    c = decay.shape[1]
    tile = ROW_TILE
    ext = tile + FILTER_HALO
    full = lambda a: pl.BlockSpec(a.shape, lambda i: (0,) * a.ndim)
    args = (bands_row, w1p, b1, w2, b2, w3, b3, wl, freq, decay)
    return pl.pallas_call(
        functools.partial(_filter_kernel, seq=seq),
        grid=(seq // tile,),
        in_specs=[full(a) for a in args],
        out_specs=[pl.BlockSpec(memory_space=pl.ANY), pl.BlockSpec((1, c), lambda i: (0, 0))],
        out_shape=[jax.ShapeDtypeStruct((2 * seq, c), F32), jax.ShapeDtypeStruct((1, c), F32)],
        scratch_shapes=[pltpu.VMEM((ext, LANES), F32), pltpu.VMEM((ext, LANES), F32),
                        pltpu.VMEM((tile, c), F32), pltpu.VMEM((tile, c), F32),
                        pltpu.SemaphoreType.DMA((2,))],
        compiler_params=_params(1, "arbitrary"),
        name="hyena_filter",
    )(*args)


def _pack_pair(re, im):
    rb = lax.bitcast_convert_type(re.astype(BF16).astype(F32), U32)
    ib = lax.bitcast_convert_type(im.astype(BF16).astype(F32), U32)
    return (rb >> 16) | ib


def _unpack_pair(word):
    re = lax.bitcast_convert_type(word << 16, F32)
    im = lax.bitcast_convert_type(word & jnp.uint32(0xFFFF0000), F32)
    return re, im


def _major_compute(g, xbuf, obuf, slot):
    _, n_ct, bsz, k, sub, lanes = xbuf.shape
    m = obuf.shape[3]
    x2 = xbuf.reshape(2 * n_ct * bsz * k * sub, lanes)
    o2 = obuf.reshape(2 * n_ct * bsz * m * sub, lanes)
    for t in range(n_ct * bsz):
        xbase = (slot * n_ct * bsz + t) * k * sub
        obase = (slot * n_ct * bsz + t) * m * sub
        for b in range(sub):
            xb = x2[pl.ds(xbase + b, k, stride=sub), :]
            if xb.dtype == U32:
                xb = jnp.concatenate(_unpack_pair(xb), axis=0)
            res = _dot(g, xb.astype(BF16))
            if obuf.dtype == U32:
                res = _pack_pair(res[:m], res[m:])
            o2[pl.ds(obase + b, m, stride=sub), :] = res


def _major_kernel(g_ref, x_hbm, o_hbm, xbuf, obuf, isem, osem, *, nj):
    s = pl.program_id(0)
    ns = pl.num_programs(0)
    slot = s % 2
    _, n_ct, bsz, k, sub, lanes = xbuf.shape
    m = obuf.shape[3]

    def copies(step, sl, hbm, buf, sem, inbound):
        i = step // nj
        j = step % nj
        out = []
        for ct in range(n_ct):
            tile = hbm.at[pl.ds(i * bsz, bsz), :, pl.ds(pl.multiple_of(j * sub, sub), sub),
                          pl.ds(ct * lanes, lanes)]
            out.append(pltpu.make_async_copy(tile, buf.at[sl, ct], sem.at[sl]) if inbound
                       else pltpu.make_async_copy(buf.at[sl, ct], tile, sem.at[sl]))
        return out

    def start(cps):
        for cp in cps:
            cp.start()

    def wait(cps):
        for cp in cps:
            cp.wait()

    @pl.when(s == 0)
    def _():
        start(copies(0, 0, x_hbm, xbuf, isem, True))

    @pl.when(s + 1 < ns)
    def _():
        start(copies(s + 1, 1 - slot, x_hbm, xbuf, isem, True))

    wait(copies(s, slot, x_hbm, xbuf, isem, True))

    @pl.when(s >= 2)
    def _():
        wait(copies(s - 2, slot, o_hbm, obuf, osem, False))

    _major_compute(g_ref[...], xbuf, obuf, slot)
    start(copies(s, slot, o_hbm, obuf, osem, False))

    @pl.when(s == ns - 1)
    def _():
        wait(copies(s, slot, o_hbm, obuf, osem, False))

        @pl.when(s >= 1)
        def _():
            wait(copies(s - 1, 1 - slot, o_hbm, obuf, osem, False))


def _dft_major(g_bf, x4, pack_out):
    b, k, n2, c = x4.shape
    m = g_bf.shape[0] // 2 if pack_out else g_bf.shape[0]
    out_dtype = U32 if pack_out else F32
    sub = 8
    n_ct = c // LANES
    bsz = max(1, min(b, MAJOR_BLOCK_BYTES // ((k + m) * sub * c * 4)))
    while b % bsz:
        bsz -= 1
    nj = n2 // sub
    return pl.pallas_call(
        functools.partial(_major_kernel, nj=nj),
        grid=((b // bsz) * nj,),
        in_specs=[pl.BlockSpec(g_bf.shape, lambda s: (0, 0)), pl.BlockSpec(memory_space=pl.ANY)],
        out_specs=pl.BlockSpec(memory_space=pl.ANY),
        out_shape=jax.ShapeDtypeStruct((b, m, n2, c), out_dtype),
        scratch_shapes=[pltpu.VMEM((2, n_ct, bsz, k, sub, LANES), x4.dtype),
                        pltpu.VMEM((2, n_ct, bsz, m, sub, LANES), out_dtype),
                        pltpu.SemaphoreType.DMA((2,)), pltpu.SemaphoreType.DMA((2,))],
        compiler_params=_params(1, "arbitrary"),
        name="dft_major",
    )(g_bf, x4)


def _minor_kernel(*refs, mode, k1b):
    if mode == "conv":
        y_ref, twb_ref, twr_ref, wf_ref, wi_ref, kf_ref, o_ref, ks_ref = refs
    else:
        y_ref, twb_ref, twr_ref, wf_ref, o_ref = refs
    n2 = y_ref.shape[2]
    ct = y_ref.shape[3]
    rep = ct // LANES

    def forward(packed, cs, sn, w):
        re, im = _unpack_pair(packed)
        return _dot(w, jnp.concatenate([re * cs + im * sn, im * cs - re * sn], axis=0).astype(BF16))

    cb, sb = twb_ref[0, 0], twb_ref[1, 0]
    for j in range(k1b):
        cr, sr = twr_ref[0, j], twr_ref[1, j]
        cs = jnp.concatenate([cb * cr - sb * sr] * rep, axis=1)
        sn = jnp.concatenate([sb * cr + cb * sr] * rep, axis=1)
        if mode == "real":
            o_ref[0, :, j, :] = forward(y_ref[0, j], cs, sn, wf_ref[...])
            continue

        @pl.when(pl.program_id(1) == 0)
        def _():
            ks_ref[j] = forward(kf_ref[0, j], cs, sn, wf_ref[...])

        z = forward(y_ref[0, j], cs, sn, wf_ref[...])
        zr, zi = z[:n2], z[n2:]
        kr = ks_ref[j, :n2]
        ki = ks_ref[j, n2:]
        p = jnp.concatenate([zr * kr - zi * ki, zr * ki + zi * kr], axis=0).astype(BF16)
        q = _dot(wi_ref[...], p)
        qr, qi = q[:n2], q[n2:]
        o_ref[0, j] = _pack_pair(qr * cs - qi * sn, qi * cs + qr * sn)


def _dft_minor(y4, wf_bf, mode, wi_bf=None, kf4=None):
    b, n1, n2, c = y4.shape
    k1b = min(MINOR_K1_BLOCK, n1)
    ct = c
    tw_block, tw_rem = _twiddles(n1, n2, k1b)
    grid = (n1 // k1b, b)
    dspec = pl.BlockSpec((1, k1b, n2, ct), lambda j, i: (i, j, 0, 0))
    in_specs = [dspec,
                pl.BlockSpec((2, 1, n2, LANES), lambda j, i: (0, j, 0, 0)),
                pl.BlockSpec((2, k1b, n2, LANES), lambda j, i: (0, 0, 0, 0)),
                pl.BlockSpec(wf_bf.shape, lambda j, i: (0, 0))]
    args = [y4, tw_block, tw_rem, wf_bf]
    scratch = []
    if mode == "conv":
        in_specs += [pl.BlockSpec(wi_bf.shape, lambda j, i: (0, 0)),
                     pl.BlockSpec((1, k1b, n2, ct), lambda j, i: (0, j, 0, 0))]
        args += [wi_bf, kf4]
        scratch = [pltpu.VMEM((k1b, 2 * n2, ct), F32)]
    if mode == "real":
        out_spec = pl.BlockSpec((1, n2, k1b, ct), lambda j, i: (i, 0, j, 0))
        out_shape = jax.ShapeDtypeStruct((b, n2, n1, c), F32)
    else:
        out_spec = dspec
        out_shape = jax.ShapeDtypeStruct((b, n1, n2, c), U32)
    return pl.pallas_call(
        functools.partial(_minor_kernel, mode=mode, k1b=k1b),
        grid=grid, in_specs=in_specs, out_specs=out_spec, out_shape=out_shape,
        scratch_shapes=scratch,
        compiler_params=pltpu.CompilerParams(dimension_semantics=("parallel", "arbitrary"),
                                             vmem_limit_bytes=VMEM_LIMIT),
        name="dft_minor_" + mode,
    )(*args)


def _cs(num, den):
    ang = 2.0 * np.pi * (num % den) / den
    return np.cos(ang), np.sin(ang)


def _complex_block(c, s, sign):
    return np.block([[c, -sign * s], [sign * s, c]]).astype(np.float32)


def _major_matrix(n1, a_in, a_out, sign, real_in=False):
    c, s = _cs(np.outer(np.arange(a_out), np.arange(a_in)), n1)
    g = _complex_block(c, s, sign)
    return g[:, :a_in] if real_in else g


def _minor_matrix(n2, sign):
    c, s = _cs(np.outer(np.arange(n2), np.arange(n2)), n2)
    return _complex_block(c, s, sign)


def _twiddles(n1, n2, k1b):
    def table(k1):
        c, s = _cs(np.outer(k1, np.arange(n2)), n1 * n2)
        tw = jnp.asarray(np.stack([c, s]).astype(np.float32))
        return jnp.broadcast_to(tw[..., None], (2, len(k1), n2, LANES))
    return table(np.arange(0, n1, k1b)), table(np.arange(k1b))


def _out_kernel(yf_ref, yc_ref, x0_ref, w_ref, x_ref, lng_ref, lnb_ref, inv_ref, skip_ref, gf_ref, gh_ref,
                wo_ref, bo_ref, g1_ref, b1_ref, wr_ref, br_ref, x1_ref, idx_ref, gate_ref, cnt_ref,
                *, alpha, apply_ln, top_k):
    c = yc_ref.shape[1]
    w = w_ref[...]
    yh = x0_ref[...] * (yc_ref[...] * inv_ref[...] + skip_ref[...] * w)
    mf = _rms_norm(yf_ref[...], gf_ref[...]).astype(BF16)
    mh = _rms_norm(yh, gh_ref[...]).astype(BF16)
    m = _dot(mf, wo_ref[:c]) + _dot(mh, wo_ref[c:]) + bo_ref[...]
    xn = _layer_norm(x_ref[...], lng_ref[...], lnb_ref[...]) if apply_ln else x_ref[...]
    x1 = _layer_norm(alpha * xn + m, g1_ref[...], b1_ref[...])
    x1_ref[...] = x1
    _route_tile(x1, wr_ref[...], br_ref[...], idx_ref, gate_ref, cnt_ref, top_k)


def _out_proj(yf, yc, x0, w, x, lng, lnb, inv_norm, skip, gf, gh, wo_bf, bo, g1, b1, wr_t, br_col, alpha,
              apply_ln):
    t, d = x.shape
    c = yc.shape[1]
    e = wr_t.shape[0]
    tm = TOKEN_TILE
    nt = t // tm
    row = lambda i: (i, 0)
    const = lambda i: (0, 0)
    tile3 = lambda i: (i, 0, 0)
    return pl.pallas_call(
        functools.partial(_out_kernel, alpha=alpha, apply_ln=apply_ln, top_k=TOP_K),
        grid=(nt,),
        in_specs=[
            pl.BlockSpec((tm, c), row),
            pl.BlockSpec((tm, c), row), pl.BlockSpec((tm, c), row), pl.BlockSpec((tm, c), row),
            pl.BlockSpec((tm, d), row),
            pl.BlockSpec((1, d), const), pl.BlockSpec((1, d), const),
            pl.BlockSpec((1, c), const), pl.BlockSpec((1, c), const),
            pl.BlockSpec((1, c), const), pl.BlockSpec((1, c), const),
            pl.BlockSpec(wo_bf.shape, const), pl.BlockSpec((1, d), const),
            pl.BlockSpec((1, d), const), pl.BlockSpec((1, d), const),
            pl.BlockSpec((e, d), const), pl.BlockSpec((e, 1), const),
        ],
        out_specs=[pl.BlockSpec((tm, d), row),
                   pl.BlockSpec((1, 2 * TOP_K, tm), tile3),
                   pl.BlockSpec((1, 8, tm), tile3),
                   pl.BlockSpec((1, e, 1), tile3)],
        out_shape=[jax.ShapeDtypeStruct((t, d), F32),
                   jax.ShapeDtypeStruct((nt, 2 * TOP_K, tm), I32),
                   jax.ShapeDtypeStruct((nt, 8, tm), F32),
                   jax.ShapeDtypeStruct((nt, e, 1), I32)],
        compiler_params=_params(1),
        name="out_proj",
    )(yf, yc, x0, w, x, lng, lnb, inv_norm, skip, gf, gh, wo_bf, bo, g1, b1, wr_t, br_col)


def _route_tile(x1, wr, br, idx_ref, gate_ref, cnt_ref, top_k):
    tt = x1.shape[0]
    e = wr.shape[0]
    nt = (((1,), (1,)), ((), ()))
    xh, xl = _split(x1)
    wh, wl = _split(wr)
    logits = (lax.dot_general(wh, xh, nt, preferred_element_type=F32)
              + lax.dot_general(wh, xl, nt, preferred_element_type=F32)
              + lax.dot_general(wl, xh, nt, preferred_element_type=F32)) + br
    iota_e = lax.broadcasted_iota(I32, (e, tt), 0).astype(F32)
    l = logits
    tops, idxs, hots = [], [], []
    for _ in range(top_k):
        m = jnp.max(l, axis=0, keepdims=True)
        idx = jnp.min(jnp.where(l == m, iota_e, float(e)), axis=0, keepdims=True)
        hot = iota_e == idx
        l = jnp.where(hot, -jnp.inf, l)
        tops.append(m)
        idxs.append(idx)
        hots.append(hot)
    ex = [jnp.exp(m - tops[0]) for m in tops]
    den = ex[0]
    for v in ex[1:]:
        den = den + v
    gates = [v / den for v in ex]
    hot_all = jnp.zeros((e, tt), F32)
    for hot in hots:
        hot_all = jnp.where(hot, 1.0, hot_all)
    upper = (lax.broadcasted_iota(I32, (tt, tt), 0) < lax.broadcasted_iota(I32, (tt, tt), 1))
    before = _dot(hot_all.astype(BF16), jnp.where(upper, 1.0, 0.0).astype(BF16))
    ranks = [jnp.sum(jnp.where(hot, before, 0.0), axis=0, keepdims=True).astype(I32) for hot in hots]
    pad = 8 - top_k
    idx_ref[0] = jnp.concatenate([v.astype(I32) for v in idxs] + ranks, axis=0)
    gate_ref[0] = jnp.concatenate(gates + [jnp.zeros((pad, tt), F32)], axis=0)
    cnt_ref[0] = jnp.sum(hot_all, axis=1, keepdims=True).astype(I32)


def _dual_specs(shape, nt0):
    first = pl.BlockSpec(shape, lambda i, *_: (jnp.minimum(i, nt0 - 1),) + (0,) * (len(shape) - 1))
    second = pl.BlockSpec(shape, lambda i, *_: (jnp.maximum(i - nt0, 0),) + (0,) * (len(shape) - 1))
    return first, second


SEG_PIECES = 3
SEG_CHUNK = SEG_ALIGN << SEG_PIECES
PIECE_CLASSES = SEG_PIECES + 1
PIECE_CAP = 32
ZERO_CHUNK = 256


def _segment_copies(first, count, loff_s, len_s, goff_s, local, remote, sem, to_remote, wait):
    def copy(lstart, gstart, size):
        lref = local.at[pl.ds(pl.multiple_of(lstart, SEG_ALIGN), size)]
        rref = remote.at[pl.ds(pl.multiple_of(gstart, SEG_ALIGN), size)]
        cp = (pltpu.make_async_copy(lref, rref, sem) if to_remote
              else pltpu.make_async_copy(rref, lref, sem))
        if wait:
            cp.wait()
        else:
            cp.start()

    def body(e, carry):
        s = first + e
        n = len_s[s] // SEG_ALIGN
        lo = 0 if loff_s is None else loff_s[s]
        go = goff_s[s]
        chunks = n >> SEG_PIECES

        def chunk(i, c):
            copy(lo + i * SEG_CHUNK, go + i * SEG_CHUNK, SEG_CHUNK)
            return c

        lax.fori_loop(0, chunks, chunk, 0)
        rem = n & ((1 << SEG_PIECES) - 1)
        base = chunks * SEG_CHUNK
        for p in reversed(range(SEG_PIECES)):
            start = base + ((rem >> (p + 1)) << (p + 1)) * SEG_ALIGN

            @pl.when(((rem >> p) & 1) == 1)
            def _():
                copy(lo + start, go + start, SEG_ALIGN << p)
        return carry

    lax.fori_loop(0, count, body, 0)


def _piece_copies(step, lo_t, go_t, cnt_t, local, remote, sem, to_remote):
    for c in range(PIECE_CLASSES):
        size = SEG_ALIGN << c
        base = (step * PIECE_CLASSES + c) * PIECE_CAP

        def body(i, carry, size=size, base=base):
            lref = local.at[pl.ds(pl.multiple_of(lo_t[base + i], SEG_ALIGN), size)]
            rref = remote.at[pl.ds(pl.multiple_of(go_t[base + i], SEG_ALIGN), size)]
            cp = (pltpu.make_async_copy(lref, rref, sem) if to_remote
                  else pltpu.make_async_copy(rref, lref, sem))
            cp.start()
            return carry

        lax.fori_loop(0, cnt_t[step * PIECE_CLASSES + c], body, 0)


def _piece_tables(seg, loff, goff):
    nt, e = seg.shape
    n = seg // SEG_ALIGN
    chunks = n >> SEG_PIECES
    rem = n & ((1 << SEG_PIECES) - 1)
    base = chunks * SEG_CHUNK

    def compact(valid, off, lo, go):
        slots = valid.shape[1]
        key = jnp.where(valid, jnp.arange(slots, dtype=I32)[None, :], slots)
        _, lo_sorted, go_sorted = lax.sort((key, lo + off, go + off), dimension=1, num_keys=1)
        return lo_sorted[:, :PIECE_CAP], go_sorted[:, :PIECE_CAP], jnp.sum(valid, axis=1)

    tables = []
    for c in range(SEG_PIECES):
        off = base + ((rem >> (c + 1)) << (c + 1)) * SEG_ALIGN
        tables.append(compact(((rem >> c) & 1) == 1, off, loff, goff))
    max_chunks = TOKEN_TILE // SEG_CHUNK
    i = jnp.arange(max_chunks, dtype=I32)[None, None, :]
    rep = lambda a: jnp.broadcast_to(a[:, :, None], (nt, e, max_chunks)).reshape(nt, e * max_chunks)
    tables.append(compact((i < chunks[:, :, None]).reshape(nt, e * max_chunks),
                          jnp.broadcast_to(i * SEG_CHUNK, (nt, e, max_chunks)).reshape(nt, e * max_chunks),
                          rep(loff), rep(goff)))
    flat = lambda k: jnp.stack([t[k] for t in tables], axis=1).reshape(-1).astype(I32)
    return flat(0), flat(1), flat(2)


def _wait_tile(step, n_experts, loff_s, len_s, local, remote, sem, to_remote):
    last = step * n_experts + n_experts - 1
    n = (loff_s[last] + len_s[last]) // SEG_ALIGN
    for p in range((local.shape[0] // SEG_ALIGN).bit_length()):
        size = SEG_ALIGN << p

        @pl.when(((n >> p) & 1) == 1)
        def _():
            lref = local.at[pl.ds(0, size)]
            rref = remote.at[pl.ds(0, size)]
            cp = (pltpu.make_async_copy(lref, rref, sem) if to_remote
                  else pltpu.make_async_copy(rref, lref, sem))
            cp.wait()


def _dispatch_kernel(loff_s, seg_s, lo_t, go_t, cnt_t, plen_s, poff_s, nb_s, xa_ref, xb_ref, idx_ref, xs_ref,
                     pos_ref,
                     buf, sems, *, n_experts, top_k, nt0):
    j = pl.program_id(0)
    nt = pl.num_programs(0)
    slot = j % 2
    tt = xa_ref.shape[0]
    lr = buf.shape[1]
    idx = idx_ref[0]
    tope = idx[:top_k]
    base = jnp.zeros_like(tope)
    for e in range(n_experts):
        base = jnp.where(tope == e, loff_s[j * n_experts + e], base)
    pos = base + idx[top_k:]
    pos_ref[0] = jnp.concatenate([pos, jnp.zeros((8 - top_k, tt), I32)], axis=0)
    rows = lax.broadcasted_iota(jnp.int16, (lr, tt), 0)
    pos16 = pos.astype(jnp.int16)
    perm = jnp.zeros((lr, tt), BF16)
    for k in range(top_k):
        perm = jnp.where(rows == pos16[k:k + 1], jnp.ones((lr, tt), BF16), perm)
    x = jnp.where(j < nt0, xa_ref[...], xb_ref[...])
    buf[slot] = _dot(perm, x.astype(BF16)).astype(BF16)

    def drained(step, s):
        _wait_tile(step, n_experts, loff_s, seg_s, buf.at[s], xs_ref, sems.at[s], True)

    _piece_copies(j, lo_t, go_t, cnt_t, buf.at[slot], xs_ref, sems.at[slot], True)

    @pl.when(j > 0)
    def _():
        drained(j - 1, 1 - slot)

    @pl.when(j == nt - 1)
    def _():
        drained(j, slot)
        zrows = EXPERT_BLOCK
        buf[0, :zrows] = jnp.zeros((zrows, buf.shape[2]), BF16)
        _segment_copies(0, n_experts, None, plen_s, poff_s, buf.at[0], xs_ref, sems.at[0], True, False)
        _segment_copies(0, n_experts, None, plen_s, poff_s, buf.at[0], xs_ref, sems.at[0], True, True)
        n_tail = xs_ref.shape[0] // zrows - nb_s[0]

        def tail_copy(i):
            dst = xs_ref.at[pl.ds(pl.multiple_of((nb_s[0] + i) * zrows, zrows), zrows)]
            return pltpu.make_async_copy(buf.at[0, pl.ds(0, zrows)], dst, sems.at[0])

        lax.fori_loop(0, n_tail, lambda i, c: (tail_copy(i).start(), c)[1], 0)
        lax.fori_loop(0, n_tail, lambda i, c: (tail_copy(i).wait(), c)[1], 0)


def _dispatch(x1s, idx, loff, seg, pieces, plen, poff, n_blocks, n_rows, local_rows):
    d = x1s[0].shape[1]
    tt = TOKEN_TILE
    nt0 = x1s[0].shape[0] // tt
    nt = nt0 + x1s[1].shape[0] // tt
    e = loff.shape[0] // nt
    xa, xb = _dual_specs((tt, d), nt0)
    grid_spec = pltpu.PrefetchScalarGridSpec(
        num_scalar_prefetch=8,
        grid=(nt,),
        in_specs=[xa, xb, pl.BlockSpec((1, 2 * TOP_K, tt), lambda i, *_: (i, 0, 0))],
        out_specs=[pl.BlockSpec(memory_space=pl.ANY),
                   pl.BlockSpec((1, 8, tt), lambda i, *_: (i, 0, 0))],
        scratch_shapes=[pltpu.VMEM((2, local_rows, d), BF16), pltpu.SemaphoreType.DMA((2,))],
    )
    return pl.pallas_call(
        functools.partial(_dispatch_kernel, n_experts=e, top_k=TOP_K, nt0=nt0),
        grid_spec=grid_spec,
        out_shape=[jax.ShapeDtypeStruct((n_rows, d), BF16), jax.ShapeDtypeStruct((nt, 8, tt), I32)],
        compiler_params=_params(1, "arbitrary"),
        name="moe_dispatch",
    )(loff, seg, *pieces, plen, poff, n_blocks, x1s[0], x1s[1], idx)


def _combine_kernel(loff_s, seg_s, lo_t, go_t, cnt_t, ys_ref, pos_ref, gate_ref, xa_ref, xb_ref, g2_ref, b2_ref,
                    oa_ref, ob_ref, buf, sems, *, n_experts, top_k, alpha, nt0):
    j = pl.program_id(0)
    nt = pl.num_programs(0)
    slot = j % 2
    tt = xa_ref.shape[0]
    lr = buf.shape[1]

    def prefetch(step, s):
        last = step * n_experts + n_experts - 1
        covered = loff_s[last] + seg_s[last]
        for r0 in range(0, lr, ZERO_CHUNK):
            @pl.when(r0 + ZERO_CHUNK > covered)
            def _(r0=r0):
                buf[s, r0:r0 + ZERO_CHUNK] = jnp.zeros((ZERO_CHUNK, buf.shape[2]), BF16)
        _piece_copies(step, lo_t, go_t, cnt_t, buf.at[s], ys_ref, sems.at[s], False)

    @pl.when(j == 0)
    def _():
        prefetch(0, 0)

    @pl.when(j + 1 < nt)
    def _():
        prefetch(j + 1, 1 - slot)

    _wait_tile(j, n_experts, loff_s, seg_s, buf.at[slot], ys_ref, sems.at[slot], False)
    pos = pos_ref[0].astype(jnp.int16)
    gate = gate_ref[0].astype(BF16)
    rows = lax.broadcasted_iota(jnp.int16, (lr, tt), 0)
    wsel = jnp.zeros((lr, tt), BF16)
    for k in range(top_k):
        wsel = jnp.where(rows == pos[k:k + 1], gate[k:k + 1], wsel)
    moe = lax.dot_general(wsel, buf[slot], (((0,), (0,)), ((), ())), preferred_element_type=F32)
    x1 = jnp.where(j < nt0, xa_ref[...], xb_ref[...])
    out = _layer_norm(alpha * x1 + moe, g2_ref[...], b2_ref[...])

    @pl.when(j < nt0)
    def _():
        oa_ref[...] = out

    @pl.when(j >= nt0)
    def _():
        ob_ref[...] = out


def _combine(ys, pos_tm, gate_tm, x1s, g2, b2, loff, seg, pieces, local_rows, alpha):
    d = x1s[0].shape[1]
    tt = TOKEN_TILE
    nt0 = x1s[0].shape[0] // tt
    nt = nt0 + x1s[1].shape[0] // tt
    e = loff.shape[0] // nt
    xa, xb = _dual_specs((tt, d), nt0)
    grid_spec = pltpu.PrefetchScalarGridSpec(
        num_scalar_prefetch=5,
        grid=(nt,),
        in_specs=[pl.BlockSpec(memory_space=pl.ANY),
                  pl.BlockSpec((1, 8, tt), lambda i, *_: (i, 0, 0)),
                  pl.BlockSpec((1, 8, tt), lambda i, *_: (i, 0, 0)),
                  xa, xb,
                  pl.BlockSpec((1, d), lambda i, *_: (0, 0)),
                  pl.BlockSpec((1, d), lambda i, *_: (0, 0))],
        out_specs=list(_dual_specs((tt, d), nt0)),
        scratch_shapes=[pltpu.VMEM((2, local_rows, d), BF16), pltpu.SemaphoreType.DMA((2,))],
    )
    return pl.pallas_call(
        functools.partial(_combine_kernel, n_experts=e, top_k=TOP_K, alpha=alpha, nt0=nt0),
        grid_spec=grid_spec,
        out_shape=[jax.ShapeDtypeStruct(x.shape, F32) for x in x1s],
        compiler_params=_params(1, "arbitrary"),
        name="moe_combine",
    )(loff, seg, *pieces, ys, pos_tm, gate_tm, x1s[0], x1s[1], g2, b2)


def _expert_kernel(be_s, nb_s, xs_ref, wgu_ref, bgu_ref, wd_ref, bd_ref, ys_ref, wgu_bf, wd_bf):
    b = pl.program_id(0)
    active = b < nb_s[0]
    fresh = jnp.logical_or(b == 0, be_s[b] != be_s[jnp.maximum(b - 1, 0)])

    @pl.when(jnp.logical_and(active, fresh))
    def _():
        wgu_bf[...] = wgu_ref[0].astype(BF16)
        wd_bf[...] = wd_ref[0].astype(BF16)

    @pl.when(active)
    def _():
        dff = wd_ref.shape[1]
        gu = _dot(xs_ref[...], wgu_bf[...]) + bgu_ref[0]
        gate = jnp.minimum(gu[:, :dff], SWIGLU_LIMIT)
        up = jnp.clip(gu[:, dff:], -SWIGLU_LIMIT, SWIGLU_LIMIT)
        h = (up + 1.0) * (gate * jax.nn.sigmoid(SWIGLU_ALPHA * gate))
        ys_ref[...] = (_dot(h.astype(BF16), wd_bf[...]) + bd_ref[0]).astype(BF16)

    @pl.when(jnp.logical_not(active))
    def _():
        ys_ref[...] = jnp.zeros_like(ys_ref)


def _experts(xs, block_expert, n_blocks, wgu, bgu, wd, bdn):
    n_rows, d = xs.shape
    bm = EXPERT_BLOCK
    f2 = wgu.shape[2]
    dff = wd.shape[1]
    blk = lambda b, be, nb: (jnp.minimum(b, nb[0] - 1), 0)
    wsel = lambda b, be, nb: (be[b], 0, 0)
    grid_spec = pltpu.PrefetchScalarGridSpec(
        num_scalar_prefetch=2,
        grid=(n_rows // bm,),
        in_specs=[pl.BlockSpec((bm, d), blk),
                  pl.BlockSpec((1, d, f2), wsel), pl.BlockSpec((1, 1, f2), wsel),
                  pl.BlockSpec((1, dff, d), wsel), pl.BlockSpec((1, 1, d), wsel)],
        out_specs=pl.BlockSpec((bm, d), lambda b, be, nb: (b, 0)),
        scratch_shapes=[pltpu.VMEM((d, f2), BF16), pltpu.VMEM((dff, d), BF16)],
    )
    return pl.pallas_call(
        _expert_kernel,
        grid_spec=grid_spec,
        out_shape=jax.ShapeDtypeStruct((n_rows, d), BF16),
        compiler_params=_params(1, "arbitrary"),
        name="moe_experts",
    )(block_expert, n_blocks, xs, wgu, bgu, wd, bdn)


def _routed_moe(routed, w_gate_up, b_gate_up, w_down, b_down, g2, b2, alpha):
    x1s = [r[0] for r in routed]
    idx, gate, cnt = (jnp.concatenate([r[i] for r in routed], axis=0) for i in (1, 2, 3))
    t = x1s[0].shape[0] + x1s[1].shape[0]
    e = cnt.shape[1]
    tt = TOKEN_TILE
    nt = t // tt
    bm = EXPERT_BLOCK

    cnt = cnt.reshape(nt, e)
    seg = (cnt + SEG_ALIGN - 1) // SEG_ALIGN * SEG_ALIGN
    loff = jnp.cumsum(seg, axis=1) - seg
    per_expert = jnp.sum(seg, axis=0)
    padded = (per_expert + bm - 1) // bm * bm
    blocks_end = jnp.cumsum(padded) // bm
    start = jnp.cumsum(padded) - padded
    goff = start[None, :] + jnp.cumsum(seg, axis=0) - seg
    local_rows = TOP_K * tt + e * SEG_ALIGN
    n_rows_max = -(-(TOP_K * t + (SEG_ALIGN - 1) * e * nt) // bm) * bm + e * bm
    n_blocks = blocks_end[-1:].astype(I32)
    bidx = jnp.minimum(jnp.arange(n_rows_max // bm, dtype=I32), n_blocks[0] - 1)
    block_expert = jnp.minimum(jnp.sum(bidx[:, None] >= blocks_end[None, :], axis=1), e - 1).astype(I32)
    flat = lambda a: a.reshape(-1).astype(I32)
    plen, poff = flat(padded - per_expert), flat(start + per_expert)

    assert e <= PIECE_CAP and local_rows // SEG_CHUNK <= PIECE_CAP
    pieces = _piece_tables(seg, loff, goff)
    xs, pos = _dispatch(x1s, idx, flat(loff), flat(seg), pieces, plen, poff, n_blocks, n_rows_max,
                        local_rows)
    ys = _experts(xs, block_expert, n_blocks, w_gate_up, b_gate_up.reshape(e, 1, -1),
                  w_down, b_down.reshape(e, 1, -1))
    return _combine(ys, pos, gate, x1s, g2, b2, flat(loff), flat(seg), pieces, local_rows, alpha)


def _mixer(x, apply_ln, lng, lnb, lay, consts, alpha):
    batch, seq, d = x.shape
    t = batch * seq
    fw = lay["g_fourier"].shape[1]
    n2 = MINOR
    x2d = x.reshape(t, d)
    z, x0, w = _in_proj(x2d, batch, seq, fw, lng, lnb, lay["w_in"], lay["b_in"], consts["bd"],
                        lay["w_short"], lay["b_short"], apply_ln)
    c = x0.shape[1]

    n1 = seq // n2
    g = jnp.asarray(_major_matrix(n1, n1, n1, -1)).astype(BF16)
    y = _dft_major(g, z.reshape(batch, n1, n2, fw), True)
    ortho = 1.0 / math.sqrt(seq * FOURIER_GROUP_DIM)
    wf_real = jnp.asarray(_minor_matrix(n2, -1)[:n2] * ortho).astype(BF16)
    yf = _dft_minor(y, wf_real, "real")

    n = 2 * seq
    m1 = n // n2
    kf, asum = _hyena_filter(seq, consts["bands"], lay["filt_w1"], lay["filt_b1"], lay["filt_w2"],
                             lay["filt_b2"], lay["filt_w3"], lay["filt_b3"], lay["filt_w_out"],
                             lay["filt_freq"], lay["filt_decay"])
    gk = jnp.asarray(_major_matrix(m1, m1, m1, -1, real_in=True)).astype(BF16)
    ky = _dft_major(gk, kf.reshape(1, m1, n2, c), True)
    gfw = jnp.asarray(_major_matrix(m1, m1 // 2, m1, -1)).astype(BF16)
    gin = jnp.asarray(_major_matrix(m1, m1, m1 // 2, +1)).astype(BF16)
    pairs = batch // 2
    u = _dft_major(gfw, w.reshape(pairs, m1, n2, c), True)
    wi = jnp.asarray(_minor_matrix(n2, +1) * (1.0 / n)).astype(BF16)
    v = _dft_minor(u, consts["wf"], "conv", wi_bf=wi, kf4=ky)
    yc = _dft_major(gin, v, False)

    inv_norm = 1.0 / asum
    return _out_proj(yf.reshape(t, fw), yc.reshape(t, c), x0, w, x2d, lng, lnb, inv_norm,
                     lay["hyena_skip"], lay["g_fourier"], lay["g_hyena"], lay["w_out"], lay["b_out"],
                     lay["ln1_g"], lay["ln1_b"], lay["w_router_t"], lay["b_router"], alpha, apply_ln)


def kernel(x_prompt, x_sample, ln_in_g, ln_in_b, w_in, b_in, w_short, b_short, filt_w1, filt_b1, filt_w2, filt_b2, filt_w3, filt_b3, filt_w_out, filt_freq, filt_decay, hyena_skip, g_fourier, g_hyena, w_out, b_out, ln1_g, ln1_b, w_router, b_router, w_gate_up, b_gate_up, w_down, b_down, ln2_g, ln2_b):
    depth, d, in_width = w_in.shape
    fw = g_fourier.shape[1]
    alpha = (2.0 * depth) ** 0.25
    row = lambda a: a.reshape(1, -1)

    gd = FOURIER_GROUP_DIM
    cg, sg = _cs(np.outer(np.arange(gd), np.arange(gd)), gd)
    eye = np.eye(LANES // gd)
    bd = np.concatenate([np.kron(eye, cg), -np.kron(eye, sg)], axis=1).astype(np.float32)
    bands = jnp.linspace(1e-4, FILTER_BANDS - 1, FILTER_BANDS, dtype=F32)
    bands_row = jnp.zeros((1, LANES), F32).at[0, 1:1 + FILTER_BANDS].set(bands)
    bands_row = bands_row.at[0, 1 + FILTER_BANDS:1 + 2 * FILTER_BANDS].set(bands)
    consts = dict(bd=jnp.asarray(bd).astype(BF16), bands=bands_row,
                  wf=jnp.asarray(_minor_matrix(MINOR, -1)).astype(BF16))

    xs = [x_prompt, x_sample]
    for l in range(depth):
        emb = filt_w1.shape[1]
        twice = lambda a: jnp.concatenate([row(a), row(a)], axis=1)
        blockdiag = lambda a: jnp.kron(jnp.eye(2, dtype=F32), a)
        w1_pad = jnp.zeros((LANES, filt_w1.shape[2]), F32).at[:emb].set(filt_w1[l])
        wl = filt_w_out[l]
        wl_halves = jnp.stack([jnp.concatenate([wl, jnp.zeros_like(wl)], axis=0),
                               jnp.concatenate([jnp.zeros_like(wl), wl], axis=0)])
        lay = dict(
            w_in=w_in[l].astype(BF16), b_in=row(b_in[l]), w_short=w_short[l], b_short=row(b_short[l]),
            filt_w1=blockdiag(w1_pad),
            filt_b1=twice(filt_b1[l]), filt_w2=blockdiag(filt_w2[l]), filt_b2=twice(filt_b2[l]),
            filt_w3=blockdiag(filt_w3[l]),
            filt_b3=twice(filt_b3[l]), filt_w_out=wl_halves, filt_freq=twice(filt_freq[l]),
            filt_decay=filt_decay[l], hyena_skip=row(hyena_skip[l]), g_fourier=row(g_fourier[l]),
            g_hyena=row(g_hyena[l]), w_out=w_out[l].astype(BF16), b_out=row(b_out[l]),
            ln1_g=row(ln1_g[l]), ln1_b=row(ln1_b[l]),
            w_router_t=w_router[l].T, b_router=b_router[l].reshape(-1, 1))
        routed = [_mixer(x, l == 0, row(ln_in_g), row(ln_in_b), lay, consts, alpha) for x in xs]
        outs = _routed_moe(routed, w_gate_up[l], b_gate_up[l], w_down[l], b_down[l],
                           row(ln2_g[l]), row(ln2_b[l]), alpha)
        xs = [o.reshape(x.shape) for o, x in zip(outs, xs)]
    return (xs[0], xs[1])
```

```python
import functools
import math

import numpy as np
import jax
import jax.numpy as jnp
from jax import lax
from jax.experimental import pallas as pl
from jax.experimental.pallas import tpu as pltpu

F32 = jnp.float32
BF16 = jnp.bfloat16
I32 = jnp.int32
U32 = jnp.uint32

LN_EPS = 1e-5
RMS_EPS = 1e-6
SWIGLU_LIMIT = 7.0
SWIGLU_ALPHA = 1.702
TOP_K = 4
FOURIER_GROUP_DIM = 64
FILTER_BANDS = 16

LANES = 128
MINOR = 128
MINOR_K1_BLOCK = 16
ROW_TILE = 512
IN_TILE = 1024
IN_SPLIT = 4
MAJOR_BLOCK_BYTES = 12 * 1024 * 1024
TOKEN_TILE = 512
SEG_ALIGN = 16
EXPERT_BLOCK = 512
VMEM_LIMIT = 56 * 1024 * 1024


def _params(n_grid, semantics="parallel"):
    return pltpu.CompilerParams(dimension_semantics=(semantics,) * n_grid,
                                vmem_limit_bytes=VMEM_LIMIT)


def _dot(a, b):
    return jnp.dot(a, b, preferred_element_type=F32)


def _split(a):
    hi = a.astype(BF16)
    lo = (a - hi.astype(F32)).astype(BF16)
    return hi, lo


def _dot3(a, b):
    ah, al = _split(a)
    bh, bl = _split(b)
    return _dot(ah, bh) + _dot(ah, bl) + _dot(al, bh)


def _layer_norm(x, g, b):
    mu = jnp.mean(x, axis=-1, keepdims=True)
    xc = x - mu
    var = jnp.mean(xc * xc, axis=-1, keepdims=True)
    return xc * lax.rsqrt(var + LN_EPS) * g + b


def _rms_norm(x, g):
    return x * lax.rsqrt(jnp.mean(x * x, axis=-1, keepdims=True) + RMS_EPS) * g


def _in_kernel(x_ref, xp_ref, xq_ref, lng_ref, lnb_ref, win_ref, bin_ref, bd_ref, wsh_ref, bsh_ref,
               z_ref, x0_ref, w_ref, *, tiles_per_seq, apply_ln, fw):
    it = pl.program_id(0) % tiles_per_seq
    tm = x_ref.shape[0]

    def norm(x):
        return _layer_norm(x, lng_ref[...], lnb_ref[...]) if apply_ln else x

    halo = norm(jnp.concatenate([xp_ref[...], xq_ref[...]], axis=0))
    projh = _dot(halo.astype(BF16), win_ref[:, fw:]) + bin_ref[:, fw:]
    rp = tm // IN_SPLIT
    projs = [_dot(norm(x_ref[r * rp:(r + 1) * rp, :]).astype(BF16), win_ref[...]) + bin_ref[...]
             for r in range(IN_SPLIT)]
    wsh = wsh_ref[...]
    for r, proj in enumerate(projs):
        sl = slice(r * rp, (r + 1) * rp)
        pf = proj[:, :fw].astype(BF16)
        zs = [_dot(pf[:, c0:c0 + LANES], bd_ref[...]) for c0 in range(0, fw, LANES)]
        z_ref[0, sl, :] = _pack_pair(jnp.concatenate([z[:, :LANES] for z in zs], axis=1),
                                     jnp.concatenate([z[:, LANES:] for z in zs], axis=1))

        ph = proj[:, fw:]
        prev_row = projs[r - 1][rp - 1:rp, fw:] if r > 0 else jnp.where(it > 0, projh[7:8], 0.0)
        next_row = (projs[r + 1][0:1, fw:] if r + 1 < IN_SPLIT
                    else jnp.where(it < tiles_per_seq - 1, projh[8:9], 0.0))
        rows = lax.broadcasted_iota(I32, ph.shape, 0)
        up = jnp.where(rows == 0, prev_row, pltpu.roll(ph, 1, 0))
        dn = jnp.where(rows == rp - 1, next_row, pltpu.roll(ph, rp - 1, 0))
        uc = up * wsh[0:1] + ph * wsh[1:2] + dn * wsh[2:3] + bsh_ref[...]
        hw = uc.shape[1] // 3
        x0_ref[sl, :] = uc[:, :hw]
        w_ref[sl, :] = uc[:, 2 * hw:] * uc[:, hw:2 * hw]


def _in_proj(x2d, batch, seq, fw, lng, lnb, win_bf, b_in, bd_bf, w_short, b_short, apply_ln):
    t, d = x2d.shape
    inw = win_bf.shape[1]
    hw = (inw - fw) // 3
    tm = IN_TILE
    tps = seq // tm
    r8 = tm // 8
    nblk8 = t // 8
    const = lambda i: (0, 0)
    kern = functools.partial(_in_kernel, tiles_per_seq=tps, apply_ln=apply_ln, fw=fw)
    return pl.pallas_call(
        kern,
        grid=(t // tm,),
        in_specs=[
            pl.BlockSpec((tm, d), lambda i: (i, 0)),
            pl.BlockSpec((8, d), lambda i: (jnp.maximum(i * r8 - 1, 0), 0)),
            pl.BlockSpec((8, d), lambda i: (jnp.minimum((i + 1) * r8, nblk8 - 1), 0)),
            pl.BlockSpec((1, d), const), pl.BlockSpec((1, d), const),
            pl.BlockSpec((d, inw), const), pl.BlockSpec((1, inw), const),
            pl.BlockSpec(bd_bf.shape, const),
            pl.BlockSpec((3, 3 * hw), const), pl.BlockSpec((1, 3 * hw), const),
        ],
        out_specs=[
            pl.BlockSpec((1, tm, fw), lambda i: (i // tps, i % tps, 0)),
            pl.BlockSpec((tm, hw), lambda i: (i, 0)),
            pl.BlockSpec((tm, hw), lambda i: (i, 0)),
        ],
        out_shape=[
            jax.ShapeDtypeStruct((batch, seq, fw), U32),
            jax.ShapeDtypeStruct((t, hw), F32),
            jax.ShapeDtypeStruct((t, hw), F32),
        ],
        compiler_params=_params(1),
        name="in_proj",
    )(x2d, x2d, x2d, lng, lnb, win_bf, b_in, bd_bf, w_short, b_short)


FILTER_HALO = 16


def _filter_kernel(bands_ref, w1_ref, b1_ref, w2_ref, b2_ref, w3_ref, b3_ref, wl_ref, fr_ref, dec_ref,
                   k_ref, asum_ref, tc_ref, ts_ref, kf_buf, kb_buf, sems, *, seq):
    i = pl.program_id(0)
    nsteps = pl.num_programs(0)
    tile, c = kf_buf.shape
    ext = tc_ref.shape[0]
    rep = c // LANES
    w_unit = 2.0 * math.pi / seq
    bands = bands_ref[...]

    @pl.when(i == 0)
    def _():
        arg = bands * (w_unit * lax.broadcasted_iota(I32, (ext, LANES), 0).astype(F32))
        tc_ref[...] = jnp.cos(arg)
        ts_ref[...] = jnp.sin(arg)
        asum_ref[...] = jnp.zeros_like(asum_ref)

    j0 = i * tile
    base = bands * (w_unit * j0.astype(F32))
    c0, s0 = jnp.cos(base), jnp.sin(base)
    tc, ts = tc_ref[...], ts_ref[...]
    cosf = c0 * tc - s0 * ts
    sinf = s0 * tc + c0 * ts
    j = j0 + lax.broadcasted_iota(I32, (ext, LANES), 0)
    t = j.astype(F32) * (1.0 / (seq - 1))
    lane = lax.broadcasted_iota(I32, (ext, LANES), 1)
    feat = jnp.where(lane == 0, t,
                     jnp.where(lane <= FILTER_BANDS, cosf,
                               jnp.where(lane <= 2 * FILTER_BANDS, -sinf, 0.0)))
    half = ext // 2
    fr = fr_ref[...]
    h = jnp.concatenate([feat[:half], feat[half:]], axis=1)
    h = jnp.sin(fr * (_dot3(h, w1_ref[...]) + b1_ref[...]))
    h = jnp.sin(fr * (_dot3(h, w2_ref[...]) + b2_ref[...]))
    h = jnp.sin(fr * (_dot3(h, w3_ref[...]) + b3_ref[...]))
    hl = jnp.concatenate([_dot3(h, wl_ref[0]), _dot3(h, wl_ref[1])], axis=0)
    t4 = jnp.concatenate([t] * rep, axis=1)
    j4 = jnp.concatenate([j] * rep, axis=1)
    dec = jnp.abs(dec_ref[...])
    hf = hl[:, :c] * jnp.exp(-t4 * dec[0:1])
    hb = hl[:, c:] * jnp.exp(-t4 * dec[1:2])
    hb = jnp.where(j4 < seq, hb, 0.0)
    asum_ref[...] += jnp.sum(jnp.abs(hf[:tile]) + jnp.abs(hb[:tile]), axis=0, keepdims=True)

    rev = (lax.broadcasted_iota(I32, (tile, ext), 1) == tile - lax.broadcasted_iota(I32, (tile, ext), 0))
    rev = jnp.where(rev, 1.0, 0.0).astype(BF16)
    b1 = hb.astype(BF16)
    r1 = hb - b1.astype(F32)
    b2 = r1.astype(BF16)
    b3 = (r1 - b2.astype(F32)).astype(BF16)
    kb = _dot(rev, b1) + _dot(rev, b2) + _dot(rev, b3)

    def copies():
        return (pltpu.make_async_copy(kf_buf, k_ref.at[pl.ds(pl.multiple_of(j0, tile), tile)], sems.at[0]),
                pltpu.make_async_copy(kb_buf, k_ref.at[pl.ds(pl.multiple_of(2 * seq - j0 - tile, tile), tile)],
                                      sems.at[1]))

    @pl.when(i > 0)
    def _():
        for cp in copies():
            cp.wait()

    kf_buf[...] = jnp.where(j4[:tile] == 0, hf[:tile] + hb[:tile], hf[:tile])
    kb_buf[...] = kb
    for cp in copies():
        cp.start()

    @pl.when(i == nsteps - 1)
    def _():
        for cp in copies():
            cp.wait()


def _hyena_filter(seq, bands_row, w1p, b1, w2, b2, w3, b3, wl, freq, decay):
    c = decay.shape[1]
    tile = ROW_TILE
    ext = tile + FILTER_HALO
    full = lambda a: pl.BlockSpec(a.shape, lambda i: (0,) * a.ndim)
    args = (bands_row, w1p, b1, w2, b2, w3, b3, wl, freq, decay)
    return pl.pallas_call(
        functools.partial(_filter_kernel, seq=seq),
        grid=(seq // tile,),
        in_specs=[full(a) for a in args],
        out_specs=[pl.BlockSpec(memory_space=pl.ANY), pl.BlockSpec((1, c), lambda i: (0, 0))],
        out_shape=[jax.ShapeDtypeStruct((2 * seq, c), F32), jax.ShapeDtypeStruct((1, c), F32)],
        scratch_shapes=[pltpu.VMEM((ext, LANES), F32), pltpu.VMEM((ext, LANES), F32),
                        pltpu.VMEM((tile, c), F32), pltpu.VMEM((tile, c), F32),
                        pltpu.SemaphoreType.DMA((2,))],
        compiler_params=_params(1, "arbitrary"),
        name="hyena_filter",
    )(*args)


def _pack_pair(re, im):
    rb = lax.bitcast_convert_type(re.astype(BF16).astype(F32), U32)
    ib = lax.bitcast_convert_type(im.astype(BF16).astype(F32), U32)
    return (rb >> 16) | ib


def _unpack_pair(word):
    re = lax.bitcast_convert_type(word << 16, F32)
    im = lax.bitcast_convert_type(word & jnp.uint32(0xFFFF0000), F32)
    return re, im


def _major_compute(g, xbuf, obuf, slot):
    _, n_ct, bsz, k, sub, lanes = xbuf.shape
    m = obuf.shape[3]
    x2 = xbuf.reshape(2 * n_ct * bsz * k * sub, lanes)
    o2 = obuf.reshape(2 * n_ct * bsz * m * sub, lanes)
    for t in range(n_ct * bsz):
        xbase = (slot * n_ct * bsz + t) * k * sub
        obase = (slot * n_ct * bsz + t) * m * sub
        for b in range(sub):
            xb = x2[pl.ds(xbase + b, k, stride=sub), :]
            if xb.dtype == U32:
                xb = jnp.concatenate(_unpack_pair(xb), axis=0)
            res = _dot(g, xb.astype(BF16))
            if obuf.dtype == U32:
                res = _pack_pair(res[:m], res[m:])
            o2[pl.ds(obase + b, m, stride=sub), :] = res


def _major_kernel(g_ref, x_hbm, o_hbm, xbuf, obuf, isem, osem, *, nj):
    s = pl.program_id(0)
    ns = pl.num_programs(0)
    slot = s % 2
    _, n_ct, bsz, k, sub, lanes = xbuf.shape
    m = obuf.shape[3]

    def copies(step, sl, hbm, buf, sem, inbound):
        i = step // nj
        j = step % nj
        out = []
        for ct in range(n_ct):
            tile = hbm.at[pl.ds(i * bsz, bsz), :, pl.ds(pl.multiple_of(j * sub, sub), sub),
                          pl.ds(ct * lanes, lanes)]
            out.append(pltpu.make_async_copy(tile, buf.at[sl, ct], sem.at[sl]) if inbound
                       else pltpu.make_async_copy(buf.at[sl, ct], tile, sem.at[sl]))
        return out

    def start(cps):
        for cp in cps:
            cp.start()

    def wait(cps):
        for cp in cps:
            cp.wait()

    @pl.when(s == 0)
    def _():
        start(copies(0, 0, x_hbm, xbuf, isem, True))

    @pl.when(s + 1 < ns)
    def _():
        start(copies(s + 1, 1 - slot, x_hbm, xbuf, isem, True))

    wait(copies(s, slot, x_hbm, xbuf, isem, True))

    @pl.when(s >= 2)
    def _():
        wait(copies(s - 2, slot, o_hbm, obuf, osem, False))

    _major_compute(g_ref[...], xbuf, obuf, slot)
    start(copies(s, slot, o_hbm, obuf, osem, False))

    @pl.when(s == ns - 1)
    def _():
        wait(copies(s, slot, o_hbm, obuf, osem, False))

        @pl.when(s >= 1)
        def _():
            wait(copies(s - 1, 1 - slot, o_hbm, obuf, osem, False))


def _dft_major(g_bf, x4, pack_out):
    b, k, n2, c = x4.shape
    m = g_bf.shape[0] // 2 if pack_out else g_bf.shape[0]
    out_dtype = U32 if pack_out else F32
    sub = 8
    n_ct = c // LANES
    bsz = max(1, min(b, MAJOR_BLOCK_BYTES // ((k + m) * sub * c * 4)))
    while b % bsz:
        bsz -= 1
    nj = n2 // sub
    return pl.pallas_call(
        functools.partial(_major_kernel, nj=nj),
        grid=((b // bsz) * nj,),
        in_specs=[pl.BlockSpec(g_bf.shape, lambda s: (0, 0)), pl.BlockSpec(memory_space=pl.ANY)],
        out_specs=pl.BlockSpec(memory_space=pl.ANY),
        out_shape=jax.ShapeDtypeStruct((b, m, n2, c), out_dtype),
        scratch_shapes=[pltpu.VMEM((2, n_ct, bsz, k, sub, LANES), x4.dtype),
                        pltpu.VMEM((2, n_ct, bsz, m, sub, LANES), out_dtype),
                        pltpu.SemaphoreType.DMA((2,)), pltpu.SemaphoreType.DMA((2,))],
        compiler_params=_params(1, "arbitrary"),
        name="dft_major",
    )(g_bf, x4)


def _minor_kernel(*refs, mode, k1b):
    if mode == "conv":
        y_ref, twb_ref, twr_ref, wf_ref, wi_ref, kf_ref, o_ref, ks_ref = refs
    else:
        y_ref, twb_ref, twr_ref, wf_ref, o_ref = refs
    n2 = y_ref.shape[2]
    ct = y_ref.shape[3]
    rep = ct // LANES

    def forward(packed, cs, sn, w):
        re, im = _unpack_pair(packed)
        return _dot(w, jnp.concatenate([re * cs + im * sn, im * cs - re * sn], axis=0).astype(BF16))

    cb, sb = twb_ref[0, 0], twb_ref[1, 0]
    for j in range(k1b):
        cr, sr = twr_ref[0, j], twr_ref[1, j]
        cs = jnp.concatenate([cb * cr - sb * sr] * rep, axis=1)
        sn = jnp.concatenate([sb * cr + cb * sr] * rep, axis=1)
        if mode == "real":
            o_ref[0, :, j, :] = forward(y_ref[0, j], cs, sn, wf_ref[...])
            continue

        @pl.when(pl.program_id(1) == 0)
        def _():
            ks_ref[j] = forward(kf_ref[0, j], cs, sn, wf_ref[...])

        z = forward(y_ref[0, j], cs, sn, wf_ref[...])
        zr, zi = z[:n2], z[n2:]
        kr = ks_ref[j, :n2]
        ki = ks_ref[j, n2:]
        p = jnp.concatenate([zr * kr - zi * ki, zr * ki + zi * kr], axis=0).astype(BF16)
        q = _dot(wi_ref[...], p)
        qr, qi = q[:n2], q[n2:]
        o_ref[0, j] = _pack_pair(qr * cs - qi * sn, qi * cs + qr * sn)


def _dft_minor(y4, wf_bf, mode, wi_bf=None, kf4=None):
    b, n1, n2, c = y4.shape
    k1b = min(MINOR_K1_BLOCK, n1)
    ct = c
    tw_block, tw_rem = _twiddles(n1, n2, k1b)
    grid = (n1 // k1b, b)
    dspec = pl.BlockSpec((1, k1b, n2, ct), lambda j, i: (i, j, 0, 0))
    in_specs = [dspec,
                pl.BlockSpec((2, 1, n2, LANES), lambda j, i: (0, j, 0, 0)),
                pl.BlockSpec((2, k1b, n2, LANES), lambda j, i: (0, 0, 0, 0)),
                pl.BlockSpec(wf_bf.shape, lambda j, i: (0, 0))]
    args = [y4, tw_block, tw_rem, wf_bf]
    scratch = []
    if mode == "conv":
        in_specs += [pl.BlockSpec(wi_bf.shape, lambda j, i: (0, 0)),
                     pl.BlockSpec((1, k1b, n2, ct), lambda j, i: (0, j, 0, 0))]
        args += [wi_bf, kf4]
        scratch = [pltpu.VMEM((k1b, 2 * n2, ct), F32)]
    if mode == "real":
        out_spec = pl.BlockSpec((1, n2, k1b, ct), lambda j, i: (i, 0, j, 0))
        out_shape = jax.ShapeDtypeStruct((b, n2, n1, c), F32)
    else:
        out_spec = dspec
        out_shape = jax.ShapeDtypeStruct((b, n1, n2, c), U32)
    return pl.pallas_call(
        functools.partial(_minor_kernel, mode=mode, k1b=k1b),
        grid=grid, in_specs=in_specs, out_specs=out_spec, out_shape=out_shape,
        scratch_shapes=scratch,
        compiler_params=pltpu.CompilerParams(dimension_semantics=("parallel", "arbitrary"),
                                             vmem_limit_bytes=VMEM_LIMIT),
        name="dft_minor_" + mode,
    )(*args)


def _cs(num, den):
    ang = 2.0 * np.pi * (num % den) / den
    return np.cos(ang), np.sin(ang)


def _complex_block(c, s, sign):
    return np.block([[c, -sign * s], [sign * s, c]]).astype(np.float32)


def _major_matrix(n1, a_in, a_out, sign, real_in=False):
    c, s = _cs(np.outer(np.arange(a_out), np.arange(a_in)), n1)
    g = _complex_block(c, s, sign)
    return g[:, :a_in] if real_in else g


def _minor_matrix(n2, sign):
    c, s = _cs(np.outer(np.arange(n2), np.arange(n2)), n2)
    return _complex_block(c, s, sign)


def _twiddles(n1, n2, k1b):
    def table(k1):
        c, s = _cs(np.outer(k1, np.arange(n2)), n1 * n2)
        tw = jnp.asarray(np.stack([c, s]).astype(np.float32))
        return jnp.broadcast_to(tw[..., None], (2, len(k1), n2, LANES))
    return table(np.arange(0, n1, k1b)), table(np.arange(k1b))


def _out_kernel(yf_ref, yc_ref, x0_ref, w_ref, x_ref, lng_ref, lnb_ref, inv_ref, skip_ref, gf_ref, gh_ref,
                wo_ref, bo_ref, g1_ref, b1_ref, wr_ref, br_ref, x1_ref, idx_ref, gate_ref, cnt_ref,
                *, alpha, apply_ln, top_k):
    c = yc_ref.shape[1]
    w = w_ref[...]
    yh = x0_ref[...] * (yc_ref[...] * inv_ref[...] + skip_ref[...] * w)
    mf = _rms_norm(yf_ref[...], gf_ref[...]).astype(BF16)
    mh = _rms_norm(yh, gh_ref[...]).astype(BF16)
    m = _dot(mf, wo_ref[:c]) + _dot(mh, wo_ref[c:]) + bo_ref[...]
    xn = _layer_norm(x_ref[...], lng_ref[...], lnb_ref[...]) if apply_ln else x_ref[...]
    x1 = _layer_norm(alpha * xn + m, g1_ref[...], b1_ref[...])
    x1_ref[...] = x1
    _route_tile(x1, wr_ref[...], br_ref[...], idx_ref, gate_ref, cnt_ref, top_k)


def _out_proj(yf, yc, x0, w, x, lng, lnb, inv_norm, skip, gf, gh, wo_bf, bo, g1, b1, wr_t, br_col, alpha,
              apply_ln):
    t, d = x.shape
    c = yc.shape[1]
    e = wr_t.shape[0]
    tm = TOKEN_TILE
    nt = t // tm
    row = lambda i: (i, 0)
    const = lambda i: (0, 0)
    tile3 = lambda i: (i, 0, 0)
    return pl.pallas_call(
        functools.partial(_out_kernel, alpha=alpha, apply_ln=apply_ln, top_k=TOP_K),
        grid=(nt,),
        in_specs=[
            pl.BlockSpec((tm, c), row),
            pl.BlockSpec((tm, c), row), pl.BlockSpec((tm, c), row), pl.BlockSpec((tm, c), row),
            pl.BlockSpec((tm, d), row),
            pl.BlockSpec((1, d), const), pl.BlockSpec((1, d), const),
            pl.BlockSpec((1, c), const), pl.BlockSpec((1, c), const),
            pl.BlockSpec((1, c), const), pl.BlockSpec((1, c), const),
            pl.BlockSpec(wo_bf.shape, const), pl.BlockSpec((1, d), const),
            pl.BlockSpec((1, d), const), pl.BlockSpec((1, d), const),
            pl.BlockSpec((e, d), const), pl.BlockSpec((e, 1), const),
        ],
        out_specs=[pl.BlockSpec((tm, d), row),
                   pl.BlockSpec((1, 2 * TOP_K, tm), tile3),
                   pl.BlockSpec((1, 8, tm), tile3),
                   pl.BlockSpec((1, e, 1), tile3)],
        out_shape=[jax.ShapeDtypeStruct((t, d), F32),
                   jax.ShapeDtypeStruct((nt, 2 * TOP_K, tm), I32),
                   jax.ShapeDtypeStruct((nt, 8, tm), F32),
                   jax.ShapeDtypeStruct((nt, e, 1), I32)],
        compiler_params=_params(1),
        name="out_proj",
    )(yf, yc, x0, w, x, lng, lnb, inv_norm, skip, gf, gh, wo_bf, bo, g1, b1, wr_t, br_col)


def _route_tile(x1, wr, br, idx_ref, gate_ref, cnt_ref, top_k):
    tt = x1.shape[0]
    e = wr.shape[0]
    nt = (((1,), (1,)), ((), ()))
    xh, xl = _split(x1)
    wh, wl = _split(wr)
    logits = (lax.dot_general(wh, xh, nt, preferred_element_type=F32)
              + lax.dot_general(wh, xl, nt, preferred_element_type=F32)
              + lax.dot_general(wl, xh, nt, preferred_element_type=F32)) + br
    iota_e = lax.broadcasted_iota(I32, (e, tt), 0).astype(F32)
    l = logits
    tops, idxs, hots = [], [], []
    for _ in range(top_k):
        m = jnp.max(l, axis=0, keepdims=True)
        idx = jnp.min(jnp.where(l == m, iota_e, float(e)), axis=0, keepdims=True)
        hot = iota_e == idx
        l = jnp.where(hot, -jnp.inf, l)
        tops.append(m)
        idxs.append(idx)
        hots.append(hot)
    ex = [jnp.exp(m - tops[0]) for m in tops]
    den = ex[0]
    for v in ex[1:]:
        den = den + v
    gates = [v / den for v in ex]
    hot_all = jnp.zeros((e, tt), F32)
    for hot in hots:
        hot_all = jnp.where(hot, 1.0, hot_all)
    upper = (lax.broadcasted_iota(I32, (tt, tt), 0) < lax.broadcasted_iota(I32, (tt, tt), 1))
    before = _dot(hot_all.astype(BF16), jnp.where(upper, 1.0, 0.0).astype(BF16))
    ranks = [jnp.sum(jnp.where(hot, before, 0.0), axis=0, keepdims=True).astype(I32) for hot in hots]
    pad = 8 - top_k
    idx_ref[0] = jnp.concatenate([v.astype(I32) for v in idxs] + ranks, axis=0)
    gate_ref[0] = jnp.concatenate(gates + [jnp.zeros((pad, tt), F32)], axis=0)
    cnt_ref[0] = jnp.sum(hot_all, axis=1, keepdims=True).astype(I32)


def _dual_specs(shape, nt0):
    first = pl.BlockSpec(shape, lambda i, *_: (jnp.minimum(i, nt0 - 1),) + (0,) * (len(shape) - 1))
    second = pl.BlockSpec(shape, lambda i, *_: (jnp.maximum(i - nt0, 0),) + (0,) * (len(shape) - 1))
    return first, second


SEG_PIECES = 3
SEG_CHUNK = SEG_ALIGN << SEG_PIECES
PIECE_CLASSES = SEG_PIECES + 1
PIECE_CAP = 32
ZERO_CHUNK = 256


def _segment_copies(first, count, loff_s, len_s, goff_s, local, remote, sem, to_remote, wait):
    def copy(lstart, gstart, size):
        lref = local.at[pl.ds(pl.multiple_of(lstart, SEG_ALIGN), size)]
        rref = remote.at[pl.ds(pl.multiple_of(gstart, SEG_ALIGN), size)]
        cp = (pltpu.make_async_copy(lref, rref, sem) if to_remote
              else pltpu.make_async_copy(rref, lref, sem))
        if wait:
            cp.wait()
        else:
            cp.start()

    def body(e, carry):
        s = first + e
        n = len_s[s] // SEG_ALIGN
        lo = 0 if loff_s is None else loff_s[s]
        go = goff_s[s]
        chunks = n >> SEG_PIECES

        def chunk(i, c):
            copy(lo + i * SEG_CHUNK, go + i * SEG_CHUNK, SEG_CHUNK)
            return c

        lax.fori_loop(0, chunks, chunk, 0)
        rem = n & ((1 << SEG_PIECES) - 1)
        base = chunks * SEG_CHUNK
        for p in reversed(range(SEG_PIECES)):
            start = base + ((rem >> (p + 1)) << (p + 1)) * SEG_ALIGN

            @pl.when(((rem >> p) & 1) == 1)
            def _():
                copy(lo + start, go + start, SEG_ALIGN << p)
        return carry

    lax.fori_loop(0, count, body, 0)


def _piece_copies(step, lo_t, go_t, cnt_t, local, remote, sem, to_remote):
    for c in range(PIECE_CLASSES):
        size = SEG_ALIGN << c
        base = (step * PIECE_CLASSES + c) * PIECE_CAP

        def body(i, carry, size=size, base=base):
            lref = local.at[pl.ds(pl.multiple_of(lo_t[base + i], SEG_ALIGN), size)]
            rref = remote.at[pl.ds(pl.multiple_of(go_t[base + i], SEG_ALIGN), size)]
            cp = (pltpu.make_async_copy(lref, rref, sem) if to_remote
                  else pltpu.make_async_copy(rref, lref, sem))
            cp.start()
            return carry

        lax.fori_loop(0, cnt_t[step * PIECE_CLASSES + c], body, 0)


def _piece_tables(seg, loff, goff):
    nt, e = seg.shape
    n = seg // SEG_ALIGN
    chunks = n >> SEG_PIECES
    rem = n & ((1 << SEG_PIECES) - 1)
    base = chunks * SEG_CHUNK

    def compact(valid, off, lo, go):
        slots = valid.shape[1]
        key = jnp.where(valid, jnp.arange(slots, dtype=I32)[None, :], slots)
        _, lo_sorted, go_sorted = lax.sort((key, lo + off, go + off), dimension=1, num_keys=1)
        return lo_sorted[:, :PIECE_CAP], go_sorted[:, :PIECE_CAP], jnp.sum(valid, axis=1)

    tables = []
    for c in range(SEG_PIECES):
        off = base + ((rem >> (c + 1)) << (c + 1)) * SEG_ALIGN
        tables.append(compact(((rem >> c) & 1) == 1, off, loff, goff))
    max_chunks = TOKEN_TILE // SEG_CHUNK
    i = jnp.arange(max_chunks, dtype=I32)[None, None, :]
    rep = lambda a: jnp.broadcast_to(a[:, :, None], (nt, e, max_chunks)).reshape(nt, e * max_chunks)
    tables.append(compact((i < chunks[:, :, None]).reshape(nt, e * max_chunks),
                          jnp.broadcast_to(i * SEG_CHUNK, (nt, e, max_chunks)).reshape(nt, e * max_chunks),
                          rep(loff), rep(goff)))
    flat = lambda k: jnp.stack([t[k] for t in tables], axis=1).reshape(-1).astype(I32)
    return flat(0), flat(1), flat(2)


def _wait_tile(step, n_experts, loff_s, len_s, local, remote, sem, to_remote):
    last = step * n_experts + n_experts - 1
    n = (loff_s[last] + len_s[last]) // SEG_ALIGN
    for p in range((local.shape[0] // SEG_ALIGN).bit_length()):
        size = SEG_ALIGN << p

        @pl.when(((n >> p) & 1) == 1)
        def _():
            lref = local.at[pl.ds(0, size)]
            rref = remote.at[pl.ds(0, size)]
            cp = (pltpu.make_async_copy(lref, rref, sem) if to_remote
                  else pltpu.make_async_copy(rref, lref, sem))
            cp.wait()


def _dispatch_kernel(loff_s, seg_s, lo_t, go_t, cnt_t, plen_s, poff_s, nb_s, xa_ref, xb_ref, idx_ref, xs_ref,
                     pos_ref,
                     buf, sems, *, n_experts, top_k, nt0):
    j = pl.program_id(0)
    nt = pl.num_programs(0)
    slot = j % 2
    tt = xa_ref.shape[0]
    lr = buf.shape[1]
    idx = idx_ref[0]
    tope = idx[:top_k]
    base = jnp.zeros_like(tope)
    for e in range(n_experts):
        base = jnp.where(tope == e, loff_s[j * n_experts + e], base)
    pos = base + idx[top_k:]
    pos_ref[0] = jnp.concatenate([pos, jnp.zeros((8 - top_k, tt), I32)], axis=0)
    rows = lax.broadcasted_iota(jnp.int16, (lr, tt), 0)
    pos16 = pos.astype(jnp.int16)
    perm = jnp.zeros((lr, tt), BF16)
    for k in range(top_k):
        perm = jnp.where(rows == pos16[k:k + 1], jnp.ones((lr, tt), BF16), perm)
    x = jnp.where(j < nt0, xa_ref[...], xb_ref[...])
    buf[slot] = _dot(perm, x.astype(BF16)).astype(BF16)

    def drained(step, s):
        _wait_tile(step, n_experts, loff_s, seg_s, buf.at[s], xs_ref, sems.at[s], True)

    _piece_copies(j, lo_t, go_t, cnt_t, buf.at[slot], xs_ref, sems.at[slot], True)

    @pl.when(j > 0)
    def _():
        drained(j - 1, 1 - slot)

    @pl.when(j == nt - 1)
    def _():
        drained(j, slot)
        zrows = EXPERT_BLOCK
        buf[0, :zrows] = jnp.zeros((zrows, buf.shape[2]), BF16)
        _segment_copies(0, n_experts, None, plen_s, poff_s, buf.at[0], xs_ref, sems.at[0], True, False)
        _segment_copies(0, n_experts, None, plen_s, poff_s, buf.at[0], xs_ref, sems.at[0], True, True)
        n_tail = xs_ref.shape[0] // zrows - nb_s[0]

        def tail_copy(i):
            dst = xs_ref.at[pl.ds(pl.multiple_of((nb_s[0] + i) * zrows, zrows), zrows)]
            return pltpu.make_async_copy(buf.at[0, pl.ds(0, zrows)], dst, sems.at[0])

        lax.fori_loop(0, n_tail, lambda i, c: (tail_copy(i).start(), c)[1], 0)
        lax.fori_loop(0, n_tail, lambda i, c: (tail_copy(i).wait(), c)[1], 0)


def _dispatch(x1s, idx, loff, seg, pieces, plen, poff, n_blocks, n_rows, local_rows):
    d = x1s[0].shape[1]
    tt = TOKEN_TILE
    nt0 = x1s[0].shape[0] // tt
    nt = nt0 + x1s[1].shape[0] // tt
    e = loff.shape[0] // nt
    xa, xb = _dual_specs((tt, d), nt0)
    grid_spec = pltpu.PrefetchScalarGridSpec(
        num_scalar_prefetch=8,
        grid=(nt,),
        in_specs=[xa, xb, pl.BlockSpec((1, 2 * TOP_K, tt), lambda i, *_: (i, 0, 0))],
        out_specs=[pl.BlockSpec(memory_space=pl.ANY),
                   pl.BlockSpec((1, 8, tt), lambda i, *_: (i, 0, 0))],
        scratch_shapes=[pltpu.VMEM((2, local_rows, d), BF16), pltpu.SemaphoreType.DMA((2,))],
    )
    return pl.pallas_call(
        functools.partial(_dispatch_kernel, n_experts=e, top_k=TOP_K, nt0=nt0),
        grid_spec=grid_spec,
        out_shape=[jax.ShapeDtypeStruct((n_rows, d), BF16), jax.ShapeDtypeStruct((nt, 8, tt), I32)],
        compiler_params=_params(1, "arbitrary"),
        name="moe_dispatch",
    )(loff, seg, *pieces, plen, poff, n_blocks, x1s[0], x1s[1], idx)


def _combine_kernel(loff_s, seg_s, lo_t, go_t, cnt_t, ys_ref, pos_ref, gate_ref, xa_ref, xb_ref, g2_ref, b2_ref,
                    oa_ref, ob_ref, buf, wsel_ref, sems, *, n_experts, top_k, alpha, nt0):
    j = pl.program_id(0)
    nt = pl.num_programs(0)
    slot = j % 2
    tt = xa_ref.shape[0]
    lr = buf.shape[1]

    def prefetch(step, s):
        last = step * n_experts + n_experts - 1
        covered = loff_s[last] + seg_s[last]
        for r0 in range(0, lr, ZERO_CHUNK):
            @pl.when(r0 + ZERO_CHUNK > covered)
            def _(r0=r0):
                buf[s, r0:r0 + ZERO_CHUNK] = jnp.zeros((ZERO_CHUNK, buf.shape[2]), BF16)
        _piece_copies(step, lo_t, go_t, cnt_t, buf.at[s], ys_ref, sems.at[s], False)

    @pl.when(j == 0)
    def _():
        prefetch(0, 0)

    pos = pos_ref[0].astype(jnp.int16)
    gate = gate_ref[0].astype(BF16)
    rows = lax.broadcasted_iota(jnp.int16, (lr, tt), 0)
    wsel = jnp.zeros((lr, tt), BF16)
    for k in range(top_k):
        wsel = jnp.where(rows == pos[k:k + 1], gate[k:k + 1], wsel)
    wsel_ref[...] = wsel

    @pl.when(j + 1 < nt)
    def _():
        prefetch(j + 1, 1 - slot)

    _wait_tile(j, n_experts, loff_s, seg_s, buf.at[slot], ys_ref, sems.at[slot], False)
    moe = lax.dot_general(wsel_ref[...], buf[slot], (((0,), (0,)), ((), ())), preferred_element_type=F32)
    x1 = jnp.where(j < nt0, xa_ref[...], xb_ref[...])
    out = _layer_norm(alpha * x1 + moe, g2_ref[...], b2_ref[...])

    @pl.when(j < nt0)
    def _():
        oa_ref[...] = out

    @pl.when(j >= nt0)
    def _():
        ob_ref[...] = out


def _combine(ys, pos_tm, gate_tm, x1s, g2, b2, loff, seg, pieces, local_rows, alpha):
    d = x1s[0].shape[1]
    tt = TOKEN_TILE
    nt0 = x1s[0].shape[0] // tt
    nt = nt0 + x1s[1].shape[0] // tt
    e = loff.shape[0] // nt
    xa, xb = _dual_specs((tt, d), nt0)
    grid_spec = pltpu.PrefetchScalarGridSpec(
        num_scalar_prefetch=5,
        grid=(nt,),
        in_specs=[pl.BlockSpec(memory_space=pl.ANY),
                  pl.BlockSpec((1, 8, tt), lambda i, *_: (i, 0, 0)),
                  pl.BlockSpec((1, 8, tt), lambda i, *_: (i, 0, 0)),
                  xa, xb,
                  pl.BlockSpec((1, d), lambda i, *_: (0, 0)),
                  pl.BlockSpec((1, d), lambda i, *_: (0, 0))],
        out_specs=list(_dual_specs((tt, d), nt0)),
        scratch_shapes=[pltpu.VMEM((2, local_rows, d), BF16), pltpu.VMEM((local_rows, tt), BF16),
                        pltpu.SemaphoreType.DMA((2,))],
    )
    return pl.pallas_call(
        functools.partial(_combine_kernel, n_experts=e, top_k=TOP_K, alpha=alpha, nt0=nt0),
        grid_spec=grid_spec,
        out_shape=[jax.ShapeDtypeStruct(x.shape, F32) for x in x1s],
        compiler_params=_params(1, "arbitrary"),
        name="moe_combine",
    )(loff, seg, *pieces, ys, pos_tm, gate_tm, x1s[0], x1s[1], g2, b2)


def _expert_kernel(be_s, nb_s, xs_ref, wgu_ref, bgu_ref, wd_ref, bd_ref, ys_ref, wgu_bf, wd_bf):
    b = pl.program_id(0)
    active = b < nb_s[0]
    fresh = jnp.logical_or(b == 0, be_s[b] != be_s[jnp.maximum(b - 1, 0)])

    @pl.when(jnp.logical_and(active, fresh))
    def _():
        wgu_bf[...] = wgu_ref[0].astype(BF16)
        wd_bf[...] = wd_ref[0].astype(BF16)

    @pl.when(active)
    def _():
        dff = wd_ref.shape[1]
        gu = _dot(xs_ref[...], wgu_bf[...]) + bgu_ref[0]
        gate = jnp.minimum(gu[:, :dff], SWIGLU_LIMIT)
        up = jnp.clip(gu[:, dff:], -SWIGLU_LIMIT, SWIGLU_LIMIT)
        h = (up + 1.0) * (gate * jax.nn.sigmoid(SWIGLU_ALPHA * gate))
        ys_ref[...] = (_dot(h.astype(BF16), wd_bf[...]) + bd_ref[0]).astype(BF16)

    @pl.when(jnp.logical_not(active))
    def _():
        ys_ref[...] = jnp.zeros_like(ys_ref)


def _experts(xs, block_expert, n_blocks, wgu, bgu, wd, bdn):
    n_rows, d = xs.shape
    bm = EXPERT_BLOCK
    f2 = wgu.shape[2]
    dff = wd.shape[1]
    blk = lambda b, be, nb: (jnp.minimum(b, nb[0] - 1), 0)
    wsel = lambda b, be, nb: (be[b], 0, 0)
    grid_spec = pltpu.PrefetchScalarGridSpec(
        num_scalar_prefetch=2,
        grid=(n_rows // bm,),
        in_specs=[pl.BlockSpec((bm, d), blk),
                  pl.BlockSpec((1, d, f2), wsel), pl.BlockSpec((1, 1, f2), wsel),
                  pl.BlockSpec((1, dff, d), wsel), pl.BlockSpec((1, 1, d), wsel)],
        out_specs=pl.BlockSpec((bm, d), lambda b, be, nb: (b, 0)),
        scratch_shapes=[pltpu.VMEM((d, f2), BF16), pltpu.VMEM((dff, d), BF16)],
    )
    return pl.pallas_call(
        _expert_kernel,
        grid_spec=grid_spec,
        out_shape=jax.ShapeDtypeStruct((n_rows, d), BF16),
        compiler_params=_params(1, "arbitrary"),
        name="moe_experts",
    )(block_expert, n_blocks, xs, wgu, bgu, wd, bdn)


def _routed_moe(routed, w_gate_up, b_gate_up, w_down, b_down, g2, b2, alpha):
    x1s = [r[0] for r in routed]
    idx, gate, cnt = (jnp.concatenate([r[i] for r in routed], axis=0) for i in (1, 2, 3))
    t = x1s[0].shape[0] + x1s[1].shape[0]
    e = cnt.shape[1]
    tt = TOKEN_TILE
    nt = t // tt
    bm = EXPERT_BLOCK

    cnt = cnt.reshape(nt, e)
    seg = (cnt + SEG_ALIGN - 1) // SEG_ALIGN * SEG_ALIGN
    loff = jnp.cumsum(seg, axis=1) - seg
    per_expert = jnp.sum(seg, axis=0)
    padded = (per_expert + bm - 1) // bm * bm
    blocks_end = jnp.cumsum(padded) // bm
    start = jnp.cumsum(padded) - padded
    goff = start[None, :] + jnp.cumsum(seg, axis=0) - seg
    local_rows = TOP_K * tt + e * SEG_ALIGN
    n_rows_max = -(-(TOP_K * t + (SEG_ALIGN - 1) * e * nt) // bm) * bm + e * bm
    n_blocks = blocks_end[-1:].astype(I32)
    bidx = jnp.minimum(jnp.arange(n_rows_max // bm, dtype=I32), n_blocks[0] - 1)
    block_expert = jnp.minimum(jnp.sum(bidx[:, None] >= blocks_end[None, :], axis=1), e - 1).astype(I32)
    flat = lambda a: a.reshape(-1).astype(I32)
    plen, poff = flat(padded - per_expert), flat(start + per_expert)

    assert e <= PIECE_CAP and local_rows // SEG_CHUNK <= PIECE_CAP
    pieces = _piece_tables(seg, loff, goff)
    xs, pos = _dispatch(x1s, idx, flat(loff), flat(seg), pieces, plen, poff, n_blocks, n_rows_max,
                        local_rows)
    ys = _experts(xs, block_expert, n_blocks, w_gate_up, b_gate_up.reshape(e, 1, -1),
                  w_down, b_down.reshape(e, 1, -1))
    return _combine(ys, pos, gate, x1s, g2, b2, flat(loff), flat(seg), pieces, local_rows, alpha)


def _mixer(x, apply_ln, lng, lnb, lay, consts, alpha):
    batch, seq, d = x.shape
    t = batch * seq
    fw = lay["g_fourier"].shape[1]
    n2 = MINOR
    x2d = x.reshape(t, d)
    z, x0, w = _in_proj(x2d, batch, seq, fw, lng, lnb, lay["w_in"], lay["b_in"], consts["bd"],
                        lay["w_short"], lay["b_short"], apply_ln)
    c = x0.shape[1]

    n1 = seq // n2
    g = jnp.asarray(_major_matrix(n1, n1, n1, -1)).astype(BF16)
    y = _dft_major(g, z.reshape(batch, n1, n2, fw), True)
    ortho = 1.0 / math.sqrt(seq * FOURIER_GROUP_DIM)
    wf_real = jnp.asarray(_minor_matrix(n2, -1)[:n2] * ortho).astype(BF16)
    yf = _dft_minor(y, wf_real, "real")

    n = 2 * seq
    m1 = n // n2
    kf, asum = _hyena_filter(seq, consts["bands"], lay["filt_w1"], lay["filt_b1"], lay["filt_w2"],
                             lay["filt_b2"], lay["filt_w3"], lay["filt_b3"], lay["filt_w_out"],
                             lay["filt_freq"], lay["filt_decay"])
    gk = jnp.asarray(_major_matrix(m1, m1, m1, -1, real_in=True)).astype(BF16)
    ky = _dft_major(gk, kf.reshape(1, m1, n2, c), True)
    gfw = jnp.asarray(_major_matrix(m1, m1 // 2, m1, -1)).astype(BF16)
    gin = jnp.asarray(_major_matrix(m1, m1, m1 // 2, +1)).astype(BF16)
    pairs = batch // 2
    u = _dft_major(gfw, w.reshape(pairs, m1, n2, c), True)
    wi = jnp.asarray(_minor_matrix(n2, +1) * (1.0 / n)).astype(BF16)
    v = _dft_minor(u, consts["wf"], "conv", wi_bf=wi, kf4=ky)
    yc = _dft_major(gin, v, False)

    inv_norm = 1.0 / asum
    return _out_proj(yf.reshape(t, fw), yc.reshape(t, c), x0, w, x2d, lng, lnb, inv_norm,
                     lay["hyena_skip"], lay["g_fourier"], lay["g_hyena"], lay["w_out"], lay["b_out"],
                     lay["ln1_g"], lay["ln1_b"], lay["w_router_t"], lay["b_router"], alpha, apply_ln)


def kernel(x_prompt, x_sample, ln_in_g, ln_in_b, w_in, b_in, w_short, b_short, filt_w1, filt_b1, filt_w2, filt_b2, filt_w3, filt_b3, filt_w_out, filt_freq, filt_decay, hyena_skip, g_fourier, g_hyena, w_out, b_out, ln1_g, ln1_b, w_router, b_router, w_gate_up, b_gate_up, w_down, b_down, ln2_g, ln2_b):
    depth, d, in_width = w_in.shape
    fw = g_fourier.shape[1]
    alpha = (2.0 * depth) ** 0.25
    row = lambda a: a.reshape(1, -1)

    gd = FOURIER_GROUP_DIM
    cg, sg = _cs(np.outer(np.arange(gd), np.arange(gd)), gd)
    eye = np.eye(LANES // gd)
    bd = np.concatenate([np.kron(eye, cg), -np.kron(eye, sg)], axis=1).astype(np.float32)
    bands = jnp.linspace(1e-4, FILTER_BANDS - 1, FILTER_BANDS, dtype=F32)
    bands_row = jnp.zeros((1, LANES), F32).at[0, 1:1 + FILTER_BANDS].set(bands)
    bands_row = bands_row.at[0, 1 + FILTER_BANDS:1 + 2 * FILTER_BANDS].set(bands)
    consts = dict(bd=jnp.asarray(bd).astype(BF16), bands=bands_row,
                  wf=jnp.asarray(_minor_matrix(MINOR, -1)).astype(BF16))

    xs = [x_prompt, x_sample]
    for l in range(depth):
        emb = filt_w1.shape[1]
        twice = lambda a: jnp.concatenate([row(a), row(a)], axis=1)
        blockdiag = lambda a: jnp.kron(jnp.eye(2, dtype=F32), a)
        w1_pad = jnp.zeros((LANES, filt_w1.shape[2]), F32).at[:emb].set(filt_w1[l])
        wl = filt_w_out[l]
        wl_halves = jnp.stack([jnp.concatenate([wl, jnp.zeros_like(wl)], axis=0),
                               jnp.concatenate([jnp.zeros_like(wl), wl], axis=0)])
        lay = dict(
            w_in=w_in[l].astype(BF16), b_in=row(b_in[l]), w_short=w_short[l], b_short=row(b_short[l]),
            filt_w1=blockdiag(w1_pad),
            filt_b1=twice(filt_b1[l]), filt_w2=blockdiag(filt_w2[l]), filt_b2=twice(filt_b2[l]),
            filt_w3=blockdiag(filt_w3[l]),
            filt_b3=twice(filt_b3[l]), filt_w_out=wl_halves, filt_freq=twice(filt_freq[l]),
            filt_decay=filt_decay[l], hyena_skip=row(hyena_skip[l]), g_fourier=row(g_fourier[l]),
            g_hyena=row(g_hyena[l]), w_out=w_out[l].astype(BF16), b_out=row(b_out[l]),
            ln1_g=row(ln1_g[l]), ln1_b=row(ln1_b[l]),
            w_router_t=w_router[l].T, b_router=b_router[l].reshape(-1, 1))
        routed = [_mixer(x, l == 0, row(ln_in_g), row(ln_in_b), lay, consts, alpha) for x in xs]
        outs = _routed_moe(routed, w_gate_up[l], b_gate_up[l], w_down[l], b_down[l],
                           row(ln2_g[l]), row(ln2_b[l]), alpha)
        xs = [o.reshape(x.shape) for o, x in zip(outs, xs)]
    return (xs[0], xs[1])
```
